```python
import math
import jax, jax.numpy as jnp
from jax import lax
import numpy as np

D_MODEL = 1024
BATCH = 8
SEQ = 8192
DEPTH = 1

SB_HEADS = 8
SB_HEAD_DIM = 64
SB_WIDTH = SB_HEADS * SB_HEAD_DIM
MLA_HEADS = 8
MLA_NOPE_DIM = 64
MLA_ROPE_DIM = 32
MLA_V_DIM = 64
MLA_Q_RANK = 384
MLA_KV_RANK = 256
MLA_WIDTH = MLA_HEADS * MLA_V_DIM
ROPE_THETA = 10000.0
D_FF = -(-(8 * D_MODEL) // (3 * 256)) * 256
N_BRANCHES = 2
BLOCK_Q = 128
EPS = 1e-6

IN_SPLITS = [
    SB_WIDTH,
    SB_WIDTH,
    SB_WIDTH,
    MLA_Q_RANK,
    MLA_KV_RANK,
    MLA_ROPE_DIM,
    N_BRANCHES * D_MODEL,
]
IN_OFFSETS = list(np.cumsum(IN_SPLITS)[:-1])
D_IN = int(sum(IN_SPLITS))

kernel_name = "sandwich_gated_sb_mla_swiglu_block"


def rms_norm(x, g):
    xf = x.astype(jnp.float32)
    y = xf * lax.rsqrt(jnp.mean(xf * xf, axis=-1, keepdims=True) + EPS)
    return (y * g.astype(jnp.float32)).astype(x.dtype)


def rope_tables(positions, dtype):
    inv_freq = ROPE_THETA ** (-jnp.arange(0, MLA_ROPE_DIM, 2, dtype=jnp.float32) / MLA_ROPE_DIM)
    ang = positions.astype(jnp.float32)[..., None] * inv_freq
    return jnp.cos(ang)[:, :, None, :].astype(dtype), jnp.sin(ang)[:, :, None, :].astype(dtype)


def apply_rope(x, cos, sin):
    x1, x2 = jnp.split(x, 2, axis=-1)
    return jnp.concatenate([x1 * cos - x2 * sin, x2 * cos + x1 * sin], axis=-1)


def stick_breaking_block(q_blk, k_pre, v_pre, q_start):
    tq, tk = q_blk.shape[1], k_pre.shape[1]
    z = jnp.einsum('bqhd,bkhd->bhqk', q_blk, k_pre).astype(jnp.float32) / math.sqrt(SB_HEAD_DIM)
    qpos = q_start + jnp.arange(tq)
    kpos = jnp.arange(tk)
    causal = kpos[None, :] < qpos[:, None]
    log_1m_beta = jnp.where(causal, jax.nn.log_sigmoid(-z), 0.0)
    rev_excl = lax.cumsum(log_1m_beta, axis=3, reverse=True) - log_1m_beta
    log_a = jax.nn.log_sigmoid(z) + rev_excl
    a = jnp.where(causal, jnp.exp(log_a), 0.0)
    return jnp.einsum('bhqk,bkhd->bqhd', a.astype(v_pre.dtype), v_pre)


def mla_block(qn_blk, qr_blk, kn_pre, kr_pre, v_pre, q_start):
    tq, tk = qn_blk.shape[1], kn_pre.shape[1]
    s = (jnp.einsum('bqhd,bkhd->bhqk', qn_blk, kn_pre)
         + jnp.einsum('bqhr,bkr->bhqk', qr_blk, kr_pre)).astype(jnp.float32)
    s = s / math.sqrt(MLA_NOPE_DIM + MLA_ROPE_DIM)
    causal = jnp.arange(tk)[None, :] <= (q_start + jnp.arange(tq))[:, None]
    s = jnp.where(causal, s, jnp.finfo(jnp.float32).min)
    p = jax.nn.softmax(s, axis=-1)
    return jnp.einsum('bhqk,bkhd->bqhd', p.astype(v_pre.dtype), v_pre)


def _fwd_setup_inputs(seed: int = 0) -> dict:
    key = jax.random.key(seed)
    ks = jax.random.split(key, 20)

    def w(k, shape, fan_in):
        return jax.random.normal(k, shape, jnp.float32) * fan_in ** -0.5

    def gain(k, n):
        return 1.0 + 0.02 * jax.random.normal(k, (DEPTH, n), jnp.float32)

    x = jax.random.normal(ks[0], (BATCH, SEQ, D_MODEL), jnp.float32)
    offset = jax.random.randint(ks[1], (BATCH, 1), 0, 4096, dtype=jnp.int32)
    positions = (offset + jnp.arange(SEQ, dtype=jnp.int32)[None, :]).astype(jnp.int32)
    return {
        "x": x,
        "positions": positions,
        "norm_mix_pre": gain(ks[2], D_MODEL),
        "norm_mix_post": gain(ks[3], D_MODEL),
        "w_in": w(ks[4], (DEPTH, D_MODEL, D_IN), D_MODEL),
        "b_gate": 0.02 * jax.random.normal(ks[5], (DEPTH, N_BRANCHES * D_MODEL), jnp.float32),
        "q_norm": gain(ks[6], MLA_Q_RANK),
        "w_uq": w(ks[7], (DEPTH, MLA_Q_RANK, MLA_HEADS * (MLA_NOPE_DIM + MLA_ROPE_DIM)), MLA_Q_RANK),
        "kv_norm": gain(ks[8], MLA_KV_RANK),
        "w_ukv": w(ks[9], (DEPTH, MLA_KV_RANK, MLA_HEADS * (MLA_NOPE_DIM + MLA_V_DIM)), MLA_KV_RANK),
        "w_proj_sb": w(ks[10], (DEPTH, SB_WIDTH, D_MODEL), SB_WIDTH),
        "w_proj_mla": w(ks[11], (DEPTH, MLA_WIDTH, D_MODEL), MLA_WIDTH),
        "w_out": w(ks[12], (DEPTH, D_MODEL, D_MODEL), D_MODEL),
        "norm_ffn_pre": gain(ks[13], D_MODEL),
        "norm_ffn_post": gain(ks[14], D_MODEL),
        "w_gate_up": w(ks[15], (DEPTH, D_MODEL, 2 * D_FF), D_MODEL),
        "w_down": w(ks[16], (DEPTH, D_FF, D_MODEL), D_FF),
    }


def _fwd_reference(x, positions, norm_mix_pre, norm_mix_post, w_in, b_gate, q_norm, w_uq,
              kv_norm, w_ukv, w_proj_sb, w_proj_mla, w_out, norm_ffn_pre, norm_ffn_post,
              w_gate_up, w_down):
    B, S, _ = x.shape
    n_blocks = S // BLOCK_Q
    cos, sin = rope_tables(positions, x.dtype)

    for l in range(DEPTH):
        h = rms_norm(x, norm_mix_pre[l])
        proj = h @ w_in[l]
        q_sb, k_sb, v_sb, c_q, c_kv, k_rope, gate_logits = jnp.split(proj, IN_OFFSETS, axis=-1)

        q_sb = q_sb.reshape(B, S, SB_HEADS, SB_HEAD_DIM)
        k_sb = k_sb.reshape(B, S, SB_HEADS, SB_HEAD_DIM)
        v_sb = v_sb.reshape(B, S, SB_HEADS, SB_HEAD_DIM)

        q_mla = (rms_norm(c_q, q_norm[l]) @ w_uq[l]).reshape(B, S, MLA_HEADS, MLA_NOPE_DIM + MLA_ROPE_DIM)
        q_nope, q_rope = q_mla[..., :MLA_NOPE_DIM], q_mla[..., MLA_NOPE_DIM:]
        q_rope = apply_rope(q_rope, cos, sin)
        kv = (rms_norm(c_kv, kv_norm[l]) @ w_ukv[l]).reshape(B, S, MLA_HEADS, MLA_NOPE_DIM + MLA_V_DIM)
        k_nope, v_mla = kv[..., :MLA_NOPE_DIM], kv[..., MLA_NOPE_DIM:]
        k_rope = apply_rope(k_rope[:, :, None, :], cos, sin)[:, :, 0, :]

        sb_outs, mla_outs = [], []
        for i in range(n_blocks):
            s0, s1 = i * BLOCK_Q, (i + 1) * BLOCK_Q
            sb_outs.append(stick_breaking_block(q_sb[:, s0:s1], k_sb[:, :s1], v_sb[:, :s1], s0))
            mla_outs.append(mla_block(q_nope[:, s0:s1], q_rope[:, s0:s1], k_nope[:, :s1],
                                      k_rope[:, :s1], v_mla[:, :s1], s0))
        o_sb = jnp.concatenate(sb_outs, axis=1).reshape(B, S, SB_WIDTH)
        o_mla = jnp.concatenate(mla_outs, axis=1).reshape(B, S, MLA_WIDTH)

        gates = jax.nn.sigmoid(gate_logits + b_gate[l]).reshape(B, S, N_BRANCHES, D_MODEL)
        merged = gates[:, :, 0] * (o_sb @ w_proj_sb[l]) + gates[:, :, 1] * (o_mla @ w_proj_mla[l])
        y = merged @ w_out[l]
        x = x + rms_norm(y, norm_mix_post[l])

        h = rms_norm(x, norm_ffn_pre[l])
        g, u = jnp.split(h @ w_gate_up[l], 2, axis=-1)
        f = (jax.nn.silu(g) * u) @ w_down[l]
        x = x + rms_norm(f, norm_ffn_post[l])
    return x


import jax as _jax
import jax.numpy as _jnp

TWIN_FORMAT = 'train_step'
FWD_PARAMS = ['x', 'positions', 'norm_mix_pre', 'norm_mix_post', 'w_in', 'b_gate', 'q_norm', 'w_uq', 'kv_norm', 'w_ukv', 'w_proj_sb', 'w_proj_mla', 'w_out', 'norm_ffn_pre', 'norm_ffn_post', 'w_gate_up', 'w_down']
TWIN_WEIGHTS = ['norm_mix_pre', 'norm_mix_post', 'w_in', 'b_gate', 'q_norm', 'w_uq', 'kv_norm', 'w_ukv', 'w_proj_sb', 'w_proj_mla', 'w_out', 'norm_ffn_pre', 'norm_ffn_post', 'w_gate_up', 'w_down']
TWIN_DIFF_INPUT = 'x'
TWIN_INPUTS = ['x', 'positions', 'norm_mix_pre', 'norm_mix_post', 'w_in', 'b_gate', 'q_norm', 'w_uq', 'kv_norm', 'w_ukv', 'w_proj_sb', 'w_proj_mla', 'w_out', 'norm_ffn_pre', 'norm_ffn_post', 'w_gate_up', 'w_down', 'loss_target', 'm_norm_mix_pre', 'm_norm_mix_post', 'm_w_in', 'm_b_gate', 'm_q_norm', 'm_w_uq', 'm_kv_norm', 'm_w_ukv', 'm_w_proj_sb', 'm_w_proj_mla', 'm_w_out', 'm_norm_ffn_pre', 'm_norm_ffn_post', 'm_w_gate_up', 'm_w_down', 'v_norm_mix_pre', 'v_norm_mix_post', 'v_w_in', 'v_b_gate', 'v_q_norm', 'v_w_uq', 'v_kv_norm', 'v_w_ukv', 'v_w_proj_sb', 'v_w_proj_mla', 'v_w_out', 'v_norm_ffn_pre', 'v_norm_ffn_post', 'v_w_gate_up', 'v_w_down']
TWIN_OUTPUTS = ['loss', 'grad_x', 'grad_norm_mix_pre', 'grad_norm_mix_post', 'grad_w_in', 'grad_b_gate', 'grad_q_norm', 'grad_w_uq', 'grad_kv_norm', 'grad_w_ukv', 'grad_w_proj_sb', 'grad_w_proj_mla', 'grad_w_out', 'grad_norm_ffn_pre', 'grad_norm_ffn_post', 'grad_w_gate_up', 'grad_w_down', 'delta_norm_mix_pre', 'delta_norm_mix_post', 'delta_w_in', 'delta_b_gate', 'delta_q_norm', 'delta_w_uq', 'delta_kv_norm', 'delta_w_ukv', 'delta_w_proj_sb', 'delta_w_proj_mla', 'delta_w_out', 'delta_norm_ffn_pre', 'delta_norm_ffn_post', 'delta_w_gate_up', 'delta_w_down', 'new_m_norm_mix_pre', 'new_m_norm_mix_post', 'new_m_w_in', 'new_m_b_gate', 'new_m_q_norm', 'new_m_w_uq', 'new_m_kv_norm', 'new_m_w_ukv', 'new_m_w_proj_sb', 'new_m_w_proj_mla', 'new_m_w_out', 'new_m_norm_ffn_pre', 'new_m_norm_ffn_post', 'new_m_w_gate_up', 'new_m_w_down', 'new_v_norm_mix_pre', 'new_v_norm_mix_post', 'new_v_w_in', 'new_v_b_gate', 'new_v_q_norm', 'new_v_w_uq', 'new_v_kv_norm', 'new_v_w_ukv', 'new_v_w_proj_sb', 'new_v_w_proj_mla', 'new_v_w_out', 'new_v_norm_ffn_pre', 'new_v_norm_ffn_post', 'new_v_w_gate_up', 'new_v_w_down']
TWIN_LEAF_KINDS = {'loss': 'loss', 'grad_x': 'grad_x', 'grad_norm_mix_pre': 'grad_w', 'grad_norm_mix_post': 'grad_w', 'grad_w_in': 'grad_w', 'grad_b_gate': 'grad_w', 'grad_q_norm': 'grad_w', 'grad_w_uq': 'grad_w', 'grad_kv_norm': 'grad_w', 'grad_w_ukv': 'grad_w', 'grad_w_proj_sb': 'grad_w', 'grad_w_proj_mla': 'grad_w', 'grad_w_out': 'grad_w', 'grad_norm_ffn_pre': 'grad_w', 'grad_norm_ffn_post': 'grad_w', 'grad_w_gate_up': 'grad_w', 'grad_w_down': 'grad_w', 'delta_norm_mix_pre': 'delta_w', 'delta_norm_mix_post': 'delta_w', 'delta_w_in': 'delta_w', 'delta_b_gate': 'delta_w', 'delta_q_norm': 'delta_w', 'delta_w_uq': 'delta_w', 'delta_kv_norm': 'delta_w', 'delta_w_ukv': 'delta_w', 'delta_w_proj_sb': 'delta_w', 'delta_w_proj_mla': 'delta_w', 'delta_w_out': 'delta_w', 'delta_norm_ffn_pre': 'delta_w', 'delta_norm_ffn_post': 'delta_w', 'delta_w_gate_up': 'delta_w', 'delta_w_down': 'delta_w', 'new_m_norm_mix_pre': 'new_m', 'new_m_norm_mix_post': 'new_m', 'new_m_w_in': 'new_m', 'new_m_b_gate': 'new_m', 'new_m_q_norm': 'new_m', 'new_m_w_uq': 'new_m', 'new_m_kv_norm': 'new_m', 'new_m_w_ukv': 'new_m', 'new_m_w_proj_sb': 'new_m', 'new_m_w_proj_mla': 'new_m', 'new_m_w_out': 'new_m', 'new_m_norm_ffn_pre': 'new_m', 'new_m_norm_ffn_post': 'new_m', 'new_m_w_gate_up': 'new_m', 'new_m_w_down': 'new_m', 'new_v_norm_mix_pre': 'new_v', 'new_v_norm_mix_post': 'new_v', 'new_v_w_in': 'new_v', 'new_v_b_gate': 'new_v', 'new_v_q_norm': 'new_v', 'new_v_w_uq': 'new_v', 'new_v_kv_norm': 'new_v', 'new_v_w_ukv': 'new_v', 'new_v_w_proj_sb': 'new_v', 'new_v_w_proj_mla': 'new_v', 'new_v_w_out': 'new_v', 'new_v_norm_ffn_pre': 'new_v', 'new_v_norm_ffn_post': 'new_v', 'new_v_w_gate_up': 'new_v', 'new_v_w_down': 'new_v'}


def _forward(args):
    return _fwd_reference(*[args[k] for k in FWD_PARAMS])


def _output_shape():
    out = _jax.eval_shape(lambda: _forward(_fwd_setup_inputs(0)))
    return out.shape, out.dtype

N_MICROBATCH = 1
ADAM_LR = 0.001
ADAM_B1 = 0.9
ADAM_B2 = 0.999
ADAM_EPS = 1e-08
ADAM_WD = 0.01
ADAM_STEP = 10
PER_EXAMPLE_BATCH_AXIS = {'x': 0, 'positions': 0, 'loss_target': 0}
SHARED_INPUTS = []
_WEIGHT_DTYPES = {'norm_mix_pre': _jnp.float32, 'norm_mix_post': _jnp.float32, 'w_in': _jnp.float32, 'b_gate': _jnp.float32, 'q_norm': _jnp.float32, 'w_uq': _jnp.float32, 'kv_norm': _jnp.float32, 'w_ukv': _jnp.float32, 'w_proj_sb': _jnp.float32, 'w_proj_mla': _jnp.float32, 'w_out': _jnp.float32, 'norm_ffn_pre': _jnp.float32, 'norm_ffn_post': _jnp.float32, 'w_gate_up': _jnp.float32, 'w_down': _jnp.float32}
MOMENT_SCALE = {'norm_mix_pre': 7.854091e-01, 'norm_mix_post': 6.392644e+01, 'w_in': 3.963947e-01, 'b_gate': 2.579616e-01, 'q_norm': 2.609443e-01, 'w_uq': 1.880470e-01, 'kv_norm': 5.147569e-01, 'w_ukv': 2.382622e-01, 'w_proj_sb': 7.792521e-01, 'w_proj_mla': 1.949506e-01, 'w_out': 8.061688e-01, 'norm_ffn_pre': 8.007226e-01, 'norm_ffn_post': 6.388049e+01, 'w_gate_up': 3.563406e-01, 'w_down': 7.347758e-01}


def _to_microbatches(a, axis):
    t = _jnp.moveaxis(a, axis, 0)
    t = t.reshape((N_MICROBATCH, t.shape[0] // N_MICROBATCH) + t.shape[1:])
    return _jnp.moveaxis(t, 1, axis + 1)


def setup_inputs(seed: int = 0) -> dict:
    inp = _fwd_setup_inputs(seed)
    key = _jax.random.fold_in(_jax.random.key(seed), 7919)
    shape, _ = _output_shape()
    out = dict(inp)
    out["loss_target"] = _jax.random.normal(_jax.random.fold_in(key, 0), shape, _jnp.float32)
    for i, name in enumerate(TWIN_WEIGHTS):
        w = inp[name].astype(_jnp.float32)
        if MOMENT_SCALE is None:
            s = _jnp.sqrt(_jnp.mean(_jnp.square(w)) + 1e-30)
        else:
            s = MOMENT_SCALE[name]
        km, kv = _jax.random.split(_jax.random.fold_in(key, i + 1))
        out[name] = w
        out["m_" + name] = s * _jax.random.normal(km, w.shape, _jnp.float32)
        out["v_" + name] = (s * s) * _jax.random.uniform(kv, w.shape, _jnp.float32, 0.5, 1.5)
    if N_MICROBATCH > 1:
        for name, axis in PER_EXAMPLE_BATCH_AXIS.items():
            out[name] = _to_microbatches(out[name], axis)
    return {'x': out['x'], 'positions': out['positions'], 'norm_mix_pre': out['norm_mix_pre'], 'norm_mix_post': out['norm_mix_post'], 'w_in': out['w_in'], 'b_gate': out['b_gate'], 'q_norm': out['q_norm'], 'w_uq': out['w_uq'], 'kv_norm': out['kv_norm'], 'w_ukv': out['w_ukv'], 'w_proj_sb': out['w_proj_sb'], 'w_proj_mla': out['w_proj_mla'], 'w_out': out['w_out'], 'norm_ffn_pre': out['norm_ffn_pre'], 'norm_ffn_post': out['norm_ffn_post'], 'w_gate_up': out['w_gate_up'], 'w_down': out['w_down'], 'loss_target': out['loss_target'], 'm_norm_mix_pre': out['m_norm_mix_pre'], 'm_norm_mix_post': out['m_norm_mix_post'], 'm_w_in': out['m_w_in'], 'm_b_gate': out['m_b_gate'], 'm_q_norm': out['m_q_norm'], 'm_w_uq': out['m_w_uq'], 'm_kv_norm': out['m_kv_norm'], 'm_w_ukv': out['m_w_ukv'], 'm_w_proj_sb': out['m_w_proj_sb'], 'm_w_proj_mla': out['m_w_proj_mla'], 'm_w_out': out['m_w_out'], 'm_norm_ffn_pre': out['m_norm_ffn_pre'], 'm_norm_ffn_post': out['m_norm_ffn_post'], 'm_w_gate_up': out['m_w_gate_up'], 'm_w_down': out['m_w_down'], 'v_norm_mix_pre': out['v_norm_mix_pre'], 'v_norm_mix_post': out['v_norm_mix_post'], 'v_w_in': out['v_w_in'], 'v_b_gate': out['v_b_gate'], 'v_q_norm': out['v_q_norm'], 'v_w_uq': out['v_w_uq'], 'v_kv_norm': out['v_kv_norm'], 'v_w_ukv': out['v_w_ukv'], 'v_w_proj_sb': out['v_w_proj_sb'], 'v_w_proj_mla': out['v_w_proj_mla'], 'v_w_out': out['v_w_out'], 'v_norm_ffn_pre': out['v_norm_ffn_pre'], 'v_norm_ffn_post': out['v_norm_ffn_post'], 'v_w_gate_up': out['v_w_gate_up'], 'v_w_down': out['v_w_down']}


def _loss(weights, diff, rest, loss_target):
    with _jax.named_scope("forward"):
        args = {**rest, TWIN_DIFF_INPUT: diff, **{k: w.astype(_WEIGHT_DTYPES[k]) for k, w in weights.items()}}
        y = _forward(args)
    with _jax.named_scope("loss_head"):
        err = _jnp.square(y.astype(_jnp.float32) - loss_target)
        return 0.5 * _jnp.sum(_jnp.mean(err, axis=-1)) if err.ndim else 0.5 * err


def _adamw(w, g, m, v):
    m = ADAM_B1 * m + (1.0 - ADAM_B1) * g
    v = ADAM_B2 * v + (1.0 - ADAM_B2) * _jnp.square(g)
    m_hat = m / (1.0 - ADAM_B1 ** ADAM_STEP)
    v_hat = v / (1.0 - ADAM_B2 ** ADAM_STEP)
    delta = -ADAM_LR * (m_hat / (_jnp.sqrt(v_hat) + ADAM_EPS) + ADAM_WD * w)
    return delta, m, v


def reference(x, positions, norm_mix_pre, norm_mix_post, w_in, b_gate, q_norm, w_uq, kv_norm, w_ukv, w_proj_sb, w_proj_mla, w_out, norm_ffn_pre, norm_ffn_post, w_gate_up, w_down, loss_target, m_norm_mix_pre, m_norm_mix_post, m_w_in, m_b_gate, m_q_norm, m_w_uq, m_kv_norm, m_w_ukv, m_w_proj_sb, m_w_proj_mla, m_w_out, m_norm_ffn_pre, m_norm_ffn_post, m_w_gate_up, m_w_down, v_norm_mix_pre, v_norm_mix_post, v_w_in, v_b_gate, v_q_norm, v_w_uq, v_kv_norm, v_w_ukv, v_w_proj_sb, v_w_proj_mla, v_w_out, v_norm_ffn_pre, v_norm_ffn_post, v_w_gate_up, v_w_down):
    given = dict(x=x, positions=positions, norm_mix_pre=norm_mix_pre, norm_mix_post=norm_mix_post, w_in=w_in, b_gate=b_gate, q_norm=q_norm, w_uq=w_uq, kv_norm=kv_norm, w_ukv=w_ukv, w_proj_sb=w_proj_sb, w_proj_mla=w_proj_mla, w_out=w_out, norm_ffn_pre=norm_ffn_pre, norm_ffn_post=norm_ffn_post, w_gate_up=w_gate_up, w_down=w_down, loss_target=loss_target, m_norm_mix_pre=m_norm_mix_pre, m_norm_mix_post=m_norm_mix_post, m_w_in=m_w_in, m_b_gate=m_b_gate, m_q_norm=m_q_norm, m_w_uq=m_w_uq, m_kv_norm=m_kv_norm, m_w_ukv=m_w_ukv, m_w_proj_sb=m_w_proj_sb, m_w_proj_mla=m_w_proj_mla, m_w_out=m_w_out, m_norm_ffn_pre=m_norm_ffn_pre, m_norm_ffn_post=m_norm_ffn_post, m_w_gate_up=m_w_gate_up, m_w_down=m_w_down, v_norm_mix_pre=v_norm_mix_pre, v_norm_mix_post=v_norm_mix_post, v_w_in=v_w_in, v_b_gate=v_b_gate, v_q_norm=v_q_norm, v_w_uq=v_w_uq, v_kv_norm=v_kv_norm, v_w_ukv=v_w_ukv, v_w_proj_sb=v_w_proj_sb, v_w_proj_mla=v_w_proj_mla, v_w_out=v_w_out, v_norm_ffn_pre=v_norm_ffn_pre, v_norm_ffn_post=v_norm_ffn_post, v_w_gate_up=v_w_gate_up, v_w_down=v_w_down)
    weights = {n: given[n] for n in TWIN_WEIGHTS}
    shared = {n: given[n] for n in SHARED_INPUTS}
    per_example = {n: given[n] for n in ['x', 'positions']}
    grad_fn = _jax.value_and_grad(_loss, argnums=(0, 1))

    def one_microbatch(ex, loss_target):
        ex = dict(ex)
        diff = ex.pop(TWIN_DIFF_INPUT)
        return grad_fn(weights, diff, {**shared, **ex}, loss_target)

    if N_MICROBATCH == 1:
        loss, (grad_w, grad_x) = one_microbatch(per_example, given["loss_target"])
    else:
        def body(carry, xs):
            loss_sum, grad_sum = carry
            l_k, (gw_k, gx_k) = one_microbatch(xs[0], xs[1])
            with _jax.named_scope("update"):
                return (loss_sum + l_k, _jax.tree.map(_jnp.add, grad_sum, gw_k)), gx_k

        init = (_jnp.zeros((), _jnp.float32), _jax.tree.map(_jnp.zeros_like, weights))
        (loss, grad_w), grad_x = _jax.lax.scan(body, init, (per_example, given["loss_target"]))
    with _jax.named_scope("update"):
        delta_w, new_m, new_v = {}, {}, {}
        for n in TWIN_WEIGHTS:
            delta_w[n], new_m[n], new_v[n] = _adamw(weights[n], grad_w[n], given["m_" + n], given["v_" + n])
    return (loss, grad_x, *[grad_w[n] for n in TWIN_WEIGHTS], *[delta_w[n] for n in TWIN_WEIGHTS],
            *[new_m[n] for n in TWIN_WEIGHTS], *[new_v[n] for n in TWIN_WEIGHTS])
```

```python
import functools
import math

import jax
import jax.numpy as jnp
from jax import lax
from jax.experimental import pallas as pl
from jax.experimental.pallas import tpu as pltpu

F32 = jnp.float32
BF16 = jnp.bfloat16
MESH = pl.DeviceIdType.MESH

D_MODEL = 1024
SB_WIDTH = 512
MLA_HEADS = 8
MLA_Q_RANK = 384
MLA_KV_RANK = 256
MLA_ROPE = 32
MLA_QK = 96
D_FF = 2816
ROPE_THETA = 10000.0
EPS = 1e-6
LANES = 128
HALF = 64
N_CHIPS = 4

ADAM_LR = 0.001
ADAM_B1 = 0.9
ADAM_B2 = 0.999
ADAM_EPS = 1e-08
ADAM_WD = 0.01
ADAM_STEP = 10

VMEM_LIMIT = 56 * 1024 * 1024

BIG = (
    ("w_in", 1024, 1064, 1),
    ("w_uq", 384, 192, 1),
    ("w_ukv", 256, 256, 1),
    ("w_proj_sb", 512, 256, 1),
    ("w_proj_mla", 512, 256, 1),
    ("w_out", 256, 1024, 0),
    ("w_gate_up", 1024, 1408, 1),
    ("w_down", 704, 1024, 0),
)
PACK_COLS = 1024
PACK_ELEMS = sum(r * c for _, r, c, _ in BIG)
PACK_ROWS = 3840
HALF_ROWS = PACK_ROWS // 2
SMALL = (
    ("norm_mix_pre", 1024),
    ("norm_mix_post", 1024),
    ("b_gate", 2048),
    ("q_norm", 384),
    ("kv_norm", 256),
    ("norm_ffn_pre", 1024),
    ("norm_ffn_post", 1024),
)
SMALL_ELEMS = sum(n for _, n in SMALL)
SMALL_ROWS = 8
WEIGHT_ORDER = ("norm_mix_pre", "norm_mix_post", "w_in", "b_gate", "q_norm", "w_uq", "kv_norm", "w_ukv",
                "w_proj_sb", "w_proj_mla", "w_out", "norm_ffn_pre", "norm_ffn_post", "w_gate_up", "w_down")


def _params():
    return pltpu.CompilerParams(vmem_limit_bytes=VMEM_LIMIT)


def _tile(dim, pref):
    t = (min(pref, dim) // LANES) * LANES
    while t >= LANES:
        if dim % t == 0:
            return t
        t -= LANES
    return dim


def _mm(a, b, mode, out_dtype, name, tm=1024, tn=1024, tk=512):
    if mode == "nn":
        (m, k), n = a.shape, b.shape[1]
    elif mode == "nt":
        (m, k), n = a.shape, b.shape[0]
    else:
        (k, m), n = a.shape, b.shape[1]
    tm, tn, tk = _tile(m, tm), _tile(n, tn), _tile(k, tk)
    nk = k // tk
    if mode == "tn":
        a_spec = pl.BlockSpec((tk, tm), lambda i, j, l: (l, i))
        lhs_dim = 0
    else:
        a_spec = pl.BlockSpec((tm, tk), lambda i, j, l: (i, l))
        lhs_dim = 1
    if mode == "nt":
        b_spec = pl.BlockSpec((tn, tk), lambda i, j, l: (j, l))
        rhs_dim = 1
    else:
        b_spec = pl.BlockSpec((tk, tn), lambda i, j, l: (l, j))
        rhs_dim = 0
    dims = (((lhs_dim,), (rhs_dim,)), ((), ()))

    def body(a_ref, b_ref, o_ref, acc_ref):
        l = pl.program_id(2)

        @pl.when(l == 0)
        def _():
            acc_ref[...] = jnp.zeros_like(acc_ref)

        acc_ref[...] += lax.dot_general(a_ref[...].astype(BF16), b_ref[...].astype(BF16), dims,
                                        preferred_element_type=F32)

        @pl.when(l == nk - 1)
        def _():
            o_ref[...] = acc_ref[...].astype(o_ref.dtype)

    return pl.pallas_call(
        body,
        name=name,
        grid=(m // tm, n // tn, nk),
        in_specs=[a_spec, b_spec],
        out_specs=pl.BlockSpec((tm, tn), lambda i, j, l: (i, j)),
        out_shape=jax.ShapeDtypeStruct((m, n), out_dtype),
        scratch_shapes=[pltpu.VMEM((tm, tn), F32)],
        compiler_params=_params(),
    )(a, b)


def _rowwise(fn, ins, outs, reds, tm, name):
    rows = ins[0].shape[0]
    tm = min(tm, rows)
    n_in, n_out = len(ins), len(outs)

    def spec(shape):
        if shape[0] == rows:
            return pl.BlockSpec((tm, shape[1]), lambda i: (i, 0))
        return pl.BlockSpec(shape, lambda i: (0, 0))

    def body(*refs):
        i = pl.program_id(0)
        vals = fn(*[r[...] for r in refs[:n_in]])
        for r, v in zip(refs[n_in:n_in + n_out], vals[:n_out]):
            r[...] = v.astype(r.dtype)
        for r, v in zip(refs[n_in + n_out:], vals[n_out:]):
            @pl.when(i == 0)
            def _():
                r[...] = jnp.zeros_like(r)

            r[...] += v

    out_shape = [jax.ShapeDtypeStruct((rows, d), dt) for d, dt in outs]
    out_shape += [jax.ShapeDtypeStruct((1, d), F32) for d in reds]
    out_specs = [spec(s.shape) for s in out_shape]
    return pl.pallas_call(
        body,
        name=name,
        grid=(rows // tm,),
        in_specs=[spec(a.shape) for a in ins],
        out_specs=out_specs,
        out_shape=out_shape,
        compiler_params=_params(),
    )(*ins)


def _colsum(v):
    return jnp.sum(v, axis=0, keepdims=True)


def _rms(v):
    r = lax.rsqrt(jnp.mean(v * v, axis=-1, keepdims=True) + EPS)
    return r, v * r


def _rms_bwd(dy, r, vhat, g):
    u = dy * g
    return r * (u - vhat * jnp.mean(u * vhat, axis=-1, keepdims=True))


def _sigmoid(t):
    return 1.0 / (1.0 + jnp.exp(-t))


def _lane(shape):
    return lax.broadcasted_iota(jnp.int32, shape, len(shape) - 1)


def _rope_swap(v):
    lane = _lane(v.shape)
    up = pltpu.roll(v, LANES - 16, 1)
    down = pltpu.roll(v, 16, 1)
    return jnp.where((lane >= 64) & (lane < 80), up, jnp.where((lane >= 80) & (lane < 96), down, 0.0))


def _softplus(z):
    return jnp.maximum(z, 0.0) + jnp.log(1.0 + jnp.exp(-jnp.abs(z)))


def _split_dot(v, m):
    hi = v.astype(BF16)
    lo = (v - hi.astype(F32)).astype(BF16)
    return (jnp.dot(hi, m, preferred_element_type=F32) + jnp.dot(lo, m, preferred_element_type=F32))


def _dot_nt(a, b):
    return lax.dot_general(a, b, (((1,), (1,)), ((), ())), preferred_element_type=F32)


def _dot_tn(a, b):
    return lax.dot_general(a, b, (((0,), (0,)), ((), ())), preferred_element_type=F32)


def _sb_fwd(qkv, name):
    s = qkv.shape[0]
    t = min(256, s)
    npair = SB_WIDTH // LANES

    def body(q_ref, k_ref, v_ref, o_ref, t_ref):
        i = pl.program_id(1)
        lane = _lane((1, LANES))
        row = lax.broadcasted_iota(jnp.int32, (t, t), 0)
        col = lax.broadcasted_iota(jnp.int32, (t, t), 1)
        tri = col < row
        m_after = jnp.where(row > col, 1.0, 0.0).astype(BF16)
        q = q_ref[...]
        o_tot = jnp.zeros((t, LANES), F32)
        t_tot = jnp.zeros((t, LANES), F32)
        for hh in range(2):
            hm = (lane < HALF) if hh == 0 else (lane >= HALF)
            qh = jnp.where(hm, q, jnp.zeros_like(q)) * 0.125

            def step(kb, carry, diag, qh=qh, hm=hm):
                c, acc = carry
                rows = pl.ds(pl.multiple_of(kb * t, t), t)
                ks = k_ref[rows, :]
                vs = v_ref[rows, :]
                vh = jnp.where(hm, vs, jnp.zeros_like(vs))
                z = _dot_nt(qh, ks)
                sp = _softplus(z)
                lb = -sp
                if diag:
                    lb = jnp.where(tri, lb, 0.0)
                after = _split_dot(lb, m_after)
                a = jnp.exp((z - sp) + after + c)
                if diag:
                    a = jnp.where(tri, a, 0.0)
                acc = acc + jnp.dot(a.astype(BF16), vh, preferred_element_type=F32)
                c = c + after[:, 0:1] + lb[:, 0:1]
                return c, acc

            carry = step(i, (jnp.zeros((t, 1), F32), jnp.zeros((t, LANES), F32)), True)
            c, acc = lax.fori_loop(0, i, lambda n, cr: step(i - 1 - n, cr, False), carry)
            o_tot = o_tot + acc
            t_tot = jnp.where(hm, c, t_tot)
        o_ref[...] = o_tot
        t_ref[...] = t_tot

    blk = pl.BlockSpec((t, LANES), lambda p, i: (i, p))
    return pl.pallas_call(
        body,
        name=name,
        grid=(npair, s // t),
        in_specs=[
            blk,
            pl.BlockSpec((s, LANES), lambda p, i: (0, npair + p)),
            pl.BlockSpec((s, LANES), lambda p, i: (0, 2 * npair + p)),
        ],
        out_specs=[blk, blk],
        out_shape=[jax.ShapeDtypeStruct((s, SB_WIDTH), F32), jax.ShapeDtypeStruct((s, SB_WIDTH), F32)],
        compiler_params=_params(),
    )(qkv, qkv, qkv)


def _sb_bwd(qkv, do, tot, name):
    s = qkv.shape[0]
    t = min(256, s)
    npair = SB_WIDTH // LANES

    def body(q_ref, k_ref, v_ref, do_ref, t_ref, dq_ref, dk_ref, dv_ref):
        i = pl.program_id(1)

        @pl.when(i == 0)
        def _():
            dk_ref[...] = jnp.zeros_like(dk_ref)
            dv_ref[...] = jnp.zeros_like(dv_ref)

        lane = _lane((1, LANES))
        row = lax.broadcasted_iota(jnp.int32, (t, t), 0)
        col = lax.broadcasted_iota(jnp.int32, (t, t), 1)
        tri = col < row
        m_after = jnp.where(row > col, 1.0, 0.0).astype(BF16)
        m_before = jnp.where(row < col, 1.0, 0.0).astype(BF16)
        q = q_ref[...]
        do_b = do_ref[...].astype(BF16)
        tot_v = t_ref[...]
        dq_tot = jnp.zeros((t, LANES), F32)
        for hh in range(2):
            hm = (lane < HALF) if hh == 0 else (lane >= HALF)
            qh = jnp.where(hm, q, jnp.zeros_like(q))
            qs = qh * 0.125
            doh = jnp.where(hm, do_b, jnp.zeros_like(do_b))
            th = tot_v[:, hh * HALF:hh * HALF + 1]

            def step(kb, carry, diag, qh=qh, qs=qs, doh=doh, th=th, hm=hm):
                before, e_before, dq = carry
                rows = pl.ds(pl.multiple_of(kb * t, t), t)
                ks = k_ref[rows, :]
                vs = v_ref[rows, :]
                kh = jnp.where(hm, ks, jnp.zeros_like(ks))
                z = _dot_nt(qs, ks)
                sp = _softplus(z)
                lb = -sp
                if diag:
                    lb = jnp.where(tri, lb, 0.0)
                ls = z - sp
                after = _split_dot(lb, m_after)
                bsum = after[:, 0:1] + lb[:, 0:1]
                a = jnp.exp(ls + after + (th - before - bsum))
                if diag:
                    a = jnp.where(tri, a, 0.0)
                e = a * _dot_nt(doh, vs)
                e_pre = _split_dot(e, m_before) + e_before
                beta = jnp.exp(ls)
                dz = e * (1.0 - beta) - beta * e_pre
                if diag:
                    dz = jnp.where(tri, dz, 0.0)
                dzb = (dz * 0.125).astype(BF16)
                dq = dq + jnp.dot(dzb, kh, preferred_element_type=F32)
                dk_ref[rows, :] += _dot_tn(dzb, qh)
                dv_ref[rows, :] += _dot_tn(a.astype(BF16), doh)
                e_before = e_pre[:, t - 1:t] + e[:, t - 1:t]
                return before + bsum, e_before, dq

            zero = jnp.zeros((t, 1), F32)
            carry = lax.fori_loop(0, i, lambda kb, cr: step(kb, cr, False), (zero, zero, jnp.zeros((t, LANES), F32)))
            dq_tot = dq_tot + step(i, carry, True)[2]
        dq_ref[...] = dq_tot

    blk = pl.BlockSpec((t, LANES), lambda p, i: (i, p))
    full = pl.BlockSpec((s, LANES), lambda p, i: (0, p))
    out = jax.ShapeDtypeStruct((s, SB_WIDTH), F32)
    return pl.pallas_call(
        body,
        name=name,
        grid=(npair, s // t),
        in_specs=[
            blk,
            pl.BlockSpec((s, LANES), lambda p, i: (0, npair + p)),
            pl.BlockSpec((s, LANES), lambda p, i: (0, 2 * npair + p)),
            blk,
            blk,
        ],
        out_specs=[blk, full, full],
        out_shape=[out, out, out],
        compiler_params=_params(),
    )(qkv, qkv, qkv, do, tot)


MLA_SCALE = 1.0 / math.sqrt(MLA_QK)
NEG = -1e30


def _mla_fwd(qc, kc, vp, name):
    s = qc.shape[0]
    t = min(256, s)

    def body(q_ref, k_ref, v_ref, o_ref, lse_ref):
        i = pl.program_id(1)
        row = lax.broadcasted_iota(jnp.int32, (t, t), 0)
        col = lax.broadcasted_iota(jnp.int32, (t, t), 1)
        keep = col <= row
        q = q_ref[...]

        def step(kb, carry, diag):
            m, l, acc = carry
            rows = pl.ds(pl.multiple_of(kb * t, t), t)
            sc = _dot_nt(q, k_ref[rows, :]) * MLA_SCALE
            if diag:
                sc = jnp.where(keep, sc, NEG)
            m_new = jnp.maximum(m, jnp.max(sc, axis=-1, keepdims=True))
            alpha = jnp.exp(m - m_new)
            p = jnp.exp(sc - m_new)
            l = alpha * l + jnp.sum(p, axis=-1, keepdims=True)
            acc = alpha * acc + jnp.dot(p.astype(BF16), v_ref[rows, :], preferred_element_type=F32)
            return m_new, l, acc

        init = (jnp.full((t, 1), NEG, F32), jnp.zeros((t, 1), F32), jnp.zeros((t, LANES), F32))
        carry = lax.fori_loop(0, i, lambda kb, cr: step(kb, cr, False), init)
        m, l, acc = step(i, carry, True)
        o_ref[...] = acc / l
        lse_ref[...] = jnp.broadcast_to(m + jnp.log(l), (t, LANES))

    blk = pl.BlockSpec((t, LANES), lambda h, i: (i, h))
    full = pl.BlockSpec((s, LANES), lambda h, i: (0, h))
    out = jax.ShapeDtypeStruct((s, MLA_HEADS * LANES), F32)
    return pl.pallas_call(
        body,
        name=name,
        grid=(MLA_HEADS, s // t),
        in_specs=[blk, full, full],
        out_specs=[blk, blk],
        out_shape=[out, out],
        compiler_params=_params(),
    )(qc, kc, vp)


def _mla_bwd(qc, kc, vp, do, o, lse, name):
    s = qc.shape[0]
    t = min(256, s)

    def body(q_ref, k_ref, v_ref, do_ref, o_ref, lse_ref, dq_ref, dk_ref, dv_ref):
        i = pl.program_id(1)

        @pl.when(i == 0)
        def _():
            dk_ref[...] = jnp.zeros_like(dk_ref)
            dv_ref[...] = jnp.zeros_like(dv_ref)

        row = lax.broadcasted_iota(jnp.int32, (t, t), 0)
        col = lax.broadcasted_iota(jnp.int32, (t, t), 1)
        keep = col <= row
        q = q_ref[...]
        do_f = do_ref[...]
        do_b = do_f.astype(BF16)
        delta = jnp.sum(do_f * o_ref[...], axis=-1, keepdims=True)
        lse_v = lse_ref[:, 0:1]

        def step(kb, dq, diag):
            rows = pl.ds(pl.multiple_of(kb * t, t), t)
            ks = k_ref[rows, :]
            sc = _dot_nt(q, ks) * MLA_SCALE
            p = jnp.exp(sc - lse_v)
            if diag:
                p = jnp.where(keep, p, 0.0)
            ds = p * (_dot_nt(do_b, v_ref[rows, :]) - delta) * MLA_SCALE
            dsb = ds.astype(BF16)
            dk_ref[rows, :] += _dot_tn(dsb, q)
            dv_ref[rows, :] += _dot_tn(p.astype(BF16), do_b)
            return dq + jnp.dot(dsb, ks, preferred_element_type=F32)

        dq = lax.fori_loop(0, i, lambda kb, cr: step(kb, cr, False), jnp.zeros((t, LANES), F32))
        dq_ref[...] = step(i, dq, True)

    blk = pl.BlockSpec((t, LANES), lambda h, i: (i, h))
    full = pl.BlockSpec((s, LANES), lambda h, i: (0, h))
    out = jax.ShapeDtypeStruct((s, MLA_HEADS * LANES), F32)
    return pl.pallas_call(
        body,
        name=name,
        grid=(MLA_HEADS, s // t),
        in_specs=[blk, full, full, blk, blk, blk],
        out_specs=[blk, full, full],
        out_shape=[out, out, out],
        compiler_params=_params(),
    )(qc, kc, vp, do, o, lse)


def _position():
    return lax.axis_index("x"), lax.axis_index("y"), lax.axis_index("c")


def _other_chips(x, y):
    return [(1 - x, y), (x, 1 - y), (1 - x, 1 - y)]


HBM_SPEC = pl.BlockSpec(memory_space=pltpu.HBM)


def _gather_chips(block, name):
    def body(src, out, send_sems, recv_sems, local_sem):
        x, y, c = _position()
        mine = pltpu.make_async_copy(src, out.at[2 * x + y], local_sem)
        mine.start()
        sends = []
        for k, (px, py) in enumerate(_other_chips(x, y)):
            cp = pltpu.make_async_remote_copy(src_ref=src, dst_ref=out.at[2 * x + y], send_sem=send_sems.at[k],
                                              recv_sem=recv_sems.at[k], device_id=(px, py, c), device_id_type=MESH)
            cp.start()
            sends.append(cp)
        for k, (px, py) in enumerate(_other_chips(x, y)):
            pltpu.make_async_remote_copy(src_ref=src, dst_ref=out.at[2 * px + py], send_sem=send_sems.at[k],
                                         recv_sem=recv_sems.at[k], device_id=(px, py, c),
                                         device_id_type=MESH).wait_recv()
        for cp in sends:
            cp.wait_send()
        mine.wait()

    return pl.pallas_call(
        body,
        name=name,
        in_specs=[HBM_SPEC],
        out_specs=HBM_SPEC,
        out_shape=jax.ShapeDtypeStruct((N_CHIPS,) + block.shape, block.dtype),
        scratch_shapes=[pltpu.SemaphoreType.DMA((3,)), pltpu.SemaphoreType.DMA((3,)), pltpu.SemaphoreType.DMA(())],
    )(block)


def _swap_halves(gpack, name):
    n, r, cdim = gpack.shape
    h = r // 2

    def body(src, out, send_sem, recv_sem):
        x, y, c = _position()
        cp = pltpu.make_async_remote_copy(src_ref=src.at[:, pl.ds((1 - c) * h, h), :], dst_ref=out,
                                          send_sem=send_sem, recv_sem=recv_sem, device_id=(x, y, 1 - c),
                                          device_id_type=MESH)
        cp.start()
        cp.wait()

    return pl.pallas_call(
        body,
        name=name,
        in_specs=[HBM_SPEC],
        out_specs=HBM_SPEC,
        out_shape=jax.ShapeDtypeStruct((n, h, cdim), gpack.dtype),
        scratch_shapes=[pltpu.SemaphoreType.DMA(()), pltpu.SemaphoreType.DMA(())],
    )(gpack)


def _scatter_chips(part, name):
    def body(src, out, send_sems, recv_sems, local_sem):
        x, y, c = _position()
        me = 2 * x + y
        mine = pltpu.make_async_copy(src.at[me], out.at[me], local_sem)
        mine.start()
        sends = []
        for k, (px, py) in enumerate(_other_chips(x, y)):
            cp = pltpu.make_async_remote_copy(src_ref=src.at[2 * px + py], dst_ref=out.at[me],
                                              send_sem=send_sems.at[k], recv_sem=recv_sems.at[k],
                                              device_id=(px, py, c), device_id_type=MESH)
            cp.start()
            sends.append(cp)
        for k, (px, py) in enumerate(_other_chips(x, y)):
            pltpu.make_async_remote_copy(src_ref=src.at[me], dst_ref=out.at[2 * px + py], send_sem=send_sems.at[k],
                                         recv_sem=recv_sems.at[k], device_id=(px, py, c),
                                         device_id_type=MESH).wait_recv()
        for cp in sends:
            cp.wait_send()
        mine.wait()

    return pl.pallas_call(
        body,
        name=name,
        in_specs=[HBM_SPEC],
        out_specs=HBM_SPEC,
        out_shape=jax.ShapeDtypeStruct(part.shape, part.dtype),
        scratch_shapes=[pltpu.SemaphoreType.DMA((3,)), pltpu.SemaphoreType.DMA((3,)), pltpu.SemaphoreType.DMA(())],
    )(part)


def _join_halves(half, name):
    h, cdim = half.shape

    def body(src, out, send_sem, recv_sem, local_sem):
        x, y, c = _position()
        mine = pltpu.make_async_copy(src, out.at[pl.ds(c * h, h), :], local_sem)
        mine.start()
        cp = pltpu.make_async_remote_copy(src_ref=src, dst_ref=out.at[pl.ds(c * h, h), :], send_sem=send_sem,
                                          recv_sem=recv_sem, device_id=(x, y, 1 - c), device_id_type=MESH)
        cp.start()
        cp.wait()
        mine.wait()

    return pl.pallas_call(
        body,
        name=name,
        in_specs=[HBM_SPEC],
        out_specs=HBM_SPEC,
        out_shape=jax.ShapeDtypeStruct((2 * h, cdim), half.dtype),
        scratch_shapes=[pltpu.SemaphoreType.DMA(()), pltpu.SemaphoreType.DMA(()), pltpu.SemaphoreType.DMA(())],
    )(half)


def _allsum_small(v, name):
    shape = v.shape

    def body(v_ref, o_ref, buf, send_sems, recv_sems):
        x, y, c = _position()
        me = 4 * x + 2 * y + c
        buf[me] = v_ref[...]
        peers = []
        for k in range(1, 8):
            fx, fy, fc = (k >> 2) & 1, (k >> 1) & 1, k & 1
            peers.append((jnp.where(fx == 1, 1 - x, x), jnp.where(fy == 1, 1 - y, y), jnp.where(fc == 1, 1 - c, c)))
        sends = []
        for k, peer in enumerate(peers):
            cp = pltpu.make_async_remote_copy(src_ref=v_ref, dst_ref=buf.at[me], send_sem=send_sems.at[k],
                                              recv_sem=recv_sems.at[k], device_id=peer, device_id_type=MESH)
            cp.start()
            sends.append(cp)
        for k, (px, py, pc) in enumerate(peers):
            pltpu.make_async_remote_copy(src_ref=v_ref, dst_ref=buf.at[4 * px + 2 * py + pc],
                                         send_sem=send_sems.at[k], recv_sem=recv_sems.at[k],
                                         device_id=(px, py, pc), device_id_type=MESH).wait_recv()
        for cp in sends:
            cp.wait_send()
        acc = buf[0]
        for d in range(1, 8):
            acc = acc + buf[d]
        o_ref[...] = acc

    return pl.pallas_call(
        body,
        name=name,
        in_specs=[pl.BlockSpec(memory_space=pltpu.VMEM)],
        out_specs=pl.BlockSpec(memory_space=pltpu.VMEM),
        out_shape=jax.ShapeDtypeStruct(shape, F32),
        scratch_shapes=[pltpu.VMEM((8,) + shape, F32), pltpu.SemaphoreType.DMA((7,)), pltpu.SemaphoreType.DMA((7,))],
    )(v)


def _add2(a, b, name):
    n, h, cdim = a.shape
    th = _tile8(h)

    def body(a_ref, b_ref, o_ref):
        o_ref[...] = a_ref[...] + b_ref[...]

    spec = pl.BlockSpec((1, th, cdim), lambda j, i: (j, i, 0))
    return pl.pallas_call(body, name=name, grid=(n, h // th), in_specs=[spec, spec], out_specs=spec,
                          out_shape=jax.ShapeDtypeStruct(a.shape, a.dtype), compiler_params=_params())(a, b)


def _tile8(h):
    for t in (512, 384, 256, 128, 64, 32, 16, 8):
        if h % t == 0:
            return t
    return h


def _sum_slots(p, name):
    n, h, cdim = p.shape
    th = _tile8(h)

    def body(p_ref, o_ref):
        o_ref[...] = ((p_ref[0] + p_ref[1]) + p_ref[2]) + p_ref[3]

    return pl.pallas_call(body, name=name, grid=(h // th,),
                          in_specs=[pl.BlockSpec((n, th, cdim), lambda i: (0, i, 0))],
                          out_specs=pl.BlockSpec((th, cdim), lambda i: (i, 0)),
                          out_shape=jax.ShapeDtypeStruct((h, cdim), p.dtype), compiler_params=_params())(p)


def _adamw(w, g, m, v, name):
    def fn(w, g, m, v):
        m = ADAM_B1 * m + (1.0 - ADAM_B1) * g
        v = ADAM_B2 * v + (1.0 - ADAM_B2) * (g * g)
        m_hat = m / (1.0 - ADAM_B1 ** ADAM_STEP)
        v_hat = v / (1.0 - ADAM_B2 ** ADAM_STEP)
        delta = -ADAM_LR * (m_hat / (jnp.sqrt(v_hat) + ADAM_EPS) + ADAM_WD * w)
        return delta, m, v

    cols = w.shape[1]
    return _rowwise(fn, [w, g, m, v], [(cols, F32)] * 3, [], _tile8(w.shape[0]), name)


def _pack_big(shards, dtype):
    flat = jnp.concatenate([shards[n].reshape(-1).astype(dtype) for n, _, _, _ in BIG])
    flat = jnp.pad(flat, (0, PACK_ROWS * PACK_COLS - PACK_ELEMS))
    return flat.reshape(PACK_ROWS, PACK_COLS)


def _unpack_big(pack):
    flat = pack.reshape(-1)
    out, off = {}, 0
    for n, r, c, _ in BIG:
        out[n] = flat[off:off + r * c].reshape(1, r, c)
        off += r * c
    return out


def _full_from_gathered(gathered):
    flat = gathered.reshape(N_CHIPS, -1)
    out, off = {}, 0
    for n, r, c, axis in BIG:
        seg = flat[:, off:off + r * c].reshape(N_CHIPS, r, c)
        off += r * c
        if axis == 1:
            out[n] = jnp.transpose(seg, (1, 0, 2)).reshape(r, N_CHIPS * c)
        else:
            out[n] = seg.reshape(N_CHIPS * r, c)
    return out


def _split_for_chips(full):
    parts = []
    for n, r, c, axis in BIG:
        g = full[n]
        if axis == 1:
            g = jnp.transpose(g.reshape(r, N_CHIPS, c), (1, 0, 2))
        parts.append(g.reshape(N_CHIPS, r * c))
    flat = jnp.concatenate(parts, axis=1)
    flat = jnp.pad(flat, ((0, 0), (0, PACK_ROWS * PACK_COLS - PACK_ELEMS)))
    return flat.reshape(N_CHIPS, PACK_ROWS, PACK_COLS)


def _pack_small(vecs, extra=None):
    flat = jnp.concatenate([vecs[n].reshape(-1) for n, _ in SMALL] + ([extra.reshape(-1)] if extra is not None else []))
    flat = jnp.pad(flat, (0, SMALL_ROWS * PACK_COLS - flat.shape[0]))
    return flat.reshape(SMALL_ROWS, PACK_COLS)


def _unpack_small(pack):
    flat = pack.reshape(-1)
    out, off = {}, 0
    for n, k in SMALL:
        out[n] = flat[off:off + k].reshape(1, k)
        off += k
    return out


def _forward_backward(x, pos, target, sm, w):
    s = x.shape[0]
    w_in = w["w_in"]
    w_a = w_in[:, 0:1536]
    zeros = functools.partial(jnp.zeros, dtype=BF16)
    w_b = jnp.concatenate([w_in[:, 1536:2176], zeros((D_MODEL, 64)), w_in[:, 2176:2208], zeros((D_MODEL, 32))], axis=1)
    w_g = w_in[:, 2208:4256]
    w_uq = jnp.pad(w["w_uq"].reshape(MLA_Q_RANK, MLA_HEADS, MLA_QK), ((0, 0), (0, 0), (0, LANES - MLA_QK)))
    w_uq = w_uq.reshape(MLA_Q_RANK, MLA_HEADS * LANES)
    w_ukv = w["w_ukv"]
    w_pm = jnp.pad(w["w_proj_mla"].reshape(MLA_HEADS, HALF, D_MODEL), ((0, 0), (HALF, 0), (0, 0)))
    w_pm = w_pm.reshape(MLA_HEADS * LANES, D_MODEL)
    w_ps, w_out, w_gu, w_dn = w["w_proj_sb"], w["w_out"], w["w_gate_up"], w["w_down"]
    inv_freq = ROPE_THETA ** (-jnp.arange(0, MLA_ROPE, 2, dtype=F32) / MLA_ROPE)
    inv_lane = jnp.concatenate([jnp.zeros((HALF,), F32), inv_freq, inv_freq, jnp.zeros((32,), F32)]).reshape(1, LANES)
    g_pre, g_post, b_gate = sm["norm_mix_pre"], sm["norm_mix_post"], sm["b_gate"]
    g_q, g_kv, g_fpre, g_fpost = sm["q_norm"], sm["kv_norm"], sm["norm_ffn_pre"], sm["norm_ffn_post"]

    def f_h1(x, g):
        return (_rms(x)[1] * g,)

    (h1,) = _rowwise(f_h1, [x, g_pre], [(D_MODEL, BF16)], [], 256, "h1")
    qkv = _mm(h1, w_a, "nn", BF16, "proj_sb")
    lat = _mm(h1, w_b, "nn", F32, "proj_lat")
    gl = _mm(h1, w_g, "nn", F32, "proj_gate")

    def f_lat(lat, gq, gkv, pos, inv):
        cqn = _rms(lat[:, 0:MLA_Q_RANK])[1] * gq
        ckvn = _rms(lat[:, MLA_Q_RANK:MLA_Q_RANK + MLA_KV_RANK])[1] * gkv
        kr = lat[:, 640:768]
        ang = pos.astype(F32) * inv
        lane = _lane(ang.shape)
        cosv, sinv = jnp.cos(ang), jnp.sin(ang)
        rope_c = jnp.where(lane < 64, 1.0, jnp.where(lane < 96, cosv, 0.0))
        rope_s = jnp.where((lane >= 64) & (lane < 80), -sinv, jnp.where((lane >= 80) & (lane < 96), sinv, 0.0))
        krr = jnp.where(lane >= 64, kr * rope_c, 0.0) + _rope_swap(kr) * rope_s
        return cqn, ckvn, krr, rope_c, rope_s

    cqn, ckvn, krr, rope_c, rope_s = _rowwise(
        f_lat, [lat, g_q, g_kv, pos, inv_lane],
        [(MLA_Q_RANK, BF16), (MLA_KV_RANK, BF16), (LANES, F32), (LANES, F32), (LANES, F32)], [], 256, "lat_norm")
    q_raw = _mm(cqn, w_uq, "nn", F32, "q_up")
    kv = _mm(ckvn, w_ukv, "nn", F32, "kv_up")

    def f_heads(q_raw, kv, krr, rc, rs):
        lane = _lane(krr.shape)
        qs, ks, vs = [], [], []
        for h in range(MLA_HEADS):
            qh = q_raw[:, h * LANES:(h + 1) * LANES]
            kvh = kv[:, h * LANES:(h + 1) * LANES]
            qs.append(qh * rc + _rope_swap(qh) * rs)
            ks.append(jnp.where(lane < HALF, kvh, krr))
            vs.append(jnp.where(lane >= HALF, kvh, 0.0))
        return jnp.concatenate(qs, axis=1), jnp.concatenate(ks, axis=1), jnp.concatenate(vs, axis=1)

    wide = MLA_HEADS * LANES
    qc, kc, vp = _rowwise(f_heads, [q_raw, kv, krr, rope_c, rope_s], [(wide, BF16)] * 3, [], 256, "mla_heads")
    o_sb, sb_tot = _sb_fwd(qkv, "sb_fwd")
    o_mla, lse = _mla_fwd(qc, kc, vp, "mla_fwd")
    m_sb = _mm(o_sb, w_ps, "nn", F32, "proj_o_sb")
    m_mla = _mm(o_mla, w_pm, "nn", F32, "proj_o_mla")

    def f_merge(m_sb, m_mla, gl, b):
        g = _sigmoid(gl + b)
        return (g[:, :D_MODEL] * m_sb + g[:, D_MODEL:] * m_mla,)

    (merged,) = _rowwise(f_merge, [m_sb, m_mla, gl, b_gate], [(D_MODEL, BF16)], [], 256, "merge")
    y = _mm(merged, w_out, "nn", F32, "mix_out")

    def f_res1(x, y, g2, g3):
        x2 = x + _rms(y)[1] * g2
        return x2, _rms(x2)[1] * g3

    x2, h2 = _rowwise(f_res1, [x, y, g_post, g_fpre], [(D_MODEL, F32), (D_MODEL, BF16)], [], 256, "res1")

    gu = _mm(h2, w_gu, "nn", F32, "ffn_up")

    def f_act(gu):
        g, u = gu[:, :D_FF], gu[:, D_FF:]
        return (g * _sigmoid(g) * u,)

    (act,) = _rowwise(f_act, [gu], [(D_FF, BF16)], [], 128, "ffn_act")
    f = _mm(act, w_dn, "nn", F32, "ffn_down")

    def f_loss(x2, f, g4, tgt):
        r, fhat = _rms(f)
        err = x2 + fhat * g4 - tgt
        loss = 0.5 * jnp.sum(jnp.sum(err * err, axis=-1, keepdims=True), axis=0, keepdims=True) / D_MODEL
        dout = err / D_MODEL
        return dout, _rms_bwd(dout, r, fhat, g4), loss, _colsum(dout * fhat)

    dout, df, loss, d_fpost = _rowwise(f_loss, [x2, f, g_fpost, target], [(D_MODEL, F32), (D_MODEL, BF16)],
                                       [1, D_MODEL], 256, "loss")

    dact = _mm(df, w_dn, "nt", F32, "d_act")
    dw_dn = _mm(act, df, "tn", F32, "dw_down")

    def f_dact(gu, dact):
        g, u = gu[:, :D_FF], gu[:, D_FF:]
        sg = _sigmoid(g)
        dg = dact * u * (sg * (1.0 + g * (1.0 - sg)))
        return (jnp.concatenate([dg, dact * (g * sg)], axis=1),)

    (dgu,) = _rowwise(f_dact, [gu, dact], [(2 * D_FF, BF16)], [], 128, "d_gu")
    dh2 = _mm(dgu, w_gu, "nt", F32, "d_h2")
    dw_gu = _mm(h2, dgu, "tn", F32, "dw_gate_up")

    def f_dres1(dout, dh2, x2, y, g3, g2):
        r3, x2hat = _rms(x2)
        dx2 = dout + _rms_bwd(dh2, r3, x2hat, g3)
        r2, yhat = _rms(y)
        return dx2, _rms_bwd(dx2, r2, yhat, g2), _colsum(dh2 * x2hat), _colsum(dx2 * yhat)

    dx2, dy, d_fpre, d_post = _rowwise(f_dres1, [dout, dh2, x2, y, g_fpre, g_post],
                                       [(D_MODEL, F32), (D_MODEL, BF16)], [D_MODEL, D_MODEL], 256, "d_res1")

    dmerged = _mm(dy, w_out, "nt", F32, "d_merged")
    dw_out = _mm(merged, dy, "tn", F32, "dw_out")

    def f_dmerge(dm, m_sb, m_mla, gl, b):
        g = _sigmoid(gl + b)
        g0, g1 = g[:, :D_MODEL], g[:, D_MODEL:]
        dgl = jnp.concatenate([dm * m_sb * g0 * (1.0 - g0), dm * m_mla * g1 * (1.0 - g1)], axis=1)
        return dm * g0, dm * g1, dgl, _colsum(dgl)

    dm_sb, dm_mla, dgl, d_bgate = _rowwise(f_dmerge, [dmerged, m_sb, m_mla, gl, b_gate],
                                           [(D_MODEL, BF16), (D_MODEL, BF16), (2 * D_MODEL, BF16)], [2 * D_MODEL],
                                           256, "d_merge")
    do_sb = _mm(dm_sb, w_ps, "nt", BF16, "d_o_sb")
    dw_ps = _mm(o_sb, dm_sb, "tn", F32, "dw_proj_sb")
    do_mla = _mm(dm_mla, w_pm, "nt", F32, "d_o_mla")
    dw_pm = _mm(o_mla, dm_mla, "tn", F32, "dw_proj_mla")

    dq_sb, dk_sb, dv_sb = _sb_bwd(qkv, do_sb, sb_tot, "sb_bwd")
    dqc, dkc, dvp = _mla_bwd(qc, kc, vp, do_mla, o_mla, lse, "mla_bwd")

    def f_dheads(dqc, dkc, dvp, rc, rs):
        lane = _lane(rc.shape)
        dqs, dkvs = [], []
        dkr = jnp.zeros(rc.shape, F32)
        for h in range(MLA_HEADS):
            dq = dqc[:, h * LANES:(h + 1) * LANES]
            dk = dkc[:, h * LANES:(h + 1) * LANES]
            dqs.append(dq * rc + _rope_swap(dq * rs))
            dkvs.append(jnp.where(lane < HALF, dk, dvp[:, h * LANES:(h + 1) * LANES]))
            dkr = dkr + jnp.where(lane >= HALF, dk, 0.0)
        dkr = dkr * rc + _rope_swap(dkr * rs)
        return jnp.concatenate(dqs, axis=1), jnp.concatenate(dkvs, axis=1), dkr

    dq_raw, dkv, dkr = _rowwise(f_dheads, [dqc, dkc, dvp, rope_c, rope_s],
                                [(wide, BF16), (wide, BF16), (LANES, F32)], [], 256, "d_mla_heads")
    dcqn = _mm(dq_raw, w_uq, "nt", F32, "d_cqn")
    dw_uq = _mm(cqn, dq_raw, "tn", F32, "dw_uq")
    dckvn = _mm(dkv, w_ukv, "nt", F32, "d_ckvn")
    dw_ukv = _mm(ckvn, dkv, "tn", F32, "dw_ukv")

    def f_dlat(lat, dcqn, dckvn, dkr, gq, gkv):
        rq, cqhat = _rms(lat[:, 0:MLA_Q_RANK])
        rkv, ckvhat = _rms(lat[:, MLA_Q_RANK:MLA_Q_RANK + MLA_KV_RANK])
        dlat = jnp.concatenate([_rms_bwd(dcqn, rq, cqhat, gq), _rms_bwd(dckvn, rkv, ckvhat, gkv), dkr], axis=1)
        return dlat, _colsum(dcqn * cqhat), _colsum(dckvn * ckvhat)

    dlat, d_gq, d_gkv = _rowwise(f_dlat, [lat, dcqn, dckvn, dkr, g_q, g_kv], [(768, BF16)],
                                 [MLA_Q_RANK, MLA_KV_RANK], 256, "d_lat")

    dqkv = jnp.concatenate([dq_sb, dk_sb, dv_sb], axis=1).astype(BF16)
    dh1_a = _mm(dqkv, w_a, "nt", F32, "d_h1_sb")
    dh1_b = _mm(dlat, w_b, "nt", F32, "d_h1_lat")
    dh1_g = _mm(dgl, w_g, "nt", F32, "d_h1_gate")
    dw_a = _mm(h1, dqkv, "tn", F32, "dw_in_sb")
    dw_b = _mm(h1, dlat, "tn", F32, "dw_in_lat")
    dw_g = _mm(h1, dgl, "tn", F32, "dw_in_gate")

    def f_dx(x, dx2, da, db, dg, g1):
        r, xhat = _rms(x)
        dh1 = da + db + dg
        return dx2 + _rms_bwd(dh1, r, xhat, g1), _colsum(dh1 * xhat)

    grad_x, d_pre = _rowwise(f_dx, [x, dx2, dh1_a, dh1_b, dh1_g, g_pre], [(D_MODEL, F32)], [D_MODEL], 256, "d_x")

    small = {"norm_mix_pre": d_pre, "norm_mix_post": d_post, "b_gate": d_bgate, "q_norm": d_gq, "kv_norm": d_gkv,
             "norm_ffn_pre": d_fpre, "norm_ffn_post": d_fpost}
    big = {
        "w_in": jnp.concatenate([dw_a, dw_b[:, 0:640], dw_b[:, 704:736], dw_g], axis=1),
        "w_uq": dw_uq.reshape(MLA_Q_RANK, MLA_HEADS, LANES)[:, :, :MLA_QK].reshape(MLA_Q_RANK, MLA_HEADS * MLA_QK),
        "w_ukv": dw_ukv,
        "w_proj_sb": dw_ps,
        "w_proj_mla": dw_pm.reshape(MLA_HEADS, LANES, D_MODEL)[:, HALF:, :].reshape(MLA_HEADS * HALF, D_MODEL),
        "w_out": dw_out,
        "w_gate_up": dw_gu,
        "w_down": dw_dn,
    }
    return loss, grad_x, small, big


def kernel(x, positions, norm_mix_pre, norm_mix_post, w_in, b_gate, q_norm, w_uq, kv_norm, w_ukv, w_proj_sb, w_proj_mla, w_out, norm_ffn_pre, norm_ffn_post, w_gate_up, w_down, loss_target, m_norm_mix_pre, m_norm_mix_post, m_w_in, m_b_gate, m_q_norm, m_w_uq, m_kv_norm, m_w_ukv, m_w_proj_sb, m_w_proj_mla, m_w_out, m_norm_ffn_pre, m_norm_ffn_post, m_w_gate_up, m_w_down, v_norm_mix_pre, v_norm_mix_post, v_w_in, v_b_gate, v_q_norm, v_w_uq, v_kv_norm, v_w_ukv, v_w_proj_sb, v_w_proj_mla, v_w_out, v_norm_ffn_pre, v_norm_ffn_post, v_w_gate_up, v_w_down):
    given = dict(locals())
    s = x.shape[1]
    big_w = {n: given[n][0] for n, _, _, _ in BIG}
    small_w = {n: given[n] for n, _ in SMALL}

    gathered = _gather_chips(_pack_big(big_w, BF16), "gather_weights")
    loss, grad_x, d_small, d_big = _forward_backward(
        x.reshape(s, D_MODEL), positions.reshape(s, 1), loss_target.reshape(s, D_MODEL), small_w,
        _full_from_gathered(gathered))

    c = lax.axis_index("c")
    gpack = _split_for_chips(d_big)
    from_sibling = _swap_halves(gpack, "grad_swap_halves")
    mine = lax.dynamic_slice_in_dim(gpack, c * HALF_ROWS, HALF_ROWS, axis=1)
    chip_part = _add2(mine, from_sibling, "grad_add_cores")
    parts = _scatter_chips(chip_part, "grad_scatter_chips")
    reduced = _join_halves(_sum_slots(parts, "grad_add_chips"), "grad_join_halves")
    small_sum = _allsum_small(_pack_small(d_small, loss), "small_allsum")

    delta_b, m_b, v_b = _adamw(_pack_big(big_w, F32), reduced, _pack_big({n: given["m_" + n][0] for n in big_w}, F32),
                               _pack_big({n: given["v_" + n][0] for n in big_w}, F32), "adamw_big")
    delta_s, m_s, v_s = _adamw(_pack_small(small_w), small_sum, _pack_small({n: given["m_" + n] for n in small_w}),
                               _pack_small({n: given["v_" + n] for n in small_w}), "adamw_small")

    loss_out = small_sum.reshape(-1)[SMALL_ELEMS]
    grads = {**_unpack_big(reduced), **_unpack_small(small_sum)}
    deltas = {**_unpack_big(delta_b), **_unpack_small(delta_s)}
    new_m = {**_unpack_big(m_b), **_unpack_small(m_s)}
    new_v = {**_unpack_big(v_b), **_unpack_small(v_s)}
    return (loss_out, grad_x.reshape(1, s, D_MODEL), *[grads[n] for n in WEIGHT_ORDER],
            *[deltas[n] for n in WEIGHT_ORDER], *[new_m[n] for n in WEIGHT_ORDER], *[new_v[n] for n in WEIGHT_ORDER])
```

```python
import functools
import math

import jax
import jax.numpy as jnp
from jax import lax
from jax.experimental import pallas as pl
from jax.experimental.pallas import tpu as pltpu

F32 = jnp.float32
BF16 = jnp.bfloat16
MESH = pl.DeviceIdType.MESH

D_MODEL = 1024
SB_WIDTH = 512
MLA_HEADS = 8
MLA_Q_RANK = 384
MLA_KV_RANK = 256
MLA_ROPE = 32
MLA_QK = 96
D_FF = 2816
ROPE_THETA = 10000.0
EPS = 1e-6
LANES = 128
HALF = 64
ATT_TILE = 256
N_CHIPS = 4

ADAM_LR = 0.001
ADAM_B1 = 0.9
ADAM_B2 = 0.999
ADAM_EPS = 1e-08
ADAM_WD = 0.01
ADAM_STEP = 10

VMEM_LIMIT = 56 * 1024 * 1024

BIG = (
    ("w_in", 1024, 1064, 1),
    ("w_uq", 384, 192, 1),
    ("w_ukv", 256, 256, 1),
    ("w_proj_sb", 512, 256, 1),
    ("w_proj_mla", 512, 256, 1),
    ("w_out", 256, 1024, 0),
    ("w_gate_up", 1024, 1408, 1),
    ("w_down", 704, 1024, 0),
)
PACK_COLS = 1024
PACK_ELEMS = sum(r * c for _, r, c, _ in BIG)
PACK_ROWS = 3840
HALF_ROWS = PACK_ROWS // 2
SMALL = (
    ("norm_mix_pre", 1024),
    ("norm_mix_post", 1024),
    ("b_gate", 2048),
    ("q_norm", 384),
    ("kv_norm", 256),
    ("norm_ffn_pre", 1024),
    ("norm_ffn_post", 1024),
)
SMALL_ELEMS = sum(n for _, n in SMALL)
SMALL_ROWS = 8
WEIGHT_ORDER = ("norm_mix_pre", "norm_mix_post", "w_in", "b_gate", "q_norm", "w_uq", "kv_norm", "w_ukv",
                "w_proj_sb", "w_proj_mla", "w_out", "norm_ffn_pre", "norm_ffn_post", "w_gate_up", "w_down")


def _params():
    return pltpu.CompilerParams(vmem_limit_bytes=VMEM_LIMIT)


def _tile(dim, pref):
    t = (min(pref, dim) // LANES) * LANES
    while t >= LANES:
        if dim % t == 0:
            return t
        t -= LANES
    return dim


def _mm(a, b, mode, out_dtype, name, tm=1024, tn=1024, tk=1024):
    if mode == "nn":
        (m, k), n = a.shape, b.shape[1]
    elif mode == "nt":
        (m, k), n = a.shape, b.shape[0]
    else:
        (k, m), n = a.shape, b.shape[1]
    tm, tn, tk = _tile(m, tm), _tile(n, tn), _tile(k, tk)
    nk = k // tk
    if mode == "tn":
        a_spec = pl.BlockSpec((tk, tm), lambda i, j, l: (l, i))
        lhs_dim = 0
    else:
        a_spec = pl.BlockSpec((tm, tk), lambda i, j, l: (i, l))
        lhs_dim = 1
    if mode == "nt":
        b_spec = pl.BlockSpec((tn, tk), lambda i, j, l: (j, l))
        rhs_dim = 1
    else:
        b_spec = pl.BlockSpec((tk, tn), lambda i, j, l: (l, j))
        rhs_dim = 0
    dims = (((lhs_dim,), (rhs_dim,)), ((), ()))

    def body(a_ref, b_ref, o_ref, acc_ref):
        l = pl.program_id(2)

        @pl.when(l == 0)
        def _():
            acc_ref[...] = jnp.zeros_like(acc_ref)

        acc_ref[...] += lax.dot_general(a_ref[...].astype(BF16), b_ref[...].astype(BF16), dims,
                                        preferred_element_type=F32)

        @pl.when(l == nk - 1)
        def _():
            o_ref[...] = acc_ref[...].astype(o_ref.dtype)

    return pl.pallas_call(
        body,
        name=name,
        grid=(m // tm, n // tn, nk),
        in_specs=[a_spec, b_spec],
        out_specs=pl.BlockSpec((tm, tn), lambda i, j, l: (i, j)),
        out_shape=jax.ShapeDtypeStruct((m, n), out_dtype),
        scratch_shapes=[pltpu.VMEM((tm, tn), F32)],
        compiler_params=_params(),
    )(a, b)


def _rowwise(fn, ins, outs, reds, tm, name):
    rows = ins[0].shape[0]
    tm = min(tm, rows)
    n_in, n_out = len(ins), len(outs)

    def spec(shape):
        if shape[0] == rows:
            return pl.BlockSpec((tm, shape[1]), lambda i: (i, 0))
        return pl.BlockSpec(shape, lambda i: (0, 0))

    def body(*refs):
        i = pl.program_id(0)
        vals = fn(*[r[...] for r in refs[:n_in]])
        for r, v in zip(refs[n_in:n_in + n_out], vals[:n_out]):
            r[...] = v.astype(r.dtype)
        for r, v in zip(refs[n_in + n_out:], vals[n_out:]):
            @pl.when(i == 0)
            def _():
                r[...] = jnp.zeros_like(r)

            r[...] += v

    out_shape = [jax.ShapeDtypeStruct((rows, d), dt) for d, dt in outs]
    out_shape += [jax.ShapeDtypeStruct((1, d), F32) for d in reds]
    out_specs = [spec(s.shape) for s in out_shape]
    return pl.pallas_call(
        body,
        name=name,
        grid=(rows // tm,),
        in_specs=[spec(a.shape) for a in ins],
        out_specs=out_specs,
        out_shape=out_shape,
        compiler_params=_params(),
    )(*ins)


def _colsum(v):
    return jnp.sum(v, axis=0, keepdims=True)


def _rms(v):
    r = lax.rsqrt(jnp.mean(v * v, axis=-1, keepdims=True) + EPS)
    return r, v * r


def _rms_bwd(dy, r, vhat, g):
    u = dy * g
    return r * (u - vhat * jnp.mean(u * vhat, axis=-1, keepdims=True))


def _sigmoid(t):
    return 1.0 / (1.0 + jnp.exp(-t))


def _lane(shape):
    return lax.broadcasted_iota(jnp.int32, shape, len(shape) - 1)


def _rope_swap(v):
    lane = _lane(v.shape)
    up = pltpu.roll(v, LANES - 16, 1)
    down = pltpu.roll(v, 16, 1)
    return jnp.where((lane >= 64) & (lane < 80), up, jnp.where((lane >= 80) & (lane < 96), down, 0.0))


def _softplus(z):
    return jnp.maximum(z, 0.0) + jnp.log(1.0 + jnp.exp(-jnp.abs(z)))


def _split_dot(v, m2):
    hi = v.astype(BF16)
    lo = (v - hi.astype(F32)).astype(BF16)
    return jnp.dot(jnp.concatenate([hi, lo], axis=1), m2, preferred_element_type=F32)


def _dot_nt(a, b):
    return lax.dot_general(a, b, (((1,), (1,)), ((), ())), preferred_element_type=F32)


def _dot_tn(a, b):
    return lax.dot_general(a, b, (((0,), (0,)), ((), ())), preferred_element_type=F32)


def _sb_fwd(qkv, name):
    s = qkv.shape[0]
    t = min(ATT_TILE, s // 2)
    npair = SB_WIDTH // LANES

    def body(q_ref, k_ref, v_ref, o_ref, t_ref):
        i = pl.program_id(1)
        lane = _lane((1, LANES))
        row = lax.broadcasted_iota(jnp.int32, (t, t), 0)
        col = lax.broadcasted_iota(jnp.int32, (t, t), 1)
        tri = col < row
        m_from = jnp.where(row >= col, 1.0, 0.0).astype(BF16)
        m_from = jnp.concatenate([m_from, m_from], axis=0)
        masks = (lane < HALF, lane >= HALF)
        qhs = [jnp.where(hm, q_ref[n * t:(n + 1) * t, :], jnp.zeros((t, LANES), BF16)) * 0.125
               for n in range(2) for hm in masks]

        def step(kb, carry, diags):
            rows = pl.ds(pl.multiple_of(kb * t, t), t)
            ks = k_ref[rows, :]
            vs = v_ref[rows, :]
            vhs = [jnp.where(hm, vs, jnp.zeros_like(vs)) for hm in masks]
            act = [ch for ch in range(4) if diags[ch // 2] is not None]
            zs = {ch: _dot_nt(qhs[ch], ks) for ch in act}
            lbs = {}
            for ch in act:
                lb = -_softplus(zs[ch])
                lbs[ch] = jnp.where(tri, lb, 0.0) if diags[ch // 2] else lb
            sums = {ch: _split_dot(lbs[ch], m_from) for ch in act}
            out = list(carry)
            for ch in act:
                c, acc = carry[ch]
                a = jnp.exp(zs[ch] + sums[ch] + c)
                if diags[ch // 2]:
                    a = jnp.where(tri, a, 0.0)
                acc = acc + jnp.dot(a.astype(BF16), vhs[ch % 2], preferred_element_type=F32)
                out[ch] = (c + sums[ch][:, 0:1], acc)
            return tuple(out)

        init = (jnp.zeros((t, 1), F32), jnp.zeros((t, LANES), F32))
        carry = step(2 * i + 1, (init,) * 4, (None, True))
        carry = step(2 * i, carry, (True, False))
        carry = lax.fori_loop(0, 2 * i, lambda n, cr: step(2 * i - 1 - n, cr, (False, False)), carry)
        for n in range(2):
            (c0, acc0), (c1, acc1) = carry[2 * n], carry[2 * n + 1]
            o_ref[n * t:(n + 1) * t, :] = acc0 + acc1
            t_ref[n * t:(n + 1) * t, :] = jnp.where(masks[0], c0, c1)

    blk = pl.BlockSpec((2 * t, LANES), lambda p, i: (i, p))
    return pl.pallas_call(
        body,
        name=name,
        grid=(npair, s // (2 * t)),
        in_specs=[
            blk,
            pl.BlockSpec((s, LANES), lambda p, i: (0, npair + p)),
            pl.BlockSpec((s, LANES), lambda p, i: (0, 2 * npair + p)),
        ],
        out_specs=[blk, blk],
        out_shape=[jax.ShapeDtypeStruct((s, SB_WIDTH), F32), jax.ShapeDtypeStruct((s, SB_WIDTH), F32)],
        compiler_params=_params(),
    )(qkv, qkv, qkv)


def _sb_bwd(qkv, do, tot, name):
    s = qkv.shape[0]
    t = min(ATT_TILE, s // 2)
    npair = SB_WIDTH // LANES

    def body(q_ref, k_ref, v_ref, do_ref, t_ref, dq_ref, dk_ref, dv_ref):
        i = pl.program_id(1)

        @pl.when(i == 0)
        def _():
            dk_ref[...] = jnp.zeros_like(dk_ref)
            dv_ref[...] = jnp.zeros_like(dv_ref)

        lane = _lane((1, LANES))
        row = lax.broadcasted_iota(jnp.int32, (t, t), 0)
        col = lax.broadcasted_iota(jnp.int32, (t, t), 1)
        tri = col < row
        m_from = jnp.where(row >= col, 1.0, 0.0).astype(BF16)
        m_from = jnp.concatenate([m_from, m_from], axis=0)
        m_upto = jnp.where(row <= col, 1.0, 0.0).astype(BF16)
        m_upto = jnp.concatenate([m_upto, m_upto], axis=0)
        masks = (lane < HALF, lane >= HALF)
        zero_b = jnp.zeros((t, LANES), BF16)
        qhs, dohs, tots = [], [], []
        for n in range(2):
            rs = slice(n * t, (n + 1) * t)
            do_b = do_ref[rs, :].astype(BF16)
            for h, hm in enumerate(masks):
                qhs.append(jnp.where(hm, q_ref[rs, :], zero_b))
                dohs.append(jnp.where(hm, do_b, zero_b))
                tots.append(t_ref[rs, h * HALF:h * HALF + 1])

        def step(kb, carry, diags):
            rows = pl.ds(pl.multiple_of(kb * t, t), t)
            ks = k_ref[rows, :]
            vs = v_ref[rows, :]
            khs = [jnp.where(hm, ks, jnp.zeros_like(ks)) for hm in masks]
            act = [ch for ch in range(4) if diags[ch // 2] is not None]
            zs = {ch: _dot_nt(qhs[ch] * 0.125, ks) for ch in act}
            das = {ch: _dot_nt(dohs[ch], vs) for ch in act}
            lbs, betas = {}, {}
            for ch in act:
                sp = _softplus(zs[ch])
                betas[ch] = jnp.exp(zs[ch] - sp)
                lbs[ch] = jnp.where(tri, -sp, 0.0) if diags[ch // 2] else -sp
            sums = {ch: _split_dot(lbs[ch], m_from) for ch in act}
            avs, es, befores = {}, {}, {}
            for ch in act:
                before = carry[ch][0]
                total = sums[ch][:, 0:1]
                a = jnp.exp(zs[ch] + sums[ch] + (tots[ch] - before - total))
                if diags[ch // 2]:
                    a = jnp.where(tri, a, 0.0)
                avs[ch] = a.astype(BF16)
                es[ch] = a * das[ch]
                befores[ch] = before + total
            e_sums = {ch: _split_dot(es[ch], m_upto) for ch in act}
            dzs, e_befores = {}, {}
            for ch in act:
                e_upto = e_sums[ch] + carry[ch][1]
                dz = es[ch] - betas[ch] * e_upto
                if diags[ch // 2]:
                    dz = jnp.where(tri, dz, 0.0)
                dzs[ch] = (dz * 0.125).astype(BF16)
                e_befores[ch] = e_upto[:, t - 1:t]
            out = list(carry)
            dk = jnp.zeros((t, LANES), F32)
            dv = jnp.zeros((t, LANES), F32)
            for ch in act:
                dq = carry[ch][2] + jnp.dot(dzs[ch], khs[ch % 2], preferred_element_type=F32)
                dk = dk + _dot_tn(dzs[ch], qhs[ch])
                dv = dv + _dot_tn(avs[ch], dohs[ch])
                out[ch] = (befores[ch], e_befores[ch], dq)
            dk_ref[rows, :] += dk
            dv_ref[rows, :] += dv
            return tuple(out)

        zero = jnp.zeros((t, 1), F32)
        init = (zero, zero, jnp.zeros((t, LANES), F32))
        carry = lax.fori_loop(0, 2 * i, lambda kb, cr: step(kb, cr, (False, False)), (init,) * 4)
        carry = step(2 * i, carry, (True, False))
        carry = step(2 * i + 1, carry, (None, True))
        for n in range(2):
            dq_ref[n * t:(n + 1) * t, :] = carry[2 * n][2] + carry[2 * n + 1][2]

    blk = pl.BlockSpec((2 * t, LANES), lambda p, i: (i, p))
    full = pl.BlockSpec((s, LANES), lambda p, i: (0, p))
    out = jax.ShapeDtypeStruct((s, SB_WIDTH), F32)
    return pl.pallas_call(
        body,
        name=name,
        grid=(npair, s // (2 * t)),
        in_specs=[
            blk,
            pl.BlockSpec((s, LANES), lambda p, i: (0, npair + p)),
            pl.BlockSpec((s, LANES), lambda p, i: (0, 2 * npair + p)),
            blk,
            blk,
        ],
        out_specs=[blk, full, full],
        out_shape=[out, out, out],
        compiler_params=_params(),
    )(qkv, qkv, qkv, do, tot)


MLA_SCALE = 1.0 / math.sqrt(MLA_QK)
NEG = -1e30
MLA_QBLOCKS = 4


def _diag_plan(nq):
    return [tuple(None if n < j else n == j for n in range(nq)) for j in range(nq)]


def _mla_fwd(qc, kc, vp, name):
    s = qc.shape[0]
    nq = MLA_QBLOCKS
    t = min(ATT_TILE, s // nq)

    def body(q_ref, k_ref, v_ref, o_ref, lse_ref):
        i = pl.program_id(1)
        row = lax.broadcasted_iota(jnp.int32, (t, t), 0)
        col = lax.broadcasted_iota(jnp.int32, (t, t), 1)
        keep = col <= row
        qs = [q_ref[n * t:(n + 1) * t, :] for n in range(nq)]

        def step(kb, carry, diags):
            rows = pl.ds(pl.multiple_of(kb * t, t), t)
            ks = k_ref[rows, :]
            vs = v_ref[rows, :]
            act = [n for n in range(nq) if diags[n] is not None]
            scs = {n: _dot_nt(qs[n], ks) * MLA_SCALE for n in act}
            mid = {}
            for n in act:
                m, l, acc = carry[n]
                sc = jnp.where(keep, scs[n], NEG) if diags[n] else scs[n]
                m_new = jnp.maximum(m, jnp.max(sc, axis=-1, keepdims=True))
                alpha = jnp.exp(m - m_new)
                p = jnp.exp(sc - m_new)
                mid[n] = (m_new, alpha * l + jnp.sum(p, axis=-1, keepdims=True), alpha * acc, p.astype(BF16))
            out = list(carry)
            for n in act:
                m, l, acc, p = mid[n]
                out[n] = (m, l, acc + jnp.dot(p, vs, preferred_element_type=F32))
            return tuple(out)

        init = (jnp.full((t, 1), NEG, F32), jnp.zeros((t, 1), F32), jnp.zeros((t, LANES), F32))
        carry = lax.fori_loop(0, nq * i, lambda kb, cr: step(kb, cr, (False,) * nq), (init,) * nq)
        for j, diags in enumerate(_diag_plan(nq)):
            carry = step(nq * i + j, carry, diags)
        for n, (m, l, acc) in enumerate(carry):
            o_ref[n * t:(n + 1) * t, :] = acc / l
            lse_ref[n * t:(n + 1) * t, :] = jnp.broadcast_to(m + jnp.log(l), (t, LANES))

    blk = pl.BlockSpec((nq * t, LANES), lambda h, i: (i, h))
    full = pl.BlockSpec((s, LANES), lambda h, i: (0, h))
    out = jax.ShapeDtypeStruct((s, MLA_HEADS * LANES), F32)
    return pl.pallas_call(
        body,
        name=name,
        grid=(MLA_HEADS, s // (nq * t)),
        in_specs=[blk, full, full],
        out_specs=[blk, blk],
        out_shape=[out, out],
        compiler_params=_params(),
    )(qc, kc, vp)


def _mla_bwd(qc, kc, vp, do, o, lse, name):
    s = qc.shape[0]
    nq = MLA_QBLOCKS
    t = min(ATT_TILE, s // nq)

    def body(q_ref, k_ref, v_ref, do_ref, o_ref, lse_ref, dq_ref, dk_ref, dv_ref):
        i = pl.program_id(1)

        @pl.when(i == 0)
        def _():
            dk_ref[...] = jnp.zeros_like(dk_ref)
            dv_ref[...] = jnp.zeros_like(dv_ref)

        row = lax.broadcasted_iota(jnp.int32, (t, t), 0)
        col = lax.broadcasted_iota(jnp.int32, (t, t), 1)
        keep = col <= row

        def block(n):
            rs = slice(n * t, (n + 1) * t)
            do_f = do_ref[rs, :]
            delta = jnp.sum(do_f * o_ref[rs, :], axis=-1, keepdims=True)
            return q_ref[rs, :], do_f.astype(BF16), delta, lse_ref[rs, 0:1]

        blocks = [block(n) for n in range(nq)]

        def step(kb, dqs, diags):
            rows = pl.ds(pl.multiple_of(kb * t, t), t)
            ks = k_ref[rows, :]
            vs = v_ref[rows, :]
            act = [n for n in range(nq) if diags[n] is not None]
            scs = {n: _dot_nt(blocks[n][0], ks) * MLA_SCALE for n in act}
            dps = {n: _dot_nt(blocks[n][1], vs) for n in act}
            ps, dss = {}, {}
            for n in act:
                _, _, delta, lse_v = blocks[n]
                p = jnp.exp(scs[n] - lse_v)
                if diags[n]:
                    p = jnp.where(keep, p, 0.0)
                dss[n] = (p * (dps[n] - delta) * MLA_SCALE).astype(BF16)
                ps[n] = p.astype(BF16)
            dk = jnp.zeros((t, LANES), F32)
            dv = jnp.zeros((t, LANES), F32)
            out = list(dqs)
            for n in act:
                q, do_b, _, _ = blocks[n]
                out[n] = dqs[n] + jnp.dot(dss[n], ks, preferred_element_type=F32)
                dk = dk + _dot_tn(dss[n], q)
                dv = dv + _dot_tn(ps[n], do_b)
            dk_ref[rows, :] += dk
            dv_ref[rows, :] += dv
            return tuple(out)

        zero = jnp.zeros((t, LANES), F32)
        dqs = lax.fori_loop(0, nq * i, lambda kb, cr: step(kb, cr, (False,) * nq), (zero,) * nq)
        for j, diags in enumerate(_diag_plan(nq)):
            dqs = step(nq * i + j, dqs, diags)
        for n in range(nq):
            dq_ref[n * t:(n + 1) * t, :] = dqs[n]

    blk = pl.BlockSpec((nq * t, LANES), lambda h, i: (i, h))
    full = pl.BlockSpec((s, LANES), lambda h, i: (0, h))
    out = jax.ShapeDtypeStruct((s, MLA_HEADS * LANES), F32)
    return pl.pallas_call(
        body,
        name=name,
        grid=(MLA_HEADS, s // (nq * t)),
        in_specs=[blk, full, full, blk, blk, blk],
        out_specs=[blk, full, full],
        out_shape=[out, out, out],
        compiler_params=_params(),
    )(qc, kc, vp, do, o, lse)


def _position():
    return lax.axis_index("x"), lax.axis_index("y"), lax.axis_index("c")


def _other_chips(x, y):
    return [(1 - x, y), (x, 1 - y), (1 - x, 1 - y)]


HBM_SPEC = pl.BlockSpec(memory_space=pltpu.HBM)


def _gather_chips(block, name):
    def body(src, out, send_sems, recv_sems, local_sem):
        x, y, c = _position()
        mine = pltpu.make_async_copy(src, out.at[2 * x + y], local_sem)
        mine.start()
        sends = []
        for k, (px, py) in enumerate(_other_chips(x, y)):
            cp = pltpu.make_async_remote_copy(src_ref=src, dst_ref=out.at[2 * x + y], send_sem=send_sems.at[k],
                                              recv_sem=recv_sems.at[k], device_id=(px, py, c), device_id_type=MESH)
            cp.start()
            sends.append(cp)
        for k, (px, py) in enumerate(_other_chips(x, y)):
            pltpu.make_async_remote_copy(src_ref=src, dst_ref=out.at[2 * px + py], send_sem=send_sems.at[k],
                                         recv_sem=recv_sems.at[k], device_id=(px, py, c),
                                         device_id_type=MESH).wait_recv()
        for cp in sends:
            cp.wait_send()
        mine.wait()

    return pl.pallas_call(
        body,
        name=name,
        in_specs=[HBM_SPEC],
        out_specs=HBM_SPEC,
        out_shape=jax.ShapeDtypeStruct((N_CHIPS,) + block.shape, block.dtype),
        scratch_shapes=[pltpu.SemaphoreType.DMA((3,)), pltpu.SemaphoreType.DMA((3,)), pltpu.SemaphoreType.DMA(())],
    )(block)


def _swap_halves(gpack, name):
    n, r, cdim = gpack.shape
    h = r // 2

    def body(src, out, send_sem, recv_sem):
        x, y, c = _position()
        cp = pltpu.make_async_remote_copy(src_ref=src.at[:, pl.ds((1 - c) * h, h), :], dst_ref=out,
                                          send_sem=send_sem, recv_sem=recv_sem, device_id=(x, y, 1 - c),
                                          device_id_type=MESH)
        cp.start()
        cp.wait()

    return pl.pallas_call(
        body,
        name=name,
        in_specs=[HBM_SPEC],
        out_specs=HBM_SPEC,
        out_shape=jax.ShapeDtypeStruct((n, h, cdim), gpack.dtype),
        scratch_shapes=[pltpu.SemaphoreType.DMA(()), pltpu.SemaphoreType.DMA(())],
    )(gpack)


def _scatter_chips(part, name):
    def body(src, out, send_sems, recv_sems, local_sem):
        x, y, c = _position()
        me = 2 * x + y
        mine = pltpu.make_async_copy(src.at[me], out.at[me], local_sem)
        mine.start()
        sends = []
        for k, (px, py) in enumerate(_other_chips(x, y)):
            cp = pltpu.make_async_remote_copy(src_ref=src.at[2 * px + py], dst_ref=out.at[me],
                                              send_sem=send_sems.at[k], recv_sem=recv_sems.at[k],
                                              device_id=(px, py, c), device_id_type=MESH)
            cp.start()
            sends.append(cp)
        for k, (px, py) in enumerate(_other_chips(x, y)):
            pltpu.make_async_remote_copy(src_ref=src.at[me], dst_ref=out.at[2 * px + py], send_sem=send_sems.at[k],
                                         recv_sem=recv_sems.at[k], device_id=(px, py, c),
                                         device_id_type=MESH).wait_recv()
        for cp in sends:
            cp.wait_send()
        mine.wait()

    return pl.pallas_call(
        body,
        name=name,
        in_specs=[HBM_SPEC],
        out_specs=HBM_SPEC,
        out_shape=jax.ShapeDtypeStruct(part.shape, part.dtype),
        scratch_shapes=[pltpu.SemaphoreType.DMA((3,)), pltpu.SemaphoreType.DMA((3,)), pltpu.SemaphoreType.DMA(())],
    )(part)


def _join_halves(half, name):
    h, cdim = half.shape

    def body(src, out, send_sem, recv_sem, local_sem):
        x, y, c = _position()
        mine = pltpu.make_async_copy(src, out.at[pl.ds(c * h, h), :], local_sem)
        mine.start()
        cp = pltpu.make_async_remote_copy(src_ref=src, dst_ref=out.at[pl.ds(c * h, h), :], send_sem=send_sem,
                                          recv_sem=recv_sem, device_id=(x, y, 1 - c), device_id_type=MESH)
        cp.start()
        cp.wait()
        mine.wait()

    return pl.pallas_call(
        body,
        name=name,
        in_specs=[HBM_SPEC],
        out_specs=HBM_SPEC,
        out_shape=jax.ShapeDtypeStruct((2 * h, cdim), half.dtype),
        scratch_shapes=[pltpu.SemaphoreType.DMA(()), pltpu.SemaphoreType.DMA(()), pltpu.SemaphoreType.DMA(())],
    )(half)


def _allsum_small(v, name):
    shape = v.shape

    def body(v_ref, o_ref, buf, send_sems, recv_sems):
        x, y, c = _position()
        me = 4 * x + 2 * y + c
        buf[me] = v_ref[...]
        peers = []
        for k in range(1, 8):
            fx, fy, fc = (k >> 2) & 1, (k >> 1) & 1, k & 1
            peers.append((jnp.where(fx == 1, 1 - x, x), jnp.where(fy == 1, 1 - y, y), jnp.where(fc == 1, 1 - c, c)))
        sends = []
        for k, peer in enumerate(peers):
            cp = pltpu.make_async_remote_copy(src_ref=v_ref, dst_ref=buf.at[me], send_sem=send_sems.at[k],
                                              recv_sem=recv_sems.at[k], device_id=peer, device_id_type=MESH)
            cp.start()
            sends.append(cp)
        for k, (px, py, pc) in enumerate(peers):
            pltpu.make_async_remote_copy(src_ref=v_ref, dst_ref=buf.at[4 * px + 2 * py + pc],
                                         send_sem=send_sems.at[k], recv_sem=recv_sems.at[k],
                                         device_id=(px, py, pc), device_id_type=MESH).wait_recv()
        for cp in sends:
            cp.wait_send()
        acc = buf[0]
        for d in range(1, 8):
            acc = acc + buf[d]
        o_ref[...] = acc

    return pl.pallas_call(
        body,
        name=name,
        in_specs=[pl.BlockSpec(memory_space=pltpu.VMEM)],
        out_specs=pl.BlockSpec(memory_space=pltpu.VMEM),
        out_shape=jax.ShapeDtypeStruct(shape, F32),
        scratch_shapes=[pltpu.VMEM((8,) + shape, F32), pltpu.SemaphoreType.DMA((7,)), pltpu.SemaphoreType.DMA((7,))],
    )(v)


def _add2(a, b, name):
    n, h, cdim = a.shape
    th = _tile8(h)

    def body(a_ref, b_ref, o_ref):
        o_ref[...] = a_ref[...] + b_ref[...]

    spec = pl.BlockSpec((1, th, cdim), lambda j, i: (j, i, 0))
    return pl.pallas_call(body, name=name, grid=(n, h // th), in_specs=[spec, spec], out_specs=spec,
                          out_shape=jax.ShapeDtypeStruct(a.shape, a.dtype), compiler_params=_params())(a, b)


def _tile8(h):
    for t in (512, 384, 256, 128, 64, 32, 16, 8):
        if h % t == 0:
            return t
    return h


def _sum_slots(p, name):
    n, h, cdim = p.shape
    th = _tile8(h)

    def body(p_ref, o_ref):
        o_ref[...] = ((p_ref[0] + p_ref[1]) + p_ref[2]) + p_ref[3]

    return pl.pallas_call(body, name=name, grid=(h // th,),
                          in_specs=[pl.BlockSpec((n, th, cdim), lambda i: (0, i, 0))],
                          out_specs=pl.BlockSpec((th, cdim), lambda i: (i, 0)),
                          out_shape=jax.ShapeDtypeStruct((h, cdim), p.dtype), compiler_params=_params())(p)


def _adamw(w, g, m, v, name):
    def fn(w, g, m, v):
        m = ADAM_B1 * m + (1.0 - ADAM_B1) * g
        v = ADAM_B2 * v + (1.0 - ADAM_B2) * (g * g)
        m_hat = m / (1.0 - ADAM_B1 ** ADAM_STEP)
        v_hat = v / (1.0 - ADAM_B2 ** ADAM_STEP)
        delta = -ADAM_LR * (m_hat / (jnp.sqrt(v_hat) + ADAM_EPS) + ADAM_WD * w)
        return delta, m, v

    cols = w.shape[1]
    return _rowwise(fn, [w, g, m, v], [(cols, F32)] * 3, [], _tile8(w.shape[0]), name)


def _pack_big(shards, dtype):
    flat = jnp.concatenate([shards[n].reshape(-1).astype(dtype) for n, _, _, _ in BIG])
    flat = jnp.pad(flat, (0, PACK_ROWS * PACK_COLS - PACK_ELEMS))
    return flat.reshape(PACK_ROWS, PACK_COLS)


def _unpack_big(pack):
    flat = pack.reshape(-1)
    out, off = {}, 0
    for n, r, c, _ in BIG:
        out[n] = flat[off:off + r * c].reshape(1, r, c)
        off += r * c
    return out


def _full_from_gathered(gathered):
    flat = gathered.reshape(N_CHIPS, -1)
    out, off = {}, 0
    for n, r, c, axis in BIG:
        seg = flat[:, off:off + r * c].reshape(N_CHIPS, r, c)
        off += r * c
        if axis == 1:
            out[n] = jnp.transpose(seg, (1, 0, 2)).reshape(r, N_CHIPS * c)
        else:
            out[n] = seg.reshape(N_CHIPS * r, c)
    return out


def _split_for_chips(full):
    parts = []
    for n, r, c, axis in BIG:
        g = full[n]
        if axis == 1:
            g = jnp.transpose(g.reshape(r, N_CHIPS, c), (1, 0, 2))
        parts.append(g.reshape(N_CHIPS, r * c))
    flat = jnp.concatenate(parts, axis=1)
    flat = jnp.pad(flat, ((0, 0), (0, PACK_ROWS * PACK_COLS - PACK_ELEMS)))
    return flat.reshape(N_CHIPS, PACK_ROWS, PACK_COLS)


def _pack_small(vecs, extra=None):
    flat = jnp.concatenate([vecs[n].reshape(-1) for n, _ in SMALL] + ([extra.reshape(-1)] if extra is not None else []))
    flat = jnp.pad(flat, (0, SMALL_ROWS * PACK_COLS - flat.shape[0]))
    return flat.reshape(SMALL_ROWS, PACK_COLS)


def _unpack_small(pack):
    flat = pack.reshape(-1)
    out, off = {}, 0
    for n, k in SMALL:
        out[n] = flat[off:off + k].reshape(1, k)
        off += k
    return out


def _forward_backward(x, pos, target, sm, w):
    s = x.shape[0]
    w_in = w["w_in"]
    w_a = w_in[:, 0:1536]
    zeros = functools.partial(jnp.zeros, dtype=BF16)
    w_b = jnp.concatenate([w_in[:, 1536:2176], zeros((D_MODEL, 64)), w_in[:, 2176:2208], zeros((D_MODEL, 32))], axis=1)
    w_g = w_in[:, 2208:4256]
    w_uq = jnp.pad(w["w_uq"].reshape(MLA_Q_RANK, MLA_HEADS, MLA_QK), ((0, 0), (0, 0), (0, LANES - MLA_QK)))
    w_uq = w_uq.reshape(MLA_Q_RANK, MLA_HEADS * LANES)
    w_ukv = w["w_ukv"]
    w_pm = jnp.pad(w["w_proj_mla"].reshape(MLA_HEADS, HALF, D_MODEL), ((0, 0), (HALF, 0), (0, 0)))
    w_pm = w_pm.reshape(MLA_HEADS * LANES, D_MODEL)
    w_ps, w_out, w_gu, w_dn = w["w_proj_sb"], w["w_out"], w["w_gate_up"], w["w_down"]
    inv_freq = ROPE_THETA ** (-jnp.arange(0, MLA_ROPE, 2, dtype=F32) / MLA_ROPE)
    inv_lane = jnp.concatenate([jnp.zeros((HALF,), F32), inv_freq, inv_freq, jnp.zeros((32,), F32)]).reshape(1, LANES)
    g_pre, g_post, b_gate = sm["norm_mix_pre"], sm["norm_mix_post"], sm["b_gate"]
    g_q, g_kv, g_fpre, g_fpost = sm["q_norm"], sm["kv_norm"], sm["norm_ffn_pre"], sm["norm_ffn_post"]

    def f_h1(x, g):
        return (_rms(x)[1] * g,)

    (h1,) = _rowwise(f_h1, [x, g_pre], [(D_MODEL, BF16)], [], 256, "h1")
    qkv = _mm(h1, w_a, "nn", BF16, "proj_sb")
    lat = _mm(h1, w_b, "nn", F32, "proj_lat")
    gl = _mm(h1, w_g, "nn", F32, "proj_gate")

    def f_lat(lat, gq, gkv, pos, inv):
        cqn = _rms(lat[:, 0:MLA_Q_RANK])[1] * gq
        ckvn = _rms(lat[:, MLA_Q_RANK:MLA_Q_RANK + MLA_KV_RANK])[1] * gkv
        kr = lat[:, 640:768]
        ang = pos.astype(F32) * inv
        lane = _lane(ang.shape)
        cosv, sinv = jnp.cos(ang), jnp.sin(ang)
        rope_c = jnp.where(lane < 64, 1.0, jnp.where(lane < 96, cosv, 0.0))
        rope_s = jnp.where((lane >= 64) & (lane < 80), -sinv, jnp.where((lane >= 80) & (lane < 96), sinv, 0.0))
        krr = jnp.where(lane >= 64, kr * rope_c, 0.0) + _rope_swap(kr) * rope_s
        return cqn, ckvn, krr, rope_c, rope_s

    cqn, ckvn, krr, rope_c, rope_s = _rowwise(
        f_lat, [lat, g_q, g_kv, pos, inv_lane],
        [(MLA_Q_RANK, BF16), (MLA_KV_RANK, BF16), (LANES, F32), (LANES, F32), (LANES, F32)], [], 256, "lat_norm")
    q_raw = _mm(cqn, w_uq, "nn", F32, "q_up")
    kv = _mm(ckvn, w_ukv, "nn", F32, "kv_up")

    def f_heads(q_raw, kv, krr, rc, rs):
        lane = _lane(krr.shape)
        qs, ks, vs = [], [], []
        for h in range(MLA_HEADS):
            qh = q_raw[:, h * LANES:(h + 1) * LANES]
            kvh = kv[:, h * LANES:(h + 1) * LANES]
            qs.append(qh * rc + _rope_swap(qh) * rs)
            ks.append(jnp.where(lane < HALF, kvh, krr))
            vs.append(jnp.where(lane >= HALF, kvh, 0.0))
        return jnp.concatenate(qs, axis=1), jnp.concatenate(ks, axis=1), jnp.concatenate(vs, axis=1)

    wide = MLA_HEADS * LANES
    qc, kc, vp = _rowwise(f_heads, [q_raw, kv, krr, rope_c, rope_s], [(wide, BF16)] * 3, [], 256, "mla_heads")
    o_sb, sb_tot = _sb_fwd(qkv, "sb_fwd")
    o_mla, lse = _mla_fwd(qc, kc, vp, "mla_fwd")
    m_sb = _mm(o_sb, w_ps, "nn", F32, "proj_o_sb")
    m_mla = _mm(o_mla, w_pm, "nn", F32, "proj_o_mla")

    def f_merge(m_sb, m_mla, gl, b):
        g = _sigmoid(gl + b)
        return (g[:, :D_MODEL] * m_sb + g[:, D_MODEL:] * m_mla,)

    (merged,) = _rowwise(f_merge, [m_sb, m_mla, gl, b_gate], [(D_MODEL, BF16)], [], 256, "merge")
    y = _mm(merged, w_out, "nn", F32, "mix_out")

    def f_res1(x, y, g2, g3):
        x2 = x + _rms(y)[1] * g2
        return x2, _rms(x2)[1] * g3

    x2, h2 = _rowwise(f_res1, [x, y, g_post, g_fpre], [(D_MODEL, F32), (D_MODEL, BF16)], [], 256, "res1")

    gu = _mm(h2, w_gu, "nn", F32, "ffn_up")

    def f_act(gu):
        g, u = gu[:, :D_FF], gu[:, D_FF:]
        return (g * _sigmoid(g) * u,)

    (act,) = _rowwise(f_act, [gu], [(D_FF, BF16)], [], 128, "ffn_act")
    f = _mm(act, w_dn, "nn", F32, "ffn_down")

    def f_loss(x2, f, g4, tgt):
        r, fhat = _rms(f)
        err = x2 + fhat * g4 - tgt
        loss = 0.5 * jnp.sum(jnp.sum(err * err, axis=-1, keepdims=True), axis=0, keepdims=True) / D_MODEL
        dout = err / D_MODEL
        return dout, _rms_bwd(dout, r, fhat, g4), loss, _colsum(dout * fhat)

    dout, df, loss, d_fpost = _rowwise(f_loss, [x2, f, g_fpost, target], [(D_MODEL, F32), (D_MODEL, BF16)],
                                       [1, D_MODEL], 256, "loss")

    dact = _mm(df, w_dn, "nt", F32, "d_act")
    dw_dn = _mm(act, df, "tn", F32, "dw_down")

    def f_dact(gu, dact):
        g, u = gu[:, :D_FF], gu[:, D_FF:]
        sg = _sigmoid(g)
        dg = dact * u * (sg * (1.0 + g * (1.0 - sg)))
        return (jnp.concatenate([dg, dact * (g * sg)], axis=1),)

    (dgu,) = _rowwise(f_dact, [gu, dact], [(2 * D_FF, BF16)], [], 128, "d_gu")
    dh2 = _mm(dgu, w_gu, "nt", F32, "d_h2")
    dw_gu = _mm(h2, dgu, "tn", F32, "dw_gate_up")

    def f_dres1(dout, dh2, x2, y, g3, g2):
        r3, x2hat = _rms(x2)
        dx2 = dout + _rms_bwd(dh2, r3, x2hat, g3)
        r2, yhat = _rms(y)
        return dx2, _rms_bwd(dx2, r2, yhat, g2), _colsum(dh2 * x2hat), _colsum(dx2 * yhat)

    dx2, dy, d_fpre, d_post = _rowwise(f_dres1, [dout, dh2, x2, y, g_fpre, g_post],
                                       [(D_MODEL, F32), (D_MODEL, BF16)], [D_MODEL, D_MODEL], 256, "d_res1")

    dmerged = _mm(dy, w_out, "nt", F32, "d_merged")
    dw_out = _mm(merged, dy, "tn", F32, "dw_out")

    def f_dmerge(dm, m_sb, m_mla, gl, b):
        g = _sigmoid(gl + b)
        g0, g1 = g[:, :D_MODEL], g[:, D_MODEL:]
        dgl = jnp.concatenate([dm * m_sb * g0 * (1.0 - g0), dm * m_mla * g1 * (1.0 - g1)], axis=1)
        return dm * g0, dm * g1, dgl, _colsum(dgl)

    dm_sb, dm_mla, dgl, d_bgate = _rowwise(f_dmerge, [dmerged, m_sb, m_mla, gl, b_gate],
                                           [(D_MODEL, BF16), (D_MODEL, BF16), (2 * D_MODEL, BF16)], [2 * D_MODEL],
                                           256, "d_merge")
    do_sb = _mm(dm_sb, w_ps, "nt", BF16, "d_o_sb")
    dw_ps = _mm(o_sb, dm_sb, "tn", F32, "dw_proj_sb")
    do_mla = _mm(dm_mla, w_pm, "nt", F32, "d_o_mla")
    dw_pm = _mm(o_mla, dm_mla, "tn", F32, "dw_proj_mla")

    dq_sb, dk_sb, dv_sb = _sb_bwd(qkv, do_sb, sb_tot, "sb_bwd")
    dqc, dkc, dvp = _mla_bwd(qc, kc, vp, do_mla, o_mla, lse, "mla_bwd")

    def f_dheads(dqc, dkc, dvp, rc, rs):
        lane = _lane(rc.shape)
        dqs, dkvs = [], []
        dkr = jnp.zeros(rc.shape, F32)
        for h in range(MLA_HEADS):
            dq = dqc[:, h * LANES:(h + 1) * LANES]
            dk = dkc[:, h * LANES:(h + 1) * LANES]
            dqs.append(dq * rc + _rope_swap(dq * rs))
            dkvs.append(jnp.where(lane < HALF, dk, dvp[:, h * LANES:(h + 1) * LANES]))
            dkr = dkr + jnp.where(lane >= HALF, dk, 0.0)
        dkr = dkr * rc + _rope_swap(dkr * rs)
        return jnp.concatenate(dqs, axis=1), jnp.concatenate(dkvs, axis=1), dkr

    dq_raw, dkv, dkr = _rowwise(f_dheads, [dqc, dkc, dvp, rope_c, rope_s],
                                [(wide, BF16), (wide, BF16), (LANES, F32)], [], 256, "d_mla_heads")
    dcqn = _mm(dq_raw, w_uq, "nt", F32, "d_cqn")
    dw_uq = _mm(cqn, dq_raw, "tn", F32, "dw_uq")
    dckvn = _mm(dkv, w_ukv, "nt", F32, "d_ckvn")
    dw_ukv = _mm(ckvn, dkv, "tn", F32, "dw_ukv")

    def f_dlat(lat, dcqn, dckvn, dkr, gq, gkv):
        rq, cqhat = _rms(lat[:, 0:MLA_Q_RANK])
        rkv, ckvhat = _rms(lat[:, MLA_Q_RANK:MLA_Q_RANK + MLA_KV_RANK])
        dlat = jnp.concatenate([_rms_bwd(dcqn, rq, cqhat, gq), _rms_bwd(dckvn, rkv, ckvhat, gkv), dkr], axis=1)
        return dlat, _colsum(dcqn * cqhat), _colsum(dckvn * ckvhat)

    dlat, d_gq, d_gkv = _rowwise(f_dlat, [lat, dcqn, dckvn, dkr, g_q, g_kv], [(768, BF16)],
                                 [MLA_Q_RANK, MLA_KV_RANK], 256, "d_lat")

    dqkv = jnp.concatenate([dq_sb, dk_sb, dv_sb], axis=1).astype(BF16)
    dh1_a = _mm(dqkv, w_a, "nt", F32, "d_h1_sb")
    dh1_b = _mm(dlat, w_b, "nt", F32, "d_h1_lat")
    dh1_g = _mm(dgl, w_g, "nt", F32, "d_h1_gate")
    dw_a = _mm(h1, dqkv, "tn", F32, "dw_in_sb")
    dw_b = _mm(h1, dlat, "tn", F32, "dw_in_lat")
    dw_g = _mm(h1, dgl, "tn", F32, "dw_in_gate")

    def f_dx(x, dx2, da, db, dg, g1):
        r, xhat = _rms(x)
        dh1 = da + db + dg
        return dx2 + _rms_bwd(dh1, r, xhat, g1), _colsum(dh1 * xhat)

    grad_x, d_pre = _rowwise(f_dx, [x, dx2, dh1_a, dh1_b, dh1_g, g_pre], [(D_MODEL, F32)], [D_MODEL], 256, "d_x")

    small = {"norm_mix_pre": d_pre, "norm_mix_post": d_post, "b_gate": d_bgate, "q_norm": d_gq, "kv_norm": d_gkv,
             "norm_ffn_pre": d_fpre, "norm_ffn_post": d_fpost}
    big = {
        "w_in": jnp.concatenate([dw_a, dw_b[:, 0:640], dw_b[:, 704:736], dw_g], axis=1),
        "w_uq": dw_uq.reshape(MLA_Q_RANK, MLA_HEADS, LANES)[:, :, :MLA_QK].reshape(MLA_Q_RANK, MLA_HEADS * MLA_QK),
        "w_ukv": dw_ukv,
        "w_proj_sb": dw_ps,
        "w_proj_mla": dw_pm.reshape(MLA_HEADS, LANES, D_MODEL)[:, HALF:, :].reshape(MLA_HEADS * HALF, D_MODEL),
        "w_out": dw_out,
        "w_gate_up": dw_gu,
        "w_down": dw_dn,
    }
    return loss, grad_x, small, big


def kernel(x, positions, norm_mix_pre, norm_mix_post, w_in, b_gate, q_norm, w_uq, kv_norm, w_ukv, w_proj_sb, w_proj_mla, w_out, norm_ffn_pre, norm_ffn_post, w_gate_up, w_down, loss_target, m_norm_mix_pre, m_norm_mix_post, m_w_in, m_b_gate, m_q_norm, m_w_uq, m_kv_norm, m_w_ukv, m_w_proj_sb, m_w_proj_mla, m_w_out, m_norm_ffn_pre, m_norm_ffn_post, m_w_gate_up, m_w_down, v_norm_mix_pre, v_norm_mix_post, v_w_in, v_b_gate, v_q_norm, v_w_uq, v_kv_norm, v_w_ukv, v_w_proj_sb, v_w_proj_mla, v_w_out, v_norm_ffn_pre, v_norm_ffn_post, v_w_gate_up, v_w_down):
    given = dict(locals())
    s = x.shape[1]
    big_w = {n: given[n][0] for n, _, _, _ in BIG}
    small_w = {n: given[n] for n, _ in SMALL}

    gathered = _gather_chips(_pack_big(big_w, BF16), "gather_weights")
    loss, grad_x, d_small, d_big = _forward_backward(
        x.reshape(s, D_MODEL), positions.reshape(s, 1), loss_target.reshape(s, D_MODEL), small_w,
        _full_from_gathered(gathered))

    c = lax.axis_index("c")
    gpack = _split_for_chips(d_big)
    from_sibling = _swap_halves(gpack, "grad_swap_halves")
    mine = lax.dynamic_slice_in_dim(gpack, c * HALF_ROWS, HALF_ROWS, axis=1)
    chip_part = _add2(mine, from_sibling, "grad_add_cores")
    parts = _scatter_chips(chip_part, "grad_scatter_chips")
    reduced = _join_halves(_sum_slots(parts, "grad_add_chips"), "grad_join_halves")
    small_sum = _allsum_small(_pack_small(d_small, loss), "small_allsum")

    delta_b, m_b, v_b = _adamw(_pack_big(big_w, F32), reduced, _pack_big({n: given["m_" + n][0] for n in big_w}, F32),
                               _pack_big({n: given["v_" + n][0] for n in big_w}, F32), "adamw_big")
    delta_s, m_s, v_s = _adamw(_pack_small(small_w), small_sum, _pack_small({n: given["m_" + n] for n in small_w}),
                               _pack_small({n: given["v_" + n] for n in small_w}), "adamw_small")

    loss_out = small_sum.reshape(-1)[SMALL_ELEMS]
    grads = {**_unpack_big(reduced), **_unpack_small(small_sum)}
    deltas = {**_unpack_big(delta_b), **_unpack_small(delta_s)}
    new_m = {**_unpack_big(m_b), **_unpack_small(m_s)}
    new_v = {**_unpack_big(v_b), **_unpack_small(v_s)}
    return (loss_out, grad_x.reshape(1, s, D_MODEL), *[grads[n] for n in WEIGHT_ORDER],
            *[deltas[n] for n in WEIGHT_ORDER], *[new_m[n] for n in WEIGHT_ORDER], *[new_v[n] for n in WEIGHT_ORDER])
```

```python
import functools
import math

import jax
import jax.numpy as jnp
from jax import lax
from jax.experimental import pallas as pl
from jax.experimental.pallas import tpu as pltpu

F32 = jnp.float32
BF16 = jnp.bfloat16
MESH = pl.DeviceIdType.MESH

D_MODEL = 1024
SB_WIDTH = 512
MLA_HEADS = 8
MLA_Q_RANK = 384
MLA_KV_RANK = 256
MLA_ROPE = 32
MLA_QK = 96
D_FF = 2816
ROPE_THETA = 10000.0
EPS = 1e-6
LANES = 128
HALF = 64
ATT_TILE = 256
N_CHIPS = 4

ADAM_LR = 0.001
ADAM_B1 = 0.9
ADAM_B2 = 0.999
ADAM_EPS = 1e-08
ADAM_WD = 0.01
ADAM_STEP = 10

VMEM_LIMIT = 56 * 1024 * 1024

BIG = (
    ("w_in", 1024, 1064, 1),
    ("w_uq", 384, 192, 1),
    ("w_ukv", 256, 256, 1),
    ("w_proj_sb", 512, 256, 1),
    ("w_proj_mla", 512, 256, 1),
    ("w_out", 256, 1024, 0),
    ("w_gate_up", 1024, 1408, 1),
    ("w_down", 704, 1024, 0),
)
PACK_COLS = 1024
SMALL = (
    ("norm_mix_pre", 1024),
    ("norm_mix_post", 1024),
    ("b_gate", 2048),
    ("q_norm", 384),
    ("kv_norm", 256),
    ("norm_ffn_pre", 1024),
    ("norm_ffn_post", 1024),
)
SMALL_ELEMS = sum(n for _, n in SMALL)
SMALL_ROWS = 8
WEIGHT_ORDER = ("norm_mix_pre", "norm_mix_post", "w_in", "b_gate", "q_norm", "w_uq", "kv_norm", "w_ukv",
                "w_proj_sb", "w_proj_mla", "w_out", "norm_ffn_pre", "norm_ffn_post", "w_gate_up", "w_down")


def _params():
    return pltpu.CompilerParams(vmem_limit_bytes=VMEM_LIMIT)


def _tile(dim, pref):
    t = (min(pref, dim) // LANES) * LANES
    while t >= LANES:
        if dim % t == 0:
            return t
        t -= LANES
    return dim


def _mm(a, b, mode, out_dtype, name, tm=1024, tn=1024, tk=1024):
    if mode == "nn":
        (m, k), n = a.shape, b.shape[1]
    elif mode == "nt":
        (m, k), n = a.shape, b.shape[0]
    else:
        (k, m), n = a.shape, b.shape[1]
    tm, tn, tk = _tile(m, tm), _tile(n, tn), _tile(k, tk)
    nk = k // tk
    if mode == "tn":
        a_spec = pl.BlockSpec((tk, tm), lambda i, j, l: (l, i))
        lhs_dim = 0
    else:
        a_spec = pl.BlockSpec((tm, tk), lambda i, j, l: (i, l))
        lhs_dim = 1
    if mode == "nt":
        b_spec = pl.BlockSpec((tn, tk), lambda i, j, l: (j, l))
        rhs_dim = 1
    else:
        b_spec = pl.BlockSpec((tk, tn), lambda i, j, l: (l, j))
        rhs_dim = 0
    dims = (((lhs_dim,), (rhs_dim,)), ((), ()))

    def body(a_ref, b_ref, o_ref, acc_ref):
        l = pl.program_id(2)

        @pl.when(l == 0)
        def _():
            acc_ref[...] = jnp.zeros_like(acc_ref)

        acc_ref[...] += lax.dot_general(a_ref[...].astype(BF16), b_ref[...].astype(BF16), dims,
                                        preferred_element_type=F32)

        @pl.when(l == nk - 1)
        def _():
            o_ref[...] = acc_ref[...].astype(o_ref.dtype)

    return pl.pallas_call(
        body,
        name=name,
        grid=(m // tm, n // tn, nk),
        in_specs=[a_spec, b_spec],
        out_specs=pl.BlockSpec((tm, tn), lambda i, j, l: (i, j)),
        out_shape=jax.ShapeDtypeStruct((m, n), out_dtype),
        scratch_shapes=[pltpu.VMEM((tm, tn), F32)],
        compiler_params=_params(),
    )(a, b)


def _rowwise(fn, ins, outs, reds, tm, name):
    rows = ins[0].shape[0]
    tm = min(tm, rows)
    n_in, n_out = len(ins), len(outs)

    def spec(shape):
        if shape[0] == rows:
            return pl.BlockSpec((tm, shape[1]), lambda i: (i, 0))
        return pl.BlockSpec(shape, lambda i: (0, 0))

    def body(*refs):
        i = pl.program_id(0)
        vals = fn(*[r[...] for r in refs[:n_in]])
        for r, v in zip(refs[n_in:n_in + n_out], vals[:n_out]):
            r[...] = v.astype(r.dtype)
        for r, v in zip(refs[n_in + n_out:], vals[n_out:]):
            @pl.when(i == 0)
            def _():
                r[...] = jnp.zeros_like(r)

            r[...] += v

    out_shape = [jax.ShapeDtypeStruct((rows, d), dt) for d, dt in outs]
    out_shape += [jax.ShapeDtypeStruct((1, d), F32) for d in reds]
    out_specs = [spec(s.shape) for s in out_shape]
    return pl.pallas_call(
        body,
        name=name,
        grid=(rows // tm,),
        in_specs=[spec(a.shape) for a in ins],
        out_specs=out_specs,
        out_shape=out_shape,
        compiler_params=_params(),
    )(*ins)


def _colsum(v):
    return jnp.sum(v, axis=0, keepdims=True)


def _rms(v):
    r = lax.rsqrt(jnp.mean(v * v, axis=-1, keepdims=True) + EPS)
    return r, v * r


def _rms_bwd(dy, r, vhat, g):
    u = dy * g
    return r * (u - vhat * jnp.mean(u * vhat, axis=-1, keepdims=True))


def _sigmoid(t):
    return 1.0 / (1.0 + jnp.exp(-t))


def _lane(shape):
    return lax.broadcasted_iota(jnp.int32, shape, len(shape) - 1)


def _rope_swap(v):
    lane = _lane(v.shape)
    up = pltpu.roll(v, LANES - 16, 1)
    down = pltpu.roll(v, 16, 1)
    return jnp.where((lane >= 64) & (lane < 80), up, jnp.where((lane >= 80) & (lane < 96), down, 0.0))


def _softplus(z):
    return jnp.maximum(z, 0.0) + jnp.log(1.0 + jnp.exp(-jnp.abs(z)))


def _split_dot(v, m2):
    hi = v.astype(BF16)
    lo = (v - hi.astype(F32)).astype(BF16)
    return jnp.dot(jnp.concatenate([hi, lo], axis=1), m2, preferred_element_type=F32)


def _dot_nt(a, b):
    return lax.dot_general(a, b, (((1,), (1,)), ((), ())), preferred_element_type=F32)


def _dot_tn(a, b):
    return lax.dot_general(a, b, (((0,), (0,)), ((), ())), preferred_element_type=F32)


def _sb_fwd(qkv, name):
    s = qkv.shape[0]
    t = min(ATT_TILE, s // 2)
    npair = SB_WIDTH // LANES

    def body(q_ref, k_ref, v_ref, o_ref, t_ref):
        i = pl.program_id(1)
        lane = _lane((1, LANES))
        row = lax.broadcasted_iota(jnp.int32, (t, t), 0)
        col = lax.broadcasted_iota(jnp.int32, (t, t), 1)
        tri = col < row
        m_from = jnp.where(row >= col, 1.0, 0.0).astype(BF16)
        m_from = jnp.concatenate([m_from, m_from], axis=0)
        masks = (lane < HALF, lane >= HALF)
        qhs = [jnp.where(hm, q_ref[n * t:(n + 1) * t, :], jnp.zeros((t, LANES), BF16)) * 0.125
               for n in range(2) for hm in masks]

        def step(kb, carry, diags):
            rows = pl.ds(pl.multiple_of(kb * t, t), t)
            ks = k_ref[rows, :]
            vs = v_ref[rows, :]
            vhs = [jnp.where(hm, vs, jnp.zeros_like(vs)) for hm in masks]
            act = [ch for ch in range(4) if diags[ch // 2] is not None]
            zs = {ch: _dot_nt(qhs[ch], ks) for ch in act}
            lbs = {}
            for ch in act:
                lb = -_softplus(zs[ch])
                lbs[ch] = jnp.where(tri, lb, 0.0) if diags[ch // 2] else lb
            sums = {ch: _split_dot(lbs[ch], m_from) for ch in act}
            out = list(carry)
            for ch in act:
                c, acc = carry[ch]
                a = jnp.exp(zs[ch] + sums[ch] + c)
                if diags[ch // 2]:
                    a = jnp.where(tri, a, 0.0)
                acc = acc + jnp.dot(a.astype(BF16), vhs[ch % 2], preferred_element_type=F32)
                out[ch] = (c + sums[ch][:, 0:1], acc)
            return tuple(out)

        init = (jnp.zeros((t, 1), F32), jnp.zeros((t, LANES), F32))
        carry = step(2 * i + 1, (init,) * 4, (None, True))
        carry = step(2 * i, carry, (True, False))
        carry = lax.fori_loop(0, 2 * i, lambda n, cr: step(2 * i - 1 - n, cr, (False, False)), carry)
        for n in range(2):
            (c0, acc0), (c1, acc1) = carry[2 * n], carry[2 * n + 1]
            o_ref[n * t:(n + 1) * t, :] = acc0 + acc1
            t_ref[n * t:(n + 1) * t, :] = jnp.where(masks[0], c0, c1)

    blk = pl.BlockSpec((2 * t, LANES), lambda p, i: (i, p))
    return pl.pallas_call(
        body,
        name=name,
        grid=(npair, s // (2 * t)),
        in_specs=[
            blk,
            pl.BlockSpec((s, LANES), lambda p, i: (0, npair + p)),
            pl.BlockSpec((s, LANES), lambda p, i: (0, 2 * npair + p)),
        ],
        out_specs=[blk, blk],
        out_shape=[jax.ShapeDtypeStruct((s, SB_WIDTH), F32), jax.ShapeDtypeStruct((s, SB_WIDTH), F32)],
        compiler_params=_params(),
    )(qkv, qkv, qkv)


def _sb_bwd(qkv, do, tot, name):
    s = qkv.shape[0]
    t = min(ATT_TILE, s // 2)
    npair = SB_WIDTH // LANES

    def body(q_ref, k_ref, v_ref, do_ref, t_ref, dq_ref, dk_ref, dv_ref):
        i = pl.program_id(1)

        @pl.when(i == 0)
        def _():
            dk_ref[...] = jnp.zeros_like(dk_ref)
            dv_ref[...] = jnp.zeros_like(dv_ref)

        lane = _lane((1, LANES))
        row = lax.broadcasted_iota(jnp.int32, (t, t), 0)
        col = lax.broadcasted_iota(jnp.int32, (t, t), 1)
        tri = col < row
        m_from = jnp.where(row >= col, 1.0, 0.0).astype(BF16)
        m_from = jnp.concatenate([m_from, m_from], axis=0)
        m_upto = jnp.where(row <= col, 1.0, 0.0).astype(BF16)
        m_upto = jnp.concatenate([m_upto, m_upto], axis=0)
        masks = (lane < HALF, lane >= HALF)
        zero_b = jnp.zeros((t, LANES), BF16)
        qhs, dohs, tots = [], [], []
        for n in range(2):
            rs = slice(n * t, (n + 1) * t)
            do_b = do_ref[rs, :].astype(BF16)
            for h, hm in enumerate(masks):
                qhs.append(jnp.where(hm, q_ref[rs, :], zero_b))
                dohs.append(jnp.where(hm, do_b, zero_b))
                tots.append(t_ref[rs, h * HALF:h * HALF + 1])

        def step(kb, carry, diags):
            rows = pl.ds(pl.multiple_of(kb * t, t), t)
            ks = k_ref[rows, :]
            vs = v_ref[rows, :]
            khs = [jnp.where(hm, ks, jnp.zeros_like(ks)) for hm in masks]
            act = [ch for ch in range(4) if diags[ch // 2] is not None]
            zs = {ch: _dot_nt(qhs[ch] * 0.125, ks) for ch in act}
            das = {ch: _dot_nt(dohs[ch], vs) for ch in act}
            lbs, betas = {}, {}
            for ch in act:
                sp = _softplus(zs[ch])
                betas[ch] = jnp.exp(zs[ch] - sp)
                lbs[ch] = jnp.where(tri, -sp, 0.0) if diags[ch // 2] else -sp
            sums = {ch: _split_dot(lbs[ch], m_from) for ch in act}
            avs, es, befores = {}, {}, {}
            for ch in act:
                before = carry[ch][0]
                total = sums[ch][:, 0:1]
                a = jnp.exp(zs[ch] + sums[ch] + (tots[ch] - before - total))
                if diags[ch // 2]:
                    a = jnp.where(tri, a, 0.0)
                avs[ch] = a.astype(BF16)
                es[ch] = a * das[ch]
                befores[ch] = before + total
            e_sums = {ch: _split_dot(es[ch], m_upto) for ch in act}
            dzs, e_befores = {}, {}
            for ch in act:
                e_upto = e_sums[ch] + carry[ch][1]
                dz = es[ch] - betas[ch] * e_upto
                if diags[ch // 2]:
                    dz = jnp.where(tri, dz, 0.0)
                dzs[ch] = (dz * 0.125).astype(BF16)
                e_befores[ch] = e_upto[:, t - 1:t]
            out = list(carry)
            dk = jnp.zeros((t, LANES), F32)
            dv = jnp.zeros((t, LANES), F32)
            for ch in act:
                dq = carry[ch][2] + jnp.dot(dzs[ch], khs[ch % 2], preferred_element_type=F32)
                dk = dk + _dot_tn(dzs[ch], qhs[ch])
                dv = dv + _dot_tn(avs[ch], dohs[ch])
                out[ch] = (befores[ch], e_befores[ch], dq)
            dk_ref[rows, :] += dk
            dv_ref[rows, :] += dv
            return tuple(out)

        zero = jnp.zeros((t, 1), F32)
        init = (zero, zero, jnp.zeros((t, LANES), F32))
        carry = lax.fori_loop(0, 2 * i, lambda kb, cr: step(kb, cr, (False, False)), (init,) * 4)
        carry = step(2 * i, carry, (True, False))
        carry = step(2 * i + 1, carry, (None, True))
        for n in range(2):
            dq_ref[n * t:(n + 1) * t, :] = carry[2 * n][2] + carry[2 * n + 1][2]

    blk = pl.BlockSpec((2 * t, LANES), lambda p, i: (i, p))
    full = pl.BlockSpec((s, LANES), lambda p, i: (0, p))
    out = jax.ShapeDtypeStruct((s, SB_WIDTH), F32)
    return pl.pallas_call(
        body,
        name=name,
        grid=(npair, s // (2 * t)),
        in_specs=[
            blk,
            pl.BlockSpec((s, LANES), lambda p, i: (0, npair + p)),
            pl.BlockSpec((s, LANES), lambda p, i: (0, 2 * npair + p)),
            blk,
            blk,
        ],
        out_specs=[blk, full, full],
        out_shape=[out, out, out],
        compiler_params=_params(),
    )(qkv, qkv, qkv, do, tot)


MLA_SCALE = 1.0 / math.sqrt(MLA_QK)
NEG = -1e30
MLA_QBLOCKS = 4


def _diag_plan(nq):
    return [tuple(None if n < j else n == j for n in range(nq)) for j in range(nq)]


def _mla_fwd(qc, kc, vp, name):
    s = qc.shape[0]
    nq = MLA_QBLOCKS
    t = min(ATT_TILE, s // nq)

    def body(q_ref, k_ref, v_ref, o_ref, lse_ref):
        i = pl.program_id(1)
        row = lax.broadcasted_iota(jnp.int32, (t, t), 0)
        col = lax.broadcasted_iota(jnp.int32, (t, t), 1)
        keep = col <= row
        qs = [q_ref[n * t:(n + 1) * t, :] for n in range(nq)]

        def step(kb, carry, diags):
            rows = pl.ds(pl.multiple_of(kb * t, t), t)
            ks = k_ref[rows, :]
            vs = v_ref[rows, :]
            act = [n for n in range(nq) if diags[n] is not None]
            scs = {n: _dot_nt(qs[n], ks) * MLA_SCALE for n in act}
            mid = {}
            for n in act:
                m, l, acc = carry[n]
                sc = jnp.where(keep, scs[n], NEG) if diags[n] else scs[n]
                m_new = jnp.maximum(m, jnp.max(sc, axis=-1, keepdims=True))
                alpha = jnp.exp(m - m_new)
                p = jnp.exp(sc - m_new)
                mid[n] = (m_new, alpha * l + jnp.sum(p, axis=-1, keepdims=True), alpha * acc, p.astype(BF16))
            out = list(carry)
            for n in act:
                m, l, acc, p = mid[n]
                out[n] = (m, l, acc + jnp.dot(p, vs, preferred_element_type=F32))
            return tuple(out)

        init = (jnp.full((t, 1), NEG, F32), jnp.zeros((t, 1), F32), jnp.zeros((t, LANES), F32))
        carry = lax.fori_loop(0, nq * i, lambda kb, cr: step(kb, cr, (False,) * nq), (init,) * nq)
        for j, diags in enumerate(_diag_plan(nq)):
            carry = step(nq * i + j, carry, diags)
        for n, (m, l, acc) in enumerate(carry):
            o_ref[n * t:(n + 1) * t, :] = acc / l
            lse_ref[n * t:(n + 1) * t, :] = jnp.broadcast_to(m + jnp.log(l), (t, LANES))

    blk = pl.BlockSpec((nq * t, LANES), lambda h, i: (i, h))
    full = pl.BlockSpec((s, LANES), lambda h, i: (0, h))
    out = jax.ShapeDtypeStruct((s, MLA_HEADS * LANES), F32)
    return pl.pallas_call(
        body,
        name=name,
        grid=(MLA_HEADS, s // (nq * t)),
        in_specs=[blk, full, full],
        out_specs=[blk, blk],
        out_shape=[out, out],
        compiler_params=_params(),
    )(qc, kc, vp)


def _mla_bwd(qc, kc, vp, do, o, lse, name):
    s = qc.shape[0]
    nq = MLA_QBLOCKS
    t = min(ATT_TILE, s // nq)

    def body(q_ref, k_ref, v_ref, do_ref, o_ref, lse_ref, dq_ref, dk_ref, dv_ref):
        i = pl.program_id(1)

        @pl.when(i == 0)
        def _():
            dk_ref[...] = jnp.zeros_like(dk_ref)
            dv_ref[...] = jnp.zeros_like(dv_ref)

        row = lax.broadcasted_iota(jnp.int32, (t, t), 0)
        col = lax.broadcasted_iota(jnp.int32, (t, t), 1)
        keep = col <= row

        def block(n):
            rs = slice(n * t, (n + 1) * t)
            do_f = do_ref[rs, :]
            delta = jnp.sum(do_f * o_ref[rs, :], axis=-1, keepdims=True)
            return q_ref[rs, :], do_f.astype(BF16), delta, lse_ref[rs, 0:1]

        blocks = [block(n) for n in range(nq)]

        def step(kb, dqs, diags):
            rows = pl.ds(pl.multiple_of(kb * t, t), t)
            ks = k_ref[rows, :]
            vs = v_ref[rows, :]
            act = [n for n in range(nq) if diags[n] is not None]
            scs = {n: _dot_nt(blocks[n][0], ks) * MLA_SCALE for n in act}
            dps = {n: _dot_nt(blocks[n][1], vs) for n in act}
            ps, dss = {}, {}
            for n in act:
                _, _, delta, lse_v = blocks[n]
                p = jnp.exp(scs[n] - lse_v)
                if diags[n]:
                    p = jnp.where(keep, p, 0.0)
                dss[n] = (p * (dps[n] - delta) * MLA_SCALE).astype(BF16)
                ps[n] = p.astype(BF16)
            dk = jnp.zeros((t, LANES), F32)
            dv = jnp.zeros((t, LANES), F32)
            out = list(dqs)
            for n in act:
                q, do_b, _, _ = blocks[n]
                out[n] = dqs[n] + jnp.dot(dss[n], ks, preferred_element_type=F32)
                dk = dk + _dot_tn(dss[n], q)
                dv = dv + _dot_tn(ps[n], do_b)
            dk_ref[rows, :] += dk
            dv_ref[rows, :] += dv
            return tuple(out)

        zero = jnp.zeros((t, LANES), F32)
        dqs = lax.fori_loop(0, nq * i, lambda kb, cr: step(kb, cr, (False,) * nq), (zero,) * nq)
        for j, diags in enumerate(_diag_plan(nq)):
            dqs = step(nq * i + j, dqs, diags)
        for n in range(nq):
            dq_ref[n * t:(n + 1) * t, :] = dqs[n]

    blk = pl.BlockSpec((nq * t, LANES), lambda h, i: (i, h))
    full = pl.BlockSpec((s, LANES), lambda h, i: (0, h))
    out = jax.ShapeDtypeStruct((s, MLA_HEADS * LANES), F32)
    return pl.pallas_call(
        body,
        name=name,
        grid=(MLA_HEADS, s // (nq * t)),
        in_specs=[blk, full, full, blk, blk, blk],
        out_specs=[blk, full, full],
        out_shape=[out, out, out],
        compiler_params=_params(),
    )(qc, kc, vp, do, o, lse)


def _position():
    return lax.axis_index("x"), lax.axis_index("y"), lax.axis_index("c")


def _other_chips(x, y):
    return [(1 - x, y), (x, 1 - y), (1 - x, 1 - y)]


HBM_SPEC = pl.BlockSpec(memory_space=pltpu.HBM)


def _remote(src, dst, send_sem, recv_sem, device):
    return pltpu.make_async_remote_copy(src_ref=src, dst_ref=dst, send_sem=send_sem, recv_sem=recv_sem,
                                        device_id=device, device_id_type=MESH)


def _dma_sems(n, count):
    return [pltpu.SemaphoreType.DMA((n,)) for _ in range(count)]


def _gather_weights(ws, name):
    na = len(ws)

    def body(*refs):
        srcs, outs = refs[:na], refs[na:2 * na]
        s1, r1, s2, r2, s3, r3, s4, r4 = refs[2 * na:]
        x, y, c = _position()
        me = 2 * x + y
        px, py = jnp.where(c == 1, 1 - x, x), jnp.where(c == 1, y, 1 - y)
        qx, qy = jnp.where(c == 1, x, 1 - x), jnp.where(c == 1, 1 - y, y)
        p_idx, q_idx, d_idx = 2 * px + py, 2 * qx + qy, 2 * (1 - x) + (1 - y)
        partner, across, sibling = (px, py, c), (qx, qy, c), (x, y, 1 - c)
        my_half, other_half = 1 - c, c

        def half(i, slot, h):
            kh = ws[i].shape[0] // 2
            return outs[i].at[slot, pl.ds(h * kh, kh), :]

        sends = [_remote(srcs[i], outs[i].at[me], s1.at[i], r1.at[i], partner) for i in range(na)]
        for cp in sends:
            cp.start()
        for i in range(na):
            _remote(srcs[i], outs[i].at[p_idx], s1.at[i], r1.at[i], partner).wait_recv()
            onward = [_remote(half(i, p_idx, my_half), half(i, p_idx, my_half), s2.at[i], r2.at[i], across),
                      _remote(outs[i].at[p_idx], outs[i].at[p_idx], s3.at[i], r3.at[i], sibling)]
            for cp in onward:
                cp.start()
            sends += onward
        for i in range(na):
            _remote(half(i, d_idx, my_half), half(i, d_idx, my_half), s2.at[i], r2.at[i], across).wait_recv()
            cp = _remote(half(i, d_idx, my_half), half(i, d_idx, my_half), s4.at[i], r4.at[i], sibling)
            cp.start()
            sends.append(cp)
        for i in range(na):
            _remote(outs[i].at[q_idx], outs[i].at[q_idx], s3.at[i], r3.at[i], sibling).wait_recv()
            _remote(half(i, d_idx, other_half), half(i, d_idx, other_half), s4.at[i], r4.at[i], sibling).wait_recv()
        for cp in sends:
            cp.wait_send()

    return pl.pallas_call(
        body,
        name=name,
        in_specs=[HBM_SPEC] * na,
        out_specs=[HBM_SPEC] * na,
        out_shape=[jax.ShapeDtypeStruct((N_CHIPS,) + w.shape, w.dtype) for w in ws],
        scratch_shapes=_dma_sems(na, 8),
    )(*ws)


def _swap_halves(gs, name):
    na = len(gs)

    def body(*refs):
        srcs, outs, send_sems, recv_sems = refs[:na], refs[na:2 * na], refs[2 * na], refs[2 * na + 1]
        x, y, c = _position()
        cps = []
        for i in range(na):
            kh = gs[i].shape[1] // 2
            cps.append(_remote(srcs[i].at[:, pl.ds((1 - c) * kh, kh), :], outs[i], send_sems.at[i], recv_sems.at[i],
                               (x, y, 1 - c)))
            cps[-1].start()
        for cp in cps:
            cp.wait()

    return pl.pallas_call(
        body,
        name=name,
        in_specs=[HBM_SPEC] * na,
        out_specs=[HBM_SPEC] * na,
        out_shape=[jax.ShapeDtypeStruct((g.shape[0], g.shape[1] // 2, g.shape[2]), g.dtype) for g in gs],
        scratch_shapes=_dma_sems(na, 2),
    )(*gs)


def _scatter_chips(parts, name):
    na = len(parts)

    def body(*refs):
        srcs, outs, send_sems, recv_sems = refs[:na], refs[na:2 * na], refs[2 * na], refs[2 * na + 1]
        x, y, c = _position()
        me = 2 * x + y
        sends = []
        for i in range(na):
            for k, (px, py) in enumerate(_other_chips(x, y)):
                cp = _remote(srcs[i].at[2 * px + py], outs[i].at[me], send_sems.at[3 * i + k],
                             recv_sems.at[3 * i + k], (px, py, c))
                cp.start()
                sends.append(cp)
        for i in range(na):
            for k, (px, py) in enumerate(_other_chips(x, y)):
                _remote(srcs[i].at[me], outs[i].at[2 * px + py], send_sems.at[3 * i + k], recv_sems.at[3 * i + k],
                        (px, py, c)).wait_recv()
        for cp in sends:
            cp.wait_send()

    return pl.pallas_call(
        body,
        name=name,
        in_specs=[HBM_SPEC] * na,
        out_specs=[HBM_SPEC] * na,
        out_shape=[jax.ShapeDtypeStruct(p.shape, p.dtype) for p in parts],
        scratch_shapes=_dma_sems(3 * na, 2),
    )(*parts)


def _to_sibling(halves, name):
    na = len(halves)

    def body(*refs):
        srcs, outs, send_sems, recv_sems = refs[:na], refs[na:2 * na], refs[2 * na], refs[2 * na + 1]
        x, y, c = _position()
        cps = [_remote(srcs[i], outs[i], send_sems.at[i], recv_sems.at[i], (x, y, 1 - c)) for i in range(na)]
        for cp in cps:
            cp.start()
        for cp in cps:
            cp.wait()

    return pl.pallas_call(
        body,
        name=name,
        in_specs=[HBM_SPEC] * na,
        out_specs=[HBM_SPEC] * na,
        out_shape=[jax.ShapeDtypeStruct(h.shape, h.dtype) for h in halves],
        scratch_shapes=_dma_sems(na, 2),
    )(*halves)


def _allsum_small(v, name):
    shape = v.shape

    def body(v_ref, o_ref, buf, send_sems, recv_sems):
        x, y, c = _position()
        me = 4 * x + 2 * y + c
        buf[me] = v_ref[...]
        peers = []
        for k in range(1, 8):
            fx, fy, fc = (k >> 2) & 1, (k >> 1) & 1, k & 1
            peers.append((jnp.where(fx == 1, 1 - x, x), jnp.where(fy == 1, 1 - y, y), jnp.where(fc == 1, 1 - c, c)))
        sends = []
        for k, peer in enumerate(peers):
            cp = pltpu.make_async_remote_copy(src_ref=v_ref, dst_ref=buf.at[me], send_sem=send_sems.at[k],
                                              recv_sem=recv_sems.at[k], device_id=peer, device_id_type=MESH)
            cp.start()
            sends.append(cp)
        for k, (px, py, pc) in enumerate(peers):
            pltpu.make_async_remote_copy(src_ref=v_ref, dst_ref=buf.at[4 * px + 2 * py + pc],
                                         send_sem=send_sems.at[k], recv_sem=recv_sems.at[k],
                                         device_id=(px, py, pc), device_id_type=MESH).wait_recv()
        for cp in sends:
            cp.wait_send()
        acc = buf[0]
        for d in range(1, 8):
            acc = acc + buf[d]
        o_ref[...] = acc

    return pl.pallas_call(
        body,
        name=name,
        in_specs=[pl.BlockSpec(memory_space=pltpu.VMEM)],
        out_specs=pl.BlockSpec(memory_space=pltpu.VMEM),
        out_shape=jax.ShapeDtypeStruct(shape, F32),
        scratch_shapes=[pltpu.VMEM((8,) + shape, F32), pltpu.SemaphoreType.DMA((7,)), pltpu.SemaphoreType.DMA((7,))],
    )(v)


def _row_tile(h):
    for t in range(256, 0, -16):
        if h % t == 0:
            return t
    return h


def _add_cores(g, a, name):
    n4, k, n = g.shape
    kh = k // 2
    th = _row_tile(kh)

    def body(g_ref, a_ref, f_ref, b_ref):
        c = lax.axis_index("c")
        total = jnp.where(c == 0, g_ref[0, 0], g_ref[0, 1]) + a_ref[0]
        f_ref[0] = total
        b_ref[0] = total.astype(BF16)

    spec = pl.BlockSpec((1, th, n), lambda j, r: (j, r, 0))
    return pl.pallas_call(
        body,
        name=name,
        grid=(n4, kh // th),
        in_specs=[pl.BlockSpec((1, 2, th, n), lambda j, r: (j, 0, r, 0)), spec],
        out_specs=[spec, spec],
        out_shape=[jax.ShapeDtypeStruct((n4, kh, n), F32), jax.ShapeDtypeStruct((n4, kh, n), BF16)],
        compiler_params=_params(),
    )(g.reshape(n4, 2, kh, n), a)


def _add_chips(pf, b, name):
    n4, h, n = pf.shape
    th = _row_tile(h)

    def body(pf_ref, b_ref, o_ref):
        me = 2 * lax.axis_index("x") + lax.axis_index("y")
        terms = [jnp.where(me == j, pf_ref[j], b_ref[j].astype(F32)) for j in range(n4)]
        o_ref[...] = ((terms[0] + terms[1]) + terms[2]) + terms[3]

    spec = pl.BlockSpec((n4, th, n), lambda r: (0, r, 0))
    return pl.pallas_call(
        body,
        name=name,
        grid=(h // th,),
        in_specs=[spec, spec],
        out_specs=pl.BlockSpec((th, n), lambda r: (r, 0)),
        out_shape=jax.ShapeDtypeStruct((h, n), F32),
        compiler_params=_params(),
    )(pf, b)


def _adam_math(w, g, m, v):
    m = ADAM_B1 * m + (1.0 - ADAM_B1) * g
    v = ADAM_B2 * v + (1.0 - ADAM_B2) * (g * g)
    m_hat = m / (1.0 - ADAM_B1 ** ADAM_STEP)
    v_hat = v / (1.0 - ADAM_B2 ** ADAM_STEP)
    delta = -ADAM_LR * (m_hat / (jnp.sqrt(v_hat) + ADAM_EPS) + ADAM_WD * w)
    return delta, m, v


def _adamw_shard(w, m, v, mine, other, name):
    k, n = w.shape
    kh = k // 2
    th = _row_tile(kh)
    nb = kh // th

    def body(w_ref, m_ref, v_ref, a_ref, b_ref, g_out, d_out, m_out, v_out):
        g = jnp.where(pl.program_id(0) == lax.axis_index("c"), a_ref[...], b_ref[...])
        g_out[...] = g
        d_out[...], m_out[...], v_out[...] = _adam_math(w_ref[...], g, m_ref[...], v_ref[...])

    full = pl.BlockSpec((th, n), lambda hh, r: (hh * nb + r, 0))
    half = pl.BlockSpec((th, n), lambda hh, r: (r, 0))
    return pl.pallas_call(
        body,
        name=name,
        grid=(2, nb),
        in_specs=[full, full, full, half, half],
        out_specs=[full] * 4,
        out_shape=[jax.ShapeDtypeStruct((k, n), F32)] * 4,
        compiler_params=_params(),
    )(w, m, v, mine, other)


def _adamw_small(w, g, m, v, name):
    cols = w.shape[1]
    return _rowwise(_adam_math, [w, g, m, v], [(cols, F32)] * 3, [], w.shape[0], name)


def _full_matrices(own, gathered):
    me = 2 * lax.axis_index("x") + lax.axis_index("y")
    out = {}
    for (n, _, _, axis), w, g in zip(BIG, own, gathered):
        out[n] = jnp.concatenate([jnp.where(me == j, w, g[j]) for j in range(N_CHIPS)], axis=axis)
    return out


def _split_for_chips(full):
    out = []
    for n, r, c, axis in BIG:
        g = full[n]
        out.append(jnp.stack([g[:, j * c:(j + 1) * c] for j in range(N_CHIPS)]) if axis == 1
                   else g.reshape(N_CHIPS, r, c))
    return out


def _pack_small(vecs, extra=None):
    flat = jnp.concatenate([vecs[n].reshape(-1) for n, _ in SMALL] + ([extra.reshape(-1)] if extra is not None else []))
    flat = jnp.pad(flat, (0, SMALL_ROWS * PACK_COLS - flat.shape[0]))
    return flat.reshape(SMALL_ROWS, PACK_COLS)


def _unpack_small(pack):
    flat = pack.reshape(-1)
    out, off = {}, 0
    for n, k in SMALL:
        out[n] = flat[off:off + k].reshape(1, k)
        off += k
    return out


def _forward_backward(x, pos, target, sm, w):
    s = x.shape[0]
    w_in = w["w_in"]
    w_a = w_in[:, 0:1536]
    zeros = functools.partial(jnp.zeros, dtype=BF16)
    w_b = jnp.concatenate([w_in[:, 1536:2176], zeros((D_MODEL, 64)), w_in[:, 2176:2208], zeros((D_MODEL, 32))], axis=1)
    w_g = w_in[:, 2208:4256]
    w_uq = jnp.pad(w["w_uq"].reshape(MLA_Q_RANK, MLA_HEADS, MLA_QK), ((0, 0), (0, 0), (0, LANES - MLA_QK)))
    w_uq = w_uq.reshape(MLA_Q_RANK, MLA_HEADS * LANES)
    w_ukv = w["w_ukv"]
    w_pm = jnp.pad(w["w_proj_mla"].reshape(MLA_HEADS, HALF, D_MODEL), ((0, 0), (HALF, 0), (0, 0)))
    w_pm = w_pm.reshape(MLA_HEADS * LANES, D_MODEL)
    w_ps, w_out, w_gu, w_dn = w["w_proj_sb"], w["w_out"], w["w_gate_up"], w["w_down"]
    inv_freq = ROPE_THETA ** (-jnp.arange(0, MLA_ROPE, 2, dtype=F32) / MLA_ROPE)
    inv_lane = jnp.concatenate([jnp.zeros((HALF,), F32), inv_freq, inv_freq, jnp.zeros((32,), F32)]).reshape(1, LANES)
    g_pre, g_post, b_gate = sm["norm_mix_pre"], sm["norm_mix_post"], sm["b_gate"]
    g_q, g_kv, g_fpre, g_fpost = sm["q_norm"], sm["kv_norm"], sm["norm_ffn_pre"], sm["norm_ffn_post"]

    def f_h1(x, g):
        return (_rms(x)[1] * g,)

    (h1,) = _rowwise(f_h1, [x, g_pre], [(D_MODEL, BF16)], [], 256, "h1")
    qkv = _mm(h1, w_a, "nn", BF16, "proj_sb")
    lat = _mm(h1, w_b, "nn", F32, "proj_lat")
    gl = _mm(h1, w_g, "nn", F32, "proj_gate")

    def f_lat(lat, gq, gkv, pos, inv):
        cqn = _rms(lat[:, 0:MLA_Q_RANK])[1] * gq
        ckvn = _rms(lat[:, MLA_Q_RANK:MLA_Q_RANK + MLA_KV_RANK])[1] * gkv
        kr = lat[:, 640:768]
        ang = pos.astype(F32) * inv
        lane = _lane(ang.shape)
        cosv, sinv = jnp.cos(ang), jnp.sin(ang)
        rope_c = jnp.where(lane < 64, 1.0, jnp.where(lane < 96, cosv, 0.0))
        rope_s = jnp.where((lane >= 64) & (lane < 80), -sinv, jnp.where((lane >= 80) & (lane < 96), sinv, 0.0))
        krr = jnp.where(lane >= 64, kr * rope_c, 0.0) + _rope_swap(kr) * rope_s
        return cqn, ckvn, krr, rope_c, rope_s

    cqn, ckvn, krr, rope_c, rope_s = _rowwise(
        f_lat, [lat, g_q, g_kv, pos, inv_lane],
        [(MLA_Q_RANK, BF16), (MLA_KV_RANK, BF16), (LANES, F32), (LANES, F32), (LANES, F32)], [], 256, "lat_norm")
    q_raw = _mm(cqn, w_uq, "nn", F32, "q_up")
    kv = _mm(ckvn, w_ukv, "nn", F32, "kv_up")

    def f_heads(q_raw, kv, krr, rc, rs):
        lane = _lane(krr.shape)
        qs, ks, vs = [], [], []
        for h in range(MLA_HEADS):
            qh = q_raw[:, h * LANES:(h + 1) * LANES]
            kvh = kv[:, h * LANES:(h + 1) * LANES]
            qs.append(qh * rc + _rope_swap(qh) * rs)
            ks.append(jnp.where(lane < HALF, kvh, krr))
            vs.append(jnp.where(lane >= HALF, kvh, 0.0))
        return jnp.concatenate(qs, axis=1), jnp.concatenate(ks, axis=1), jnp.concatenate(vs, axis=1)

    wide = MLA_HEADS * LANES
    qc, kc, vp = _rowwise(f_heads, [q_raw, kv, krr, rope_c, rope_s], [(wide, BF16)] * 3, [], 256, "mla_heads")
    o_sb, sb_tot = _sb_fwd(qkv, "sb_fwd")
    o_mla, lse = _mla_fwd(qc, kc, vp, "mla_fwd")
    m_sb = _mm(o_sb, w_ps, "nn", F32, "proj_o_sb")
    m_mla = _mm(o_mla, w_pm, "nn", F32, "proj_o_mla")

    def f_merge(m_sb, m_mla, gl, b):
        g = _sigmoid(gl + b)
        return (g[:, :D_MODEL] * m_sb + g[:, D_MODEL:] * m_mla,)

    (merged,) = _rowwise(f_merge, [m_sb, m_mla, gl, b_gate], [(D_MODEL, BF16)], [], 256, "merge")
    y = _mm(merged, w_out, "nn", F32, "mix_out")

    def f_res1(x, y, g2, g3):
        x2 = x + _rms(y)[1] * g2
        return x2, _rms(x2)[1] * g3

    x2, h2 = _rowwise(f_res1, [x, y, g_post, g_fpre], [(D_MODEL, F32), (D_MODEL, BF16)], [], 256, "res1")

    gu = _mm(h2, w_gu, "nn", F32, "ffn_up")

    def f_act(gu):
        g, u = gu[:, :D_FF], gu[:, D_FF:]
        return (g * _sigmoid(g) * u,)

    (act,) = _rowwise(f_act, [gu], [(D_FF, BF16)], [], 128, "ffn_act")
    f = _mm(act, w_dn, "nn", F32, "ffn_down")

    def f_loss(x2, f, g4, tgt):
        r, fhat = _rms(f)
        err = x2 + fhat * g4 - tgt
        loss = 0.5 * jnp.sum(jnp.sum(err * err, axis=-1, keepdims=True), axis=0, keepdims=True) / D_MODEL
        dout = err / D_MODEL
        return dout, _rms_bwd(dout, r, fhat, g4), loss, _colsum(dout * fhat)

    dout, df, loss, d_fpost = _rowwise(f_loss, [x2, f, g_fpost, target], [(D_MODEL, F32), (D_MODEL, BF16)],
                                       [1, D_MODEL], 256, "loss")

    dact = _mm(df, w_dn, "nt", F32, "d_act")
    dw_dn = _mm(act, df, "tn", F32, "dw_down")

    def f_dact(gu, dact):
        g, u = gu[:, :D_FF], gu[:, D_FF:]
        sg = _sigmoid(g)
        dg = dact * u * (sg * (1.0 + g * (1.0 - sg)))
        return (jnp.concatenate([dg, dact * (g * sg)], axis=1),)

    (dgu,) = _rowwise(f_dact, [gu, dact], [(2 * D_FF, BF16)], [], 128, "d_gu")
    dh2 = _mm(dgu, w_gu, "nt", F32, "d_h2")
    dw_gu = _mm(h2, dgu, "tn", F32, "dw_gate_up")

    def f_dres1(dout, dh2, x2, y, g3, g2):
        r3, x2hat = _rms(x2)
        dx2 = dout + _rms_bwd(dh2, r3, x2hat, g3)
        r2, yhat = _rms(y)
        return dx2, _rms_bwd(dx2, r2, yhat, g2), _colsum(dh2 * x2hat), _colsum(dx2 * yhat)

    dx2, dy, d_fpre, d_post = _rowwise(f_dres1, [dout, dh2, x2, y, g_fpre, g_post],
                                       [(D_MODEL, F32), (D_MODEL, BF16)], [D_MODEL, D_MODEL], 256, "d_res1")

    dmerged = _mm(dy, w_out, "nt", F32, "d_merged")
    dw_out = _mm(merged, dy, "tn", F32, "dw_out")

    def f_dmerge(dm, m_sb, m_mla, gl, b):
        g = _sigmoid(gl + b)
        g0, g1 = g[:, :D_MODEL], g[:, D_MODEL:]
        dgl = jnp.concatenate([dm * m_sb * g0 * (1.0 - g0), dm * m_mla * g1 * (1.0 - g1)], axis=1)
        return dm * g0, dm * g1, dgl, _colsum(dgl)

    dm_sb, dm_mla, dgl, d_bgate = _rowwise(f_dmerge, [dmerged, m_sb, m_mla, gl, b_gate],
                                           [(D_MODEL, BF16), (D_MODEL, BF16), (2 * D_MODEL, BF16)], [2 * D_MODEL],
                                           256, "d_merge")
    do_sb = _mm(dm_sb, w_ps, "nt", BF16, "d_o_sb")
    dw_ps = _mm(o_sb, dm_sb, "tn", F32, "dw_proj_sb")
    do_mla = _mm(dm_mla, w_pm, "nt", F32, "d_o_mla")
    dw_pm = _mm(o_mla, dm_mla, "tn", F32, "dw_proj_mla")

    dq_sb, dk_sb, dv_sb = _sb_bwd(qkv, do_sb, sb_tot, "sb_bwd")
    dqc, dkc, dvp = _mla_bwd(qc, kc, vp, do_mla, o_mla, lse, "mla_bwd")

    def f_dheads(dqc, dkc, dvp, rc, rs):
        lane = _lane(rc.shape)
        dqs, dkvs = [], []
        dkr = jnp.zeros(rc.shape, F32)
        for h in range(MLA_HEADS):
            dq = dqc[:, h * LANES:(h + 1) * LANES]
            dk = dkc[:, h * LANES:(h + 1) * LANES]
            dqs.append(dq * rc + _rope_swap(dq * rs))
            dkvs.append(jnp.where(lane < HALF, dk, dvp[:, h * LANES:(h + 1) * LANES]))
            dkr = dkr + jnp.where(lane >= HALF, dk, 0.0)
        dkr = dkr * rc + _rope_swap(dkr * rs)
        return jnp.concatenate(dqs, axis=1), jnp.concatenate(dkvs, axis=1), dkr

    dq_raw, dkv, dkr = _rowwise(f_dheads, [dqc, dkc, dvp, rope_c, rope_s],
                                [(wide, BF16), (wide, BF16), (LANES, F32)], [], 256, "d_mla_heads")
    dcqn = _mm(dq_raw, w_uq, "nt", F32, "d_cqn")
    dw_uq = _mm(cqn, dq_raw, "tn", F32, "dw_uq")
    dckvn = _mm(dkv, w_ukv, "nt", F32, "d_ckvn")
    dw_ukv = _mm(ckvn, dkv, "tn", F32, "dw_ukv")

    def f_dlat(lat, dcqn, dckvn, dkr, gq, gkv):
        rq, cqhat = _rms(lat[:, 0:MLA_Q_RANK])
        rkv, ckvhat = _rms(lat[:, MLA_Q_RANK:MLA_Q_RANK + MLA_KV_RANK])
        dlat = jnp.concatenate([_rms_bwd(dcqn, rq, cqhat, gq), _rms_bwd(dckvn, rkv, ckvhat, gkv), dkr], axis=1)
        return dlat, _colsum(dcqn * cqhat), _colsum(dckvn * ckvhat)

    dlat, d_gq, d_gkv = _rowwise(f_dlat, [lat, dcqn, dckvn, dkr, g_q, g_kv], [(768, BF16)],
                                 [MLA_Q_RANK, MLA_KV_RANK], 256, "d_lat")

    dqkv = jnp.concatenate([dq_sb, dk_sb, dv_sb], axis=1).astype(BF16)
    dh1_a = _mm(dqkv, w_a, "nt", F32, "d_h1_sb")
    dh1_b = _mm(dlat, w_b, "nt", F32, "d_h1_lat")
    dh1_g = _mm(dgl, w_g, "nt", F32, "d_h1_gate")
    dw_a = _mm(h1, dqkv, "tn", F32, "dw_in_sb")
    dw_b = _mm(h1, dlat, "tn", F32, "dw_in_lat")
    dw_g = _mm(h1, dgl, "tn", F32, "dw_in_gate")

    def f_dx(x, dx2, da, db, dg, g1):
        r, xhat = _rms(x)
        dh1 = da + db + dg
        return dx2 + _rms_bwd(dh1, r, xhat, g1), _colsum(dh1 * xhat)

    grad_x, d_pre = _rowwise(f_dx, [x, dx2, dh1_a, dh1_b, dh1_g, g_pre], [(D_MODEL, F32)], [D_MODEL], 256, "d_x")

    small = {"norm_mix_pre": d_pre, "norm_mix_post": d_post, "b_gate": d_bgate, "q_norm": d_gq, "kv_norm": d_gkv,
             "norm_ffn_pre": d_fpre, "norm_ffn_post": d_fpost}
    big = {
        "w_in": jnp.concatenate([dw_a, dw_b[:, 0:640], dw_b[:, 704:736], dw_g], axis=1),
        "w_uq": dw_uq.reshape(MLA_Q_RANK, MLA_HEADS, LANES)[:, :, :MLA_QK].reshape(MLA_Q_RANK, MLA_HEADS * MLA_QK),
        "w_ukv": dw_ukv,
        "w_proj_sb": dw_ps,
        "w_proj_mla": dw_pm.reshape(MLA_HEADS, LANES, D_MODEL)[:, HALF:, :].reshape(MLA_HEADS * HALF, D_MODEL),
        "w_out": dw_out,
        "w_gate_up": dw_gu,
        "w_down": dw_dn,
    }
    return loss, grad_x, small, big


def kernel(x, positions, norm_mix_pre, norm_mix_post, w_in, b_gate, q_norm, w_uq, kv_norm, w_ukv, w_proj_sb, w_proj_mla, w_out, norm_ffn_pre, norm_ffn_post, w_gate_up, w_down, loss_target, m_norm_mix_pre, m_norm_mix_post, m_w_in, m_b_gate, m_q_norm, m_w_uq, m_kv_norm, m_w_ukv, m_w_proj_sb, m_w_proj_mla, m_w_out, m_norm_ffn_pre, m_norm_ffn_post, m_w_gate_up, m_w_down, v_norm_mix_pre, v_norm_mix_post, v_w_in, v_b_gate, v_q_norm, v_w_uq, v_kv_norm, v_w_ukv, v_w_proj_sb, v_w_proj_mla, v_w_out, v_norm_ffn_pre, v_norm_ffn_post, v_w_gate_up, v_w_down):
    given = dict(locals())
    s = x.shape[1]
    names = [n for n, _, _, _ in BIG]
    small_w = {n: given[n] for n, _ in SMALL}

    own = [given[n][0].astype(BF16) for n in names]
    gathered = _gather_weights(own, "gather_weights")
    loss, grad_x, d_small, d_big = _forward_backward(
        x.reshape(s, D_MODEL), positions.reshape(s, 1), loss_target.reshape(s, D_MODEL), small_w,
        _full_matrices(own, gathered))

    pieces = _split_for_chips(d_big)
    from_sibling = _swap_halves(pieces, "grad_swap_halves")
    partial = [_add_cores(g, a, "grad_add_cores_" + n) for n, g, a in zip(names, pieces, from_sibling)]
    received = _scatter_chips([b for _, b in partial], "grad_scatter_chips")
    mine = [_add_chips(f, r, "grad_add_chips_" + n) for n, (f, _), r in zip(names, partial, received)]
    other = _to_sibling(mine, "grad_to_sibling")
    small_sum = _allsum_small(_pack_small(d_small, loss), "small_allsum")

    grads, deltas, new_m, new_v = {}, {}, {}, {}
    for n, a, b in zip(names, mine, other):
        out = _adamw_shard(given[n][0], given["m_" + n][0], given["v_" + n][0], a, b, "adamw_" + n)
        grads[n], deltas[n], new_m[n], new_v[n] = [o[None] for o in out]
    delta_s, m_s, v_s = _adamw_small(_pack_small(small_w), small_sum,
                                     _pack_small({n: given["m_" + n] for n in small_w}),
                                     _pack_small({n: given["v_" + n] for n in small_w}), "adamw_small")

    loss_out = small_sum.reshape(-1)[SMALL_ELEMS]
    grads.update(_unpack_small(small_sum))
    deltas.update(_unpack_small(delta_s))
    new_m.update(_unpack_small(m_s))
    new_v.update(_unpack_small(v_s))
    return (loss_out, grad_x.reshape(1, s, D_MODEL), *[grads[n] for n in WEIGHT_ORDER],
            *[deltas[n] for n in WEIGHT_ORDER], *[new_m[n] for n in WEIGHT_ORDER], *[new_v[n] for n in WEIGHT_ORDER])
```

```python
import functools
import math

import jax
import jax.numpy as jnp
from jax import lax
from jax.experimental import pallas as pl
from jax.experimental.pallas import tpu as pltpu

F32 = jnp.float32
BF16 = jnp.bfloat16
MESH = pl.DeviceIdType.MESH

D_MODEL = 1024
SB_WIDTH = 512
MLA_HEADS = 8
MLA_Q_RANK = 384
MLA_KV_RANK = 256
MLA_ROPE = 32
MLA_QK = 96
D_FF = 2816
ROPE_THETA = 10000.0
EPS = 1e-6
LANES = 128
HALF = 64
ATT_TILE = 256
N_CHIPS = 4

ADAM_LR = 0.001
ADAM_B1 = 0.9
ADAM_B2 = 0.999
ADAM_EPS = 1e-08
ADAM_WD = 0.01
ADAM_STEP = 10

VMEM_LIMIT = 56 * 1024 * 1024

BIG = (
    ("w_in", 1024, 1064, 1),
    ("w_uq", 384, 192, 1),
    ("w_ukv", 256, 256, 1),
    ("w_proj_sb", 512, 256, 1),
    ("w_proj_mla", 512, 256, 1),
    ("w_out", 256, 1024, 0),
    ("w_gate_up", 1024, 1408, 1),
    ("w_down", 704, 1024, 0),
)
PACK_COLS = 1024
SMALL = (
    ("norm_mix_pre", 1024),
    ("norm_mix_post", 1024),
    ("b_gate", 2048),
    ("q_norm", 384),
    ("kv_norm", 256),
    ("norm_ffn_pre", 1024),
    ("norm_ffn_post", 1024),
)
SMALL_ELEMS = sum(n for _, n in SMALL)
SMALL_ROWS = 8
WEIGHT_ORDER = ("norm_mix_pre", "norm_mix_post", "w_in", "b_gate", "q_norm", "w_uq", "kv_norm", "w_ukv",
                "w_proj_sb", "w_proj_mla", "w_out", "norm_ffn_pre", "norm_ffn_post", "w_gate_up", "w_down")


def _params():
    return pltpu.CompilerParams(vmem_limit_bytes=VMEM_LIMIT)


def _tile(dim, pref):
    t = (min(pref, dim) // LANES) * LANES
    while t >= LANES:
        if dim % t == 0:
            return t
        t -= LANES
    return dim


def _mm(a, b, mode, out_dtype, name, tm=1024, tn=1024, tk=1024):
    if mode == "nn":
        (m, k), n = a.shape, b.shape[1]
    elif mode == "nt":
        (m, k), n = a.shape, b.shape[0]
    else:
        (k, m), n = a.shape, b.shape[1]
    tm, tn, tk = _tile(m, tm), _tile(n, tn), _tile(k, tk)
    nk = k // tk
    if mode == "tn":
        a_spec = pl.BlockSpec((tk, tm), lambda i, j, l: (l, i))
        lhs_dim = 0
    else:
        a_spec = pl.BlockSpec((tm, tk), lambda i, j, l: (i, l))
        lhs_dim = 1
    if mode == "nt":
        b_spec = pl.BlockSpec((tn, tk), lambda i, j, l: (j, l))
        rhs_dim = 1
    else:
        b_spec = pl.BlockSpec((tk, tn), lambda i, j, l: (l, j))
        rhs_dim = 0
    dims = (((lhs_dim,), (rhs_dim,)), ((), ()))

    def body(a_ref, b_ref, o_ref, acc_ref):
        l = pl.program_id(2)

        @pl.when(l == 0)
        def _():
            acc_ref[...] = jnp.zeros_like(acc_ref)

        acc_ref[...] += lax.dot_general(a_ref[...].astype(BF16), b_ref[...].astype(BF16), dims,
                                        preferred_element_type=F32)

        @pl.when(l == nk - 1)
        def _():
            o_ref[...] = acc_ref[...].astype(o_ref.dtype)

    return pl.pallas_call(
        body,
        name=name,
        grid=(m // tm, n // tn, nk),
        in_specs=[a_spec, b_spec],
        out_specs=pl.BlockSpec((tm, tn), lambda i, j, l: (i, j)),
        out_shape=jax.ShapeDtypeStruct((m, n), out_dtype),
        scratch_shapes=[pltpu.VMEM((tm, tn), F32)],
        compiler_params=_params(),
    )(a, b)


def _rowwise(fn, ins, outs, reds, tm, name):
    rows = ins[0].shape[0]
    tm = min(tm, rows)
    n_in, n_out = len(ins), len(outs)

    def spec(shape):
        if shape[0] == rows:
            return pl.BlockSpec((tm, shape[1]), lambda i: (i, 0))
        return pl.BlockSpec(shape, lambda i: (0, 0))

    def body(*refs):
        i = pl.program_id(0)
        vals = fn(*[r[...] for r in refs[:n_in]])
        for r, v in zip(refs[n_in:n_in + n_out], vals[:n_out]):
            r[...] = v.astype(r.dtype)
        for r, v in zip(refs[n_in + n_out:], vals[n_out:]):
            @pl.when(i == 0)
            def _():
                r[...] = jnp.zeros_like(r)

            r[...] += v

    out_shape = [jax.ShapeDtypeStruct((rows, d), dt) for d, dt in outs]
    out_shape += [jax.ShapeDtypeStruct((1, d), F32) for d in reds]
    out_specs = [spec(s.shape) for s in out_shape]
    return pl.pallas_call(
        body,
        name=name,
        grid=(rows // tm,),
        in_specs=[spec(a.shape) for a in ins],
        out_specs=out_specs,
        out_shape=out_shape,
        compiler_params=_params(),
    )(*ins)


def _colsum(v):
    return jnp.sum(v, axis=0, keepdims=True)


def _rms(v):
    r = lax.rsqrt(jnp.mean(v * v, axis=-1, keepdims=True) + EPS)
    return r, v * r


def _rms_bwd(dy, r, vhat, g):
    u = dy * g
    return r * (u - vhat * jnp.mean(u * vhat, axis=-1, keepdims=True))


def _sigmoid(t):
    return 1.0 / (1.0 + jnp.exp(-t))


def _lane(shape):
    return lax.broadcasted_iota(jnp.int32, shape, len(shape) - 1)


def _rope_swap(v):
    lane = _lane(v.shape)
    up = pltpu.roll(v, LANES - 16, 1)
    down = pltpu.roll(v, 16, 1)
    return jnp.where((lane >= 64) & (lane < 80), up, jnp.where((lane >= 80) & (lane < 96), down, 0.0))


SB_DEAD = -120.0


def _softplus(z):
    return jnp.maximum(z, 0.0) + jnp.log(1.0 + jnp.exp(-jnp.abs(z)))


def _split_dot(v, m2):
    hi = v.astype(BF16)
    lo = (v - hi.astype(F32)).astype(BF16)
    return jnp.dot(jnp.concatenate([hi, lo], axis=1), m2, preferred_element_type=F32)


def _dot_nt(a, b):
    return lax.dot_general(a, b, (((1,), (1,)), ((), ())), preferred_element_type=F32)


def _dot_tn(a, b):
    return lax.dot_general(a, b, (((0,), (0,)), ((), ())), preferred_element_type=F32)


def _sb_fwd(qkv, name):
    s = qkv.shape[0]
    t = min(ATT_TILE, s // 2)
    npair = SB_WIDTH // LANES

    def body(q_ref, k_ref, v_ref, o_ref, t_ref, first_ref):
        i = pl.program_id(1)
        lane = _lane((1, LANES))
        row = lax.broadcasted_iota(jnp.int32, (t, t), 0)
        col = lax.broadcasted_iota(jnp.int32, (t, t), 1)
        tri = col < row
        m_from = jnp.where(row >= col, 1.0, 0.0).astype(BF16)
        m_from = jnp.concatenate([m_from, m_from], axis=0)
        masks = (lane < HALF, lane >= HALF)
        qhs = [jnp.where(hm, q_ref[n * t:(n + 1) * t, :], jnp.zeros((t, LANES), BF16)) * 0.125
               for n in range(2) for hm in masks]

        def step(kb, carry, diags):
            rows = pl.ds(pl.multiple_of(kb * t, t), t)
            ks = k_ref[rows, :]
            vs = v_ref[rows, :]
            vhs = [jnp.where(hm, vs, jnp.zeros_like(vs)) for hm in masks]
            act = [ch for ch in range(4) if diags[ch // 2] is not None]
            zs = {ch: _dot_nt(qhs[ch], ks) for ch in act}
            lbs = {}
            for ch in act:
                lb = -_softplus(zs[ch])
                lbs[ch] = jnp.where(tri, lb, 0.0) if diags[ch // 2] else lb
            sums = {ch: _split_dot(lbs[ch], m_from) for ch in act}
            out = list(carry)
            for ch in act:
                c, acc = carry[ch]
                a = jnp.exp(zs[ch] + sums[ch] + c)
                if diags[ch // 2]:
                    a = jnp.where(tri, a, 0.0)
                acc = acc + jnp.dot(a.astype(BF16), vhs[ch % 2], preferred_element_type=F32)
                out[ch] = (c + sums[ch][:, 0:1], acc)
            return tuple(out)

        def live(carry):
            top = jnp.max(carry[0][0])
            for ch in range(1, 4):
                top = jnp.maximum(top, jnp.max(carry[ch][0]))
            return top

        init = (jnp.zeros((t, 1), F32), jnp.zeros((t, LANES), F32))
        carry = step(2 * i + 1, (init,) * 4, (None, True))
        carry = step(2 * i, carry, (True, False))
        blocks_left = 2 * i

        def more(state):
            n, top, _ = state
            return (n < blocks_left) & (top > SB_DEAD)

        def walk(state):
            n, _, cr = state
            cr = step(blocks_left - 1 - n, cr, (False, False))
            return n + 1, live(cr), cr

        n_done, _, carry = lax.while_loop(more, walk, (jnp.int32(0), live(carry), carry))
        first_ref[0, 0] = jnp.full((8, LANES), (blocks_left - n_done).astype(F32))
        for n in range(2):
            (c0, acc0), (c1, acc1) = carry[2 * n], carry[2 * n + 1]
            o_ref[n * t:(n + 1) * t, :] = acc0 + acc1
            t_ref[n * t:(n + 1) * t, :] = jnp.where(masks[0], c0, c1)

    blk = pl.BlockSpec((2 * t, LANES), lambda p, i: (i, p))
    steps = s // (2 * t)
    return pl.pallas_call(
        body,
        name=name,
        grid=(npair, steps),
        in_specs=[
            blk,
            pl.BlockSpec((s, LANES), lambda p, i: (0, npair + p)),
            pl.BlockSpec((s, LANES), lambda p, i: (0, 2 * npair + p)),
        ],
        out_specs=[blk, blk, pl.BlockSpec((1, 1, 8, LANES), lambda p, i: (p, i, 0, 0))],
        out_shape=[jax.ShapeDtypeStruct((s, SB_WIDTH), F32), jax.ShapeDtypeStruct((s, SB_WIDTH), F32),
                   jax.ShapeDtypeStruct((npair, steps, 8, LANES), F32)],
        compiler_params=_params(),
    )(qkv, qkv, qkv)


def _sb_bwd(qkv, do, tot, first, name):
    s = qkv.shape[0]
    t = min(ATT_TILE, s // 2)
    npair = SB_WIDTH // LANES

    def body(q_ref, k_ref, v_ref, do_ref, t_ref, first_ref, dq_ref, dk_ref, dv_ref):
        i = pl.program_id(1)

        @pl.when(i == 0)
        def _():
            dk_ref[...] = jnp.zeros_like(dk_ref)
            dv_ref[...] = jnp.zeros_like(dv_ref)

        lane = _lane((1, LANES))
        row = lax.broadcasted_iota(jnp.int32, (t, t), 0)
        col = lax.broadcasted_iota(jnp.int32, (t, t), 1)
        tri = col < row
        m_from = jnp.where(row >= col, 1.0, 0.0).astype(BF16)
        m_from = jnp.concatenate([m_from, m_from], axis=0)
        m_upto = jnp.where(row <= col, 1.0, 0.0).astype(BF16)
        m_upto = jnp.concatenate([m_upto, m_upto], axis=0)
        masks = (lane < HALF, lane >= HALF)
        zero_b = jnp.zeros((t, LANES), BF16)
        qhs, dohs, tots = [], [], []
        for n in range(2):
            rs = slice(n * t, (n + 1) * t)
            do_b = do_ref[rs, :].astype(BF16)
            for h, hm in enumerate(masks):
                qhs.append(jnp.where(hm, q_ref[rs, :], zero_b))
                dohs.append(jnp.where(hm, do_b, zero_b))
                tots.append(t_ref[rs, h * HALF:h * HALF + 1])

        def step(kb, carry, diags):
            rows = pl.ds(pl.multiple_of(kb * t, t), t)
            ks = k_ref[rows, :]
            vs = v_ref[rows, :]
            khs = [jnp.where(hm, ks, jnp.zeros_like(ks)) for hm in masks]
            act = [ch for ch in range(4) if diags[ch // 2] is not None]
            zs = {ch: _dot_nt(qhs[ch] * 0.125, ks) for ch in act}
            das = {ch: _dot_nt(dohs[ch], vs) for ch in act}
            lbs, betas = {}, {}
            for ch in act:
                sp = _softplus(zs[ch])
                betas[ch] = jnp.exp(zs[ch] - sp)
                lbs[ch] = jnp.where(tri, -sp, 0.0) if diags[ch // 2] else -sp
            sums = {ch: _split_dot(lbs[ch], m_from) for ch in act}
            avs, es, befores = {}, {}, {}
            for ch in act:
                before = carry[ch][0]
                total = sums[ch][:, 0:1]
                a = jnp.exp(zs[ch] + sums[ch] + (tots[ch] - before - total))
                if diags[ch // 2]:
                    a = jnp.where(tri, a, 0.0)
                avs[ch] = a.astype(BF16)
                es[ch] = a * das[ch]
                befores[ch] = before + total
            e_sums = {ch: _split_dot(es[ch], m_upto) for ch in act}
            dzs, e_befores = {}, {}
            for ch in act:
                e_upto = e_sums[ch] + carry[ch][1]
                dz = es[ch] - betas[ch] * e_upto
                if diags[ch // 2]:
                    dz = jnp.where(tri, dz, 0.0)
                dzs[ch] = (dz * 0.125).astype(BF16)
                e_befores[ch] = e_upto[:, t - 1:t]
            out = list(carry)
            dk = jnp.zeros((t, LANES), F32)
            dv = jnp.zeros((t, LANES), F32)
            for ch in act:
                dq = carry[ch][2] + jnp.dot(dzs[ch], khs[ch % 2], preferred_element_type=F32)
                dk = dk + _dot_tn(dzs[ch], qhs[ch])
                dv = dv + _dot_tn(avs[ch], dohs[ch])
                out[ch] = (befores[ch], e_befores[ch], dq)
            dk_ref[rows, :] += dk
            dv_ref[rows, :] += dv
            return tuple(out)

        zero = jnp.zeros((t, 1), F32)
        init = (zero, zero, jnp.zeros((t, LANES), F32))
        start = jnp.clip(jnp.max(first_ref[0, 0]).astype(jnp.int32), 0, 2 * i)
        carry = lax.fori_loop(start, 2 * i, lambda kb, cr: step(kb, cr, (False, False)), (init,) * 4)
        carry = step(2 * i, carry, (True, False))
        carry = step(2 * i + 1, carry, (None, True))
        for n in range(2):
            dq_ref[n * t:(n + 1) * t, :] = carry[2 * n][2] + carry[2 * n + 1][2]

    blk = pl.BlockSpec((2 * t, LANES), lambda p, i: (i, p))
    full = pl.BlockSpec((s, LANES), lambda p, i: (0, p))
    out = jax.ShapeDtypeStruct((s, SB_WIDTH), F32)
    return pl.pallas_call(
        body,
        name=name,
        grid=(npair, s // (2 * t)),
        in_specs=[
            blk,
            pl.BlockSpec((s, LANES), lambda p, i: (0, npair + p)),
            pl.BlockSpec((s, LANES), lambda p, i: (0, 2 * npair + p)),
            blk,
            blk,
            pl.BlockSpec((1, 1, 8, LANES), lambda p, i: (p, i, 0, 0)),
        ],
        out_specs=[blk, full, full],
        out_shape=[out, out, out],
        compiler_params=_params(),
    )(qkv, qkv, qkv, do, tot, first)


MLA_SCALE = 1.0 / math.sqrt(MLA_QK)
NEG = -1e30
MLA_QBLOCKS = 4


def _diag_plan(nq):
    return [tuple(None if n < j else n == j for n in range(nq)) for j in range(nq)]


def _mla_fwd(qc, kc, vp, name):
    s = qc.shape[0]
    nq = MLA_QBLOCKS
    t = min(ATT_TILE, s // nq)

    def body(q_ref, k_ref, v_ref, o_ref, lse_ref):
        i = pl.program_id(1)
        row = lax.broadcasted_iota(jnp.int32, (t, t), 0)
        col = lax.broadcasted_iota(jnp.int32, (t, t), 1)
        keep = col <= row
        qs = [q_ref[n * t:(n + 1) * t, :] for n in range(nq)]

        def step(kb, carry, diags):
            rows = pl.ds(pl.multiple_of(kb * t, t), t)
            ks = k_ref[rows, :]
            vs = v_ref[rows, :]
            act = [n for n in range(nq) if diags[n] is not None]
            scs = {n: _dot_nt(qs[n], ks) * MLA_SCALE for n in act}
            mid = {}
            for n in act:
                m, l, acc = carry[n]
                sc = jnp.where(keep, scs[n], NEG) if diags[n] else scs[n]
                m_new = jnp.maximum(m, jnp.max(sc, axis=-1, keepdims=True))
                alpha = jnp.exp(m - m_new)
                p = jnp.exp(sc - m_new)
                mid[n] = (m_new, alpha * l + jnp.sum(p, axis=-1, keepdims=True), alpha * acc, p.astype(BF16))
            out = list(carry)
            for n in act:
                m, l, acc, p = mid[n]
                out[n] = (m, l, acc + jnp.dot(p, vs, preferred_element_type=F32))
            return tuple(out)

        init = (jnp.full((t, 1), NEG, F32), jnp.zeros((t, 1), F32), jnp.zeros((t, LANES), F32))
        carry = lax.fori_loop(0, nq * i, lambda kb, cr: step(kb, cr, (False,) * nq), (init,) * nq)
        for j, diags in enumerate(_diag_plan(nq)):
            carry = step(nq * i + j, carry, diags)
        for n, (m, l, acc) in enumerate(carry):
            o_ref[n * t:(n + 1) * t, :] = acc / l
            lse_ref[n * t:(n + 1) * t, :] = jnp.broadcast_to(m + jnp.log(l), (t, LANES))

    blk = pl.BlockSpec((nq * t, LANES), lambda h, i: (i, h))
    full = pl.BlockSpec((s, LANES), lambda h, i: (0, h))
    out = jax.ShapeDtypeStruct((s, MLA_HEADS * LANES), F32)
    return pl.pallas_call(
        body,
        name=name,
        grid=(MLA_HEADS, s // (nq * t)),
        in_specs=[blk, full, full],
        out_specs=[blk, blk],
        out_shape=[out, out],
        compiler_params=_params(),
    )(qc, kc, vp)


def _mla_bwd(qc, kc, vp, do, o, lse, name):
    s = qc.shape[0]
    nq = MLA_QBLOCKS
    t = min(ATT_TILE, s // nq)

    def body(q_ref, k_ref, v_ref, do_ref, o_ref, lse_ref, dq_ref, dk_ref, dv_ref):
        i = pl.program_id(1)

        @pl.when(i == 0)
        def _():
            dk_ref[...] = jnp.zeros_like(dk_ref)
            dv_ref[...] = jnp.zeros_like(dv_ref)

        row = lax.broadcasted_iota(jnp.int32, (t, t), 0)
        col = lax.broadcasted_iota(jnp.int32, (t, t), 1)
        keep = col <= row

        def block(n):
            rs = slice(n * t, (n + 1) * t)
            do_f = do_ref[rs, :]
            delta = jnp.sum(do_f * o_ref[rs, :], axis=-1, keepdims=True)
            return q_ref[rs, :], do_f.astype(BF16), delta, lse_ref[rs, 0:1]

        blocks = [block(n) for n in range(nq)]

        def step(kb, dqs, diags):
            rows = pl.ds(pl.multiple_of(kb * t, t), t)
            ks = k_ref[rows, :]
            vs = v_ref[rows, :]
            act = [n for n in range(nq) if diags[n] is not None]
            scs = {n: _dot_nt(blocks[n][0], ks) * MLA_SCALE for n in act}
            dps = {n: _dot_nt(blocks[n][1], vs) for n in act}
            ps, dss = {}, {}
            for n in act:
                _, _, delta, lse_v = blocks[n]
                p = jnp.exp(scs[n] - lse_v)
                if diags[n]:
                    p = jnp.where(keep, p, 0.0)
                dss[n] = (p * (dps[n] - delta) * MLA_SCALE).astype(BF16)
                ps[n] = p.astype(BF16)
            dk = jnp.zeros((t, LANES), F32)
            dv = jnp.zeros((t, LANES), F32)
            out = list(dqs)
            for n in act:
                q, do_b, _, _ = blocks[n]
                out[n] = dqs[n] + jnp.dot(dss[n], ks, preferred_element_type=F32)
                dk = dk + _dot_tn(dss[n], q)
                dv = dv + _dot_tn(ps[n], do_b)
            dk_ref[rows, :] += dk
            dv_ref[rows, :] += dv
            return tuple(out)

        zero = jnp.zeros((t, LANES), F32)
        dqs = lax.fori_loop(0, nq * i, lambda kb, cr: step(kb, cr, (False,) * nq), (zero,) * nq)
        for j, diags in enumerate(_diag_plan(nq)):
            dqs = step(nq * i + j, dqs, diags)
        for n in range(nq):
            dq_ref[n * t:(n + 1) * t, :] = dqs[n]

    blk = pl.BlockSpec((nq * t, LANES), lambda h, i: (i, h))
    full = pl.BlockSpec((s, LANES), lambda h, i: (0, h))
    out = jax.ShapeDtypeStruct((s, MLA_HEADS * LANES), F32)
    return pl.pallas_call(
        body,
        name=name,
        grid=(MLA_HEADS, s // (nq * t)),
        in_specs=[blk, full, full, blk, blk, blk],
        out_specs=[blk, full, full],
        out_shape=[out, out, out],
        compiler_params=_params(),
    )(qc, kc, vp, do, o, lse)


def _position():
    return lax.axis_index("x"), lax.axis_index("y"), lax.axis_index("c")


def _other_chips(x, y):
    return [(1 - x, y), (x, 1 - y), (1 - x, 1 - y)]


HBM_SPEC = pl.BlockSpec(memory_space=pltpu.HBM)


def _remote(src, dst, send_sem, recv_sem, device):
    return pltpu.make_async_remote_copy(src_ref=src, dst_ref=dst, send_sem=send_sem, recv_sem=recv_sem,
                                        device_id=device, device_id_type=MESH)


def _dma_sems(n, count):
    return [pltpu.SemaphoreType.DMA((n,)) for _ in range(count)]


def _gather_weights(ws, name):
    na = len(ws)

    def body(*refs):
        srcs, outs = refs[:na], refs[na:2 * na]
        s1, r1, s2, r2, s3, r3, s4, r4 = refs[2 * na:]
        x, y, c = _position()
        me = 2 * x + y
        px, py = jnp.where(c == 1, 1 - x, x), jnp.where(c == 1, y, 1 - y)
        qx, qy = jnp.where(c == 1, x, 1 - x), jnp.where(c == 1, 1 - y, y)
        p_idx, q_idx, d_idx = 2 * px + py, 2 * qx + qy, 2 * (1 - x) + (1 - y)
        partner, across, sibling = (px, py, c), (qx, qy, c), (x, y, 1 - c)
        my_half, other_half = 1 - c, c

        def half(i, slot, h):
            kh = ws[i].shape[0] // 2
            return outs[i].at[slot, pl.ds(h * kh, kh), :]

        sends = [_remote(srcs[i], outs[i].at[me], s1.at[i], r1.at[i], partner) for i in range(na)]
        for cp in sends:
            cp.start()
        for i in range(na):
            _remote(srcs[i], outs[i].at[p_idx], s1.at[i], r1.at[i], partner).wait_recv()
            onward = [_remote(half(i, p_idx, my_half), half(i, p_idx, my_half), s2.at[i], r2.at[i], across),
                      _remote(outs[i].at[p_idx], outs[i].at[p_idx], s3.at[i], r3.at[i], sibling)]
            for cp in onward:
                cp.start()
            sends += onward
        for i in range(na):
            _remote(half(i, d_idx, my_half), half(i, d_idx, my_half), s2.at[i], r2.at[i], across).wait_recv()
            cp = _remote(half(i, d_idx, my_half), half(i, d_idx, my_half), s4.at[i], r4.at[i], sibling)
            cp.start()
            sends.append(cp)
        for i in range(na):
            _remote(outs[i].at[q_idx], outs[i].at[q_idx], s3.at[i], r3.at[i], sibling).wait_recv()
            _remote(half(i, d_idx, other_half), half(i, d_idx, other_half), s4.at[i], r4.at[i], sibling).wait_recv()
        for cp in sends:
            cp.wait_send()

    return pl.pallas_call(
        body,
        name=name,
        in_specs=[HBM_SPEC] * na,
        out_specs=[HBM_SPEC] * na,
        out_shape=[jax.ShapeDtypeStruct((N_CHIPS,) + w.shape, w.dtype) for w in ws],
        scratch_shapes=_dma_sems(na, 8),
    )(*ws)


def _swap_halves(gs, name):
    na = len(gs)

    def body(*refs):
        srcs, outs, send_sems, recv_sems = refs[:na], refs[na:2 * na], refs[2 * na], refs[2 * na + 1]
        x, y, c = _position()
        cps = []
        for i in range(na):
            kh = gs[i].shape[1] // 2
            cps.append(_remote(srcs[i].at[:, pl.ds((1 - c) * kh, kh), :], outs[i], send_sems.at[i], recv_sems.at[i],
                               (x, y, 1 - c)))
            cps[-1].start()
        for cp in cps:
            cp.wait()

    return pl.pallas_call(
        body,
        name=name,
        in_specs=[HBM_SPEC] * na,
        out_specs=[HBM_SPEC] * na,
        out_shape=[jax.ShapeDtypeStruct((g.shape[0], g.shape[1] // 2, g.shape[2]), g.dtype) for g in gs],
        scratch_shapes=_dma_sems(na, 2),
    )(*gs)


def _scatter_chips(parts, name):
    na = len(parts)

    def body(*refs):
        srcs, outs, send_sems, recv_sems = refs[:na], refs[na:2 * na], refs[2 * na], refs[2 * na + 1]
        x, y, c = _position()
        me = 2 * x + y
        sends = []
        for i in range(na):
            for k, (px, py) in enumerate(_other_chips(x, y)):
                cp = _remote(srcs[i].at[2 * px + py], outs[i].at[me], send_sems.at[3 * i + k],
                             recv_sems.at[3 * i + k], (px, py, c))
                cp.start()
                sends.append(cp)
        for i in range(na):
            for k, (px, py) in enumerate(_other_chips(x, y)):
                _remote(srcs[i].at[me], outs[i].at[2 * px + py], send_sems.at[3 * i + k], recv_sems.at[3 * i + k],
                        (px, py, c)).wait_recv()
        for cp in sends:
            cp.wait_send()

    return pl.pallas_call(
        body,
        name=name,
        in_specs=[HBM_SPEC] * na,
        out_specs=[HBM_SPEC] * na,
        out_shape=[jax.ShapeDtypeStruct(p.shape, p.dtype) for p in parts],
        scratch_shapes=_dma_sems(3 * na, 2),
    )(*parts)


def _to_sibling(halves, name):
    na = len(halves)

    def body(*refs):
        srcs, outs, send_sems, recv_sems = refs[:na], refs[na:2 * na], refs[2 * na], refs[2 * na + 1]
        x, y, c = _position()
        cps = [_remote(srcs[i], outs[i], send_sems.at[i], recv_sems.at[i], (x, y, 1 - c)) for i in range(na)]
        for cp in cps:
            cp.start()
        for cp in cps:
            cp.wait()

    return pl.pallas_call(
        body,
        name=name,
        in_specs=[HBM_SPEC] * na,
        out_specs=[HBM_SPEC] * na,
        out_shape=[jax.ShapeDtypeStruct(h.shape, h.dtype) for h in halves],
        scratch_shapes=_dma_sems(na, 2),
    )(*halves)


def _allsum_small(v, name):
    shape = v.shape

    def body(v_ref, o_ref, buf, send_sems, recv_sems):
        x, y, c = _position()
        me = 4 * x + 2 * y + c
        buf[me] = v_ref[...]
        peers = []
        for k in range(1, 8):
            fx, fy, fc = (k >> 2) & 1, (k >> 1) & 1, k & 1
            peers.append((jnp.where(fx == 1, 1 - x, x), jnp.where(fy == 1, 1 - y, y), jnp.where(fc == 1, 1 - c, c)))
        sends = []
        for k, peer in enumerate(peers):
            cp = pltpu.make_async_remote_copy(src_ref=v_ref, dst_ref=buf.at[me], send_sem=send_sems.at[k],
                                              recv_sem=recv_sems.at[k], device_id=peer, device_id_type=MESH)
            cp.start()
            sends.append(cp)
        for k, (px, py, pc) in enumerate(peers):
            pltpu.make_async_remote_copy(src_ref=v_ref, dst_ref=buf.at[4 * px + 2 * py + pc],
                                         send_sem=send_sems.at[k], recv_sem=recv_sems.at[k],
                                         device_id=(px, py, pc), device_id_type=MESH).wait_recv()
        for cp in sends:
            cp.wait_send()
        acc = buf[0]
        for d in range(1, 8):
            acc = acc + buf[d]
        o_ref[...] = acc

    return pl.pallas_call(
        body,
        name=name,
        in_specs=[pl.BlockSpec(memory_space=pltpu.VMEM)],
        out_specs=pl.BlockSpec(memory_space=pltpu.VMEM),
        out_shape=jax.ShapeDtypeStruct(shape, F32),
        scratch_shapes=[pltpu.VMEM((8,) + shape, F32), pltpu.SemaphoreType.DMA((7,)), pltpu.SemaphoreType.DMA((7,))],
    )(v)


def _row_tile(h):
    for t in range(256, 0, -16):
        if h % t == 0:
            return t
    return h


def _add_cores(g, a, name):
    n4, k, n = g.shape
    kh = k // 2
    th = _row_tile(kh)

    def body(g_ref, a_ref, f_ref, b_ref):
        c = lax.axis_index("c")
        total = jnp.where(c == 0, g_ref[0, 0], g_ref[0, 1]) + a_ref[0]
        f_ref[0] = total
        b_ref[0] = total.astype(BF16)

    spec = pl.BlockSpec((1, th, n), lambda j, r: (j, r, 0))
    return pl.pallas_call(
        body,
        name=name,
        grid=(n4, kh // th),
        in_specs=[pl.BlockSpec((1, 2, th, n), lambda j, r: (j, 0, r, 0)), spec],
        out_specs=[spec, spec],
        out_shape=[jax.ShapeDtypeStruct((n4, kh, n), F32), jax.ShapeDtypeStruct((n4, kh, n), BF16)],
        compiler_params=_params(),
    )(g.reshape(n4, 2, kh, n), a)


def _add_chips(pf, b, name):
    n4, h, n = pf.shape
    th = _row_tile(h)

    def body(pf_ref, b_ref, o_ref):
        me = 2 * lax.axis_index("x") + lax.axis_index("y")
        terms = [jnp.where(me == j, pf_ref[j], b_ref[j].astype(F32)) for j in range(n4)]
        o_ref[...] = ((terms[0] + terms[1]) + terms[2]) + terms[3]

    spec = pl.BlockSpec((n4, th, n), lambda r: (0, r, 0))
    return pl.pallas_call(
        body,
        name=name,
        grid=(h // th,),
        in_specs=[spec, spec],
        out_specs=pl.BlockSpec((th, n), lambda r: (r, 0)),
        out_shape=jax.ShapeDtypeStruct((h, n), F32),
        compiler_params=_params(),
    )(pf, b)


def _adam_math(w, g, m, v):
    m = ADAM_B1 * m + (1.0 - ADAM_B1) * g
    v = ADAM_B2 * v + (1.0 - ADAM_B2) * (g * g)
    m_hat = m / (1.0 - ADAM_B1 ** ADAM_STEP)
    v_hat = v / (1.0 - ADAM_B2 ** ADAM_STEP)
    delta = -ADAM_LR * (m_hat / (jnp.sqrt(v_hat) + ADAM_EPS) + ADAM_WD * w)
    return delta, m, v


def _adamw_shard(w, m, v, mine, other, name):
    k, n = w.shape
    kh = k // 2
    th = _row_tile(kh)
    nb = kh // th

    def body(w_ref, m_ref, v_ref, a_ref, b_ref, g_out, d_out, m_out, v_out):
        g = jnp.where(pl.program_id(0) == lax.axis_index("c"), a_ref[...], b_ref[...])
        g_out[...] = g
        d_out[...], m_out[...], v_out[...] = _adam_math(w_ref[...], g, m_ref[...], v_ref[...])

    full = pl.BlockSpec((th, n), lambda hh, r: (hh * nb + r, 0))
    half = pl.BlockSpec((th, n), lambda hh, r: (r, 0))
    return pl.pallas_call(
        body,
        name=name,
        grid=(2, nb),
        in_specs=[full, full, full, half, half],
        out_specs=[full] * 4,
        out_shape=[jax.ShapeDtypeStruct((k, n), F32)] * 4,
        compiler_params=_params(),
    )(w, m, v, mine, other)


def _adamw_small(w, g, m, v, name):
    cols = w.shape[1]
    return _rowwise(_adam_math, [w, g, m, v], [(cols, F32)] * 3, [], w.shape[0], name)


def _full_matrices(own, gathered):
    me = 2 * lax.axis_index("x") + lax.axis_index("y")
    out = {}
    for (n, _, _, axis), w, g in zip(BIG, own, gathered):
        out[n] = jnp.concatenate([jnp.where(me == j, w, g[j]) for j in range(N_CHIPS)], axis=axis)
    return out


def _split_for_chips(full):
    out = []
    for n, r, c, axis in BIG:
        g = full[n]
        out.append(jnp.stack([g[:, j * c:(j + 1) * c] for j in range(N_CHIPS)]) if axis == 1
                   else g.reshape(N_CHIPS, r, c))
    return out


def _pack_small(vecs, extra=None):
    flat = jnp.concatenate([vecs[n].reshape(-1) for n, _ in SMALL] + ([extra.reshape(-1)] if extra is not None else []))
    flat = jnp.pad(flat, (0, SMALL_ROWS * PACK_COLS - flat.shape[0]))
    return flat.reshape(SMALL_ROWS, PACK_COLS)


def _unpack_small(pack):
    flat = pack.reshape(-1)
    out, off = {}, 0
    for n, k in SMALL:
        out[n] = flat[off:off + k].reshape(1, k)
        off += k
    return out


def _forward_backward(x, pos, target, sm, w):
    s = x.shape[0]
    w_in = w["w_in"]
    w_a = w_in[:, 0:1536]
    zeros = functools.partial(jnp.zeros, dtype=BF16)
    w_b = jnp.concatenate([w_in[:, 1536:2176], zeros((D_MODEL, 64)), w_in[:, 2176:2208], zeros((D_MODEL, 32))], axis=1)
    w_g = w_in[:, 2208:4256]
    w_uq = jnp.pad(w["w_uq"].reshape(MLA_Q_RANK, MLA_HEADS, MLA_QK), ((0, 0), (0, 0), (0, LANES - MLA_QK)))
    w_uq = w_uq.reshape(MLA_Q_RANK, MLA_HEADS * LANES)
    w_ukv = w["w_ukv"]
    w_pm = jnp.pad(w["w_proj_mla"].reshape(MLA_HEADS, HALF, D_MODEL), ((0, 0), (HALF, 0), (0, 0)))
    w_pm = w_pm.reshape(MLA_HEADS * LANES, D_MODEL)
    w_ps, w_out, w_gu, w_dn = w["w_proj_sb"], w["w_out"], w["w_gate_up"], w["w_down"]
    inv_freq = ROPE_THETA ** (-jnp.arange(0, MLA_ROPE, 2, dtype=F32) / MLA_ROPE)
    inv_lane = jnp.concatenate([jnp.zeros((HALF,), F32), inv_freq, inv_freq, jnp.zeros((32,), F32)]).reshape(1, LANES)
    g_pre, g_post, b_gate = sm["norm_mix_pre"], sm["norm_mix_post"], sm["b_gate"]
    g_q, g_kv, g_fpre, g_fpost = sm["q_norm"], sm["kv_norm"], sm["norm_ffn_pre"], sm["norm_ffn_post"]

    def f_h1(x, g):
        return (_rms(x)[1] * g,)

    (h1,) = _rowwise(f_h1, [x, g_pre], [(D_MODEL, BF16)], [], 256, "h1")
    qkv = _mm(h1, w_a, "nn", BF16, "proj_sb")
    lat = _mm(h1, w_b, "nn", F32, "proj_lat")
    gl = _mm(h1, w_g, "nn", F32, "proj_gate")

    def f_lat(lat, gq, gkv, pos, inv):
        cqn = _rms(lat[:, 0:MLA_Q_RANK])[1] * gq
        ckvn = _rms(lat[:, MLA_Q_RANK:MLA_Q_RANK + MLA_KV_RANK])[1] * gkv
        kr = lat[:, 640:768]
        ang = pos.astype(F32) * inv
        lane = _lane(ang.shape)
        cosv, sinv = jnp.cos(ang), jnp.sin(ang)
        rope_c = jnp.where(lane < 64, 1.0, jnp.where(lane < 96, cosv, 0.0))
        rope_s = jnp.where((lane >= 64) & (lane < 80), -sinv, jnp.where((lane >= 80) & (lane < 96), sinv, 0.0))
        krr = jnp.where(lane >= 64, kr * rope_c, 0.0) + _rope_swap(kr) * rope_s
        return cqn, ckvn, krr, rope_c, rope_s

    cqn, ckvn, krr, rope_c, rope_s = _rowwise(
        f_lat, [lat, g_q, g_kv, pos, inv_lane],
        [(MLA_Q_RANK, BF16), (MLA_KV_RANK, BF16), (LANES, F32), (LANES, F32), (LANES, F32)], [], 256, "lat_norm")
    q_raw = _mm(cqn, w_uq, "nn", F32, "q_up")
    kv = _mm(ckvn, w_ukv, "nn", F32, "kv_up")

    def f_heads(q_raw, kv, krr, rc, rs):
        lane = _lane(krr.shape)
        qs, ks, vs = [], [], []
        for h in range(MLA_HEADS):
            qh = q_raw[:, h * LANES:(h + 1) * LANES]
            kvh = kv[:, h * LANES:(h + 1) * LANES]
            qs.append(qh * rc + _rope_swap(qh) * rs)
            ks.append(jnp.where(lane < HALF, kvh, krr))
            vs.append(jnp.where(lane >= HALF, kvh, 0.0))
        return jnp.concatenate(qs, axis=1), jnp.concatenate(ks, axis=1), jnp.concatenate(vs, axis=1)

    wide = MLA_HEADS * LANES
    qc, kc, vp = _rowwise(f_heads, [q_raw, kv, krr, rope_c, rope_s], [(wide, BF16)] * 3, [], 256, "mla_heads")
    o_sb, sb_tot, sb_first = _sb_fwd(qkv, "sb_fwd")
    o_mla, lse = _mla_fwd(qc, kc, vp, "mla_fwd")
    m_sb = _mm(o_sb, w_ps, "nn", F32, "proj_o_sb")
    m_mla = _mm(o_mla, w_pm, "nn", F32, "proj_o_mla")

    def f_merge(m_sb, m_mla, gl, b):
        g = _sigmoid(gl + b)
        return (g[:, :D_MODEL] * m_sb + g[:, D_MODEL:] * m_mla,)

    (merged,) = _rowwise(f_merge, [m_sb, m_mla, gl, b_gate], [(D_MODEL, BF16)], [], 256, "merge")
    y = _mm(merged, w_out, "nn", F32, "mix_out")

    def f_res1(x, y, g2, g3):
        x2 = x + _rms(y)[1] * g2
        return x2, _rms(x2)[1] * g3

    x2, h2 = _rowwise(f_res1, [x, y, g_post, g_fpre], [(D_MODEL, F32), (D_MODEL, BF16)], [], 256, "res1")

    gu = _mm(h2, w_gu, "nn", F32, "ffn_up")

    def f_act(gu):
        g, u = gu[:, :D_FF], gu[:, D_FF:]
        return (g * _sigmoid(g) * u,)

    (act,) = _rowwise(f_act, [gu], [(D_FF, BF16)], [], 128, "ffn_act")
    f = _mm(act, w_dn, "nn", F32, "ffn_down")

    def f_loss(x2, f, g4, tgt):
        r, fhat = _rms(f)
        err = x2 + fhat * g4 - tgt
        loss = 0.5 * jnp.sum(jnp.sum(err * err, axis=-1, keepdims=True), axis=0, keepdims=True) / D_MODEL
        dout = err / D_MODEL
        return dout, _rms_bwd(dout, r, fhat, g4), loss, _colsum(dout * fhat)

    dout, df, loss, d_fpost = _rowwise(f_loss, [x2, f, g_fpost, target], [(D_MODEL, F32), (D_MODEL, BF16)],
                                       [1, D_MODEL], 256, "loss")

    dact = _mm(df, w_dn, "nt", F32, "d_act")
    dw_dn = _mm(act, df, "tn", F32, "dw_down")

    def f_dact(gu, dact):
        g, u = gu[:, :D_FF], gu[:, D_FF:]
        sg = _sigmoid(g)
        dg = dact * u * (sg * (1.0 + g * (1.0 - sg)))
        return (jnp.concatenate([dg, dact * (g * sg)], axis=1),)

    (dgu,) = _rowwise(f_dact, [gu, dact], [(2 * D_FF, BF16)], [], 128, "d_gu")
    dh2 = _mm(dgu, w_gu, "nt", F32, "d_h2")
    dw_gu = _mm(h2, dgu, "tn", F32, "dw_gate_up")

    def f_dres1(dout, dh2, x2, y, g3, g2):
        r3, x2hat = _rms(x2)
        dx2 = dout + _rms_bwd(dh2, r3, x2hat, g3)
        r2, yhat = _rms(y)
        return dx2, _rms_bwd(dx2, r2, yhat, g2), _colsum(dh2 * x2hat), _colsum(dx2 * yhat)

    dx2, dy, d_fpre, d_post = _rowwise(f_dres1, [dout, dh2, x2, y, g_fpre, g_post],
                                       [(D_MODEL, F32), (D_MODEL, BF16)], [D_MODEL, D_MODEL], 256, "d_res1")

    dmerged = _mm(dy, w_out, "nt", F32, "d_merged")
    dw_out = _mm(merged, dy, "tn", F32, "dw_out")

    def f_dmerge(dm, m_sb, m_mla, gl, b):
        g = _sigmoid(gl + b)
        g0, g1 = g[:, :D_MODEL], g[:, D_MODEL:]
        dgl = jnp.concatenate([dm * m_sb * g0 * (1.0 - g0), dm * m_mla * g1 * (1.0 - g1)], axis=1)
        return dm * g0, dm * g1, dgl, _colsum(dgl)

    dm_sb, dm_mla, dgl, d_bgate = _rowwise(f_dmerge, [dmerged, m_sb, m_mla, gl, b_gate],
                                           [(D_MODEL, BF16), (D_MODEL, BF16), (2 * D_MODEL, BF16)], [2 * D_MODEL],
                                           256, "d_merge")
    do_sb = _mm(dm_sb, w_ps, "nt", BF16, "d_o_sb")
    dw_ps = _mm(o_sb, dm_sb, "tn", F32, "dw_proj_sb")
    do_mla = _mm(dm_mla, w_pm, "nt", F32, "d_o_mla")
    dw_pm = _mm(o_mla, dm_mla, "tn", F32, "dw_proj_mla")

    dq_sb, dk_sb, dv_sb = _sb_bwd(qkv, do_sb, sb_tot, sb_first, "sb_bwd")
    dqc, dkc, dvp = _mla_bwd(qc, kc, vp, do_mla, o_mla, lse, "mla_bwd")

    def f_dheads(dqc, dkc, dvp, rc, rs):
        lane = _lane(rc.shape)
        dqs, dkvs = [], []
        dkr = jnp.zeros(rc.shape, F32)
        for h in range(MLA_HEADS):
            dq = dqc[:, h * LANES:(h + 1) * LANES]
            dk = dkc[:, h * LANES:(h + 1) * LANES]
            dqs.append(dq * rc + _rope_swap(dq * rs))
            dkvs.append(jnp.where(lane < HALF, dk, dvp[:, h * LANES:(h + 1) * LANES]))
            dkr = dkr + jnp.where(lane >= HALF, dk, 0.0)
        dkr = dkr * rc + _rope_swap(dkr * rs)
        return jnp.concatenate(dqs, axis=1), jnp.concatenate(dkvs, axis=1), dkr

    dq_raw, dkv, dkr = _rowwise(f_dheads, [dqc, dkc, dvp, rope_c, rope_s],
                                [(wide, BF16), (wide, BF16), (LANES, F32)], [], 256, "d_mla_heads")
    dcqn = _mm(dq_raw, w_uq, "nt", F32, "d_cqn")
    dw_uq = _mm(cqn, dq_raw, "tn", F32, "dw_uq")
    dckvn = _mm(dkv, w_ukv, "nt", F32, "d_ckvn")
    dw_ukv = _mm(ckvn, dkv, "tn", F32, "dw_ukv")

    def f_dlat(lat, dcqn, dckvn, dkr, gq, gkv):
        rq, cqhat = _rms(lat[:, 0:MLA_Q_RANK])
        rkv, ckvhat = _rms(lat[:, MLA_Q_RANK:MLA_Q_RANK + MLA_KV_RANK])
        dlat = jnp.concatenate([_rms_bwd(dcqn, rq, cqhat, gq), _rms_bwd(dckvn, rkv, ckvhat, gkv), dkr], axis=1)
        return dlat, _colsum(dcqn * cqhat), _colsum(dckvn * ckvhat)

    dlat, d_gq, d_gkv = _rowwise(f_dlat, [lat, dcqn, dckvn, dkr, g_q, g_kv], [(768, BF16)],
                                 [MLA_Q_RANK, MLA_KV_RANK], 256, "d_lat")

    dqkv = jnp.concatenate([dq_sb, dk_sb, dv_sb], axis=1).astype(BF16)
    dh1_a = _mm(dqkv, w_a, "nt", F32, "d_h1_sb")
    dh1_b = _mm(dlat, w_b, "nt", F32, "d_h1_lat")
    dh1_g = _mm(dgl, w_g, "nt", F32, "d_h1_gate")
    dw_a = _mm(h1, dqkv, "tn", F32, "dw_in_sb")
    dw_b = _mm(h1, dlat, "tn", F32, "dw_in_lat")
    dw_g = _mm(h1, dgl, "tn", F32, "dw_in_gate")

    def f_dx(x, dx2, da, db, dg, g1):
        r, xhat = _rms(x)
        dh1 = da + db + dg
        return dx2 + _rms_bwd(dh1, r, xhat, g1), _colsum(dh1 * xhat)

    grad_x, d_pre = _rowwise(f_dx, [x, dx2, dh1_a, dh1_b, dh1_g, g_pre], [(D_MODEL, F32)], [D_MODEL], 256, "d_x")

    small = {"norm_mix_pre": d_pre, "norm_mix_post": d_post, "b_gate": d_bgate, "q_norm": d_gq, "kv_norm": d_gkv,
             "norm_ffn_pre": d_fpre, "norm_ffn_post": d_fpost}
    big = {
        "w_in": jnp.concatenate([dw_a, dw_b[:, 0:640], dw_b[:, 704:736], dw_g], axis=1),
        "w_uq": dw_uq.reshape(MLA_Q_RANK, MLA_HEADS, LANES)[:, :, :MLA_QK].reshape(MLA_Q_RANK, MLA_HEADS * MLA_QK),
        "w_ukv": dw_ukv,
        "w_proj_sb": dw_ps,
        "w_proj_mla": dw_pm.reshape(MLA_HEADS, LANES, D_MODEL)[:, HALF:, :].reshape(MLA_HEADS * HALF, D_MODEL),
        "w_out": dw_out,
        "w_gate_up": dw_gu,
        "w_down": dw_dn,
    }
    return loss, grad_x, small, big


def kernel(x, positions, norm_mix_pre, norm_mix_post, w_in, b_gate, q_norm, w_uq, kv_norm, w_ukv, w_proj_sb, w_proj_mla, w_out, norm_ffn_pre, norm_ffn_post, w_gate_up, w_down, loss_target, m_norm_mix_pre, m_norm_mix_post, m_w_in, m_b_gate, m_q_norm, m_w_uq, m_kv_norm, m_w_ukv, m_w_proj_sb, m_w_proj_mla, m_w_out, m_norm_ffn_pre, m_norm_ffn_post, m_w_gate_up, m_w_down, v_norm_mix_pre, v_norm_mix_post, v_w_in, v_b_gate, v_q_norm, v_w_uq, v_kv_norm, v_w_ukv, v_w_proj_sb, v_w_proj_mla, v_w_out, v_norm_ffn_pre, v_norm_ffn_post, v_w_gate_up, v_w_down):
    given = dict(locals())
    s = x.shape[1]
    names = [n for n, _, _, _ in BIG]
    small_w = {n: given[n] for n, _ in SMALL}

    own = [given[n][0].astype(BF16) for n in names]
    gathered = _gather_weights(own, "gather_weights")
    loss, grad_x, d_small, d_big = _forward_backward(
        x.reshape(s, D_MODEL), positions.reshape(s, 1), loss_target.reshape(s, D_MODEL), small_w,
        _full_matrices(own, gathered))

    pieces = _split_for_chips(d_big)
    from_sibling = _swap_halves(pieces, "grad_swap_halves")
    partial = [_add_cores(g, a, "grad_add_cores_" + n) for n, g, a in zip(names, pieces, from_sibling)]
    received = _scatter_chips([b for _, b in partial], "grad_scatter_chips")
    mine = [_add_chips(f, r, "grad_add_chips_" + n) for n, (f, _), r in zip(names, partial, received)]
    other = _to_sibling(mine, "grad_to_sibling")
    small_sum = _allsum_small(_pack_small(d_small, loss), "small_allsum")

    grads, deltas, new_m, new_v = {}, {}, {}, {}
    for n, a, b in zip(names, mine, other):
        out = _adamw_shard(given[n][0], given["m_" + n][0], given["v_" + n][0], a, b, "adamw_" + n)
        grads[n], deltas[n], new_m[n], new_v[n] = [o[None] for o in out]
    delta_s, m_s, v_s = _adamw_small(_pack_small(small_w), small_sum,
                                     _pack_small({n: given["m_" + n] for n in small_w}),
                                     _pack_small({n: given["v_" + n] for n in small_w}), "adamw_small")

    loss_out = small_sum.reshape(-1)[SMALL_ELEMS]
    grads.update(_unpack_small(small_sum))
    deltas.update(_unpack_small(delta_s))
    new_m.update(_unpack_small(m_s))
    new_v.update(_unpack_small(v_s))
    return (loss_out, grad_x.reshape(1, s, D_MODEL), *[grads[n] for n in WEIGHT_ORDER],
            *[deltas[n] for n in WEIGHT_ORDER], *[new_m[n] for n in WEIGHT_ORDER], *[new_v[n] for n in WEIGHT_ORDER])
```

```python
import functools
import math

import jax
import jax.numpy as jnp
from jax import lax
from jax.experimental import pallas as pl
from jax.experimental.pallas import tpu as pltpu

F32 = jnp.float32
BF16 = jnp.bfloat16
MESH = pl.DeviceIdType.MESH

D_MODEL = 1024
SB_WIDTH = 512
MLA_HEADS = 8
MLA_Q_RANK = 384
MLA_KV_RANK = 256
MLA_ROPE = 32
MLA_QK = 96
D_FF = 2816
ROPE_THETA = 10000.0
EPS = 1e-6
LANES = 128
HALF = 64
ATT_TILE = 256
N_CHIPS = 4

ADAM_LR = 0.001
ADAM_B1 = 0.9
ADAM_B2 = 0.999
ADAM_EPS = 1e-08
ADAM_WD = 0.01
ADAM_STEP = 10

VMEM_LIMIT = 56 * 1024 * 1024

BIG = (
    ("w_in", 1024, 1064, 1),
    ("w_uq", 384, 192, 1),
    ("w_ukv", 256, 256, 1),
    ("w_proj_sb", 512, 256, 1),
    ("w_proj_mla", 512, 256, 1),
    ("w_out", 256, 1024, 0),
    ("w_gate_up", 1024, 1408, 1),
    ("w_down", 704, 1024, 0),
)
PACK_COLS = 1024
SMALL = (
    ("norm_mix_pre", 1024),
    ("norm_mix_post", 1024),
    ("b_gate", 2048),
    ("q_norm", 384),
    ("kv_norm", 256),
    ("norm_ffn_pre", 1024),
    ("norm_ffn_post", 1024),
)
SMALL_ELEMS = sum(n for _, n in SMALL)
SMALL_ROWS = 8
WEIGHT_ORDER = ("norm_mix_pre", "norm_mix_post", "w_in", "b_gate", "q_norm", "w_uq", "kv_norm", "w_ukv",
                "w_proj_sb", "w_proj_mla", "w_out", "norm_ffn_pre", "norm_ffn_post", "w_gate_up", "w_down")


def _params():
    return pltpu.CompilerParams(vmem_limit_bytes=VMEM_LIMIT)


def _tile(dim, pref):
    t = (min(pref, dim) // LANES) * LANES
    while t >= LANES:
        if dim % t == 0:
            return t
        t -= LANES
    return dim


def _mm(a, b, mode, out_dtype, name, tm=1408, tn=1408, tk=1408):
    if mode == "nn":
        (m, k), n = a.shape, b.shape[1]
    elif mode == "nt":
        (m, k), n = a.shape, b.shape[0]
    else:
        (k, m), n = a.shape, b.shape[1]
    tm, tn, tk = _tile(m, tm), _tile(n, tn), _tile(k, tk)
    nk = k // tk
    if mode == "tn":
        a_spec = pl.BlockSpec((tk, tm), lambda i, j, l: (l, i))
        lhs_dim = 0
    else:
        a_spec = pl.BlockSpec((tm, tk), lambda i, j, l: (i, l))
        lhs_dim = 1
    if mode == "nt":
        b_spec = pl.BlockSpec((tn, tk), lambda i, j, l: (j, l))
        rhs_dim = 1
    else:
        b_spec = pl.BlockSpec((tk, tn), lambda i, j, l: (l, j))
        rhs_dim = 0
    dims = (((lhs_dim,), (rhs_dim,)), ((), ()))

    def body(a_ref, b_ref, o_ref, acc_ref):
        l = pl.program_id(2)

        @pl.when(l == 0)
        def _():
            acc_ref[...] = jnp.zeros_like(acc_ref)

        acc_ref[...] += lax.dot_general(a_ref[...].astype(BF16), b_ref[...].astype(BF16), dims,
                                        preferred_element_type=F32)

        @pl.when(l == nk - 1)
        def _():
            o_ref[...] = acc_ref[...].astype(o_ref.dtype)

    return pl.pallas_call(
        body,
        name=name,
        grid=(m // tm, n // tn, nk),
        in_specs=[a_spec, b_spec],
        out_specs=pl.BlockSpec((tm, tn), lambda i, j, l: (i, j)),
        out_shape=jax.ShapeDtypeStruct((m, n), out_dtype),
        scratch_shapes=[pltpu.VMEM((tm, tn), F32)],
        compiler_params=_params(),
    )(a, b)


def _rowwise(fn, ins, outs, reds, tm, name):
    rows = ins[0].shape[0]
    tm = min(tm, rows)
    n_in, n_out = len(ins), len(outs)

    def spec(shape):
        if shape[0] == rows:
            return pl.BlockSpec((tm, shape[1]), lambda i: (i, 0))
        return pl.BlockSpec(shape, lambda i: (0, 0))

    def body(*refs):
        i = pl.program_id(0)
        vals = fn(*[r[...] for r in refs[:n_in]])
        for r, v in zip(refs[n_in:n_in + n_out], vals[:n_out]):
            r[...] = v.astype(r.dtype)
        for r, v in zip(refs[n_in + n_out:], vals[n_out:]):
            @pl.when(i == 0)
            def _():
                r[...] = jnp.zeros_like(r)

            r[...] += v

    out_shape = [jax.ShapeDtypeStruct((rows, d), dt) for d, dt in outs]
    out_shape += [jax.ShapeDtypeStruct((1, d), F32) for d in reds]
    out_specs = [spec(s.shape) for s in out_shape]
    return pl.pallas_call(
        body,
        name=name,
        grid=(rows // tm,),
        in_specs=[spec(a.shape) for a in ins],
        out_specs=out_specs,
        out_shape=out_shape,
        compiler_params=_params(),
    )(*ins)


def _colsum(v):
    return jnp.sum(v, axis=0, keepdims=True)


def _rms(v):
    r = lax.rsqrt(jnp.mean(v * v, axis=-1, keepdims=True) + EPS)
    return r, v * r


def _rms_bwd(dy, r, vhat, g):
    u = dy * g
    return r * (u - vhat * jnp.mean(u * vhat, axis=-1, keepdims=True))


def _sigmoid(t):
    return 1.0 / (1.0 + jnp.exp(-t))


def _lane(shape):
    return lax.broadcasted_iota(jnp.int32, shape, len(shape) - 1)


def _rope_swap(v):
    lane = _lane(v.shape)
    up = pltpu.roll(v, LANES - 16, 1)
    down = pltpu.roll(v, 16, 1)
    return jnp.where((lane >= 64) & (lane < 80), up, jnp.where((lane >= 80) & (lane < 96), down, 0.0))


SB_DEAD = -120.0


def _softplus(z):
    return jnp.maximum(z, 0.0) + jnp.log(1.0 + jnp.exp(-jnp.abs(z)))


def _split_dot(v, m2):
    hi = v.astype(BF16)
    lo = (v - hi.astype(F32)).astype(BF16)
    return jnp.dot(jnp.concatenate([hi, lo], axis=1), m2, preferred_element_type=F32)


def _dot_nt(a, b):
    return lax.dot_general(a, b, (((1,), (1,)), ((), ())), preferred_element_type=F32)


def _dot_tn(a, b):
    return lax.dot_general(a, b, (((0,), (0,)), ((), ())), preferred_element_type=F32)


def _sb_fwd(qkv, name):
    s = qkv.shape[0]
    t = min(ATT_TILE, s // 2)
    npair = SB_WIDTH // LANES

    def body(q_ref, k_ref, v_ref, o_ref, t_ref, first_ref):
        i = pl.program_id(1)
        lane = _lane((1, LANES))
        row = lax.broadcasted_iota(jnp.int32, (t, t), 0)
        col = lax.broadcasted_iota(jnp.int32, (t, t), 1)
        tri = col < row
        m_from = jnp.where(row >= col, 1.0, 0.0).astype(BF16)
        m_from = jnp.concatenate([m_from, m_from], axis=0)
        masks = (lane < HALF, lane >= HALF)
        qhs = [jnp.where(hm, q_ref[n * t:(n + 1) * t, :], jnp.zeros((t, LANES), BF16)) * 0.125
               for n in range(2) for hm in masks]

        def step(kb, carry, diags):
            rows = pl.ds(pl.multiple_of(kb * t, t), t)
            ks = k_ref[rows, :]
            vs = v_ref[rows, :]
            vhs = [jnp.where(hm, vs, jnp.zeros_like(vs)) for hm in masks]
            act = [ch for ch in range(4) if diags[ch // 2] is not None]
            zs = {ch: _dot_nt(qhs[ch], ks) for ch in act}
            lbs = {}
            for ch in act:
                lb = -_softplus(zs[ch])
                lbs[ch] = jnp.where(tri, lb, 0.0) if diags[ch // 2] else lb
            sums = {ch: _split_dot(lbs[ch], m_from) for ch in act}
            out = list(carry)
            for ch in act:
                c, acc = carry[ch]
                a = jnp.exp(zs[ch] + sums[ch] + c)
                if diags[ch // 2]:
                    a = jnp.where(tri, a, 0.0)
                acc = acc + jnp.dot(a.astype(BF16), vhs[ch % 2], preferred_element_type=F32)
                out[ch] = (c + sums[ch][:, 0:1], acc)
            return tuple(out)

        def live(carry):
            top = jnp.max(carry[0][0])
            for ch in range(1, 4):
                top = jnp.maximum(top, jnp.max(carry[ch][0]))
            return top

        init = (jnp.zeros((t, 1), F32), jnp.zeros((t, LANES), F32))
        carry = step(2 * i + 1, (init,) * 4, (None, True))
        carry = step(2 * i, carry, (True, False))
        blocks_left = 2 * i

        def more(state):
            n, top, _ = state
            return (n < blocks_left) & (top > SB_DEAD)

        def walk(state):
            n, _, cr = state
            cr = step(blocks_left - 1 - n, cr, (False, False))
            return n + 1, live(cr), cr

        n_done, _, carry = lax.while_loop(more, walk, (jnp.int32(0), live(carry), carry))
        first_ref[0, 0] = jnp.full((8, LANES), (blocks_left - n_done).astype(F32))
        for n in range(2):
            (c0, acc0), (c1, acc1) = carry[2 * n], carry[2 * n + 1]
            o_ref[n * t:(n + 1) * t, :] = acc0 + acc1
            t_ref[n * t:(n + 1) * t, :] = jnp.where(masks[0], c0, c1)

    blk = pl.BlockSpec((2 * t, LANES), lambda p, i: (i, p))
    steps = s // (2 * t)
    return pl.pallas_call(
        body,
        name=name,
        grid=(npair, steps),
        in_specs=[
            blk,
            pl.BlockSpec((s, LANES), lambda p, i: (0, npair + p)),
            pl.BlockSpec((s, LANES), lambda p, i: (0, 2 * npair + p)),
        ],
        out_specs=[blk, blk, pl.BlockSpec((1, 1, 8, LANES), lambda p, i: (p, i, 0, 0))],
        out_shape=[jax.ShapeDtypeStruct((s, SB_WIDTH), F32), jax.ShapeDtypeStruct((s, SB_WIDTH), F32),
                   jax.ShapeDtypeStruct((npair, steps, 8, LANES), F32)],
        compiler_params=_params(),
    )(qkv, qkv, qkv)


def _sb_bwd(qkv, do, tot, first, name):
    s = qkv.shape[0]
    t = min(ATT_TILE, s // 2)
    npair = SB_WIDTH // LANES

    def body(q_ref, k_ref, v_ref, do_ref, t_ref, first_ref, dq_ref, dk_ref, dv_ref):
        i = pl.program_id(1)

        @pl.when(i == 0)
        def _():
            dk_ref[...] = jnp.zeros_like(dk_ref)
            dv_ref[...] = jnp.zeros_like(dv_ref)

        lane = _lane((1, LANES))
        row = lax.broadcasted_iota(jnp.int32, (t, t), 0)
        col = lax.broadcasted_iota(jnp.int32, (t, t), 1)
        tri = col < row
        m_from = jnp.where(row >= col, 1.0, 0.0).astype(BF16)
        m_from = jnp.concatenate([m_from, m_from], axis=0)
        m_upto = jnp.where(row <= col, 1.0, 0.0).astype(BF16)
        m_upto = jnp.concatenate([m_upto, m_upto], axis=0)
        masks = (lane < HALF, lane >= HALF)
        zero_b = jnp.zeros((t, LANES), BF16)
        qhs, dohs, tots = [], [], []
        for n in range(2):
            rs = slice(n * t, (n + 1) * t)
            do_b = do_ref[rs, :].astype(BF16)
            for h, hm in enumerate(masks):
                qhs.append(jnp.where(hm, q_ref[rs, :], zero_b))
                dohs.append(jnp.where(hm, do_b, zero_b))
                tots.append(t_ref[rs, h * HALF:h * HALF + 1])

        def step(kb, carry, diags):
            rows = pl.ds(pl.multiple_of(kb * t, t), t)
            ks = k_ref[rows, :]
            vs = v_ref[rows, :]
            khs = [jnp.where(hm, ks, jnp.zeros_like(ks)) for hm in masks]
            act = [ch for ch in range(4) if diags[ch // 2] is not None]
            zs = {ch: _dot_nt(qhs[ch] * 0.125, ks) for ch in act}
            das = {ch: _dot_nt(dohs[ch], vs) for ch in act}
            lbs, betas = {}, {}
            for ch in act:
                sp = _softplus(zs[ch])
                betas[ch] = jnp.exp(zs[ch] - sp)
                lbs[ch] = jnp.where(tri, -sp, 0.0) if diags[ch // 2] else -sp
            sums = {ch: _split_dot(lbs[ch], m_from) for ch in act}
            avs, es, befores = {}, {}, {}
            for ch in act:
                before = carry[ch][0]
                total = sums[ch][:, 0:1]
                a = jnp.exp(zs[ch] + sums[ch] + (tots[ch] - before - total))
                if diags[ch // 2]:
                    a = jnp.where(tri, a, 0.0)
                avs[ch] = a.astype(BF16)
                es[ch] = a * das[ch]
                befores[ch] = before + total
            e_sums = {ch: _split_dot(es[ch], m_upto) for ch in act}
            dzs, e_befores = {}, {}
            for ch in act:
                e_upto = e_sums[ch] + carry[ch][1]
                dz = es[ch] - betas[ch] * e_upto
                if diags[ch // 2]:
                    dz = jnp.where(tri, dz, 0.0)
                dzs[ch] = (dz * 0.125).astype(BF16)
                e_befores[ch] = e_upto[:, t - 1:t]
            out = list(carry)
            dk = jnp.zeros((t, LANES), F32)
            dv = jnp.zeros((t, LANES), F32)
            for ch in act:
                dq = carry[ch][2] + jnp.dot(dzs[ch], khs[ch % 2], preferred_element_type=F32)
                dk = dk + _dot_tn(dzs[ch], qhs[ch])
                dv = dv + _dot_tn(avs[ch], dohs[ch])
                out[ch] = (befores[ch], e_befores[ch], dq)
            dk_ref[rows, :] += dk
            dv_ref[rows, :] += dv
            return tuple(out)

        zero = jnp.zeros((t, 1), F32)
        init = (zero, zero, jnp.zeros((t, LANES), F32))
        start = jnp.clip(jnp.max(first_ref[0, 0]).astype(jnp.int32), 0, 2 * i)
        carry = lax.fori_loop(start, 2 * i, lambda kb, cr: step(kb, cr, (False, False)), (init,) * 4)
        carry = step(2 * i, carry, (True, False))
        carry = step(2 * i + 1, carry, (None, True))
        for n in range(2):
            dq_ref[n * t:(n + 1) * t, :] = carry[2 * n][2] + carry[2 * n + 1][2]

    blk = pl.BlockSpec((2 * t, LANES), lambda p, i: (i, p))
    full = pl.BlockSpec((s, LANES), lambda p, i: (0, p))
    out = jax.ShapeDtypeStruct((s, SB_WIDTH), F32)
    return pl.pallas_call(
        body,
        name=name,
        grid=(npair, s // (2 * t)),
        in_specs=[
            blk,
            pl.BlockSpec((s, LANES), lambda p, i: (0, npair + p)),
            pl.BlockSpec((s, LANES), lambda p, i: (0, 2 * npair + p)),
            blk,
            blk,
            pl.BlockSpec((1, 1, 8, LANES), lambda p, i: (p, i, 0, 0)),
        ],
        out_specs=[blk, full, full],
        out_shape=[out, out, out],
        compiler_params=_params(),
    )(qkv, qkv, qkv, do, tot, first)


MLA_SCALE = 1.0 / math.sqrt(MLA_QK)
MLA_SCALE_LOG2E = MLA_SCALE * math.log2(math.e)
NEG = -1e30
MLA_QBLOCKS = 4


def _diag_plan(nq):
    return [tuple(None if n < j else n == j for n in range(nq)) for j in range(nq)]


def _mla_fwd(qc, kc, vp, name):
    s = qc.shape[0]
    nq = MLA_QBLOCKS
    t = min(ATT_TILE, s // nq)

    def body(q_ref, k_ref, v_ref, o_ref, lse_ref, m_s, l_s, acc_s):
        i = pl.program_id(1)
        row = lax.broadcasted_iota(jnp.int32, (t, t), 0)
        col = lax.broadcasted_iota(jnp.int32, (t, t), 1)
        keep = col <= row
        m_s[...] = jnp.full(m_s.shape, NEG, F32)
        l_s[...] = jnp.zeros_like(l_s)
        acc_s[...] = jnp.zeros_like(acc_s)

        def step(kb, diags):
            rows = pl.ds(pl.multiple_of(kb * t, t), t)
            ks = k_ref[rows, :]
            vs = v_ref[rows, :]
            act = [n for n in range(nq) if diags[n] is not None]
            scs = {n: _dot_nt(q_ref[n * t:(n + 1) * t, :], ks) for n in act}
            mid = {}
            for n in act:
                sc = jnp.where(keep, scs[n], NEG) if diags[n] else scs[n]
                m = m_s[n]
                m_new = jnp.maximum(m, jnp.max(sc, axis=-1, keepdims=True))
                alpha = jnp.exp2((m - m_new) * MLA_SCALE_LOG2E)
                p = jnp.exp2((sc - jnp.concatenate([m_new] * (t // LANES), axis=1)) * MLA_SCALE_LOG2E)
                m_s[n] = m_new
                l_s[n] = alpha * l_s[n] + jnp.sum(p, axis=-1, keepdims=True)
                mid[n] = (alpha, p.astype(BF16))
            for n in act:
                alpha, p = mid[n]
                acc_s[n] = alpha * acc_s[n] + jnp.dot(p, vs, preferred_element_type=F32)

        def walk(kb, carry):
            step(kb, (False,) * nq)
            return carry

        lax.fori_loop(0, nq * i, walk, 0)
        for j, diags in enumerate(_diag_plan(nq)):
            step(nq * i + j, diags)
        for n in range(nq):
            l = l_s[n]
            o_ref[n * t:(n + 1) * t, :] = acc_s[n] / l
            lse_ref[n * t:(n + 1) * t, :] = m_s[n] * MLA_SCALE + jnp.log(l)

    blk = pl.BlockSpec((nq * t, LANES), lambda h, i: (i, h))
    full = pl.BlockSpec((s, LANES), lambda h, i: (0, h))
    out = jax.ShapeDtypeStruct((s, MLA_HEADS * LANES), F32)
    return pl.pallas_call(
        body,
        name=name,
        grid=(MLA_HEADS, s // (nq * t)),
        in_specs=[blk, full, full],
        out_specs=[blk, blk],
        out_shape=[out, out],
        scratch_shapes=[pltpu.VMEM((nq, t, LANES), F32)] * 3,
        compiler_params=_params(),
    )(qc, kc, vp)


def _mla_bwd(qc, kc, vp, do, o, lse, name):
    s = qc.shape[0]
    nq = MLA_QBLOCKS
    t = min(ATT_TILE, s // nq)

    def body(q_ref, k_ref, v_ref, do_ref, o_ref, lse_ref, dq_ref, dk_ref, dv_ref):
        i = pl.program_id(1)

        @pl.when(i == 0)
        def _():
            dk_ref[...] = jnp.zeros_like(dk_ref)
            dv_ref[...] = jnp.zeros_like(dv_ref)

        row = lax.broadcasted_iota(jnp.int32, (t, t), 0)
        col = lax.broadcasted_iota(jnp.int32, (t, t), 1)
        keep = col <= row

        def block(n):
            rs = slice(n * t, (n + 1) * t)
            do_f = do_ref[rs, :]
            delta = jnp.sum(do_f * o_ref[rs, :], axis=-1, keepdims=True)
            return q_ref[rs, :], do_f.astype(BF16), delta * MLA_SCALE, lse_ref[rs, 0:1] * math.log2(math.e)

        blocks = [block(n) for n in range(nq)]
        dq_ref[...] = jnp.zeros_like(dq_ref)

        def step(kb, diags):
            rows = pl.ds(pl.multiple_of(kb * t, t), t)
            ks = k_ref[rows, :]
            vs = v_ref[rows, :]
            act = [n for n in range(nq) if diags[n] is not None]
            scs = {n: _dot_nt(blocks[n][0], ks) for n in act}
            dps = {n: _dot_nt(blocks[n][1], vs) for n in act}
            ps, dss = {}, {}
            for n in act:
                _, _, delta_s, lse2 = blocks[n]
                p = jnp.exp2(scs[n] * MLA_SCALE_LOG2E - lse2)
                if diags[n]:
                    p = jnp.where(keep, p, 0.0)
                dss[n] = (p * (dps[n] * MLA_SCALE - delta_s)).astype(BF16)
                ps[n] = p.astype(BF16)
            dk = jnp.zeros((t, LANES), F32)
            dv = jnp.zeros((t, LANES), F32)
            for n in act:
                q, do_b, _, _ = blocks[n]
                dq_ref[n * t:(n + 1) * t, :] += jnp.dot(dss[n], ks, preferred_element_type=F32)
                dk = dk + _dot_tn(dss[n], q)
                dv = dv + _dot_tn(ps[n], do_b)
            dk_ref[rows, :] += dk
            dv_ref[rows, :] += dv

        def walk(kb, carry):
            step(kb, (False,) * nq)
            return carry

        lax.fori_loop(0, nq * i, walk, 0)
        for j, diags in enumerate(_diag_plan(nq)):
            step(nq * i + j, diags)

    blk = pl.BlockSpec((nq * t, LANES), lambda h, i: (i, h))
    full = pl.BlockSpec((s, LANES), lambda h, i: (0, h))
    out = jax.ShapeDtypeStruct((s, MLA_HEADS * LANES), F32)
    return pl.pallas_call(
        body,
        name=name,
        grid=(MLA_HEADS, s // (nq * t)),
        in_specs=[blk, full, full, blk, blk, blk],
        out_specs=[blk, full, full],
        out_shape=[out, out, out],
        compiler_params=_params(),
    )(qc, kc, vp, do, o, lse)


def _position():
    return lax.axis_index("x"), lax.axis_index("y"), lax.axis_index("c")


def _other_chips(x, y):
    return [(1 - x, y), (x, 1 - y), (1 - x, 1 - y)]


HBM_SPEC = pl.BlockSpec(memory_space=pltpu.HBM)


def _remote(src, dst, send_sem, recv_sem, device):
    return pltpu.make_async_remote_copy(src_ref=src, dst_ref=dst, send_sem=send_sem, recv_sem=recv_sem,
                                        device_id=device, device_id_type=MESH)


def _dma_sems(n, count):
    return [pltpu.SemaphoreType.DMA((n,)) for _ in range(count)]


def _gather_weights(ws, name):
    na = len(ws)

    def body(*refs):
        srcs, outs = refs[:na], refs[na:2 * na]
        s1, r1, s2, r2, s3, r3, s4, r4 = refs[2 * na:]
        x, y, c = _position()
        me = 2 * x + y
        px, py = jnp.where(c == 1, 1 - x, x), jnp.where(c == 1, y, 1 - y)
        qx, qy = jnp.where(c == 1, x, 1 - x), jnp.where(c == 1, 1 - y, y)
        p_idx, q_idx, d_idx = 2 * px + py, 2 * qx + qy, 2 * (1 - x) + (1 - y)
        partner, across, sibling = (px, py, c), (qx, qy, c), (x, y, 1 - c)
        my_half, other_half = 1 - c, c

        def half(i, slot, h):
            kh = ws[i].shape[0] // 2
            return outs[i].at[slot, pl.ds(h * kh, kh), :]

        sends = [_remote(srcs[i], outs[i].at[me], s1.at[i], r1.at[i], partner) for i in range(na)]
        for cp in sends:
            cp.start()
        for i in range(na):
            _remote(srcs[i], outs[i].at[p_idx], s1.at[i], r1.at[i], partner).wait_recv()
            onward = [_remote(half(i, p_idx, my_half), half(i, p_idx, my_half), s2.at[i], r2.at[i], across),
                      _remote(outs[i].at[p_idx], outs[i].at[p_idx], s3.at[i], r3.at[i], sibling)]
            for cp in onward:
                cp.start()
            sends += onward
        for i in range(na):
            _remote(half(i, d_idx, my_half), half(i, d_idx, my_half), s2.at[i], r2.at[i], across).wait_recv()
            cp = _remote(half(i, d_idx, my_half), half(i, d_idx, my_half), s4.at[i], r4.at[i], sibling)
            cp.start()
            sends.append(cp)
        for i in range(na):
            _remote(outs[i].at[q_idx], outs[i].at[q_idx], s3.at[i], r3.at[i], sibling).wait_recv()
            _remote(half(i, d_idx, other_half), half(i, d_idx, other_half), s4.at[i], r4.at[i], sibling).wait_recv()
        for cp in sends:
            cp.wait_send()

    return pl.pallas_call(
        body,
        name=name,
        in_specs=[HBM_SPEC] * na,
        out_specs=[HBM_SPEC] * na,
        out_shape=[jax.ShapeDtypeStruct((N_CHIPS,) + w.shape, w.dtype) for w in ws],
        scratch_shapes=_dma_sems(na, 8),
    )(*ws)


def _swap_halves(gs, name):
    na = len(gs)

    def body(*refs):
        srcs, outs, send_sems, recv_sems = refs[:na], refs[na:2 * na], refs[2 * na], refs[2 * na + 1]
        x, y, c = _position()
        cps = []
        for i in range(na):
            kh = gs[i].shape[1] // 2
            cps.append(_remote(srcs[i].at[:, pl.ds((1 - c) * kh, kh), :], outs[i], send_sems.at[i], recv_sems.at[i],
                               (x, y, 1 - c)))
            cps[-1].start()
        for cp in cps:
            cp.wait()

    return pl.pallas_call(
        body,
        name=name,
        in_specs=[HBM_SPEC] * na,
        out_specs=[HBM_SPEC] * na,
        out_shape=[jax.ShapeDtypeStruct((g.shape[0], g.shape[1] // 2, g.shape[2]), g.dtype) for g in gs],
        scratch_shapes=_dma_sems(na, 2),
    )(*gs)


def _scatter_chips(parts, name):
    na = len(parts)

    def body(*refs):
        srcs, outs, send_sems, recv_sems = refs[:na], refs[na:2 * na], refs[2 * na], refs[2 * na + 1]
        x, y, c = _position()
        me = 2 * x + y
        sends = []
        for i in range(na):
            for k, (px, py) in enumerate(_other_chips(x, y)):
                cp = _remote(srcs[i].at[2 * px + py], outs[i].at[me], send_sems.at[3 * i + k],
                             recv_sems.at[3 * i + k], (px, py, c))
                cp.start()
                sends.append(cp)
        for i in range(na):
            for k, (px, py) in enumerate(_other_chips(x, y)):
                _remote(srcs[i].at[me], outs[i].at[2 * px + py], send_sems.at[3 * i + k], recv_sems.at[3 * i + k],
                        (px, py, c)).wait_recv()
        for cp in sends:
            cp.wait_send()

    return pl.pallas_call(
        body,
        name=name,
        in_specs=[HBM_SPEC] * na,
        out_specs=[HBM_SPEC] * na,
        out_shape=[jax.ShapeDtypeStruct(p.shape, p.dtype) for p in parts],
        scratch_shapes=_dma_sems(3 * na, 2),
    )(*parts)


def _to_sibling(halves, name):
    na = len(halves)

    def body(*refs):
        srcs, outs, send_sems, recv_sems = refs[:na], refs[na:2 * na], refs[2 * na], refs[2 * na + 1]
        x, y, c = _position()
        cps = [_remote(srcs[i], outs[i], send_sems.at[i], recv_sems.at[i], (x, y, 1 - c)) for i in range(na)]
        for cp in cps:
            cp.start()
        for cp in cps:
            cp.wait()

    return pl.pallas_call(
        body,
        name=name,
        in_specs=[HBM_SPEC] * na,
        out_specs=[HBM_SPEC] * na,
        out_shape=[jax.ShapeDtypeStruct(h.shape, h.dtype) for h in halves],
        scratch_shapes=_dma_sems(na, 2),
    )(*halves)


def _allsum_small(v, name):
    shape = v.shape

    def body(v_ref, o_ref, buf, send_sems, recv_sems):
        x, y, c = _position()
        me = 4 * x + 2 * y + c
        buf[me] = v_ref[...]
        peers = []
        for k in range(1, 8):
            fx, fy, fc = (k >> 2) & 1, (k >> 1) & 1, k & 1
            peers.append((jnp.where(fx == 1, 1 - x, x), jnp.where(fy == 1, 1 - y, y), jnp.where(fc == 1, 1 - c, c)))
        sends = []
        for k, peer in enumerate(peers):
            cp = pltpu.make_async_remote_copy(src_ref=v_ref, dst_ref=buf.at[me], send_sem=send_sems.at[k],
                                              recv_sem=recv_sems.at[k], device_id=peer, device_id_type=MESH)
            cp.start()
            sends.append(cp)
        for k, (px, py, pc) in enumerate(peers):
            pltpu.make_async_remote_copy(src_ref=v_ref, dst_ref=buf.at[4 * px + 2 * py + pc],
                                         send_sem=send_sems.at[k], recv_sem=recv_sems.at[k],
                                         device_id=(px, py, pc), device_id_type=MESH).wait_recv()
        for cp in sends:
            cp.wait_send()
        acc = buf[0]
        for d in range(1, 8):
            acc = acc + buf[d]
        o_ref[...] = acc

    return pl.pallas_call(
        body,
        name=name,
        in_specs=[pl.BlockSpec(memory_space=pltpu.VMEM)],
        out_specs=pl.BlockSpec(memory_space=pltpu.VMEM),
        out_shape=jax.ShapeDtypeStruct(shape, F32),
        scratch_shapes=[pltpu.VMEM((8,) + shape, F32), pltpu.SemaphoreType.DMA((7,)), pltpu.SemaphoreType.DMA((7,))],
    )(v)


def _row_tile(h):
    for t in range(256, 0, -16):
        if h % t == 0:
            return t
    return h


def _add_cores(g, a, name):
    n4, k, n = g.shape
    kh = k // 2
    th = _row_tile(kh)

    def body(g_ref, a_ref, f_ref, b_ref):
        c = lax.axis_index("c")
        total = jnp.where(c == 0, g_ref[0, 0], g_ref[0, 1]) + a_ref[0]
        f_ref[0] = total
        b_ref[0] = total.astype(BF16)

    spec = pl.BlockSpec((1, th, n), lambda j, r: (j, r, 0))
    return pl.pallas_call(
        body,
        name=name,
        grid=(n4, kh // th),
        in_specs=[pl.BlockSpec((1, 2, th, n), lambda j, r: (j, 0, r, 0)), spec],
        out_specs=[spec, spec],
        out_shape=[jax.ShapeDtypeStruct((n4, kh, n), F32), jax.ShapeDtypeStruct((n4, kh, n), BF16)],
        compiler_params=_params(),
    )(g.reshape(n4, 2, kh, n), a)


def _add_chips(pf, b, name):
    n4, h, n = pf.shape
    th = _row_tile(h)

    def body(pf_ref, b_ref, o_ref):
        me = 2 * lax.axis_index("x") + lax.axis_index("y")
        terms = [jnp.where(me == j, pf_ref[j], b_ref[j].astype(F32)) for j in range(n4)]
        o_ref[...] = ((terms[0] + terms[1]) + terms[2]) + terms[3]

    spec = pl.BlockSpec((n4, th, n), lambda r: (0, r, 0))
    return pl.pallas_call(
        body,
        name=name,
        grid=(h // th,),
        in_specs=[spec, spec],
        out_specs=pl.BlockSpec((th, n), lambda r: (r, 0)),
        out_shape=jax.ShapeDtypeStruct((h, n), F32),
        compiler_params=_params(),
    )(pf, b)


def _adam_math(w, g, m, v):
    m = ADAM_B1 * m + (1.0 - ADAM_B1) * g
    v = ADAM_B2 * v + (1.0 - ADAM_B2) * (g * g)
    m_hat = m / (1.0 - ADAM_B1 ** ADAM_STEP)
    v_hat = v / (1.0 - ADAM_B2 ** ADAM_STEP)
    delta = -ADAM_LR * (m_hat / (jnp.sqrt(v_hat) + ADAM_EPS) + ADAM_WD * w)
    return delta, m, v


def _adamw_shard(w, m, v, mine, other, name):
    k, n = w.shape
    kh = k // 2
    th = _row_tile(kh)
    nb = kh // th

    def body(w_ref, m_ref, v_ref, a_ref, b_ref, g_out, d_out, m_out, v_out):
        g = jnp.where(pl.program_id(0) == lax.axis_index("c"), a_ref[...], b_ref[...])
        g_out[...] = g
        d_out[...], m_out[...], v_out[...] = _adam_math(w_ref[...], g, m_ref[...], v_ref[...])

    full = pl.BlockSpec((th, n), lambda hh, r: (hh * nb + r, 0))
    half = pl.BlockSpec((th, n), lambda hh, r: (r, 0))
    return pl.pallas_call(
        body,
        name=name,
        grid=(2, nb),
        in_specs=[full, full, full, half, half],
        out_specs=[full] * 4,
        out_shape=[jax.ShapeDtypeStruct((k, n), F32)] * 4,
        compiler_params=_params(),
    )(w, m, v, mine, other)


def _adamw_small(w, g, m, v, name):
    cols = w.shape[1]
    return _rowwise(_adam_math, [w, g, m, v], [(cols, F32)] * 3, [], w.shape[0], name)


def _full_matrices(own, gathered):
    me = 2 * lax.axis_index("x") + lax.axis_index("y")
    out = {}
    for (n, _, _, axis), w, g in zip(BIG, own, gathered):
        out[n] = jnp.concatenate([jnp.where(me == j, w, g[j]) for j in range(N_CHIPS)], axis=axis)
    return out


def _split_for_chips(full):
    out = []
    for n, r, c, axis in BIG:
        g = full[n]
        out.append(jnp.stack([g[:, j * c:(j + 1) * c] for j in range(N_CHIPS)]) if axis == 1
                   else g.reshape(N_CHIPS, r, c))
    return out


def _pack_small(vecs, extra=None):
    flat = jnp.concatenate([vecs[n].reshape(-1) for n, _ in SMALL] + ([extra.reshape(-1)] if extra is not None else []))
    flat = jnp.pad(flat, (0, SMALL_ROWS * PACK_COLS - flat.shape[0]))
    return flat.reshape(SMALL_ROWS, PACK_COLS)


def _unpack_small(pack):
    flat = pack.reshape(-1)
    out, off = {}, 0
    for n, k in SMALL:
        out[n] = flat[off:off + k].reshape(1, k)
        off += k
    return out


def _forward_backward(x, pos, target, sm, w):
    s = x.shape[0]
    w_in = w["w_in"]
    w_a = w_in[:, 0:1536]
    zeros = functools.partial(jnp.zeros, dtype=BF16)
    w_b = jnp.concatenate([w_in[:, 1536:2176], zeros((D_MODEL, 64)), w_in[:, 2176:2208], zeros((D_MODEL, 32))], axis=1)
    w_g = w_in[:, 2208:4256]
    w_uq = jnp.pad(w["w_uq"].reshape(MLA_Q_RANK, MLA_HEADS, MLA_QK), ((0, 0), (0, 0), (0, LANES - MLA_QK)))
    w_uq = w_uq.reshape(MLA_Q_RANK, MLA_HEADS * LANES)
    w_ukv = w["w_ukv"]
    w_pm = jnp.pad(w["w_proj_mla"].reshape(MLA_HEADS, HALF, D_MODEL), ((0, 0), (HALF, 0), (0, 0)))
    w_pm = w_pm.reshape(MLA_HEADS * LANES, D_MODEL)
    w_ps, w_out, w_gu, w_dn = w["w_proj_sb"], w["w_out"], w["w_gate_up"], w["w_down"]
    inv_freq = ROPE_THETA ** (-jnp.arange(0, MLA_ROPE, 2, dtype=F32) / MLA_ROPE)
    inv_lane = jnp.concatenate([jnp.zeros((HALF,), F32), inv_freq, inv_freq, jnp.zeros((32,), F32)]).reshape(1, LANES)
    g_pre, g_post, b_gate = sm["norm_mix_pre"], sm["norm_mix_post"], sm["b_gate"]
    g_q, g_kv, g_fpre, g_fpost = sm["q_norm"], sm["kv_norm"], sm["norm_ffn_pre"], sm["norm_ffn_post"]

    def f_h1(x, g):
        return (_rms(x)[1] * g,)

    (h1,) = _rowwise(f_h1, [x, g_pre], [(D_MODEL, BF16)], [], 256, "h1")
    qkv = _mm(h1, w_a, "nn", BF16, "proj_sb")
    lat = _mm(h1, w_b, "nn", F32, "proj_lat")
    gl = _mm(h1, w_g, "nn", F32, "proj_gate")

    def f_lat(lat, gq, gkv, pos, inv):
        cqn = _rms(lat[:, 0:MLA_Q_RANK])[1] * gq
        ckvn = _rms(lat[:, MLA_Q_RANK:MLA_Q_RANK + MLA_KV_RANK])[1] * gkv
        kr = lat[:, 640:768]
        ang = pos.astype(F32) * inv
        lane = _lane(ang.shape)
        cosv, sinv = jnp.cos(ang), jnp.sin(ang)
        rope_c = jnp.where(lane < 64, 1.0, jnp.where(lane < 96, cosv, 0.0))
        rope_s = jnp.where((lane >= 64) & (lane < 80), -sinv, jnp.where((lane >= 80) & (lane < 96), sinv, 0.0))
        krr = jnp.where(lane >= 64, kr * rope_c, 0.0) + _rope_swap(kr) * rope_s
        return cqn, ckvn, krr, rope_c, rope_s

    cqn, ckvn, krr, rope_c, rope_s = _rowwise(
        f_lat, [lat, g_q, g_kv, pos, inv_lane],
        [(MLA_Q_RANK, BF16), (MLA_KV_RANK, BF16), (LANES, F32), (LANES, F32), (LANES, F32)], [], 256, "lat_norm")
    q_raw = _mm(cqn, w_uq, "nn", F32, "q_up")
    kv = _mm(ckvn, w_ukv, "nn", F32, "kv_up")

    def f_heads(q_raw, kv, krr, rc, rs):
        lane = _lane(krr.shape)
        qs, ks, vs = [], [], []
        for h in range(MLA_HEADS):
            qh = q_raw[:, h * LANES:(h + 1) * LANES]
            kvh = kv[:, h * LANES:(h + 1) * LANES]
            qs.append(qh * rc + _rope_swap(qh) * rs)
            ks.append(jnp.where(lane < HALF, kvh, krr))
            vs.append(jnp.where(lane >= HALF, kvh, 0.0))
        return jnp.concatenate(qs, axis=1), jnp.concatenate(ks, axis=1), jnp.concatenate(vs, axis=1)

    wide = MLA_HEADS * LANES
    qc, kc, vp = _rowwise(f_heads, [q_raw, kv, krr, rope_c, rope_s], [(wide, BF16)] * 3, [], 256, "mla_heads")
    o_sb, sb_tot, sb_first = _sb_fwd(qkv, "sb_fwd")
    o_mla, lse = _mla_fwd(qc, kc, vp, "mla_fwd")
    m_sb = _mm(o_sb, w_ps, "nn", F32, "proj_o_sb")
    m_mla = _mm(o_mla, w_pm, "nn", F32, "proj_o_mla")

    def f_merge(m_sb, m_mla, gl, b):
        g = _sigmoid(gl + b)
        return (g[:, :D_MODEL] * m_sb + g[:, D_MODEL:] * m_mla,)

    (merged,) = _rowwise(f_merge, [m_sb, m_mla, gl, b_gate], [(D_MODEL, BF16)], [], 256, "merge")
    y = _mm(merged, w_out, "nn", F32, "mix_out")

    def f_res1(x, y, g2, g3):
        x2 = x + _rms(y)[1] * g2
        return x2, _rms(x2)[1] * g3

    x2, h2 = _rowwise(f_res1, [x, y, g_post, g_fpre], [(D_MODEL, F32), (D_MODEL, BF16)], [], 256, "res1")

    gu = _mm(h2, w_gu, "nn", F32, "ffn_up")

    def f_act(gu):
        g, u = gu[:, :D_FF], gu[:, D_FF:]
        return (g * _sigmoid(g) * u,)

    (act,) = _rowwise(f_act, [gu], [(D_FF, BF16)], [], 128, "ffn_act")
    f = _mm(act, w_dn, "nn", F32, "ffn_down")

    def f_loss(x2, f, g4, tgt):
        r, fhat = _rms(f)
        err = x2 + fhat * g4 - tgt
        loss = 0.5 * jnp.sum(jnp.sum(err * err, axis=-1, keepdims=True), axis=0, keepdims=True) / D_MODEL
        dout = err / D_MODEL
        return dout, _rms_bwd(dout, r, fhat, g4), loss, _colsum(dout * fhat)

    dout, df, loss, d_fpost = _rowwise(f_loss, [x2, f, g_fpost, target], [(D_MODEL, F32), (D_MODEL, BF16)],
                                       [1, D_MODEL], 256, "loss")

    dact = _mm(df, w_dn, "nt", F32, "d_act")
    dw_dn = _mm(act, df, "tn", F32, "dw_down")

    def f_dact(gu, dact):
        g, u = gu[:, :D_FF], gu[:, D_FF:]
        sg = _sigmoid(g)
        dg = dact * u * (sg * (1.0 + g * (1.0 - sg)))
        return (jnp.concatenate([dg, dact * (g * sg)], axis=1),)

    (dgu,) = _rowwise(f_dact, [gu, dact], [(2 * D_FF, BF16)], [], 128, "d_gu")
    dh2 = _mm(dgu, w_gu, "nt", F32, "d_h2")
    dw_gu = _mm(h2, dgu, "tn", F32, "dw_gate_up")

    def f_dres1(dout, dh2, x2, y, g3, g2):
        r3, x2hat = _rms(x2)
        dx2 = dout + _rms_bwd(dh2, r3, x2hat, g3)
        r2, yhat = _rms(y)
        return dx2, _rms_bwd(dx2, r2, yhat, g2), _colsum(dh2 * x2hat), _colsum(dx2 * yhat)

    dx2, dy, d_fpre, d_post = _rowwise(f_dres1, [dout, dh2, x2, y, g_fpre, g_post],
                                       [(D_MODEL, F32), (D_MODEL, BF16)], [D_MODEL, D_MODEL], 256, "d_res1")

    dmerged = _mm(dy, w_out, "nt", F32, "d_merged")
    dw_out = _mm(merged, dy, "tn", F32, "dw_out")

    def f_dmerge(dm, m_sb, m_mla, gl, b):
        g = _sigmoid(gl + b)
        g0, g1 = g[:, :D_MODEL], g[:, D_MODEL:]
        dgl = jnp.concatenate([dm * m_sb * g0 * (1.0 - g0), dm * m_mla * g1 * (1.0 - g1)], axis=1)
        return dm * g0, dm * g1, dgl, _colsum(dgl)

    dm_sb, dm_mla, dgl, d_bgate = _rowwise(f_dmerge, [dmerged, m_sb, m_mla, gl, b_gate],
                                           [(D_MODEL, BF16), (D_MODEL, BF16), (2 * D_MODEL, BF16)], [2 * D_MODEL],
                                           256, "d_merge")
    do_sb = _mm(dm_sb, w_ps, "nt", BF16, "d_o_sb")
    dw_ps = _mm(o_sb, dm_sb, "tn", F32, "dw_proj_sb")
    do_mla = _mm(dm_mla, w_pm, "nt", F32, "d_o_mla")
    dw_pm = _mm(o_mla, dm_mla, "tn", F32, "dw_proj_mla")

    dq_sb, dk_sb, dv_sb = _sb_bwd(qkv, do_sb, sb_tot, sb_first, "sb_bwd")
    dqc, dkc, dvp = _mla_bwd(qc, kc, vp, do_mla, o_mla, lse, "mla_bwd")

    def f_dheads(dqc, dkc, dvp, rc, rs):
        lane = _lane(rc.shape)
        dqs, dkvs = [], []
        dkr = jnp.zeros(rc.shape, F32)
        for h in range(MLA_HEADS):
            dq = dqc[:, h * LANES:(h + 1) * LANES]
            dk = dkc[:, h * LANES:(h + 1) * LANES]
            dqs.append(dq * rc + _rope_swap(dq * rs))
            dkvs.append(jnp.where(lane < HALF, dk, dvp[:, h * LANES:(h + 1) * LANES]))
            dkr = dkr + jnp.where(lane >= HALF, dk, 0.0)
        dkr = dkr * rc + _rope_swap(dkr * rs)
        return jnp.concatenate(dqs, axis=1), jnp.concatenate(dkvs, axis=1), dkr

    dq_raw, dkv, dkr = _rowwise(f_dheads, [dqc, dkc, dvp, rope_c, rope_s],
                                [(wide, BF16), (wide, BF16), (LANES, F32)], [], 256, "d_mla_heads")
    dcqn = _mm(dq_raw, w_uq, "nt", F32, "d_cqn")
    dw_uq = _mm(cqn, dq_raw, "tn", F32, "dw_uq")
    dckvn = _mm(dkv, w_ukv, "nt", F32, "d_ckvn")
    dw_ukv = _mm(ckvn, dkv, "tn", F32, "dw_ukv")

    def f_dlat(lat, dcqn, dckvn, dkr, gq, gkv):
        rq, cqhat = _rms(lat[:, 0:MLA_Q_RANK])
        rkv, ckvhat = _rms(lat[:, MLA_Q_RANK:MLA_Q_RANK + MLA_KV_RANK])
        dlat = jnp.concatenate([_rms_bwd(dcqn, rq, cqhat, gq), _rms_bwd(dckvn, rkv, ckvhat, gkv), dkr], axis=1)
        return dlat, _colsum(dcqn * cqhat), _colsum(dckvn * ckvhat)

    dlat, d_gq, d_gkv = _rowwise(f_dlat, [lat, dcqn, dckvn, dkr, g_q, g_kv], [(768, BF16)],
                                 [MLA_Q_RANK, MLA_KV_RANK], 256, "d_lat")

    dqkv = jnp.concatenate([dq_sb, dk_sb, dv_sb], axis=1).astype(BF16)
    dh1_a = _mm(dqkv, w_a, "nt", F32, "d_h1_sb")
    dh1_b = _mm(dlat, w_b, "nt", F32, "d_h1_lat")
    dh1_g = _mm(dgl, w_g, "nt", F32, "d_h1_gate")
    dw_a = _mm(h1, dqkv, "tn", F32, "dw_in_sb")
    dw_b = _mm(h1, dlat, "tn", F32, "dw_in_lat")
    dw_g = _mm(h1, dgl, "tn", F32, "dw_in_gate")

    def f_dx(x, dx2, da, db, dg, g1):
        r, xhat = _rms(x)
        dh1 = da + db + dg
        return dx2 + _rms_bwd(dh1, r, xhat, g1), _colsum(dh1 * xhat)

    grad_x, d_pre = _rowwise(f_dx, [x, dx2, dh1_a, dh1_b, dh1_g, g_pre], [(D_MODEL, F32)], [D_MODEL], 256, "d_x")

    small = {"norm_mix_pre": d_pre, "norm_mix_post": d_post, "b_gate": d_bgate, "q_norm": d_gq, "kv_norm": d_gkv,
             "norm_ffn_pre": d_fpre, "norm_ffn_post": d_fpost}
    big = {
        "w_in": jnp.concatenate([dw_a, dw_b[:, 0:640], dw_b[:, 704:736], dw_g], axis=1),
        "w_uq": dw_uq.reshape(MLA_Q_RANK, MLA_HEADS, LANES)[:, :, :MLA_QK].reshape(MLA_Q_RANK, MLA_HEADS * MLA_QK),
        "w_ukv": dw_ukv,
        "w_proj_sb": dw_ps,
        "w_proj_mla": dw_pm.reshape(MLA_HEADS, LANES, D_MODEL)[:, HALF:, :].reshape(MLA_HEADS * HALF, D_MODEL),
        "w_out": dw_out,
        "w_gate_up": dw_gu,
        "w_down": dw_dn,
    }
    return loss, grad_x, small, big


def kernel(x, positions, norm_mix_pre, norm_mix_post, w_in, b_gate, q_norm, w_uq, kv_norm, w_ukv, w_proj_sb, w_proj_mla, w_out, norm_ffn_pre, norm_ffn_post, w_gate_up, w_down, loss_target, m_norm_mix_pre, m_norm_mix_post, m_w_in, m_b_gate, m_q_norm, m_w_uq, m_kv_norm, m_w_ukv, m_w_proj_sb, m_w_proj_mla, m_w_out, m_norm_ffn_pre, m_norm_ffn_post, m_w_gate_up, m_w_down, v_norm_mix_pre, v_norm_mix_post, v_w_in, v_b_gate, v_q_norm, v_w_uq, v_kv_norm, v_w_ukv, v_w_proj_sb, v_w_proj_mla, v_w_out, v_norm_ffn_pre, v_norm_ffn_post, v_w_gate_up, v_w_down):
    given = dict(locals())
    s = x.shape[1]
    names = [n for n, _, _, _ in BIG]
    small_w = {n: given[n] for n, _ in SMALL}

    own = [given[n][0].astype(BF16) for n in names]
    gathered = _gather_weights(own, "gather_weights")
    loss, grad_x, d_small, d_big = _forward_backward(
        x.reshape(s, D_MODEL), positions.reshape(s, 1), loss_target.reshape(s, D_MODEL), small_w,
        _full_matrices(own, gathered))

    pieces = _split_for_chips(d_big)
    from_sibling = _swap_halves(pieces, "grad_swap_halves")
    partial = [_add_cores(g, a, "grad_add_cores_" + n) for n, g, a in zip(names, pieces, from_sibling)]
    received = _scatter_chips([b for _, b in partial], "grad_scatter_chips")
    mine = [_add_chips(f, r, "grad_add_chips_" + n) for n, (f, _), r in zip(names, partial, received)]
    other = _to_sibling(mine, "grad_to_sibling")
    small_sum = _allsum_small(_pack_small(d_small, loss), "small_allsum")

    grads, deltas, new_m, new_v = {}, {}, {}, {}
    for n, a, b in zip(names, mine, other):
        out = _adamw_shard(given[n][0], given["m_" + n][0], given["v_" + n][0], a, b, "adamw_" + n)
        grads[n], deltas[n], new_m[n], new_v[n] = [o[None] for o in out]
    delta_s, m_s, v_s = _adamw_small(_pack_small(small_w), small_sum,
                                     _pack_small({n: given["m_" + n] for n in small_w}),
                                     _pack_small({n: given["v_" + n] for n in small_w}), "adamw_small")

    loss_out = small_sum.reshape(-1)[SMALL_ELEMS]
    grads.update(_unpack_small(small_sum))
    deltas.update(_unpack_small(delta_s))
    new_m.update(_unpack_small(m_s))
    new_v.update(_unpack_small(v_s))
    return (loss_out, grad_x.reshape(1, s, D_MODEL), *[grads[n] for n in WEIGHT_ORDER],
            *[deltas[n] for n in WEIGHT_ORDER], *[new_m[n] for n in WEIGHT_ORDER], *[new_v[n] for n in WEIGHT_ORDER])
```

```python
import functools
import math

import jax
import jax.numpy as jnp
from jax import lax
from jax.experimental import pallas as pl
from jax.experimental.pallas import tpu as pltpu

F32 = jnp.float32
BF16 = jnp.bfloat16
MESH = pl.DeviceIdType.MESH

D_MODEL = 1024
SB_WIDTH = 512
MLA_HEADS = 8
MLA_Q_RANK = 384
MLA_KV_RANK = 256
MLA_ROPE = 32
MLA_QK = 96
D_FF = 2816
ROPE_THETA = 10000.0
EPS = 1e-6
LANES = 128
HALF = 64
ATT_TILE = 256
N_CHIPS = 4

ADAM_LR = 0.001
ADAM_B1 = 0.9
ADAM_B2 = 0.999
ADAM_EPS = 1e-08
ADAM_WD = 0.01
ADAM_STEP = 10

VMEM_LIMIT = 56 * 1024 * 1024

BIG = (
    ("w_in", 1024, 1064, 1),
    ("w_uq", 384, 192, 1),
    ("w_ukv", 256, 256, 1),
    ("w_proj_sb", 512, 256, 1),
    ("w_proj_mla", 512, 256, 1),
    ("w_out", 256, 1024, 0),
    ("w_gate_up", 1024, 1408, 1),
    ("w_down", 704, 1024, 0),
)
PACK_COLS = 1024
SMALL = (
    ("norm_mix_pre", 1024),
    ("norm_mix_post", 1024),
    ("b_gate", 2048),
    ("q_norm", 384),
    ("kv_norm", 256),
    ("norm_ffn_pre", 1024),
    ("norm_ffn_post", 1024),
)
SMALL_ELEMS = sum(n for _, n in SMALL)
SMALL_ROWS = 8
WEIGHT_ORDER = ("norm_mix_pre", "norm_mix_post", "w_in", "b_gate", "q_norm", "w_uq", "kv_norm", "w_ukv",
                "w_proj_sb", "w_proj_mla", "w_out", "norm_ffn_pre", "norm_ffn_post", "w_gate_up", "w_down")


def _params():
    return pltpu.CompilerParams(vmem_limit_bytes=VMEM_LIMIT)


def _tile(dim, pref):
    t = (min(pref, dim) // LANES) * LANES
    while t >= LANES:
        if dim % t == 0:
            return t
        t -= LANES
    return dim


def _mm(a, b, mode, out_dtype, name, tm=1408, tn=1408, tk=1408):
    if mode == "nn":
        (m, k), n = a.shape, b.shape[1]
    elif mode == "nt":
        (m, k), n = a.shape, b.shape[0]
    else:
        (k, m), n = a.shape, b.shape[1]
    tm, tn, tk = _tile(m, tm), _tile(n, tn), _tile(k, tk)
    nk = k // tk
    if mode == "tn":
        a_spec = pl.BlockSpec((tk, tm), lambda i, j, l: (l, i))
        lhs_dim = 0
    else:
        a_spec = pl.BlockSpec((tm, tk), lambda i, j, l: (i, l))
        lhs_dim = 1
    if mode == "nt":
        b_spec = pl.BlockSpec((tn, tk), lambda i, j, l: (j, l))
        rhs_dim = 1
    else:
        b_spec = pl.BlockSpec((tk, tn), lambda i, j, l: (l, j))
        rhs_dim = 0
    dims = (((lhs_dim,), (rhs_dim,)), ((), ()))

    def body(a_ref, b_ref, o_ref, acc_ref):
        l = pl.program_id(2)

        @pl.when(l == 0)
        def _():
            acc_ref[...] = jnp.zeros_like(acc_ref)

        acc_ref[...] += lax.dot_general(a_ref[...].astype(BF16), b_ref[...].astype(BF16), dims,
                                        preferred_element_type=F32)

        @pl.when(l == nk - 1)
        def _():
            o_ref[...] = acc_ref[...].astype(o_ref.dtype)

    return pl.pallas_call(
        body,
        name=name,
        grid=(m // tm, n // tn, nk),
        in_specs=[a_spec, b_spec],
        out_specs=pl.BlockSpec((tm, tn), lambda i, j, l: (i, j)),
        out_shape=jax.ShapeDtypeStruct((m, n), out_dtype),
        scratch_shapes=[pltpu.VMEM((tm, tn), F32)],
        compiler_params=_params(),
    )(a, b)


def _mm_nt_sum(pairs, name, tm=512, tn=1024):
    m, n = pairs[0][0].shape[0], pairs[0][1].shape[0]
    tm, tn = _tile(m, tm), _tile(n, tn)

    def body(*refs):
        total = None
        for p in range(len(pairs)):
            part = _dot_nt(refs[2 * p][...].astype(BF16), refs[2 * p + 1][...].astype(BF16))
            total = part if total is None else total + part
        refs[-1][...] = total

    in_specs = []
    for a, b in pairs:
        in_specs += [pl.BlockSpec((tm, a.shape[1]), lambda i, j: (i, 0)),
                     pl.BlockSpec((tn, b.shape[1]), lambda i, j: (j, 0))]
    return pl.pallas_call(
        body,
        name=name,
        grid=(m // tm, n // tn),
        in_specs=in_specs,
        out_specs=pl.BlockSpec((tm, tn), lambda i, j: (i, j)),
        out_shape=jax.ShapeDtypeStruct((m, n), F32),
        compiler_params=_params(),
    )(*[r for pair in pairs for r in pair])


def _rowwise(fn, ins, outs, reds, tm, name):
    rows = ins[0].shape[0]
    tm = min(tm, rows)
    n_in, n_out = len(ins), len(outs)

    def spec(shape):
        if shape[0] == rows:
            return pl.BlockSpec((tm, shape[1]), lambda i: (i, 0))
        return pl.BlockSpec(shape, lambda i: (0, 0))

    def body(*refs):
        i = pl.program_id(0)
        vals = fn(*[r[...] for r in refs[:n_in]])
        for r, v in zip(refs[n_in:n_in + n_out], vals[:n_out]):
            r[...] = v.astype(r.dtype)
        for r, v in zip(refs[n_in + n_out:], vals[n_out:]):
            @pl.when(i == 0)
            def _():
                r[...] = jnp.zeros_like(r)

            r[...] += v

    out_shape = [jax.ShapeDtypeStruct((rows, d), dt) for d, dt in outs]
    out_shape += [jax.ShapeDtypeStruct((1, d), F32) for d in reds]
    out_specs = [spec(s.shape) for s in out_shape]
    return pl.pallas_call(
        body,
        name=name,
        grid=(rows // tm,),
        in_specs=[spec(a.shape) for a in ins],
        out_specs=out_specs,
        out_shape=out_shape,
        compiler_params=_params(),
    )(*ins)


def _colsum(v):
    return jnp.sum(v, axis=0, keepdims=True)


def _rms(v):
    r = lax.rsqrt(jnp.mean(v * v, axis=-1, keepdims=True) + EPS)
    return r, v * r


def _rms_bwd(dy, r, vhat, g):
    u = dy * g
    return r * (u - vhat * jnp.mean(u * vhat, axis=-1, keepdims=True))


def _sigmoid(t):
    return 1.0 / (1.0 + jnp.exp(-t))


def _lane(shape):
    return lax.broadcasted_iota(jnp.int32, shape, len(shape) - 1)


def _rope_swap(v):
    lane = _lane(v.shape)
    up = pltpu.roll(v, LANES - 16, 1)
    down = pltpu.roll(v, 16, 1)
    return jnp.where((lane >= 64) & (lane < 80), up, jnp.where((lane >= 80) & (lane < 96), down, 0.0))


SB_DEAD = -120.0


def _softplus(z):
    return jnp.maximum(z, 0.0) + jnp.log(1.0 + jnp.exp(-jnp.abs(z)))


def _split_dot(v, m2):
    hi = v.astype(BF16)
    lo = (v - hi.astype(F32)).astype(BF16)
    return jnp.dot(jnp.concatenate([hi, lo], axis=1), m2, preferred_element_type=F32)


def _dot_nt(a, b):
    return lax.dot_general(a, b, (((1,), (1,)), ((), ())), preferred_element_type=F32)


def _dot_tn(a, b):
    return lax.dot_general(a, b, (((0,), (0,)), ((), ())), preferred_element_type=F32)


def _sb_fwd(qkv, name):
    s = qkv.shape[0]
    t = min(ATT_TILE, s // 2)
    npair = SB_WIDTH // LANES

    def body(q_ref, k_ref, v_ref, o_ref, t_ref, first_ref):
        i = pl.program_id(1)
        lane = _lane((1, LANES))
        row = lax.broadcasted_iota(jnp.int32, (t, t), 0)
        col = lax.broadcasted_iota(jnp.int32, (t, t), 1)
        tri = col < row
        m_from = jnp.where(row >= col, 1.0, 0.0).astype(BF16)
        m_from = jnp.concatenate([m_from, m_from], axis=0)
        masks = (lane < HALF, lane >= HALF)
        qhs = [jnp.where(hm, q_ref[n * t:(n + 1) * t, :], jnp.zeros((t, LANES), BF16)) * 0.125
               for n in range(2) for hm in masks]

        def step(kb, carry, diags):
            rows = pl.ds(pl.multiple_of(kb * t, t), t)
            ks = k_ref[rows, :]
            vs = v_ref[rows, :]
            vhs = [jnp.where(hm, vs, jnp.zeros_like(vs)) for hm in masks]
            act = [ch for ch in range(4) if diags[ch // 2] is not None]
            zs = {ch: _dot_nt(qhs[ch], ks) for ch in act}
            lbs = {}
            for ch in act:
                lb = -_softplus(zs[ch])
                lbs[ch] = jnp.where(tri, lb, 0.0) if diags[ch // 2] else lb
            sums = {ch: _split_dot(lbs[ch], m_from) for ch in act}
            out = list(carry)
            for ch in act:
                c, acc = carry[ch]
                a = jnp.exp(zs[ch] + sums[ch] + c)
                if diags[ch // 2]:
                    a = jnp.where(tri, a, 0.0)
                acc = acc + jnp.dot(a.astype(BF16), vhs[ch % 2], preferred_element_type=F32)
                out[ch] = (c + sums[ch][:, 0:1], acc)
            return tuple(out)

        def live(carry):
            top = jnp.max(carry[0][0])
            for ch in range(1, 4):
                top = jnp.maximum(top, jnp.max(carry[ch][0]))
            return top

        init = (jnp.zeros((t, 1), F32), jnp.zeros((t, LANES), F32))
        carry = step(2 * i + 1, (init,) * 4, (None, True))
        carry = step(2 * i, carry, (True, False))
        blocks_left = 2 * i

        def more(state):
            n, top, _ = state
            return (n < blocks_left) & (top > SB_DEAD)

        def walk(state):
            n, _, cr = state
            cr = step(blocks_left - 1 - n, cr, (False, False))
            return n + 1, live(cr), cr

        n_done, _, carry = lax.while_loop(more, walk, (jnp.int32(0), live(carry), carry))
        first_ref[0, 0] = jnp.full((8, LANES), (blocks_left - n_done).astype(F32))
        for n in range(2):
            (c0, acc0), (c1, acc1) = carry[2 * n], carry[2 * n + 1]
            o_ref[n * t:(n + 1) * t, :] = acc0 + acc1
            t_ref[n * t:(n + 1) * t, :] = jnp.where(masks[0], c0, c1)

    blk = pl.BlockSpec((2 * t, LANES), lambda p, i: (i, p))
    steps = s // (2 * t)
    return pl.pallas_call(
        body,
        name=name,
        grid=(npair, steps),
        in_specs=[
            blk,
            pl.BlockSpec((s, LANES), lambda p, i: (0, npair + p)),
            pl.BlockSpec((s, LANES), lambda p, i: (0, 2 * npair + p)),
        ],
        out_specs=[blk, blk, pl.BlockSpec((1, 1, 8, LANES), lambda p, i: (p, i, 0, 0))],
        out_shape=[jax.ShapeDtypeStruct((s, SB_WIDTH), F32), jax.ShapeDtypeStruct((s, SB_WIDTH), F32),
                   jax.ShapeDtypeStruct((npair, steps, 8, LANES), F32)],
        compiler_params=_params(),
    )(qkv, qkv, qkv)


def _sb_bwd(qkv, do, tot, first, name):
    s = qkv.shape[0]
    t = min(ATT_TILE, s // 2)
    npair = SB_WIDTH // LANES

    def body(q_ref, k_ref, v_ref, do_ref, t_ref, first_ref, dq_ref, dk_ref, dv_ref):
        i = pl.program_id(1)

        @pl.when(i == 0)
        def _():
            dk_ref[...] = jnp.zeros_like(dk_ref)
            dv_ref[...] = jnp.zeros_like(dv_ref)

        lane = _lane((1, LANES))
        row = lax.broadcasted_iota(jnp.int32, (t, t), 0)
        col = lax.broadcasted_iota(jnp.int32, (t, t), 1)
        tri = col < row
        m_from = jnp.where(row >= col, 1.0, 0.0).astype(BF16)
        m_from = jnp.concatenate([m_from, m_from], axis=0)
        m_upto = jnp.where(row <= col, 1.0, 0.0).astype(BF16)
        m_upto = jnp.concatenate([m_upto, m_upto], axis=0)
        masks = (lane < HALF, lane >= HALF)
        zero_b = jnp.zeros((t, LANES), BF16)
        qhs, dohs, tots = [], [], []
        for n in range(2):
            rs = slice(n * t, (n + 1) * t)
            do_b = do_ref[rs, :].astype(BF16)
            for h, hm in enumerate(masks):
                qhs.append(jnp.where(hm, q_ref[rs, :], zero_b))
                dohs.append(jnp.where(hm, do_b, zero_b))
                tots.append(t_ref[rs, h * HALF:h * HALF + 1])

        def step(kb, carry, diags):
            rows = pl.ds(pl.multiple_of(kb * t, t), t)
            ks = k_ref[rows, :]
            vs = v_ref[rows, :]
            khs = [jnp.where(hm, ks, jnp.zeros_like(ks)) for hm in masks]
            act = [ch for ch in range(4) if diags[ch // 2] is not None]
            zs = {ch: _dot_nt(qhs[ch] * 0.125, ks) for ch in act}
            das = {ch: _dot_nt(dohs[ch], vs) for ch in act}
            lbs, betas = {}, {}
            for ch in act:
                sp = _softplus(zs[ch])
                betas[ch] = jnp.exp(zs[ch] - sp)
                lbs[ch] = jnp.where(tri, -sp, 0.0) if diags[ch // 2] else -sp
            sums = {ch: _split_dot(lbs[ch], m_from) for ch in act}
            avs, es, befores = {}, {}, {}
            for ch in act:
                before = carry[ch][0]
                total = sums[ch][:, 0:1]
                a = jnp.exp(zs[ch] + sums[ch] + (tots[ch] - before - total))
                if diags[ch // 2]:
                    a = jnp.where(tri, a, 0.0)
                avs[ch] = a.astype(BF16)
                es[ch] = a * das[ch]
                befores[ch] = before + total
            e_sums = {ch: _split_dot(es[ch], m_upto) for ch in act}
            dzs, e_befores = {}, {}
            for ch in act:
                e_upto = e_sums[ch] + carry[ch][1]
                dz = es[ch] - betas[ch] * e_upto
                if diags[ch // 2]:
                    dz = jnp.where(tri, dz, 0.0)
                dzs[ch] = (dz * 0.125).astype(BF16)
                e_befores[ch] = e_upto[:, t - 1:t]
            out = list(carry)
            dk = jnp.zeros((t, LANES), F32)
            dv = jnp.zeros((t, LANES), F32)
            for ch in act:
                dq = carry[ch][2] + jnp.dot(dzs[ch], khs[ch % 2], preferred_element_type=F32)
                dk = dk + _dot_tn(dzs[ch], qhs[ch])
                dv = dv + _dot_tn(avs[ch], dohs[ch])
                out[ch] = (befores[ch], e_befores[ch], dq)
            dk_ref[rows, :] += dk
            dv_ref[rows, :] += dv
            return tuple(out)

        zero = jnp.zeros((t, 1), F32)
        init = (zero, zero, jnp.zeros((t, LANES), F32))
        start = jnp.clip(jnp.max(first_ref[0, 0]).astype(jnp.int32), 0, 2 * i)
        carry = lax.fori_loop(start, 2 * i, lambda kb, cr: step(kb, cr, (False, False)), (init,) * 4)
        carry = step(2 * i, carry, (True, False))
        carry = step(2 * i + 1, carry, (None, True))
        for n in range(2):
            dq_ref[n * t:(n + 1) * t, :] = carry[2 * n][2] + carry[2 * n + 1][2]

    blk = pl.BlockSpec((2 * t, LANES), lambda p, i: (i, p))
    full = pl.BlockSpec((s, LANES), lambda p, i: (0, p))
    out = jax.ShapeDtypeStruct((s, SB_WIDTH), F32)
    return pl.pallas_call(
        body,
        name=name,
        grid=(npair, s // (2 * t)),
        in_specs=[
            blk,
            pl.BlockSpec((s, LANES), lambda p, i: (0, npair + p)),
            pl.BlockSpec((s, LANES), lambda p, i: (0, 2 * npair + p)),
            blk,
            blk,
            pl.BlockSpec((1, 1, 8, LANES), lambda p, i: (p, i, 0, 0)),
        ],
        out_specs=[blk, full, full],
        out_shape=[out, out, out],
        compiler_params=_params(),
    )(qkv, qkv, qkv, do, tot, first)


MLA_SCALE = 1.0 / math.sqrt(MLA_QK)
MLA_SCALE_LOG2E = MLA_SCALE * math.log2(math.e)
NEG = -1e30
MLA_QBLOCKS = 8


def _diag_plan(nq):
    return [tuple(None if n < j else n == j for n in range(nq)) for j in range(nq)]


def _mla_fwd(qc, kc, vp, name):
    s = qc.shape[0]
    nq = MLA_QBLOCKS
    t = min(ATT_TILE, s // nq)

    def body(q_ref, k_ref, v_ref, o_ref, lse_ref, m_s, l_s, acc_s):
        i = pl.program_id(1)
        row = lax.broadcasted_iota(jnp.int32, (t, t), 0)
        col = lax.broadcasted_iota(jnp.int32, (t, t), 1)
        keep = col <= row
        m_s[...] = jnp.full(m_s.shape, NEG, F32)
        l_s[...] = jnp.zeros_like(l_s)
        acc_s[...] = jnp.zeros_like(acc_s)

        def step(kb, diags):
            rows = pl.ds(pl.multiple_of(kb * t, t), t)
            ks = k_ref[rows, :]
            vs = v_ref[rows, :]
            act = [n for n in range(nq) if diags[n] is not None]
            scs = {n: _dot_nt(q_ref[n * t:(n + 1) * t, :], ks) for n in act}
            mid = {}
            for n in act:
                sc = jnp.where(keep, scs[n], NEG) if diags[n] else scs[n]
                m = m_s[n]
                m_new = jnp.maximum(m, jnp.max(sc, axis=-1, keepdims=True))
                alpha = jnp.exp2((m - m_new) * MLA_SCALE_LOG2E)
                p = jnp.exp2((sc - jnp.concatenate([m_new] * (t // LANES), axis=1)) * MLA_SCALE_LOG2E)
                m_s[n] = m_new
                l_s[n] = alpha * l_s[n] + jnp.sum(p, axis=-1, keepdims=True)
                mid[n] = (alpha, p.astype(BF16))
            for n in act:
                alpha, p = mid[n]
                acc_s[n] = alpha * acc_s[n] + jnp.dot(p, vs, preferred_element_type=F32)

        def walk(kb, carry):
            step(kb, (False,) * nq)
            return carry

        lax.fori_loop(0, nq * i, walk, 0)
        for j, diags in enumerate(_diag_plan(nq)):
            step(nq * i + j, diags)
        for n in range(nq):
            l = l_s[n]
            o_ref[n * t:(n + 1) * t, :] = acc_s[n] / l
            lse_ref[n * t:(n + 1) * t, :] = m_s[n] * MLA_SCALE + jnp.log(l)

    blk = pl.BlockSpec((nq * t, LANES), lambda h, i: (i, h))
    full = pl.BlockSpec((s, LANES), lambda h, i: (0, h))
    out = jax.ShapeDtypeStruct((s, MLA_HEADS * LANES), F32)
    return pl.pallas_call(
        body,
        name=name,
        grid=(MLA_HEADS, s // (nq * t)),
        in_specs=[blk, full, full],
        out_specs=[blk, blk],
        out_shape=[out, out],
        scratch_shapes=[pltpu.VMEM((nq, t, LANES), F32)] * 3,
        compiler_params=_params(),
    )(qc, kc, vp)


def _mla_bwd(qc, kc, vp, do, o, lse, name):
    s = qc.shape[0]
    nq = MLA_QBLOCKS
    t = min(ATT_TILE, s // nq)

    def body(q_ref, k_ref, v_ref, do_ref, o_ref, lse_ref, dq_ref, dk_ref, dv_ref):
        i = pl.program_id(1)

        @pl.when(i == 0)
        def _():
            dk_ref[...] = jnp.zeros_like(dk_ref)
            dv_ref[...] = jnp.zeros_like(dv_ref)

        row = lax.broadcasted_iota(jnp.int32, (t, t), 0)
        col = lax.broadcasted_iota(jnp.int32, (t, t), 1)
        keep = col <= row

        def block(n):
            rs = slice(n * t, (n + 1) * t)
            do_f = do_ref[rs, :]
            delta = jnp.sum(do_f * o_ref[rs, :], axis=-1, keepdims=True)
            return q_ref[rs, :], do_f.astype(BF16), delta * MLA_SCALE, lse_ref[rs, 0:1] * math.log2(math.e)

        blocks = [block(n) for n in range(nq)]
        dq_ref[...] = jnp.zeros_like(dq_ref)

        def step(kb, diags):
            rows = pl.ds(pl.multiple_of(kb * t, t), t)
            ks = k_ref[rows, :]
            vs = v_ref[rows, :]
            act = [n for n in range(nq) if diags[n] is not None]
            scs = {n: _dot_nt(blocks[n][0], ks) for n in act}
            dps = {n: _dot_nt(blocks[n][1], vs) for n in act}
            ps, dss = {}, {}
            for n in act:
                _, _, delta_s, lse2 = blocks[n]
                p = jnp.exp2(scs[n] * MLA_SCALE_LOG2E - lse2)
                if diags[n]:
                    p = jnp.where(keep, p, 0.0)
                dss[n] = (p * (dps[n] * MLA_SCALE - delta_s)).astype(BF16)
                ps[n] = p.astype(BF16)
            dk = jnp.zeros((t, LANES), F32)
            dv = jnp.zeros((t, LANES), F32)
            for n in act:
                q, do_b, _, _ = blocks[n]
                dq_ref[n * t:(n + 1) * t, :] += jnp.dot(dss[n], ks, preferred_element_type=F32)
                dk = dk + _dot_tn(dss[n], q)
                dv = dv + _dot_tn(ps[n], do_b)
            dk_ref[rows, :] += dk
            dv_ref[rows, :] += dv

        def walk(kb, carry):
            step(kb, (False,) * nq)
            return carry

        lax.fori_loop(0, nq * i, walk, 0)
        for j, diags in enumerate(_diag_plan(nq)):
            step(nq * i + j, diags)

    blk = pl.BlockSpec((nq * t, LANES), lambda h, i: (i, h))
    full = pl.BlockSpec((s, LANES), lambda h, i: (0, h))
    out = jax.ShapeDtypeStruct((s, MLA_HEADS * LANES), F32)
    return pl.pallas_call(
        body,
        name=name,
        grid=(MLA_HEADS, s // (nq * t)),
        in_specs=[blk, full, full, blk, blk, blk],
        out_specs=[blk, full, full],
        out_shape=[out, out, out],
        compiler_params=_params(),
    )(qc, kc, vp, do, o, lse)


def _position():
    return lax.axis_index("x"), lax.axis_index("y"), lax.axis_index("c")


def _other_chips(x, y):
    return [(1 - x, y), (x, 1 - y), (1 - x, 1 - y)]


HBM_SPEC = pl.BlockSpec(memory_space=pltpu.HBM)


def _remote(src, dst, send_sem, recv_sem, device):
    return pltpu.make_async_remote_copy(src_ref=src, dst_ref=dst, send_sem=send_sem, recv_sem=recv_sem,
                                        device_id=device, device_id_type=MESH)


def _dma_sems(n, count):
    return [pltpu.SemaphoreType.DMA((n,)) for _ in range(count)]


def _gather_weights(ws, name):
    na = len(ws)

    def body(*refs):
        srcs, outs = refs[:na], refs[na:2 * na]
        s1, r1, s2, r2, s3, r3, s4, r4 = refs[2 * na:]
        x, y, c = _position()
        me = 2 * x + y
        px, py = jnp.where(c == 1, 1 - x, x), jnp.where(c == 1, y, 1 - y)
        qx, qy = jnp.where(c == 1, x, 1 - x), jnp.where(c == 1, 1 - y, y)
        p_idx, q_idx, d_idx = 2 * px + py, 2 * qx + qy, 2 * (1 - x) + (1 - y)
        partner, across, sibling = (px, py, c), (qx, qy, c), (x, y, 1 - c)
        my_half, other_half = 1 - c, c

        def half(i, slot, h):
            kh = ws[i].shape[0] // 2
            return outs[i].at[slot, pl.ds(h * kh, kh), :]

        sends = [_remote(srcs[i], outs[i].at[me], s1.at[i], r1.at[i], partner) for i in range(na)]
        for cp in sends:
            cp.start()
        for i in range(na):
            _remote(srcs[i], outs[i].at[p_idx], s1.at[i], r1.at[i], partner).wait_recv()
            onward = [_remote(half(i, p_idx, my_half), half(i, p_idx, my_half), s2.at[i], r2.at[i], across),
                      _remote(outs[i].at[p_idx], outs[i].at[p_idx], s3.at[i], r3.at[i], sibling)]
            for cp in onward:
                cp.start()
            sends += onward
        for i in range(na):
            _remote(half(i, d_idx, my_half), half(i, d_idx, my_half), s2.at[i], r2.at[i], across).wait_recv()
            cp = _remote(half(i, d_idx, my_half), half(i, d_idx, my_half), s4.at[i], r4.at[i], sibling)
            cp.start()
            sends.append(cp)
        for i in range(na):
            _remote(outs[i].at[q_idx], outs[i].at[q_idx], s3.at[i], r3.at[i], sibling).wait_recv()
            _remote(half(i, d_idx, other_half), half(i, d_idx, other_half), s4.at[i], r4.at[i], sibling).wait_recv()
        for cp in sends:
            cp.wait_send()

    return pl.pallas_call(
        body,
        name=name,
        in_specs=[HBM_SPEC] * na,
        out_specs=[HBM_SPEC] * na,
        out_shape=[jax.ShapeDtypeStruct((N_CHIPS,) + w.shape, w.dtype) for w in ws],
        scratch_shapes=_dma_sems(na, 8),
    )(*ws)


def _swap_halves(gs, name):
    na = len(gs)

    def body(*refs):
        srcs, outs, send_sems, recv_sems = refs[:na], refs[na:2 * na], refs[2 * na], refs[2 * na + 1]
        x, y, c = _position()
        cps = []
        for i in range(na):
            kh = gs[i].shape[1] // 2
            cps.append(_remote(srcs[i].at[:, pl.ds((1 - c) * kh, kh), :], outs[i], send_sems.at[i], recv_sems.at[i],
                               (x, y, 1 - c)))
            cps[-1].start()
        for cp in cps:
            cp.wait()

    return pl.pallas_call(
        body,
        name=name,
        in_specs=[HBM_SPEC] * na,
        out_specs=[HBM_SPEC] * na,
        out_shape=[jax.ShapeDtypeStruct((g.shape[0], g.shape[1] // 2, g.shape[2]), g.dtype) for g in gs],
        scratch_shapes=_dma_sems(na, 2),
    )(*gs)


def _scatter_chips(parts, name):
    na = len(parts)

    def body(*refs):
        srcs, outs, send_sems, recv_sems = refs[:na], refs[na:2 * na], refs[2 * na], refs[2 * na + 1]
        x, y, c = _position()
        me = 2 * x + y
        sends = []
        for i in range(na):
            for k, (px, py) in enumerate(_other_chips(x, y)):
                cp = _remote(srcs[i].at[2 * px + py], outs[i].at[me], send_sems.at[3 * i + k],
                             recv_sems.at[3 * i + k], (px, py, c))
                cp.start()
                sends.append(cp)
        for i in range(na):
            for k, (px, py) in enumerate(_other_chips(x, y)):
                _remote(srcs[i].at[me], outs[i].at[2 * px + py], send_sems.at[3 * i + k], recv_sems.at[3 * i + k],
                        (px, py, c)).wait_recv()
        for cp in sends:
            cp.wait_send()

    return pl.pallas_call(
        body,
        name=name,
        in_specs=[HBM_SPEC] * na,
        out_specs=[HBM_SPEC] * na,
        out_shape=[jax.ShapeDtypeStruct(p.shape, p.dtype) for p in parts],
        scratch_shapes=_dma_sems(3 * na, 2),
    )(*parts)


def _to_sibling(halves, name):
    na = len(halves)

    def body(*refs):
        srcs, outs, send_sems, recv_sems = refs[:na], refs[na:2 * na], refs[2 * na], refs[2 * na + 1]
        x, y, c = _position()
        cps = [_remote(srcs[i], outs[i], send_sems.at[i], recv_sems.at[i], (x, y, 1 - c)) for i in range(na)]
        for cp in cps:
            cp.start()
        for cp in cps:
            cp.wait()

    return pl.pallas_call(
        body,
        name=name,
        in_specs=[HBM_SPEC] * na,
        out_specs=[HBM_SPEC] * na,
        out_shape=[jax.ShapeDtypeStruct(h.shape, h.dtype) for h in halves],
        scratch_shapes=_dma_sems(na, 2),
    )(*halves)


def _allsum_small(v, name):
    shape = v.shape

    def body(v_ref, o_ref, buf, send_sems, recv_sems):
        x, y, c = _position()
        me = 4 * x + 2 * y + c
        buf[me] = v_ref[...]
        peers = []
        for k in range(1, 8):
            fx, fy, fc = (k >> 2) & 1, (k >> 1) & 1, k & 1
            peers.append((jnp.where(fx == 1, 1 - x, x), jnp.where(fy == 1, 1 - y, y), jnp.where(fc == 1, 1 - c, c)))
        sends = []
        for k, peer in enumerate(peers):
            cp = pltpu.make_async_remote_copy(src_ref=v_ref, dst_ref=buf.at[me], send_sem=send_sems.at[k],
                                              recv_sem=recv_sems.at[k], device_id=peer, device_id_type=MESH)
            cp.start()
            sends.append(cp)
        for k, (px, py, pc) in enumerate(peers):
            pltpu.make_async_remote_copy(src_ref=v_ref, dst_ref=buf.at[4 * px + 2 * py + pc],
                                         send_sem=send_sems.at[k], recv_sem=recv_sems.at[k],
                                         device_id=(px, py, pc), device_id_type=MESH).wait_recv()
        for cp in sends:
            cp.wait_send()
        acc = buf[0]
        for d in range(1, 8):
            acc = acc + buf[d]
        o_ref[...] = acc

    return pl.pallas_call(
        body,
        name=name,
        in_specs=[pl.BlockSpec(memory_space=pltpu.VMEM)],
        out_specs=pl.BlockSpec(memory_space=pltpu.VMEM),
        out_shape=jax.ShapeDtypeStruct(shape, F32),
        scratch_shapes=[pltpu.VMEM((8,) + shape, F32), pltpu.SemaphoreType.DMA((7,)), pltpu.SemaphoreType.DMA((7,))],
    )(v)


def _row_tile(h):
    for t in range(256, 0, -16):
        if h % t == 0:
            return t
    return h


def _add_cores(g, a, name):
    n4, k, n = g.shape
    kh = k // 2
    th = _row_tile(kh)

    def body(g_ref, a_ref, f_ref, b_ref):
        c = lax.axis_index("c")
        total = jnp.where(c == 0, g_ref[0, 0], g_ref[0, 1]) + a_ref[0]
        f_ref[0] = total
        b_ref[0] = total.astype(BF16)

    spec = pl.BlockSpec((1, th, n), lambda j, r: (j, r, 0))
    return pl.pallas_call(
        body,
        name=name,
        grid=(n4, kh // th),
        in_specs=[pl.BlockSpec((1, 2, th, n), lambda j, r: (j, 0, r, 0)), spec],
        out_specs=[spec, spec],
        out_shape=[jax.ShapeDtypeStruct((n4, kh, n), F32), jax.ShapeDtypeStruct((n4, kh, n), BF16)],
        compiler_params=_params(),
    )(g.reshape(n4, 2, kh, n), a)


def _add_chips(pf, b, name):
    n4, h, n = pf.shape
    th = _row_tile(h)

    def body(pf_ref, b_ref, o_ref):
        me = 2 * lax.axis_index("x") + lax.axis_index("y")
        terms = [jnp.where(me == j, pf_ref[j], b_ref[j].astype(F32)) for j in range(n4)]
        o_ref[...] = ((terms[0] + terms[1]) + terms[2]) + terms[3]

    spec = pl.BlockSpec((n4, th, n), lambda r: (0, r, 0))
    return pl.pallas_call(
        body,
        name=name,
        grid=(h // th,),
        in_specs=[spec, spec],
        out_specs=pl.BlockSpec((th, n), lambda r: (r, 0)),
        out_shape=jax.ShapeDtypeStruct((h, n), F32),
        compiler_params=_params(),
    )(pf, b)


def _adam_math(w, g, m, v):
    m = ADAM_B1 * m + (1.0 - ADAM_B1) * g
    v = ADAM_B2 * v + (1.0 - ADAM_B2) * (g * g)
    m_hat = m / (1.0 - ADAM_B1 ** ADAM_STEP)
    v_hat = v / (1.0 - ADAM_B2 ** ADAM_STEP)
    delta = -ADAM_LR * (m_hat / (jnp.sqrt(v_hat) + ADAM_EPS) + ADAM_WD * w)
    return delta, m, v


def _adamw_shard(w, m, v, mine, other, name):
    _, k, n = w.shape
    kh = k // 2
    th = _row_tile(kh)
    nb = kh // th

    def body(w_ref, m_ref, v_ref, a_ref, b_ref, g_out, d_out, m_out, v_out):
        g = jnp.where(pl.program_id(0) == lax.axis_index("c"), a_ref[...], b_ref[...])
        g_out[...] = g
        d_out[...], m_out[...], v_out[...] = _adam_math(w_ref[...], g, m_ref[...], v_ref[...])

    full = pl.BlockSpec((None, th, n), lambda hh, r: (0, hh * nb + r, 0))
    half = pl.BlockSpec((th, n), lambda hh, r: (r, 0))
    return pl.pallas_call(
        body,
        name=name,
        grid=(2, nb),
        in_specs=[full, full, full, half, half],
        out_specs=[full] * 4,
        out_shape=[jax.ShapeDtypeStruct((1, k, n), F32)] * 4,
        compiler_params=_params(),
    )(w, m, v, mine, other)


def _adamw_small(w, g, m, v, name):
    cols = w.shape[1]
    return _rowwise(_adam_math, [w, g, m, v], [(cols, F32)] * 3, [], w.shape[0], name)


def _full_matrices(own, gathered):
    me = 2 * lax.axis_index("x") + lax.axis_index("y")
    out = {}
    for (n, _, _, axis), w, g in zip(BIG, own, gathered):
        out[n] = jnp.concatenate([jnp.where(me == j, w, g[j]) for j in range(N_CHIPS)], axis=axis)
    return out


def _split_for_chips(full):
    out = []
    for n, r, c, axis in BIG:
        g = full[n]
        out.append(jnp.stack([g[:, j * c:(j + 1) * c] for j in range(N_CHIPS)]) if axis == 1
                   else g.reshape(N_CHIPS, r, c))
    return out


def _pack_small(vecs, extra=None):
    flat = jnp.concatenate([vecs[n].reshape(-1) for n, _ in SMALL] + ([extra.reshape(-1)] if extra is not None else []))
    flat = jnp.pad(flat, (0, SMALL_ROWS * PACK_COLS - flat.shape[0]))
    return flat.reshape(SMALL_ROWS, PACK_COLS)


def _unpack_small(pack):
    flat = pack.reshape(-1)
    out, off = {}, 0
    for n, k in SMALL:
        out[n] = flat[off:off + k].reshape(1, k)
        off += k
    return out


def _forward_backward(x, pos, target, sm, w):
    s = x.shape[0]
    w_in = w["w_in"]
    w_a = w_in[:, 0:1536]
    zeros = functools.partial(jnp.zeros, dtype=BF16)
    w_b = jnp.concatenate([w_in[:, 1536:2176], zeros((D_MODEL, 64)), w_in[:, 2176:2208], zeros((D_MODEL, 32))], axis=1)
    w_g = w_in[:, 2208:4256]
    w_uq = jnp.pad(w["w_uq"].reshape(MLA_Q_RANK, MLA_HEADS, MLA_QK), ((0, 0), (0, 0), (0, LANES - MLA_QK)))
    w_uq = w_uq.reshape(MLA_Q_RANK, MLA_HEADS * LANES)
    w_ukv = w["w_ukv"]
    w_pm = jnp.pad(w["w_proj_mla"].reshape(MLA_HEADS, HALF, D_MODEL), ((0, 0), (HALF, 0), (0, 0)))
    w_pm = w_pm.reshape(MLA_HEADS * LANES, D_MODEL)
    w_ps, w_out, w_gu, w_dn = w["w_proj_sb"], w["w_out"], w["w_gate_up"], w["w_down"]
    inv_freq = ROPE_THETA ** (-jnp.arange(0, MLA_ROPE, 2, dtype=F32) / MLA_ROPE)
    inv_lane = jnp.concatenate([jnp.zeros((HALF,), F32), inv_freq, inv_freq, jnp.zeros((32,), F32)]).reshape(1, LANES)
    g_pre, g_post, b_gate = sm["norm_mix_pre"], sm["norm_mix_post"], sm["b_gate"]
    g_q, g_kv, g_fpre, g_fpost = sm["q_norm"], sm["kv_norm"], sm["norm_ffn_pre"], sm["norm_ffn_post"]

    def f_h1(x, g):
        return (_rms(x)[1] * g,)

    (h1,) = _rowwise(f_h1, [x, g_pre], [(D_MODEL, BF16)], [], 256, "h1")
    qkv = _mm(h1, w_a, "nn", BF16, "proj_sb")
    lat = _mm(h1, w_b, "nn", F32, "proj_lat")
    gl = _mm(h1, w_g, "nn", F32, "proj_gate")

    def f_lat(lat, gq, gkv, pos, inv):
        cqn = _rms(lat[:, 0:MLA_Q_RANK])[1] * gq
        ckvn = _rms(lat[:, MLA_Q_RANK:MLA_Q_RANK + MLA_KV_RANK])[1] * gkv
        kr = lat[:, 640:768]
        ang = pos.astype(F32) * inv
        lane = _lane(ang.shape)
        cosv, sinv = jnp.cos(ang), jnp.sin(ang)
        rope_c = jnp.where(lane < 64, 1.0, jnp.where(lane < 96, cosv, 0.0))
        rope_s = jnp.where((lane >= 64) & (lane < 80), -sinv, jnp.where((lane >= 80) & (lane < 96), sinv, 0.0))
        krr = jnp.where(lane >= 64, kr * rope_c, 0.0) + _rope_swap(kr) * rope_s
        return cqn, ckvn, krr, rope_c, rope_s

    cqn, ckvn, krr, rope_c, rope_s = _rowwise(
        f_lat, [lat, g_q, g_kv, pos, inv_lane],
        [(MLA_Q_RANK, BF16), (MLA_KV_RANK, BF16), (LANES, F32), (LANES, F32), (LANES, F32)], [], 256, "lat_norm")
    q_raw = _mm(cqn, w_uq, "nn", F32, "q_up")
    kv = _mm(ckvn, w_ukv, "nn", F32, "kv_up")

    def f_heads(q_raw, kv, krr, rc, rs):
        lane = _lane(krr.shape)
        qs, ks, vs = [], [], []
        for h in range(MLA_HEADS):
            qh = q_raw[:, h * LANES:(h + 1) * LANES]
            kvh = kv[:, h * LANES:(h + 1) * LANES]
            qs.append(qh * rc + _rope_swap(qh) * rs)
            ks.append(jnp.where(lane < HALF, kvh, krr))
            vs.append(jnp.where(lane >= HALF, kvh, 0.0))
        return jnp.concatenate(qs, axis=1), jnp.concatenate(ks, axis=1), jnp.concatenate(vs, axis=1)

    wide = MLA_HEADS * LANES
    qc, kc, vp = _rowwise(f_heads, [q_raw, kv, krr, rope_c, rope_s], [(wide, BF16)] * 3, [], 256, "mla_heads")
    o_sb, sb_tot, sb_first = _sb_fwd(qkv, "sb_fwd")
    o_mla, lse = _mla_fwd(qc, kc, vp, "mla_fwd")
    m_sb = _mm(o_sb, w_ps, "nn", F32, "proj_o_sb")
    m_mla = _mm(o_mla, w_pm, "nn", F32, "proj_o_mla")

    def f_merge(m_sb, m_mla, gl, b):
        g = _sigmoid(gl + b)
        return (g[:, :D_MODEL] * m_sb + g[:, D_MODEL:] * m_mla,)

    (merged,) = _rowwise(f_merge, [m_sb, m_mla, gl, b_gate], [(D_MODEL, BF16)], [], 256, "merge")
    y = _mm(merged, w_out, "nn", F32, "mix_out")

    def f_res1(x, y, g2, g3):
        x2 = x + _rms(y)[1] * g2
        return x2, _rms(x2)[1] * g3

    x2, h2 = _rowwise(f_res1, [x, y, g_post, g_fpre], [(D_MODEL, F32), (D_MODEL, BF16)], [], 256, "res1")

    gu = _mm(h2, w_gu, "nn", F32, "ffn_up")

    def f_act(gu):
        g, u = gu[:, :D_FF], gu[:, D_FF:]
        return (g * _sigmoid(g) * u,)

    (act,) = _rowwise(f_act, [gu], [(D_FF, BF16)], [], 128, "ffn_act")
    f = _mm(act, w_dn, "nn", F32, "ffn_down")

    def f_loss(x2, f, g4, tgt):
        r, fhat = _rms(f)
        err = x2 + fhat * g4 - tgt
        loss = 0.5 * jnp.sum(jnp.sum(err * err, axis=-1, keepdims=True), axis=0, keepdims=True) / D_MODEL
        dout = err / D_MODEL
        return dout, _rms_bwd(dout, r, fhat, g4), loss, _colsum(dout * fhat)

    dout, df, loss, d_fpost = _rowwise(f_loss, [x2, f, g_fpost, target], [(D_MODEL, F32), (D_MODEL, BF16)],
                                       [1, D_MODEL], 256, "loss")

    dact = _mm(df, w_dn, "nt", F32, "d_act")
    dw_dn = _mm(act, df, "tn", F32, "dw_down")

    def f_dact(gu, dact):
        g, u = gu[:, :D_FF], gu[:, D_FF:]
        sg = _sigmoid(g)
        dg = dact * u * (sg * (1.0 + g * (1.0 - sg)))
        return (jnp.concatenate([dg, dact * (g * sg)], axis=1),)

    (dgu,) = _rowwise(f_dact, [gu, dact], [(2 * D_FF, BF16)], [], 128, "d_gu")
    dh2 = _mm(dgu, w_gu, "nt", F32, "d_h2")
    dw_gu = _mm(h2, dgu, "tn", F32, "dw_gate_up")

    def f_dres1(dout, dh2, x2, y, g3, g2):
        r3, x2hat = _rms(x2)
        dx2 = dout + _rms_bwd(dh2, r3, x2hat, g3)
        r2, yhat = _rms(y)
        return dx2, _rms_bwd(dx2, r2, yhat, g2), _colsum(dh2 * x2hat), _colsum(dx2 * yhat)

    dx2, dy, d_fpre, d_post = _rowwise(f_dres1, [dout, dh2, x2, y, g_fpre, g_post],
                                       [(D_MODEL, F32), (D_MODEL, BF16)], [D_MODEL, D_MODEL], 256, "d_res1")

    dmerged = _mm(dy, w_out, "nt", F32, "d_merged")
    dw_out = _mm(merged, dy, "tn", F32, "dw_out")

    def f_dmerge(dm, m_sb, m_mla, gl, b):
        g = _sigmoid(gl + b)
        g0, g1 = g[:, :D_MODEL], g[:, D_MODEL:]
        dgl = jnp.concatenate([dm * m_sb * g0 * (1.0 - g0), dm * m_mla * g1 * (1.0 - g1)], axis=1)
        return dm * g0, dm * g1, dgl, _colsum(dgl)

    dm_sb, dm_mla, dgl, d_bgate = _rowwise(f_dmerge, [dmerged, m_sb, m_mla, gl, b_gate],
                                           [(D_MODEL, BF16), (D_MODEL, BF16), (2 * D_MODEL, BF16)], [2 * D_MODEL],
                                           256, "d_merge")
    do_sb = _mm(dm_sb, w_ps, "nt", BF16, "d_o_sb")
    dw_ps = _mm(o_sb, dm_sb, "tn", F32, "dw_proj_sb")
    do_mla = _mm(dm_mla, w_pm, "nt", F32, "d_o_mla")
    dw_pm = _mm(o_mla, dm_mla, "tn", F32, "dw_proj_mla")

    dq_sb, dk_sb, dv_sb = _sb_bwd(qkv, do_sb, sb_tot, sb_first, "sb_bwd")
    dqc, dkc, dvp = _mla_bwd(qc, kc, vp, do_mla, o_mla, lse, "mla_bwd")

    def f_dheads(dqc, dkc, dvp, rc, rs):
        lane = _lane(rc.shape)
        dqs, dkvs = [], []
        dkr = jnp.zeros(rc.shape, F32)
        for h in range(MLA_HEADS):
            dq = dqc[:, h * LANES:(h + 1) * LANES]
            dk = dkc[:, h * LANES:(h + 1) * LANES]
            dqs.append(dq * rc + _rope_swap(dq * rs))
            dkvs.append(jnp.where(lane < HALF, dk, dvp[:, h * LANES:(h + 1) * LANES]))
            dkr = dkr + jnp.where(lane >= HALF, dk, 0.0)
        dkr = dkr * rc + _rope_swap(dkr * rs)
        return jnp.concatenate(dqs, axis=1), jnp.concatenate(dkvs, axis=1), dkr

    dq_raw, dkv, dkr = _rowwise(f_dheads, [dqc, dkc, dvp, rope_c, rope_s],
                                [(wide, BF16), (wide, BF16), (LANES, F32)], [], 256, "d_mla_heads")
    dcqn = _mm(dq_raw, w_uq, "nt", F32, "d_cqn")
    dw_uq = _mm(cqn, dq_raw, "tn", F32, "dw_uq")
    dckvn = _mm(dkv, w_ukv, "nt", F32, "d_ckvn")
    dw_ukv = _mm(ckvn, dkv, "tn", F32, "dw_ukv")

    def f_dlat(lat, dcqn, dckvn, dkr, gq, gkv):
        rq, cqhat = _rms(lat[:, 0:MLA_Q_RANK])
        rkv, ckvhat = _rms(lat[:, MLA_Q_RANK:MLA_Q_RANK + MLA_KV_RANK])
        dlat = jnp.concatenate([_rms_bwd(dcqn, rq, cqhat, gq), _rms_bwd(dckvn, rkv, ckvhat, gkv), dkr], axis=1)
        return dlat, _colsum(dcqn * cqhat), _colsum(dckvn * ckvhat)

    dlat, d_gq, d_gkv = _rowwise(f_dlat, [lat, dcqn, dckvn, dkr, g_q, g_kv], [(768, BF16)],
                                 [MLA_Q_RANK, MLA_KV_RANK], 256, "d_lat")

    sb_parts = [(dq_sb, w_a[:, 0:SB_WIDTH]), (dk_sb, w_a[:, SB_WIDTH:2 * SB_WIDTH]), (dv_sb, w_a[:, 2 * SB_WIDTH:])]
    dh1 = _mm_nt_sum(sb_parts + [(dlat, w_b), (dgl, w_g)], "d_h1")
    dw_sb = [_mm(h1, d, "tn", F32, "dw_in_sb_" + tag) for tag, (d, _) in zip("qkv", sb_parts)]
    dw_b = _mm(h1, dlat, "tn", F32, "dw_in_lat")
    dw_g = _mm(h1, dgl, "tn", F32, "dw_in_gate")

    def f_dx(x, dx2, dh1, g1):
        r, xhat = _rms(x)
        return dx2 + _rms_bwd(dh1, r, xhat, g1), _colsum(dh1 * xhat)

    grad_x, d_pre = _rowwise(f_dx, [x, dx2, dh1, g_pre], [(D_MODEL, F32)], [D_MODEL], 256, "d_x")

    small = {"norm_mix_pre": d_pre, "norm_mix_post": d_post, "b_gate": d_bgate, "q_norm": d_gq, "kv_norm": d_gkv,
             "norm_ffn_pre": d_fpre, "norm_ffn_post": d_fpost}
    big = {
        "w_in": jnp.concatenate(dw_sb + [dw_b[:, 0:640], dw_b[:, 704:736], dw_g], axis=1),
        "w_uq": dw_uq.reshape(MLA_Q_RANK, MLA_HEADS, LANES)[:, :, :MLA_QK].reshape(MLA_Q_RANK, MLA_HEADS * MLA_QK),
        "w_ukv": dw_ukv,
        "w_proj_sb": dw_ps,
        "w_proj_mla": dw_pm.reshape(MLA_HEADS, LANES, D_MODEL)[:, HALF:, :].reshape(MLA_HEADS * HALF, D_MODEL),
        "w_out": dw_out,
        "w_gate_up": dw_gu,
        "w_down": dw_dn,
    }
    return loss, grad_x, small, big


def kernel(x, positions, norm_mix_pre, norm_mix_post, w_in, b_gate, q_norm, w_uq, kv_norm, w_ukv, w_proj_sb, w_proj_mla, w_out, norm_ffn_pre, norm_ffn_post, w_gate_up, w_down, loss_target, m_norm_mix_pre, m_norm_mix_post, m_w_in, m_b_gate, m_q_norm, m_w_uq, m_kv_norm, m_w_ukv, m_w_proj_sb, m_w_proj_mla, m_w_out, m_norm_ffn_pre, m_norm_ffn_post, m_w_gate_up, m_w_down, v_norm_mix_pre, v_norm_mix_post, v_w_in, v_b_gate, v_q_norm, v_w_uq, v_kv_norm, v_w_ukv, v_w_proj_sb, v_w_proj_mla, v_w_out, v_norm_ffn_pre, v_norm_ffn_post, v_w_gate_up, v_w_down):
    given = dict(locals())
    s = x.shape[1]
    names = [n for n, _, _, _ in BIG]
    small_w = {n: given[n] for n, _ in SMALL}

    own = [given[n][0].astype(BF16) for n in names]
    gathered = _gather_weights(own, "gather_weights")
    loss, grad_x, d_small, d_big = _forward_backward(
        x.reshape(s, D_MODEL), positions.reshape(s, 1), loss_target.reshape(s, D_MODEL), small_w,
        _full_matrices(own, gathered))

    pieces = _split_for_chips(d_big)
    from_sibling = _swap_halves(pieces, "grad_swap_halves")
    partial = [_add_cores(g, a, "grad_add_cores_" + n) for n, g, a in zip(names, pieces, from_sibling)]
    received = _scatter_chips([b for _, b in partial], "grad_scatter_chips")
    mine = [_add_chips(f, r, "grad_add_chips_" + n) for n, (f, _), r in zip(names, partial, received)]
    other = _to_sibling(mine, "grad_to_sibling")
    small_sum = _allsum_small(_pack_small(d_small, loss), "small_allsum")

    grads, deltas, new_m, new_v = {}, {}, {}, {}
    for n, a, b in zip(names, mine, other):
        grads[n], deltas[n], new_m[n], new_v[n] = _adamw_shard(given[n], given["m_" + n], given["v_" + n], a, b,
                                                               "adamw_" + n)
    delta_s, m_s, v_s = _adamw_small(_pack_small(small_w), small_sum,
                                     _pack_small({n: given["m_" + n] for n in small_w}),
                                     _pack_small({n: given["v_" + n] for n in small_w}), "adamw_small")

    loss_out = small_sum.reshape(-1)[SMALL_ELEMS]
    grads.update(_unpack_small(small_sum))
    deltas.update(_unpack_small(delta_s))
    new_m.update(_unpack_small(m_s))
    new_v.update(_unpack_small(v_s))
    return (loss_out, grad_x.reshape(1, s, D_MODEL), *[grads[n] for n in WEIGHT_ORDER],
            *[deltas[n] for n in WEIGHT_ORDER], *[new_m[n] for n in WEIGHT_ORDER], *[new_v[n] for n in WEIGHT_ORDER])
```

```python
import functools
import math

import jax
import jax.numpy as jnp
from jax import lax
from jax.experimental import pallas as pl
from jax.experimental.pallas import tpu as pltpu

F32 = jnp.float32
BF16 = jnp.bfloat16
MESH = pl.DeviceIdType.MESH

D_MODEL = 1024
SB_WIDTH = 512
MLA_HEADS = 8
MLA_Q_RANK = 384
MLA_KV_RANK = 256
MLA_ROPE = 32
MLA_QK = 96
D_FF = 2816
ROPE_THETA = 10000.0
EPS = 1e-6
LANES = 128
HALF = 64
ATT_TILE = 256
N_CHIPS = 4

ADAM_LR = 0.001
ADAM_B1 = 0.9
ADAM_B2 = 0.999
ADAM_EPS = 1e-08
ADAM_WD = 0.01
ADAM_STEP = 10

VMEM_LIMIT = 56 * 1024 * 1024

BIG = (
    ("w_in", 1024, 1064, 1),
    ("w_uq", 384, 192, 1),
    ("w_ukv", 256, 256, 1),
    ("w_proj_sb", 512, 256, 1),
    ("w_proj_mla", 512, 256, 1),
    ("w_out", 256, 1024, 0),
    ("w_gate_up", 1024, 1408, 1),
    ("w_down", 704, 1024, 0),
)
PACK_COLS = 1024
SMALL = (
    ("norm_mix_pre", 1024),
    ("norm_mix_post", 1024),
    ("b_gate", 2048),
    ("q_norm", 384),
    ("kv_norm", 256),
    ("norm_ffn_pre", 1024),
    ("norm_ffn_post", 1024),
)
SMALL_ELEMS = sum(n for _, n in SMALL)
SMALL_ROWS = 8
WEIGHT_ORDER = ("norm_mix_pre", "norm_mix_post", "w_in", "b_gate", "q_norm", "w_uq", "kv_norm", "w_ukv",
                "w_proj_sb", "w_proj_mla", "w_out", "norm_ffn_pre", "norm_ffn_post", "w_gate_up", "w_down")


def _params():
    return pltpu.CompilerParams(vmem_limit_bytes=VMEM_LIMIT)


def _tile(dim, pref):
    t = (min(pref, dim) // LANES) * LANES
    while t >= LANES:
        if dim % t == 0:
            return t
        t -= LANES
    return dim


def _mm(a, b, mode, out_dtype, name, tm=1408, tn=1408, tk=1408):
    if mode == "nn":
        (m, k), n = a.shape, b.shape[1]
    elif mode == "nt":
        (m, k), n = a.shape, b.shape[0]
    else:
        (k, m), n = a.shape, b.shape[1]
    tm, tn, tk = _tile(m, tm), _tile(n, tn), _tile(k, tk)
    nk = k // tk
    if mode == "tn":
        a_spec = pl.BlockSpec((tk, tm), lambda i, j, l: (l, i))
        lhs_dim = 0
    else:
        a_spec = pl.BlockSpec((tm, tk), lambda i, j, l: (i, l))
        lhs_dim = 1
    if mode == "nt":
        b_spec = pl.BlockSpec((tn, tk), lambda i, j, l: (j, l))
        rhs_dim = 1
    else:
        b_spec = pl.BlockSpec((tk, tn), lambda i, j, l: (l, j))
        rhs_dim = 0
    dims = (((lhs_dim,), (rhs_dim,)), ((), ()))

    def body(a_ref, b_ref, o_ref, acc_ref):
        l = pl.program_id(2)

        @pl.when(l == 0)
        def _():
            acc_ref[...] = jnp.zeros_like(acc_ref)

        acc_ref[...] += lax.dot_general(a_ref[...].astype(BF16), b_ref[...].astype(BF16), dims,
                                        preferred_element_type=F32)

        @pl.when(l == nk - 1)
        def _():
            o_ref[...] = acc_ref[...].astype(o_ref.dtype)

    return pl.pallas_call(
        body,
        name=name,
        grid=(m // tm, n // tn, nk),
        in_specs=[a_spec, b_spec],
        out_specs=pl.BlockSpec((tm, tn), lambda i, j, l: (i, j)),
        out_shape=jax.ShapeDtypeStruct((m, n), out_dtype),
        scratch_shapes=[pltpu.VMEM((tm, tn), F32)],
        compiler_params=_params(),
    )(a, b)


def _mm_nt_sum(pairs, name, tm=512, tn=1024):
    m, n = pairs[0][0].shape[0], pairs[0][1].shape[0]
    tm, tn = _tile(m, tm), _tile(n, tn)

    def body(*refs):
        total = None
        for p in range(len(pairs)):
            part = _dot_nt(refs[2 * p][...].astype(BF16), refs[2 * p + 1][...].astype(BF16))
            total = part if total is None else total + part
        refs[-1][...] = total

    in_specs = []
    for a, b in pairs:
        in_specs += [pl.BlockSpec((tm, a.shape[1]), lambda i, j: (i, 0)),
                     pl.BlockSpec((tn, b.shape[1]), lambda i, j: (j, 0))]
    return pl.pallas_call(
        body,
        name=name,
        grid=(m // tm, n // tn),
        in_specs=in_specs,
        out_specs=pl.BlockSpec((tm, tn), lambda i, j: (i, j)),
        out_shape=jax.ShapeDtypeStruct((m, n), F32),
        compiler_params=_params(),
    )(*[r for pair in pairs for r in pair])


def _ffn_up(h, w_gu, name):
    s, k = h.shape
    tm, tn = _tile(s, 512), _tile(D_FF, 1408)
    nb = D_FF // tn

    def body(h_ref, wg_ref, wu_ref, g_ref, u_ref, act_ref):
        hv = h_ref[...]
        g = jnp.dot(hv, wg_ref[...], preferred_element_type=F32)
        u = jnp.dot(hv, wu_ref[...], preferred_element_type=F32)
        g_ref[...] = g
        u_ref[...] = u
        act_ref[...] = (g * _sigmoid(g) * u).astype(BF16)

    tile = pl.BlockSpec((tm, tn), lambda j, i: (i, j))
    out = jax.ShapeDtypeStruct((s, D_FF), F32)
    return pl.pallas_call(
        body,
        name=name,
        grid=(nb, s // tm),
        in_specs=[pl.BlockSpec((tm, k), lambda j, i: (i, 0)), pl.BlockSpec((k, tn), lambda j, i: (0, j)),
                  pl.BlockSpec((k, tn), lambda j, i: (0, nb + j))],
        out_specs=[tile, tile, tile],
        out_shape=[out, out, jax.ShapeDtypeStruct((s, D_FF), BF16)],
        compiler_params=_params(),
    )(h, w_gu, w_gu)


def _ffn_dact(df, w_dn, g, u, name):
    s, k = df.shape
    tm, tn = _tile(s, 512), _tile(D_FF, 1408)

    def body(df_ref, w_ref, g_ref, u_ref, dg_ref, du_ref):
        dact = _dot_nt(df_ref[...], w_ref[...])
        gv, uv = g_ref[...], u_ref[...]
        sg = _sigmoid(gv)
        dg_ref[...] = (dact * uv * (sg * (1.0 + gv * (1.0 - sg)))).astype(BF16)
        du_ref[...] = (dact * (gv * sg)).astype(BF16)

    tile = pl.BlockSpec((tm, tn), lambda j, i: (i, j))
    out = jax.ShapeDtypeStruct((s, D_FF), BF16)
    return pl.pallas_call(
        body,
        name=name,
        grid=(D_FF // tn, s // tm),
        in_specs=[pl.BlockSpec((tm, k), lambda j, i: (i, 0)), pl.BlockSpec((tn, k), lambda j, i: (j, 0)), tile, tile],
        out_specs=[tile, tile],
        out_shape=[out, out],
        compiler_params=_params(),
    )(df, w_dn, g, u)


def _rowwise(fn, ins, outs, reds, tm, name):
    rows = ins[0].shape[0]
    tm = min(tm, rows)
    n_in, n_out = len(ins), len(outs)

    def spec(shape):
        if shape[0] == rows:
            return pl.BlockSpec((tm, shape[1]), lambda i: (i, 0))
        return pl.BlockSpec(shape, lambda i: (0, 0))

    def body(*refs):
        i = pl.program_id(0)
        vals = fn(*[r[...] for r in refs[:n_in]])
        for r, v in zip(refs[n_in:n_in + n_out], vals[:n_out]):
            r[...] = v.astype(r.dtype)
        for r, v in zip(refs[n_in + n_out:], vals[n_out:]):
            @pl.when(i == 0)
            def _():
                r[...] = jnp.zeros_like(r)

            r[...] += v

    out_shape = [jax.ShapeDtypeStruct((rows, d), dt) for d, dt in outs]
    out_shape += [jax.ShapeDtypeStruct((1, d), F32) for d in reds]
    out_specs = [spec(s.shape) for s in out_shape]
    return pl.pallas_call(
        body,
        name=name,
        grid=(rows // tm,),
        in_specs=[spec(a.shape) for a in ins],
        out_specs=out_specs,
        out_shape=out_shape,
        compiler_params=_params(),
    )(*ins)


def _colsum(v):
    return jnp.sum(v, axis=0, keepdims=True)


def _rms(v):
    r = lax.rsqrt(jnp.mean(v * v, axis=-1, keepdims=True) + EPS)
    return r, v * r


def _rms_bwd(dy, r, vhat, g):
    u = dy * g
    return r * (u - vhat * jnp.mean(u * vhat, axis=-1, keepdims=True))


def _sigmoid(t):
    return 1.0 / (1.0 + jnp.exp(-t))


def _lane(shape):
    return lax.broadcasted_iota(jnp.int32, shape, len(shape) - 1)


def _rope_swap(v):
    lane = _lane(v.shape)
    up = pltpu.roll(v, LANES - 16, 1)
    down = pltpu.roll(v, 16, 1)
    return jnp.where((lane >= 64) & (lane < 80), up, jnp.where((lane >= 80) & (lane < 96), down, 0.0))


SB_DEAD = -120.0


def _softplus(z):
    return jnp.maximum(z, 0.0) + jnp.log(1.0 + jnp.exp(-jnp.abs(z)))


def _split_dot(v, m2):
    hi = v.astype(BF16)
    lo = (v - hi.astype(F32)).astype(BF16)
    return jnp.dot(jnp.concatenate([hi, lo], axis=1), m2, preferred_element_type=F32)


def _dot_nt(a, b):
    return lax.dot_general(a, b, (((1,), (1,)), ((), ())), preferred_element_type=F32)


def _dot_tn(a, b):
    return lax.dot_general(a, b, (((0,), (0,)), ((), ())), preferred_element_type=F32)


def _sb_fwd(qkv, name):
    s = qkv.shape[0]
    t = min(ATT_TILE, s // 2)
    npair = SB_WIDTH // LANES

    def body(q_ref, k_ref, v_ref, o_ref, t_ref, first_ref):
        i = pl.program_id(1)
        lane = _lane((1, LANES))
        row = lax.broadcasted_iota(jnp.int32, (t, t), 0)
        col = lax.broadcasted_iota(jnp.int32, (t, t), 1)
        tri = col < row
        m_from = jnp.where(row >= col, 1.0, 0.0).astype(BF16)
        m_from = jnp.concatenate([m_from, m_from], axis=0)
        masks = (lane < HALF, lane >= HALF)
        qhs = [jnp.where(hm, q_ref[n * t:(n + 1) * t, :], jnp.zeros((t, LANES), BF16)) * 0.125
               for n in range(2) for hm in masks]

        def step(kb, carry, diags):
            rows = pl.ds(pl.multiple_of(kb * t, t), t)
            ks = k_ref[rows, :]
            vs = v_ref[rows, :]
            vhs = [jnp.where(hm, vs, jnp.zeros_like(vs)) for hm in masks]
            act = [ch for ch in range(4) if diags[ch // 2] is not None]
            zs = {ch: _dot_nt(qhs[ch], ks) for ch in act}
            lbs = {}
            for ch in act:
                lb = -_softplus(zs[ch])
                lbs[ch] = jnp.where(tri, lb, 0.0) if diags[ch // 2] else lb
            sums = {ch: _split_dot(lbs[ch], m_from) for ch in act}
            out = list(carry)
            for ch in act:
                c, acc = carry[ch]
                a = jnp.exp(zs[ch] + sums[ch] + c)
                if diags[ch // 2]:
                    a = jnp.where(tri, a, 0.0)
                acc = acc + jnp.dot(a.astype(BF16), vhs[ch % 2], preferred_element_type=F32)
                out[ch] = (c + sums[ch][:, 0:1], acc)
            return tuple(out)

        def live(carry):
            top = jnp.max(carry[0][0])
            for ch in range(1, 4):
                top = jnp.maximum(top, jnp.max(carry[ch][0]))
            return top

        init = (jnp.zeros((t, 1), F32), jnp.zeros((t, LANES), F32))
        carry = step(2 * i + 1, (init,) * 4, (None, True))
        carry = step(2 * i, carry, (True, False))
        blocks_left = 2 * i

        def more(state):
            n, top, _ = state
            return (n < blocks_left) & (top > SB_DEAD)

        def walk(state):
            n, _, cr = state
            cr = step(blocks_left - 1 - n, cr, (False, False))
            return n + 1, live(cr), cr

        n_done, _, carry = lax.while_loop(more, walk, (jnp.int32(0), live(carry), carry))
        first_ref[0, 0] = jnp.full((8, LANES), (blocks_left - n_done).astype(F32))
        for n in range(2):
            (c0, acc0), (c1, acc1) = carry[2 * n], carry[2 * n + 1]
            o_ref[n * t:(n + 1) * t, :] = acc0 + acc1
            t_ref[n * t:(n + 1) * t, :] = jnp.where(masks[0], c0, c1)

    blk = pl.BlockSpec((2 * t, LANES), lambda p, i: (i, p))
    steps = s // (2 * t)
    return pl.pallas_call(
        body,
        name=name,
        grid=(npair, steps),
        in_specs=[
            blk,
            pl.BlockSpec((s, LANES), lambda p, i: (0, npair + p)),
            pl.BlockSpec((s, LANES), lambda p, i: (0, 2 * npair + p)),
        ],
        out_specs=[blk, blk, pl.BlockSpec((1, 1, 8, LANES), lambda p, i: (p, i, 0, 0))],
        out_shape=[jax.ShapeDtypeStruct((s, SB_WIDTH), F32), jax.ShapeDtypeStruct((s, SB_WIDTH), F32),
                   jax.ShapeDtypeStruct((npair, steps, 8, LANES), F32)],
        compiler_params=_params(),
    )(qkv, qkv, qkv)


def _sb_bwd(qkv, do, tot, first, name):
    s = qkv.shape[0]
    t = min(ATT_TILE, s // 2)
    npair = SB_WIDTH // LANES

    def body(q_ref, k_ref, v_ref, do_ref, t_ref, first_ref, dq_ref, dk_ref, dv_ref):
        i = pl.program_id(1)

        @pl.when(i == 0)
        def _():
            dk_ref[...] = jnp.zeros_like(dk_ref)
            dv_ref[...] = jnp.zeros_like(dv_ref)

        lane = _lane((1, LANES))
        row = lax.broadcasted_iota(jnp.int32, (t, t), 0)
        col = lax.broadcasted_iota(jnp.int32, (t, t), 1)
        tri = col < row
        m_from = jnp.where(row >= col, 1.0, 0.0).astype(BF16)
        m_from = jnp.concatenate([m_from, m_from], axis=0)
        m_upto = jnp.where(row <= col, 1.0, 0.0).astype(BF16)
        m_upto = jnp.concatenate([m_upto, m_upto], axis=0)
        masks = (lane < HALF, lane >= HALF)
        zero_b = jnp.zeros((t, LANES), BF16)
        qhs, dohs, tots = [], [], []
        for n in range(2):
            rs = slice(n * t, (n + 1) * t)
            do_b = do_ref[rs, :].astype(BF16)
            for h, hm in enumerate(masks):
                qhs.append(jnp.where(hm, q_ref[rs, :], zero_b))
                dohs.append(jnp.where(hm, do_b, zero_b))
                tots.append(t_ref[rs, h * HALF:h * HALF + 1])

        def step(kb, carry, diags):
            rows = pl.ds(pl.multiple_of(kb * t, t), t)
            ks = k_ref[rows, :]
            vs = v_ref[rows, :]
            khs = [jnp.where(hm, ks, jnp.zeros_like(ks)) for hm in masks]
            act = [ch for ch in range(4) if diags[ch // 2] is not None]
            zs = {ch: _dot_nt(qhs[ch] * 0.125, ks) for ch in act}
            das = {ch: _dot_nt(dohs[ch], vs) for ch in act}
            lbs, betas = {}, {}
            for ch in act:
                sp = _softplus(zs[ch])
                betas[ch] = jnp.exp(zs[ch] - sp)
                lbs[ch] = jnp.where(tri, -sp, 0.0) if diags[ch // 2] else -sp
            sums = {ch: _split_dot(lbs[ch], m_from) for ch in act}
            avs, es, befores = {}, {}, {}
            for ch in act:
                before = carry[ch][0]
                total = sums[ch][:, 0:1]
                a = jnp.exp(zs[ch] + sums[ch] + (tots[ch] - before - total))
                if diags[ch // 2]:
                    a = jnp.where(tri, a, 0.0)
                avs[ch] = a.astype(BF16)
                es[ch] = a * das[ch]
                befores[ch] = before + total
            e_sums = {ch: _split_dot(es[ch], m_upto) for ch in act}
            dzs, e_befores = {}, {}
            for ch in act:
                e_upto = e_sums[ch] + carry[ch][1]
                dz = es[ch] - betas[ch] * e_upto
                if diags[ch // 2]:
                    dz = jnp.where(tri, dz, 0.0)
                dzs[ch] = (dz * 0.125).astype(BF16)
                e_befores[ch] = e_upto[:, t - 1:t]
            out = list(carry)
            dk = jnp.zeros((t, LANES), F32)
            dv = jnp.zeros((t, LANES), F32)
            for ch in act:
                dq = carry[ch][2] + jnp.dot(dzs[ch], khs[ch % 2], preferred_element_type=F32)
                dk = dk + _dot_tn(dzs[ch], qhs[ch])
                dv = dv + _dot_tn(avs[ch], dohs[ch])
                out[ch] = (befores[ch], e_befores[ch], dq)
            dk_ref[rows, :] += dk
            dv_ref[rows, :] += dv
            return tuple(out)

        zero = jnp.zeros((t, 1), F32)
        init = (zero, zero, jnp.zeros((t, LANES), F32))
        start = jnp.clip(jnp.max(first_ref[0, 0]).astype(jnp.int32), 0, 2 * i)
        carry = lax.fori_loop(start, 2 * i, lambda kb, cr: step(kb, cr, (False, False)), (init,) * 4)
        carry = step(2 * i, carry, (True, False))
        carry = step(2 * i + 1, carry, (None, True))
        for n in range(2):
            dq_ref[n * t:(n + 1) * t, :] = carry[2 * n][2] + carry[2 * n + 1][2]

    blk = pl.BlockSpec((2 * t, LANES), lambda p, i: (i, p))
    full = pl.BlockSpec((s, LANES), lambda p, i: (0, p))
    out = jax.ShapeDtypeStruct((s, SB_WIDTH), F32)
    return pl.pallas_call(
        body,
        name=name,
        grid=(npair, s // (2 * t)),
        in_specs=[
            blk,
            pl.BlockSpec((s, LANES), lambda p, i: (0, npair + p)),
            pl.BlockSpec((s, LANES), lambda p, i: (0, 2 * npair + p)),
            blk,
            blk,
            pl.BlockSpec((1, 1, 8, LANES), lambda p, i: (p, i, 0, 0)),
        ],
        out_specs=[blk, full, full],
        out_shape=[out, out, out],
        compiler_params=_params(),
    )(qkv, qkv, qkv, do, tot, first)


MLA_SCALE = 1.0 / math.sqrt(MLA_QK)
MLA_SCALE_LOG2E = MLA_SCALE * math.log2(math.e)
NEG = -1e30
MLA_QBLOCKS = 8


def _diag_plan(nq):
    return [tuple(None if n < j else n == j for n in range(nq)) for j in range(nq)]


def _mla_fwd(qc, kc, vp, name):
    s = qc.shape[0]
    nq = MLA_QBLOCKS
    t = min(ATT_TILE, s // nq)

    def body(q_ref, k_ref, v_ref, o_ref, lse_ref, m_s, l_s, acc_s):
        i = pl.program_id(1)
        row = lax.broadcasted_iota(jnp.int32, (t, t), 0)
        col = lax.broadcasted_iota(jnp.int32, (t, t), 1)
        keep = col <= row
        m_s[...] = jnp.full(m_s.shape, NEG, F32)
        l_s[...] = jnp.zeros_like(l_s)
        acc_s[...] = jnp.zeros_like(acc_s)

        def step(kb, diags):
            rows = pl.ds(pl.multiple_of(kb * t, t), t)
            ks = k_ref[rows, :]
            vs = v_ref[rows, :]
            act = [n for n in range(nq) if diags[n] is not None]
            scs = {n: _dot_nt(q_ref[n * t:(n + 1) * t, :], ks) for n in act}
            mid = {}
            for n in act:
                sc = jnp.where(keep, scs[n], NEG) if diags[n] else scs[n]
                m = m_s[n]
                m_new = jnp.maximum(m, jnp.max(sc, axis=-1, keepdims=True))
                alpha = jnp.exp2((m - m_new) * MLA_SCALE_LOG2E)
                p = jnp.exp2((sc - jnp.concatenate([m_new] * (t // LANES), axis=1)) * MLA_SCALE_LOG2E)
                m_s[n] = m_new
                l_s[n] = alpha * l_s[n] + jnp.sum(p, axis=-1, keepdims=True)
                mid[n] = (alpha, p.astype(BF16))
            for n in act:
                alpha, p = mid[n]
                acc_s[n] = alpha * acc_s[n] + jnp.dot(p, vs, preferred_element_type=F32)

        def walk(kb, carry):
            step(kb, (False,) * nq)
            return carry

        lax.fori_loop(0, nq * i, walk, 0)
        for j, diags in enumerate(_diag_plan(nq)):
            step(nq * i + j, diags)
        for n in range(nq):
            l = l_s[n]
            o_ref[n * t:(n + 1) * t, :] = acc_s[n] / l
            lse_ref[n * t:(n + 1) * t, :] = m_s[n] * MLA_SCALE + jnp.log(l)

    blk = pl.BlockSpec((nq * t, LANES), lambda h, i: (i, h))
    full = pl.BlockSpec((s, LANES), lambda h, i: (0, h))
    out = jax.ShapeDtypeStruct((s, MLA_HEADS * LANES), F32)
    return pl.pallas_call(
        body,
        name=name,
        grid=(MLA_HEADS, s // (nq * t)),
        in_specs=[blk, full, full],
        out_specs=[blk, blk],
        out_shape=[out, out],
        scratch_shapes=[pltpu.VMEM((nq, t, LANES), F32)] * 3,
        compiler_params=_params(),
    )(qc, kc, vp)


def _mla_bwd(qc, kc, vp, do, o, lse, name):
    s = qc.shape[0]
    nq = MLA_QBLOCKS
    t = min(ATT_TILE, s // nq)

    def body(q_ref, k_ref, v_ref, do_ref, o_ref, lse_ref, dq_ref, dk_ref, dv_ref):
        i = pl.program_id(1)

        @pl.when(i == 0)
        def _():
            dk_ref[...] = jnp.zeros_like(dk_ref)
            dv_ref[...] = jnp.zeros_like(dv_ref)

        row = lax.broadcasted_iota(jnp.int32, (t, t), 0)
        col = lax.broadcasted_iota(jnp.int32, (t, t), 1)
        keep = col <= row

        def block(n):
            rs = slice(n * t, (n + 1) * t)
            do_f = do_ref[rs, :]
            delta = jnp.sum(do_f * o_ref[rs, :], axis=-1, keepdims=True)
            return q_ref[rs, :], do_f.astype(BF16), delta * MLA_SCALE, lse_ref[rs, 0:1] * math.log2(math.e)

        blocks = [block(n) for n in range(nq)]
        dq_ref[...] = jnp.zeros_like(dq_ref)

        def step(kb, diags):
            rows = pl.ds(pl.multiple_of(kb * t, t), t)
            ks = k_ref[rows, :]
            vs = v_ref[rows, :]
            act = [n for n in range(nq) if diags[n] is not None]
            scs = {n: _dot_nt(blocks[n][0], ks) for n in act}
            dps = {n: _dot_nt(blocks[n][1], vs) for n in act}
            ps, dss = {}, {}
            for n in act:
                _, _, delta_s, lse2 = blocks[n]
                p = jnp.exp2(scs[n] * MLA_SCALE_LOG2E - lse2)
                if diags[n]:
                    p = jnp.where(keep, p, 0.0)
                dss[n] = (p * (dps[n] * MLA_SCALE - delta_s)).astype(BF16)
                ps[n] = p.astype(BF16)
            dk = jnp.zeros((t, LANES), F32)
            dv = jnp.zeros((t, LANES), F32)
            for n in act:
                q, do_b, _, _ = blocks[n]
                dq_ref[n * t:(n + 1) * t, :] += jnp.dot(dss[n], ks, preferred_element_type=F32)
                dk = dk + _dot_tn(dss[n], q)
                dv = dv + _dot_tn(ps[n], do_b)
            dk_ref[rows, :] += dk
            dv_ref[rows, :] += dv

        def walk(kb, carry):
            step(kb, (False,) * nq)
            return carry

        lax.fori_loop(0, nq * i, walk, 0)
        for j, diags in enumerate(_diag_plan(nq)):
            step(nq * i + j, diags)

    blk = pl.BlockSpec((nq * t, LANES), lambda h, i: (i, h))
    full = pl.BlockSpec((s, LANES), lambda h, i: (0, h))
    out = jax.ShapeDtypeStruct((s, MLA_HEADS * LANES), F32)
    return pl.pallas_call(
        body,
        name=name,
        grid=(MLA_HEADS, s // (nq * t)),
        in_specs=[blk, full, full, blk, blk, blk],
        out_specs=[blk, full, full],
        out_shape=[out, out, out],
        compiler_params=_params(),
    )(qc, kc, vp, do, o, lse)


def _position():
    return lax.axis_index("x"), lax.axis_index("y"), lax.axis_index("c")


def _other_chips(x, y):
    return [(1 - x, y), (x, 1 - y), (1 - x, 1 - y)]


HBM_SPEC = pl.BlockSpec(memory_space=pltpu.HBM)


def _remote(src, dst, send_sem, recv_sem, device):
    return pltpu.make_async_remote_copy(src_ref=src, dst_ref=dst, send_sem=send_sem, recv_sem=recv_sem,
                                        device_id=device, device_id_type=MESH)


def _dma_sems(n, count):
    return [pltpu.SemaphoreType.DMA((n,)) for _ in range(count)]


def _gather_weights(ws, name):
    na = len(ws)

    def body(*refs):
        srcs, outs = refs[:na], refs[na:2 * na]
        s1, r1, s2, r2, s3, r3, s4, r4 = refs[2 * na:]
        x, y, c = _position()
        me = 2 * x + y
        px, py = jnp.where(c == 1, 1 - x, x), jnp.where(c == 1, y, 1 - y)
        qx, qy = jnp.where(c == 1, x, 1 - x), jnp.where(c == 1, 1 - y, y)
        p_idx, q_idx, d_idx = 2 * px + py, 2 * qx + qy, 2 * (1 - x) + (1 - y)
        partner, across, sibling = (px, py, c), (qx, qy, c), (x, y, 1 - c)
        my_half, other_half = 1 - c, c

        def half(i, slot, h):
            kh = ws[i].shape[0] // 2
            return outs[i].at[slot, pl.ds(h * kh, kh), :]

        sends = [_remote(srcs[i], outs[i].at[me], s1.at[i], r1.at[i], partner) for i in range(na)]
        for cp in sends:
            cp.start()
        for i in range(na):
            _remote(srcs[i], outs[i].at[p_idx], s1.at[i], r1.at[i], partner).wait_recv()
            onward = [_remote(half(i, p_idx, my_half), half(i, p_idx, my_half), s2.at[i], r2.at[i], across),
                      _remote(outs[i].at[p_idx], outs[i].at[p_idx], s3.at[i], r3.at[i], sibling)]
            for cp in onward:
                cp.start()
            sends += onward
        for i in range(na):
            _remote(half(i, d_idx, my_half), half(i, d_idx, my_half), s2.at[i], r2.at[i], across).wait_recv()
            cp = _remote(half(i, d_idx, my_half), half(i, d_idx, my_half), s4.at[i], r4.at[i], sibling)
            cp.start()
            sends.append(cp)
        for i in range(na):
            _remote(outs[i].at[q_idx], outs[i].at[q_idx], s3.at[i], r3.at[i], sibling).wait_recv()
            _remote(half(i, d_idx, other_half), half(i, d_idx, other_half), s4.at[i], r4.at[i], sibling).wait_recv()
        for cp in sends:
            cp.wait_send()

    return pl.pallas_call(
        body,
        name=name,
        in_specs=[HBM_SPEC] * na,
        out_specs=[HBM_SPEC] * na,
        out_shape=[jax.ShapeDtypeStruct((N_CHIPS,) + w.shape, w.dtype) for w in ws],
        scratch_shapes=_dma_sems(na, 8),
    )(*ws)


def _swap_halves(gs, name):
    na = len(gs)

    def body(*refs):
        srcs, outs, send_sems, recv_sems = refs[:na], refs[na:2 * na], refs[2 * na], refs[2 * na + 1]
        x, y, c = _position()
        cps = []
        for i in range(na):
            kh = gs[i].shape[1] // 2
            cps.append(_remote(srcs[i].at[:, pl.ds((1 - c) * kh, kh), :], outs[i], send_sems.at[i], recv_sems.at[i],
                               (x, y, 1 - c)))
            cps[-1].start()
        for cp in cps:
            cp.wait()

    return pl.pallas_call(
        body,
        name=name,
        in_specs=[HBM_SPEC] * na,
        out_specs=[HBM_SPEC] * na,
        out_shape=[jax.ShapeDtypeStruct((g.shape[0], g.shape[1] // 2, g.shape[2]), g.dtype) for g in gs],
        scratch_shapes=_dma_sems(na, 2),
    )(*gs)


def _scatter_chips(parts, name):
    na = len(parts)

    def body(*refs):
        srcs, outs, send_sems, recv_sems = refs[:na], refs[na:2 * na], refs[2 * na], refs[2 * na + 1]
        x, y, c = _position()
        me = 2 * x + y
        sends = []
        for i in range(na):
            for k, (px, py) in enumerate(_other_chips(x, y)):
                cp = _remote(srcs[i].at[2 * px + py], outs[i].at[me], send_sems.at[3 * i + k],
                             recv_sems.at[3 * i + k], (px, py, c))
                cp.start()
                sends.append(cp)
        for i in range(na):
            for k, (px, py) in enumerate(_other_chips(x, y)):
                _remote(srcs[i].at[me], outs[i].at[2 * px + py], send_sems.at[3 * i + k], recv_sems.at[3 * i + k],
                        (px, py, c)).wait_recv()
        for cp in sends:
            cp.wait_send()

    return pl.pallas_call(
        body,
        name=name,
        in_specs=[HBM_SPEC] * na,
        out_specs=[HBM_SPEC] * na,
        out_shape=[jax.ShapeDtypeStruct(p.shape, p.dtype) for p in parts],
        scratch_shapes=_dma_sems(3 * na, 2),
    )(*parts)


def _to_sibling(halves, name):
    na = len(halves)

    def body(*refs):
        srcs, outs, send_sems, recv_sems = refs[:na], refs[na:2 * na], refs[2 * na], refs[2 * na + 1]
        x, y, c = _position()
        cps = [_remote(srcs[i], outs[i], send_sems.at[i], recv_sems.at[i], (x, y, 1 - c)) for i in range(na)]
        for cp in cps:
            cp.start()
        for cp in cps:
            cp.wait()

    return pl.pallas_call(
        body,
        name=name,
        in_specs=[HBM_SPEC] * na,
        out_specs=[HBM_SPEC] * na,
        out_shape=[jax.ShapeDtypeStruct(h.shape, h.dtype) for h in halves],
        scratch_shapes=_dma_sems(na, 2),
    )(*halves)


def _allsum_small(v, name):
    shape = v.shape

    def body(v_ref, o_ref, buf, send_sems, recv_sems):
        x, y, c = _position()
        me = 4 * x + 2 * y + c
        buf[me] = v_ref[...]
        peers = []
        for k in range(1, 8):
            fx, fy, fc = (k >> 2) & 1, (k >> 1) & 1, k & 1
            peers.append((jnp.where(fx == 1, 1 - x, x), jnp.where(fy == 1, 1 - y, y), jnp.where(fc == 1, 1 - c, c)))
        sends = []
        for k, peer in enumerate(peers):
            cp = pltpu.make_async_remote_copy(src_ref=v_ref, dst_ref=buf.at[me], send_sem=send_sems.at[k],
                                              recv_sem=recv_sems.at[k], device_id=peer, device_id_type=MESH)
            cp.start()
            sends.append(cp)
        for k, (px, py, pc) in enumerate(peers):
            pltpu.make_async_remote_copy(src_ref=v_ref, dst_ref=buf.at[4 * px + 2 * py + pc],
                                         send_sem=send_sems.at[k], recv_sem=recv_sems.at[k],
                                         device_id=(px, py, pc), device_id_type=MESH).wait_recv()
        for cp in sends:
            cp.wait_send()
        acc = buf[0]
        for d in range(1, 8):
            acc = acc + buf[d]
        o_ref[...] = acc

    return pl.pallas_call(
        body,
        name=name,
        in_specs=[pl.BlockSpec(memory_space=pltpu.VMEM)],
        out_specs=pl.BlockSpec(memory_space=pltpu.VMEM),
        out_shape=jax.ShapeDtypeStruct(shape, F32),
        scratch_shapes=[pltpu.VMEM((8,) + shape, F32), pltpu.SemaphoreType.DMA((7,)), pltpu.SemaphoreType.DMA((7,))],
    )(v)


def _row_tile(h):
    for t in range(256, 0, -16):
        if h % t == 0:
            return t
    return h


def _add_cores(g, a, name):
    n4, k, n = g.shape
    kh = k // 2
    th = _row_tile(kh)

    def body(g_ref, a_ref, f_ref, b_ref):
        c = lax.axis_index("c")
        total = jnp.where(c == 0, g_ref[0, 0], g_ref[0, 1]) + a_ref[0]
        f_ref[0] = total
        b_ref[0] = total.astype(BF16)

    spec = pl.BlockSpec((1, th, n), lambda j, r: (j, r, 0))
    return pl.pallas_call(
        body,
        name=name,
        grid=(n4, kh // th),
        in_specs=[pl.BlockSpec((1, 2, th, n), lambda j, r: (j, 0, r, 0)), spec],
        out_specs=[spec, spec],
        out_shape=[jax.ShapeDtypeStruct((n4, kh, n), F32), jax.ShapeDtypeStruct((n4, kh, n), BF16)],
        compiler_params=_params(),
    )(g.reshape(n4, 2, kh, n), a)


def _add_chips(pf, b, name):
    n4, h, n = pf.shape
    th = _row_tile(h)

    def body(pf_ref, b_ref, o_ref):
        me = 2 * lax.axis_index("x") + lax.axis_index("y")
        terms = [jnp.where(me == j, pf_ref[j], b_ref[j].astype(F32)) for j in range(n4)]
        o_ref[...] = ((terms[0] + terms[1]) + terms[2]) + terms[3]

    spec = pl.BlockSpec((n4, th, n), lambda r: (0, r, 0))
    return pl.pallas_call(
        body,
        name=name,
        grid=(h // th,),
        in_specs=[spec, spec],
        out_specs=pl.BlockSpec((th, n), lambda r: (r, 0)),
        out_shape=jax.ShapeDtypeStruct((h, n), F32),
        compiler_params=_params(),
    )(pf, b)


def _adam_math(w, g, m, v):
    m = ADAM_B1 * m + (1.0 - ADAM_B1) * g
    v = ADAM_B2 * v + (1.0 - ADAM_B2) * (g * g)
    m_hat = m / (1.0 - ADAM_B1 ** ADAM_STEP)
    v_hat = v / (1.0 - ADAM_B2 ** ADAM_STEP)
    delta = -ADAM_LR * (m_hat / (jnp.sqrt(v_hat) + ADAM_EPS) + ADAM_WD * w)
    return delta, m, v


def _adamw_shard(w, m, v, mine, other, name):
    _, k, n = w.shape
    kh = k // 2
    th = _row_tile(kh)
    nb = kh // th

    def body(w_ref, m_ref, v_ref, a_ref, b_ref, g_out, d_out, m_out, v_out):
        g = jnp.where(pl.program_id(0) == lax.axis_index("c"), a_ref[...], b_ref[...])
        g_out[...] = g
        d_out[...], m_out[...], v_out[...] = _adam_math(w_ref[...], g, m_ref[...], v_ref[...])

    full = pl.BlockSpec((None, th, n), lambda hh, r: (0, hh * nb + r, 0))
    half = pl.BlockSpec((th, n), lambda hh, r: (r, 0))
    return pl.pallas_call(
        body,
        name=name,
        grid=(2, nb),
        in_specs=[full, full, full, half, half],
        out_specs=[full] * 4,
        out_shape=[jax.ShapeDtypeStruct((1, k, n), F32)] * 4,
        compiler_params=_params(),
    )(w, m, v, mine, other)


def _adamw_small(w, g, m, v, name):
    cols = w.shape[1]
    return _rowwise(_adam_math, [w, g, m, v], [(cols, F32)] * 3, [], w.shape[0], name)


def _full_matrices(own, gathered):
    me = 2 * lax.axis_index("x") + lax.axis_index("y")
    out = {}
    for (n, _, _, axis), w, g in zip(BIG, own, gathered):
        out[n] = jnp.concatenate([jnp.where(me == j, w, g[j]) for j in range(N_CHIPS)], axis=axis)
    return out


def _split_for_chips(full):
    out = []
    for n, r, c, axis in BIG:
        g = full[n]
        out.append(jnp.stack([g[:, j * c:(j + 1) * c] for j in range(N_CHIPS)]) if axis == 1
                   else g.reshape(N_CHIPS, r, c))
    return out


def _pack_small(vecs, extra=None):
    flat = jnp.concatenate([vecs[n].reshape(-1) for n, _ in SMALL] + ([extra.reshape(-1)] if extra is not None else []))
    flat = jnp.pad(flat, (0, SMALL_ROWS * PACK_COLS - flat.shape[0]))
    return flat.reshape(SMALL_ROWS, PACK_COLS)


def _unpack_small(pack):
    flat = pack.reshape(-1)
    out, off = {}, 0
    for n, k in SMALL:
        out[n] = flat[off:off + k].reshape(1, k)
        off += k
    return out


def _forward_backward(x, pos, target, sm, w):
    s = x.shape[0]
    w_in = w["w_in"]
    w_a = w_in[:, 0:1536]
    zeros = functools.partial(jnp.zeros, dtype=BF16)
    w_b = jnp.concatenate([w_in[:, 1536:2176], zeros((D_MODEL, 64)), w_in[:, 2176:2208], zeros((D_MODEL, 32))], axis=1)
    w_g = w_in[:, 2208:4256]
    w_uq = jnp.pad(w["w_uq"].reshape(MLA_Q_RANK, MLA_HEADS, MLA_QK), ((0, 0), (0, 0), (0, LANES - MLA_QK)))
    w_uq = w_uq.reshape(MLA_Q_RANK, MLA_HEADS * LANES)
    w_ukv = w["w_ukv"]
    w_pm = jnp.pad(w["w_proj_mla"].reshape(MLA_HEADS, HALF, D_MODEL), ((0, 0), (HALF, 0), (0, 0)))
    w_pm = w_pm.reshape(MLA_HEADS * LANES, D_MODEL)
    w_ps, w_out, w_gu, w_dn = w["w_proj_sb"], w["w_out"], w["w_gate_up"], w["w_down"]
    inv_freq = ROPE_THETA ** (-jnp.arange(0, MLA_ROPE, 2, dtype=F32) / MLA_ROPE)
    inv_lane = jnp.concatenate([jnp.zeros((HALF,), F32), inv_freq, inv_freq, jnp.zeros((32,), F32)]).reshape(1, LANES)
    g_pre, g_post, b_gate = sm["norm_mix_pre"], sm["norm_mix_post"], sm["b_gate"]
    g_q, g_kv, g_fpre, g_fpost = sm["q_norm"], sm["kv_norm"], sm["norm_ffn_pre"], sm["norm_ffn_post"]

    def f_h1(x, g):
        return (_rms(x)[1] * g,)

    (h1,) = _rowwise(f_h1, [x, g_pre], [(D_MODEL, BF16)], [], 256, "h1")
    qkv = _mm(h1, w_a, "nn", BF16, "proj_sb")
    lat = _mm(h1, w_b, "nn", F32, "proj_lat")
    gl = _mm(h1, w_g, "nn", F32, "proj_gate")

    def f_lat(lat, gq, gkv, pos, inv):
        cqn = _rms(lat[:, 0:MLA_Q_RANK])[1] * gq
        ckvn = _rms(lat[:, MLA_Q_RANK:MLA_Q_RANK + MLA_KV_RANK])[1] * gkv
        kr = lat[:, 640:768]
        ang = pos.astype(F32) * inv
        lane = _lane(ang.shape)
        cosv, sinv = jnp.cos(ang), jnp.sin(ang)
        rope_c = jnp.where(lane < 64, 1.0, jnp.where(lane < 96, cosv, 0.0))
        rope_s = jnp.where((lane >= 64) & (lane < 80), -sinv, jnp.where((lane >= 80) & (lane < 96), sinv, 0.0))
        krr = jnp.where(lane >= 64, kr * rope_c, 0.0) + _rope_swap(kr) * rope_s
        return cqn, ckvn, krr, rope_c, rope_s

    cqn, ckvn, krr, rope_c, rope_s = _rowwise(
        f_lat, [lat, g_q, g_kv, pos, inv_lane],
        [(MLA_Q_RANK, BF16), (MLA_KV_RANK, BF16), (LANES, F32), (LANES, F32), (LANES, F32)], [], 256, "lat_norm")
    q_raw = _mm(cqn, w_uq, "nn", F32, "q_up")
    kv = _mm(ckvn, w_ukv, "nn", F32, "kv_up")

    def f_heads(q_raw, kv, krr, rc, rs):
        lane = _lane(krr.shape)
        qs, ks, vs = [], [], []
        for h in range(MLA_HEADS):
            qh = q_raw[:, h * LANES:(h + 1) * LANES]
            kvh = kv[:, h * LANES:(h + 1) * LANES]
            qs.append(qh * rc + _rope_swap(qh) * rs)
            ks.append(jnp.where(lane < HALF, kvh, krr))
            vs.append(jnp.where(lane >= HALF, kvh, 0.0))
        return jnp.concatenate(qs, axis=1), jnp.concatenate(ks, axis=1), jnp.concatenate(vs, axis=1)

    wide = MLA_HEADS * LANES
    qc, kc, vp = _rowwise(f_heads, [q_raw, kv, krr, rope_c, rope_s], [(wide, BF16)] * 3, [], 256, "mla_heads")
    o_sb, sb_tot, sb_first = _sb_fwd(qkv, "sb_fwd")
    o_mla, lse = _mla_fwd(qc, kc, vp, "mla_fwd")
    m_sb = _mm(o_sb, w_ps, "nn", F32, "proj_o_sb")
    m_mla = _mm(o_mla, w_pm, "nn", F32, "proj_o_mla")

    def f_merge(m_sb, m_mla, gl, b):
        g = _sigmoid(gl + b)
        return (g[:, :D_MODEL] * m_sb + g[:, D_MODEL:] * m_mla,)

    (merged,) = _rowwise(f_merge, [m_sb, m_mla, gl, b_gate], [(D_MODEL, BF16)], [], 256, "merge")
    y = _mm(merged, w_out, "nn", F32, "mix_out")

    def f_res1(x, y, g2, g3):
        x2 = x + _rms(y)[1] * g2
        return x2, _rms(x2)[1] * g3

    x2, h2 = _rowwise(f_res1, [x, y, g_post, g_fpre], [(D_MODEL, F32), (D_MODEL, BF16)], [], 256, "res1")

    g_ff, u_ff, act = _ffn_up(h2, w_gu, "ffn_up")
    f = _mm(act, w_dn, "nn", F32, "ffn_down")

    def f_loss(x2, f, g4, tgt):
        r, fhat = _rms(f)
        err = x2 + fhat * g4 - tgt
        loss = 0.5 * jnp.sum(jnp.sum(err * err, axis=-1, keepdims=True), axis=0, keepdims=True) / D_MODEL
        dout = err / D_MODEL
        return dout, _rms_bwd(dout, r, fhat, g4), loss, _colsum(dout * fhat)

    dout, df, loss, d_fpost = _rowwise(f_loss, [x2, f, g_fpost, target], [(D_MODEL, F32), (D_MODEL, BF16)],
                                       [1, D_MODEL], 256, "loss")

    dg_ff, du_ff = _ffn_dact(df, w_dn, g_ff, u_ff, "d_act")
    dw_dn = _mm(act, df, "tn", F32, "dw_down")
    dh2 = _mm_nt_sum([(dg_ff, w_gu[:, :D_FF]), (du_ff, w_gu[:, D_FF:])], "d_h2")
    dw_gu = jnp.concatenate([_mm(h2, dg_ff, "tn", F32, "dw_gate"), _mm(h2, du_ff, "tn", F32, "dw_up")], axis=1)

    def f_dres1(dout, dh2, x2, y, g3, g2):
        r3, x2hat = _rms(x2)
        dx2 = dout + _rms_bwd(dh2, r3, x2hat, g3)
        r2, yhat = _rms(y)
        return dx2, _rms_bwd(dx2, r2, yhat, g2), _colsum(dh2 * x2hat), _colsum(dx2 * yhat)

    dx2, dy, d_fpre, d_post = _rowwise(f_dres1, [dout, dh2, x2, y, g_fpre, g_post],
                                       [(D_MODEL, F32), (D_MODEL, BF16)], [D_MODEL, D_MODEL], 256, "d_res1")

    dmerged = _mm(dy, w_out, "nt", F32, "d_merged")
    dw_out = _mm(merged, dy, "tn", F32, "dw_out")

    def f_dmerge(dm, m_sb, m_mla, gl, b):
        g = _sigmoid(gl + b)
        g0, g1 = g[:, :D_MODEL], g[:, D_MODEL:]
        dgl = jnp.concatenate([dm * m_sb * g0 * (1.0 - g0), dm * m_mla * g1 * (1.0 - g1)], axis=1)
        return dm * g0, dm * g1, dgl, _colsum(dgl)

    dm_sb, dm_mla, dgl, d_bgate = _rowwise(f_dmerge, [dmerged, m_sb, m_mla, gl, b_gate],
                                           [(D_MODEL, BF16), (D_MODEL, BF16), (2 * D_MODEL, BF16)], [2 * D_MODEL],
                                           256, "d_merge")
    do_sb = _mm(dm_sb, w_ps, "nt", BF16, "d_o_sb")
    dw_ps = _mm(o_sb, dm_sb, "tn", F32, "dw_proj_sb")
    do_mla = _mm(dm_mla, w_pm, "nt", F32, "d_o_mla")
    dw_pm = _mm(o_mla, dm_mla, "tn", F32, "dw_proj_mla")

    dq_sb, dk_sb, dv_sb = _sb_bwd(qkv, do_sb, sb_tot, sb_first, "sb_bwd")
    dqc, dkc, dvp = _mla_bwd(qc, kc, vp, do_mla, o_mla, lse, "mla_bwd")

    def f_dheads(dqc, dkc, dvp, rc, rs):
        lane = _lane(rc.shape)
        dqs, dkvs = [], []
        dkr = jnp.zeros(rc.shape, F32)
        for h in range(MLA_HEADS):
            dq = dqc[:, h * LANES:(h + 1) * LANES]
            dk = dkc[:, h * LANES:(h + 1) * LANES]
            dqs.append(dq * rc + _rope_swap(dq * rs))
            dkvs.append(jnp.where(lane < HALF, dk, dvp[:, h * LANES:(h + 1) * LANES]))
            dkr = dkr + jnp.where(lane >= HALF, dk, 0.0)
        dkr = dkr * rc + _rope_swap(dkr * rs)
        return jnp.concatenate(dqs, axis=1), jnp.concatenate(dkvs, axis=1), dkr

    dq_raw, dkv, dkr = _rowwise(f_dheads, [dqc, dkc, dvp, rope_c, rope_s],
                                [(wide, BF16), (wide, BF16), (LANES, F32)], [], 256, "d_mla_heads")
    dcqn = _mm(dq_raw, w_uq, "nt", F32, "d_cqn")
    dw_uq = _mm(cqn, dq_raw, "tn", F32, "dw_uq")
    dckvn = _mm(dkv, w_ukv, "nt", F32, "d_ckvn")
    dw_ukv = _mm(ckvn, dkv, "tn", F32, "dw_ukv")

    def f_dlat(lat, dcqn, dckvn, dkr, gq, gkv):
        rq, cqhat = _rms(lat[:, 0:MLA_Q_RANK])
        rkv, ckvhat = _rms(lat[:, MLA_Q_RANK:MLA_Q_RANK + MLA_KV_RANK])
        dlat = jnp.concatenate([_rms_bwd(dcqn, rq, cqhat, gq), _rms_bwd(dckvn, rkv, ckvhat, gkv), dkr], axis=1)
        return dlat, _colsum(dcqn * cqhat), _colsum(dckvn * ckvhat)

    dlat, d_gq, d_gkv = _rowwise(f_dlat, [lat, dcqn, dckvn, dkr, g_q, g_kv], [(768, BF16)],
                                 [MLA_Q_RANK, MLA_KV_RANK], 256, "d_lat")

    sb_parts = [(dq_sb, w_a[:, 0:SB_WIDTH]), (dk_sb, w_a[:, SB_WIDTH:2 * SB_WIDTH]), (dv_sb, w_a[:, 2 * SB_WIDTH:])]
    dh1 = _mm_nt_sum(sb_parts + [(dlat, w_b), (dgl, w_g)], "d_h1")
    dw_sb = [_mm(h1, d, "tn", F32, "dw_in_sb_" + tag) for tag, (d, _) in zip("qkv", sb_parts)]
    dw_b = _mm(h1, dlat, "tn", F32, "dw_in_lat")
    dw_g = _mm(h1, dgl, "tn", F32, "dw_in_gate")

    def f_dx(x, dx2, dh1, g1):
        r, xhat = _rms(x)
        return dx2 + _rms_bwd(dh1, r, xhat, g1), _colsum(dh1 * xhat)

    grad_x, d_pre = _rowwise(f_dx, [x, dx2, dh1, g_pre], [(D_MODEL, F32)], [D_MODEL], 256, "d_x")

    small = {"norm_mix_pre": d_pre, "norm_mix_post": d_post, "b_gate": d_bgate, "q_norm": d_gq, "kv_norm": d_gkv,
             "norm_ffn_pre": d_fpre, "norm_ffn_post": d_fpost}
    big = {
        "w_in": jnp.concatenate(dw_sb + [dw_b[:, 0:640], dw_b[:, 704:736], dw_g], axis=1),
        "w_uq": dw_uq.reshape(MLA_Q_RANK, MLA_HEADS, LANES)[:, :, :MLA_QK].reshape(MLA_Q_RANK, MLA_HEADS * MLA_QK),
        "w_ukv": dw_ukv,
        "w_proj_sb": dw_ps,
        "w_proj_mla": dw_pm.reshape(MLA_HEADS, LANES, D_MODEL)[:, HALF:, :].reshape(MLA_HEADS * HALF, D_MODEL),
        "w_out": dw_out,
        "w_gate_up": dw_gu,
        "w_down": dw_dn,
    }
    return loss, grad_x, small, big


def kernel(x, positions, norm_mix_pre, norm_mix_post, w_in, b_gate, q_norm, w_uq, kv_norm, w_ukv, w_proj_sb, w_proj_mla, w_out, norm_ffn_pre, norm_ffn_post, w_gate_up, w_down, loss_target, m_norm_mix_pre, m_norm_mix_post, m_w_in, m_b_gate, m_q_norm, m_w_uq, m_kv_norm, m_w_ukv, m_w_proj_sb, m_w_proj_mla, m_w_out, m_norm_ffn_pre, m_norm_ffn_post, m_w_gate_up, m_w_down, v_norm_mix_pre, v_norm_mix_post, v_w_in, v_b_gate, v_q_norm, v_w_uq, v_kv_norm, v_w_ukv, v_w_proj_sb, v_w_proj_mla, v_w_out, v_norm_ffn_pre, v_norm_ffn_post, v_w_gate_up, v_w_down):
    given = dict(locals())
    s = x.shape[1]
    names = [n for n, _, _, _ in BIG]
    small_w = {n: given[n] for n, _ in SMALL}

    own = [given[n][0].astype(BF16) for n in names]
    gathered = _gather_weights(own, "gather_weights")
    loss, grad_x, d_small, d_big = _forward_backward(
        x.reshape(s, D_MODEL), positions.reshape(s, 1), loss_target.reshape(s, D_MODEL), small_w,
        _full_matrices(own, gathered))

    pieces = _split_for_chips(d_big)
    from_sibling = _swap_halves(pieces, "grad_swap_halves")
    partial = [_add_cores(g, a, "grad_add_cores_" + n) for n, g, a in zip(names, pieces, from_sibling)]
    received = _scatter_chips([b for _, b in partial], "grad_scatter_chips")
    mine = [_add_chips(f, r, "grad_add_chips_" + n) for n, (f, _), r in zip(names, partial, received)]
    other = _to_sibling(mine, "grad_to_sibling")
    small_sum = _allsum_small(_pack_small(d_small, loss), "small_allsum")

    grads, deltas, new_m, new_v = {}, {}, {}, {}
    for n, a, b in zip(names, mine, other):
        grads[n], deltas[n], new_m[n], new_v[n] = _adamw_shard(given[n], given["m_" + n], given["v_" + n], a, b,
                                                               "adamw_" + n)
    delta_s, m_s, v_s = _adamw_small(_pack_small(small_w), small_sum,
                                     _pack_small({n: given["m_" + n] for n in small_w}),
                                     _pack_small({n: given["v_" + n] for n in small_w}), "adamw_small")

    loss_out = small_sum.reshape(-1)[SMALL_ELEMS]
    grads.update(_unpack_small(small_sum))
    deltas.update(_unpack_small(delta_s))
    new_m.update(_unpack_small(m_s))
    new_v.update(_unpack_small(v_s))
    return (loss_out, grad_x.reshape(1, s, D_MODEL), *[grads[n] for n in WEIGHT_ORDER],
            *[deltas[n] for n in WEIGHT_ORDER], *[new_m[n] for n in WEIGHT_ORDER], *[new_v[n] for n in WEIGHT_ORDER])
```

```python
import functools
import math

import jax
import jax.numpy as jnp
from jax import lax
from jax.experimental import pallas as pl
from jax.experimental.pallas import tpu as pltpu

F32 = jnp.float32
BF16 = jnp.bfloat16
MESH = pl.DeviceIdType.MESH

D_MODEL = 1024
SB_WIDTH = 512
MLA_HEADS = 8
MLA_Q_RANK = 384
MLA_KV_RANK = 256
MLA_ROPE = 32
MLA_QK = 96
D_FF = 2816
ROPE_THETA = 10000.0
EPS = 1e-6
LANES = 128
HALF = 64
ATT_TILE = 256
N_CHIPS = 4

ADAM_LR = 0.001
ADAM_B1 = 0.9
ADAM_B2 = 0.999
ADAM_EPS = 1e-08
ADAM_WD = 0.01
ADAM_STEP = 10

VMEM_LIMIT = 56 * 1024 * 1024

BIG = (
    ("w_in", 1024, 1064, 1),
    ("w_uq", 384, 192, 1),
    ("w_ukv", 256, 256, 1),
    ("w_proj_sb", 512, 256, 1),
    ("w_proj_mla", 512, 256, 1),
    ("w_out", 256, 1024, 0),
    ("w_gate_up", 1024, 1408, 1),
    ("w_down", 704, 1024, 0),
)
PACK_COLS = 1024
SMALL = (
    ("norm_mix_pre", 1024),
    ("norm_mix_post", 1024),
    ("b_gate", 2048),
    ("q_norm", 384),
    ("kv_norm", 256),
    ("norm_ffn_pre", 1024),
    ("norm_ffn_post", 1024),
)
SMALL_ELEMS = sum(n for _, n in SMALL)
SMALL_ROWS = 8
WEIGHT_ORDER = ("norm_mix_pre", "norm_mix_post", "w_in", "b_gate", "q_norm", "w_uq", "kv_norm", "w_ukv",
                "w_proj_sb", "w_proj_mla", "w_out", "norm_ffn_pre", "norm_ffn_post", "w_gate_up", "w_down")


def _params():
    return pltpu.CompilerParams(vmem_limit_bytes=VMEM_LIMIT)


def _tile(dim, pref):
    t = (min(pref, dim) // LANES) * LANES
    while t >= LANES:
        if dim % t == 0:
            return t
        t -= LANES
    return dim


def _mm(a, b, mode, out_dtype, name, tm=1408, tn=1408, tk=1408):
    if mode == "nn":
        (m, k), n = a.shape, b.shape[1]
    elif mode == "nt":
        (m, k), n = a.shape, b.shape[0]
    else:
        (k, m), n = a.shape, b.shape[1]
    tm, tn, tk = _tile(m, tm), _tile(n, tn), _tile(k, tk)
    nk = k // tk
    if mode == "tn":
        a_spec = pl.BlockSpec((tk, tm), lambda i, j, l: (l, i))
        lhs_dim = 0
    else:
        a_spec = pl.BlockSpec((tm, tk), lambda i, j, l: (i, l))
        lhs_dim = 1
    if mode == "nt":
        b_spec = pl.BlockSpec((tn, tk), lambda i, j, l: (j, l))
        rhs_dim = 1
    else:
        b_spec = pl.BlockSpec((tk, tn), lambda i, j, l: (l, j))
        rhs_dim = 0
    dims = (((lhs_dim,), (rhs_dim,)), ((), ()))

    def body(a_ref, b_ref, o_ref, acc_ref):
        l = pl.program_id(2)

        @pl.when(l == 0)
        def _():
            acc_ref[...] = jnp.zeros_like(acc_ref)

        acc_ref[...] += lax.dot_general(a_ref[...].astype(BF16), b_ref[...].astype(BF16), dims,
                                        preferred_element_type=F32)

        @pl.when(l == nk - 1)
        def _():
            o_ref[...] = acc_ref[...].astype(o_ref.dtype)

    return pl.pallas_call(
        body,
        name=name,
        grid=(m // tm, n // tn, nk),
        in_specs=[a_spec, b_spec],
        out_specs=pl.BlockSpec((tm, tn), lambda i, j, l: (i, j)),
        out_shape=jax.ShapeDtypeStruct((m, n), out_dtype),
        scratch_shapes=[pltpu.VMEM((tm, tn), F32)],
        compiler_params=_params(),
    )(a, b)


def _mm_nt_sum(pairs, name, tm=512, tn=1024):
    m, n = pairs[0][0].shape[0], pairs[0][1].shape[0]
    tm, tn = _tile(m, tm), _tile(n, tn)

    def body(*refs):
        total = None
        for p in range(len(pairs)):
            part = _dot_nt(refs[2 * p][...].astype(BF16), refs[2 * p + 1][...].astype(BF16))
            total = part if total is None else total + part
        refs[-1][...] = total

    in_specs = []
    for a, b in pairs:
        in_specs += [pl.BlockSpec((tm, a.shape[1]), lambda i, j: (i, 0)),
                     pl.BlockSpec((tn, b.shape[1]), lambda i, j: (j, 0))]
    return pl.pallas_call(
        body,
        name=name,
        grid=(m // tm, n // tn),
        in_specs=in_specs,
        out_specs=pl.BlockSpec((tm, tn), lambda i, j: (i, j)),
        out_shape=jax.ShapeDtypeStruct((m, n), F32),
        compiler_params=_params(),
    )(*[r for pair in pairs for r in pair])


def _ffn_up(h, w_gu, name):
    s, k = h.shape
    tm, tn = _tile(s, 512), _tile(D_FF, 1408)
    nb = D_FF // tn

    def body(h_ref, wg_ref, wu_ref, g_ref, u_ref, act_ref):
        hv = h_ref[...]
        g = jnp.dot(hv, wg_ref[...], preferred_element_type=F32)
        u = jnp.dot(hv, wu_ref[...], preferred_element_type=F32)
        g_ref[...] = g
        u_ref[...] = u
        act_ref[...] = (g * _sigmoid(g) * u).astype(BF16)

    tile = pl.BlockSpec((tm, tn), lambda j, i: (i, j))
    out = jax.ShapeDtypeStruct((s, D_FF), F32)
    return pl.pallas_call(
        body,
        name=name,
        grid=(nb, s // tm),
        in_specs=[pl.BlockSpec((tm, k), lambda j, i: (i, 0)), pl.BlockSpec((k, tn), lambda j, i: (0, j)),
                  pl.BlockSpec((k, tn), lambda j, i: (0, nb + j))],
        out_specs=[tile, tile, tile],
        out_shape=[out, out, jax.ShapeDtypeStruct((s, D_FF), BF16)],
        compiler_params=_params(),
    )(h, w_gu, w_gu)


def _ffn_dact(df, w_dn, g, u, name):
    s, k = df.shape
    tm, tn = _tile(s, 512), _tile(D_FF, 1408)

    def body(df_ref, w_ref, g_ref, u_ref, dg_ref, du_ref):
        dact = _dot_nt(df_ref[...], w_ref[...])
        gv, uv = g_ref[...], u_ref[...]
        sg = _sigmoid(gv)
        dg_ref[...] = (dact * uv * (sg * (1.0 + gv * (1.0 - sg)))).astype(BF16)
        du_ref[...] = (dact * (gv * sg)).astype(BF16)

    tile = pl.BlockSpec((tm, tn), lambda j, i: (i, j))
    out = jax.ShapeDtypeStruct((s, D_FF), BF16)
    return pl.pallas_call(
        body,
        name=name,
        grid=(D_FF // tn, s // tm),
        in_specs=[pl.BlockSpec((tm, k), lambda j, i: (i, 0)), pl.BlockSpec((tn, k), lambda j, i: (j, 0)), tile, tile],
        out_specs=[tile, tile],
        out_shape=[out, out],
        compiler_params=_params(),
    )(df, w_dn, g, u)


def _rowwise(fn, ins, outs, reds, tm, name):
    rows = ins[0].shape[0]
    tm = min(tm, rows)
    n_in, n_out = len(ins), len(outs)

    def spec(shape):
        if shape[0] == rows:
            return pl.BlockSpec((tm, shape[1]), lambda i: (i, 0))
        return pl.BlockSpec(shape, lambda i: (0, 0))

    def body(*refs):
        i = pl.program_id(0)
        vals = fn(*[r[...] for r in refs[:n_in]])
        for r, v in zip(refs[n_in:n_in + n_out], vals[:n_out]):
            r[...] = v.astype(r.dtype)
        for r, v in zip(refs[n_in + n_out:], vals[n_out:]):
            @pl.when(i == 0)
            def _():
                r[...] = jnp.zeros_like(r)

            r[...] += v

    out_shape = [jax.ShapeDtypeStruct((rows, d), dt) for d, dt in outs]
    out_shape += [jax.ShapeDtypeStruct((1, d), F32) for d in reds]
    out_specs = [spec(s.shape) for s in out_shape]
    return pl.pallas_call(
        body,
        name=name,
        grid=(rows // tm,),
        in_specs=[spec(a.shape) for a in ins],
        out_specs=out_specs,
        out_shape=out_shape,
        compiler_params=_params(),
    )(*ins)


def _colsum(v):
    return jnp.sum(v, axis=0, keepdims=True)


def _rms(v):
    r = lax.rsqrt(jnp.mean(v * v, axis=-1, keepdims=True) + EPS)
    return r, v * r


def _rms_bwd(dy, r, vhat, g):
    u = dy * g
    return r * (u - vhat * jnp.mean(u * vhat, axis=-1, keepdims=True))


def _sigmoid(t):
    return 1.0 / (1.0 + jnp.exp(-t))


def _lane(shape):
    return lax.broadcasted_iota(jnp.int32, shape, len(shape) - 1)


def _rope_swap(v):
    lane = _lane(v.shape)
    up = pltpu.roll(v, LANES - 16, 1)
    down = pltpu.roll(v, 16, 1)
    return jnp.where((lane >= 64) & (lane < 80), up, jnp.where((lane >= 80) & (lane < 96), down, 0.0))


SB_DEAD = -120.0


def _softplus(z):
    return jnp.maximum(z, 0.0) + jnp.log(1.0 + jnp.exp(-jnp.abs(z)))


def _split_dot(v, m2):
    hi = v.astype(BF16)
    lo = (v - hi.astype(F32)).astype(BF16)
    return jnp.dot(jnp.concatenate([hi, lo], axis=1), m2, preferred_element_type=F32)


def _dot_nt(a, b):
    return lax.dot_general(a, b, (((1,), (1,)), ((), ())), preferred_element_type=F32)


def _dot_tn(a, b):
    return lax.dot_general(a, b, (((0,), (0,)), ((), ())), preferred_element_type=F32)


def _sb_fwd(qkv, name):
    s = qkv.shape[0]
    t = min(ATT_TILE, s // 2)
    npair = SB_WIDTH // LANES

    def body(q_ref, k_ref, v_ref, o_ref, t_ref, first_ref, c_s, acc_s):
        i = pl.program_id(1)
        lane = _lane((1, LANES))
        row = lax.broadcasted_iota(jnp.int32, (t, t), 0)
        col = lax.broadcasted_iota(jnp.int32, (t, t), 1)
        tri = col < row
        m_from = jnp.where(row >= col, 1.0, 0.0).astype(BF16)
        m_from = jnp.concatenate([m_from, m_from], axis=0)
        masks = (lane < HALF, lane >= HALF)
        qhs = [jnp.where(hm, q_ref[n * t:(n + 1) * t, :], jnp.zeros((t, LANES), BF16)) * 0.125
               for n in range(2) for hm in masks]

        c_s[...] = jnp.zeros_like(c_s)
        acc_s[...] = jnp.zeros_like(acc_s)

        def step(k, diag, chains):
            blocks = {}
            for n in sorted({ch // 2 for ch in chains}):
                rows = pl.ds(pl.multiple_of((2 * i + n - k) * t, t), t)
                blocks[n] = (k_ref[rows, :], v_ref[rows, :])
            zs = {ch: _dot_nt(qhs[ch], blocks[ch // 2][0]) for ch in chains}
            lbs = {}
            for ch in chains:
                lb = -_softplus(zs[ch])
                lbs[ch] = jnp.where(tri, lb, 0.0) if diag else lb
            sums = {ch: _split_dot(lbs[ch], m_from) for ch in chains}
            for ch in chains:
                c = c_s[ch]
                a = jnp.exp(zs[ch] + sums[ch] + jnp.concatenate([c] * (t // LANES), axis=1))
                if diag:
                    a = jnp.where(tri, a, 0.0)
                vs = blocks[ch // 2][1]
                vh = jnp.where(masks[ch % 2], vs, jnp.zeros_like(vs))
                acc_s[ch] += jnp.dot(a.astype(BF16), vh, preferred_element_type=F32)
                c_s[ch] = c + sums[ch][:, 0:1]

        def live():
            top = jnp.max(c_s[0])
            for ch in range(1, 4):
                top = jnp.maximum(top, jnp.max(c_s[ch]))
            return top

        step(0, True, (0, 1, 2, 3))
        last = 2 * i

        def more(state):
            k, top = state
            return (k <= last) & (top > SB_DEAD)

        def walk(state):
            k, _ = state
            step(k, False, (0, 1, 2, 3))
            return k + 1, live()

        k_end, top = lax.while_loop(more, walk, (jnp.int32(1), live()))
        tail = (k_end == last + 1) & (top > SB_DEAD)

        @pl.when(tail)
        def _():
            step(last + 1, False, (2, 3))

        first_ref[0, 0] = jnp.full((8, LANES), (k_end - 1).astype(F32) + jnp.where(tail, 0.5, 0.0))
        for n in range(2):
            o_ref[n * t:(n + 1) * t, :] = acc_s[2 * n] + acc_s[2 * n + 1]
            t_ref[n * t:(n + 1) * t, :] = jnp.where(masks[0], c_s[2 * n], c_s[2 * n + 1])

    blk = pl.BlockSpec((2 * t, LANES), lambda p, i: (i, p))
    steps = s // (2 * t)
    return pl.pallas_call(
        body,
        name=name,
        grid=(npair, steps),
        in_specs=[
            blk,
            pl.BlockSpec((s, LANES), lambda p, i: (0, npair + p)),
            pl.BlockSpec((s, LANES), lambda p, i: (0, 2 * npair + p)),
        ],
        out_specs=[blk, blk, pl.BlockSpec((1, 1, 8, LANES), lambda p, i: (p, i, 0, 0))],
        out_shape=[jax.ShapeDtypeStruct((s, SB_WIDTH), F32), jax.ShapeDtypeStruct((s, SB_WIDTH), F32),
                   jax.ShapeDtypeStruct((npair, steps, 8, LANES), F32)],
        scratch_shapes=[pltpu.VMEM((4, t, LANES), F32)] * 2,
        compiler_params=_params(),
    )(qkv, qkv, qkv)


def _sb_bwd(qkv, do, tot, first, name):
    s = qkv.shape[0]
    t = min(ATT_TILE, s // 2)
    npair = SB_WIDTH // LANES

    def body(q_ref, k_ref, v_ref, do_ref, t_ref, first_ref, dq_ref, dk_ref, dv_ref, bef_s, ebef_s):
        i = pl.program_id(1)

        @pl.when(i == 0)
        def _():
            dk_ref[...] = jnp.zeros_like(dk_ref)
            dv_ref[...] = jnp.zeros_like(dv_ref)

        lane = _lane((1, LANES))
        row = lax.broadcasted_iota(jnp.int32, (t, t), 0)
        col = lax.broadcasted_iota(jnp.int32, (t, t), 1)
        tri = col < row
        m_from = jnp.where(row >= col, 1.0, 0.0).astype(BF16)
        m_from = jnp.concatenate([m_from, m_from], axis=0)
        m_upto = jnp.where(row <= col, 1.0, 0.0).astype(BF16)
        m_upto = jnp.concatenate([m_upto, m_upto], axis=0)
        masks = (lane < HALF, lane >= HALF)
        zero_b = jnp.zeros((t, LANES), BF16)
        qhs, dohs, tots = [], [], []
        for n in range(2):
            rs = slice(n * t, (n + 1) * t)
            do_b = do_ref[rs, :].astype(BF16)
            for h, hm in enumerate(masks):
                qhs.append(jnp.where(hm, q_ref[rs, :], zero_b))
                dohs.append(jnp.where(hm, do_b, zero_b))
                tots.append(t_ref[rs, h * HALF:h * HALF + 1])

        bef_s[...] = jnp.zeros_like(bef_s)
        ebef_s[...] = jnp.zeros_like(ebef_s)
        dq_ref[...] = jnp.zeros_like(dq_ref)
        reps = t // LANES

        def step(k, diag, chains):
            qblocks = sorted({ch // 2 for ch in chains})
            rows, blocks = {}, {}
            for n in qblocks:
                rows[n] = pl.ds(pl.multiple_of((2 * i + n - k) * t, t), t)
                blocks[n] = (k_ref[rows[n], :], v_ref[rows[n], :])
            zs = {ch: _dot_nt(qhs[ch] * 0.125, blocks[ch // 2][0]) for ch in chains}
            das = {ch: _dot_nt(dohs[ch], blocks[ch // 2][1]) for ch in chains}
            lbs, betas = {}, {}
            for ch in chains:
                sp = _softplus(zs[ch])
                betas[ch] = jnp.exp(zs[ch] - sp)
                lbs[ch] = jnp.where(tri, -sp, 0.0) if diag else -sp
            sums = {ch: _split_dot(lbs[ch], m_from) for ch in chains}
            avs, es = {}, {}
            for ch in chains:
                before = bef_s[ch]
                total = sums[ch][:, 0:1]
                a = jnp.exp(zs[ch] + sums[ch] + jnp.concatenate([tots[ch] - before - total] * reps, axis=1))
                if diag:
                    a = jnp.where(tri, a, 0.0)
                avs[ch] = a.astype(BF16)
                es[ch] = a * das[ch]
                bef_s[ch] = before + total
            e_sums = {ch: _split_dot(es[ch], m_upto) for ch in chains}
            dzs = {}
            for ch in chains:
                e_upto = e_sums[ch] + jnp.concatenate([ebef_s[ch]] * reps, axis=1)
                dz = es[ch] - betas[ch] * e_upto
                if diag:
                    dz = jnp.where(tri, dz, 0.0)
                dzs[ch] = (dz * 0.125).astype(BF16)
                ebef_s[ch] = jnp.broadcast_to(e_upto[:, t - 1:t], (t, LANES))
            for n in qblocks:
                ks = blocks[n][0]
                mine = [ch for ch in chains if ch // 2 == n]
                dq = sum(jnp.dot(dzs[ch], jnp.where(masks[ch % 2], ks, jnp.zeros_like(ks)),
                                 preferred_element_type=F32) for ch in mine)
                dq_ref[n * t:(n + 1) * t, :] += dq
                dk_ref[rows[n], :] += sum(_dot_tn(dzs[ch], qhs[ch]) for ch in mine)
                dv_ref[rows[n], :] += sum(_dot_tn(avs[ch], dohs[ch]) for ch in mine)

        walked = jnp.max(first_ref[0, 0])
        full_steps = jnp.clip(walked.astype(jnp.int32), 0, 2 * i)

        @pl.when(walked - full_steps.astype(F32) > 0.25)
        def _():
            step(2 * i + 1, False, (2, 3))

        def walk(n, carry):
            step(full_steps - n, False, (0, 1, 2, 3))
            return carry

        lax.fori_loop(0, full_steps, walk, 0)
        step(0, True, (0, 1, 2, 3))

    blk = pl.BlockSpec((2 * t, LANES), lambda p, i: (i, p))
    full = pl.BlockSpec((s, LANES), lambda p, i: (0, p))
    out = jax.ShapeDtypeStruct((s, SB_WIDTH), F32)
    return pl.pallas_call(
        body,
        name=name,
        grid=(npair, s // (2 * t)),
        in_specs=[
            blk,
            pl.BlockSpec((s, LANES), lambda p, i: (0, npair + p)),
            pl.BlockSpec((s, LANES), lambda p, i: (0, 2 * npair + p)),
            blk,
            blk,
            pl.BlockSpec((1, 1, 8, LANES), lambda p, i: (p, i, 0, 0)),
        ],
        out_specs=[blk, full, full],
        out_shape=[out, out, out],
        scratch_shapes=[pltpu.VMEM((4, t, LANES), F32)] * 2,
        compiler_params=_params(),
    )(qkv, qkv, qkv, do, tot, first)


MLA_SCALE = 1.0 / math.sqrt(MLA_QK)
MLA_SCALE_LOG2E = MLA_SCALE * math.log2(math.e)
NEG = -1e30
MLA_QBLOCKS = 8


def _diag_plan(nq):
    return [tuple(None if n < j else n == j for n in range(nq)) for j in range(nq)]


def _mla_fwd(qc, kc, vp, name):
    s = qc.shape[0]
    nq = MLA_QBLOCKS
    t = min(ATT_TILE, s // nq)

    def body(q_ref, k_ref, v_ref, o_ref, lse_ref, m_s, l_s, acc_s):
        i = pl.program_id(1)
        row = lax.broadcasted_iota(jnp.int32, (t, t), 0)
        col = lax.broadcasted_iota(jnp.int32, (t, t), 1)
        keep = col <= row
        m_s[...] = jnp.full(m_s.shape, NEG, F32)
        l_s[...] = jnp.zeros_like(l_s)
        acc_s[...] = jnp.zeros_like(acc_s)

        def step(kb, diags):
            rows = pl.ds(pl.multiple_of(kb * t, t), t)
            ks = k_ref[rows, :]
            vs = v_ref[rows, :]
            act = [n for n in range(nq) if diags[n] is not None]
            scs = {n: _dot_nt(q_ref[n * t:(n + 1) * t, :], ks) for n in act}
            mid = {}
            for n in act:
                sc = jnp.where(keep, scs[n], NEG) if diags[n] else scs[n]
                m = m_s[n]
                m_new = jnp.maximum(m, jnp.max(sc, axis=-1, keepdims=True))
                alpha = jnp.exp2((m - m_new) * MLA_SCALE_LOG2E)
                p = jnp.exp2((sc - jnp.concatenate([m_new] * (t // LANES), axis=1)) * MLA_SCALE_LOG2E)
                m_s[n] = m_new
                l_s[n] = alpha * l_s[n] + jnp.sum(p, axis=-1, keepdims=True)
                mid[n] = (alpha, p.astype(BF16))
            for n in act:
                alpha, p = mid[n]
                acc_s[n] = alpha * acc_s[n] + jnp.dot(p, vs, preferred_element_type=F32)

        def walk(kb, carry):
            step(kb, (False,) * nq)
            return carry

        lax.fori_loop(0, nq * i, walk, 0)
        for j, diags in enumerate(_diag_plan(nq)):
            step(nq * i + j, diags)
        for n in range(nq):
            l = l_s[n]
            o_ref[n * t:(n + 1) * t, :] = acc_s[n] / l
            lse_ref[n * t:(n + 1) * t, :] = m_s[n] * MLA_SCALE + jnp.log(l)

    blk = pl.BlockSpec((nq * t, LANES), lambda h, i: (i, h))
    full = pl.BlockSpec((s, LANES), lambda h, i: (0, h))
    out = jax.ShapeDtypeStruct((s, MLA_HEADS * LANES), F32)
    return pl.pallas_call(
        body,
        name=name,
        grid=(MLA_HEADS, s // (nq * t)),
        in_specs=[blk, full, full],
        out_specs=[blk, blk],
        out_shape=[out, out],
        scratch_shapes=[pltpu.VMEM((nq, t, LANES), F32)] * 3,
        compiler_params=_params(),
    )(qc, kc, vp)


def _mla_bwd(qc, kc, vp, do, o, lse, name):
    s = qc.shape[0]
    nq = MLA_QBLOCKS
    t = min(ATT_TILE, s // nq)

    def body(q_ref, k_ref, v_ref, do_ref, o_ref, lse_ref, dq_ref, dk_ref, dv_ref):
        i = pl.program_id(1)

        @pl.when(i == 0)
        def _():
            dk_ref[...] = jnp.zeros_like(dk_ref)
            dv_ref[...] = jnp.zeros_like(dv_ref)

        row = lax.broadcasted_iota(jnp.int32, (t, t), 0)
        col = lax.broadcasted_iota(jnp.int32, (t, t), 1)
        keep = col <= row

        def block(n):
            rs = slice(n * t, (n + 1) * t)
            do_f = do_ref[rs, :]
            delta = jnp.sum(do_f * o_ref[rs, :], axis=-1, keepdims=True)
            return q_ref[rs, :], do_f.astype(BF16), delta * MLA_SCALE, lse_ref[rs, 0:1] * math.log2(math.e)

        blocks = [block(n) for n in range(nq)]
        dq_ref[...] = jnp.zeros_like(dq_ref)

        def step(kb, diags):
            rows = pl.ds(pl.multiple_of(kb * t, t), t)
            ks = k_ref[rows, :]
            vs = v_ref[rows, :]
            act = [n for n in range(nq) if diags[n] is not None]
            scs = {n: _dot_nt(blocks[n][0], ks) for n in act}
            dps = {n: _dot_nt(blocks[n][1], vs) for n in act}
            ps, dss = {}, {}
            for n in act:
                _, _, delta_s, lse2 = blocks[n]
                p = jnp.exp2(scs[n] * MLA_SCALE_LOG2E - lse2)
                if diags[n]:
                    p = jnp.where(keep, p, 0.0)
                dss[n] = (p * (dps[n] * MLA_SCALE - delta_s)).astype(BF16)
                ps[n] = p.astype(BF16)
            dk = jnp.zeros((t, LANES), F32)
            dv = jnp.zeros((t, LANES), F32)
            for n in act:
                q, do_b, _, _ = blocks[n]
                dq_ref[n * t:(n + 1) * t, :] += jnp.dot(dss[n], ks, preferred_element_type=F32)
                dk = dk + _dot_tn(dss[n], q)
                dv = dv + _dot_tn(ps[n], do_b)
            dk_ref[rows, :] += dk
            dv_ref[rows, :] += dv

        def walk(kb, carry):
            step(kb, (False,) * nq)
            return carry

        lax.fori_loop(0, nq * i, walk, 0)
        for j, diags in enumerate(_diag_plan(nq)):
            step(nq * i + j, diags)

    blk = pl.BlockSpec((nq * t, LANES), lambda h, i: (i, h))
    full = pl.BlockSpec((s, LANES), lambda h, i: (0, h))
    out = jax.ShapeDtypeStruct((s, MLA_HEADS * LANES), F32)
    return pl.pallas_call(
        body,
        name=name,
        grid=(MLA_HEADS, s // (nq * t)),
        in_specs=[blk, full, full, blk, blk, blk],
        out_specs=[blk, full, full],
        out_shape=[out, out, out],
        compiler_params=_params(),
    )(qc, kc, vp, do, o, lse)


def _position():
    return lax.axis_index("x"), lax.axis_index("y"), lax.axis_index("c")


def _other_chips(x, y):
    return [(1 - x, y), (x, 1 - y), (1 - x, 1 - y)]


HBM_SPEC = pl.BlockSpec(memory_space=pltpu.HBM)


def _remote(src, dst, send_sem, recv_sem, device):
    return pltpu.make_async_remote_copy(src_ref=src, dst_ref=dst, send_sem=send_sem, recv_sem=recv_sem,
                                        device_id=device, device_id_type=MESH)


def _dma_sems(n, count):
    return [pltpu.SemaphoreType.DMA((n,)) for _ in range(count)]


def _gather_weights(ws, name):
    na = len(ws)

    def body(*refs):
        srcs, outs = refs[:na], refs[na:2 * na]
        s1, r1, s2, r2, s3, r3, s4, r4 = refs[2 * na:]
        x, y, c = _position()
        me = 2 * x + y
        px, py = jnp.where(c == 1, 1 - x, x), jnp.where(c == 1, y, 1 - y)
        qx, qy = jnp.where(c == 1, x, 1 - x), jnp.where(c == 1, 1 - y, y)
        p_idx, q_idx, d_idx = 2 * px + py, 2 * qx + qy, 2 * (1 - x) + (1 - y)
        partner, across, sibling = (px, py, c), (qx, qy, c), (x, y, 1 - c)
        my_half, other_half = 1 - c, c

        def half(i, slot, h):
            kh = ws[i].shape[0] // 2
            return outs[i].at[slot, pl.ds(h * kh, kh), :]

        sends = [_remote(srcs[i], outs[i].at[me], s1.at[i], r1.at[i], partner) for i in range(na)]
        for cp in sends:
            cp.start()
        for i in range(na):
            _remote(srcs[i], outs[i].at[p_idx], s1.at[i], r1.at[i], partner).wait_recv()
            onward = [_remote(half(i, p_idx, my_half), half(i, p_idx, my_half), s2.at[i], r2.at[i], across),
                      _remote(outs[i].at[p_idx], outs[i].at[p_idx], s3.at[i], r3.at[i], sibling)]
            for cp in onward:
                cp.start()
            sends += onward
        for i in range(na):
            _remote(half(i, d_idx, my_half), half(i, d_idx, my_half), s2.at[i], r2.at[i], across).wait_recv()
            cp = _remote(half(i, d_idx, my_half), half(i, d_idx, my_half), s4.at[i], r4.at[i], sibling)
            cp.start()
            sends.append(cp)
        for i in range(na):
            _remote(outs[i].at[q_idx], outs[i].at[q_idx], s3.at[i], r3.at[i], sibling).wait_recv()
            _remote(half(i, d_idx, other_half), half(i, d_idx, other_half), s4.at[i], r4.at[i], sibling).wait_recv()
        for cp in sends:
            cp.wait_send()

    return pl.pallas_call(
        body,
        name=name,
        in_specs=[HBM_SPEC] * na,
        out_specs=[HBM_SPEC] * na,
        out_shape=[jax.ShapeDtypeStruct((N_CHIPS,) + w.shape, w.dtype) for w in ws],
        scratch_shapes=_dma_sems(na, 8),
    )(*ws)


def _swap_halves(gs, name):
    na = len(gs)

    def body(*refs):
        srcs, outs, send_sems, recv_sems = refs[:na], refs[na:2 * na], refs[2 * na], refs[2 * na + 1]
        x, y, c = _position()
        cps = []
        for i in range(na):
            kh = gs[i].shape[1] // 2
            cps.append(_remote(srcs[i].at[:, pl.ds((1 - c) * kh, kh), :], outs[i], send_sems.at[i], recv_sems.at[i],
                               (x, y, 1 - c)))
            cps[-1].start()
        for cp in cps:
            cp.wait()

    return pl.pallas_call(
        body,
        name=name,
        in_specs=[HBM_SPEC] * na,
        out_specs=[HBM_SPEC] * na,
        out_shape=[jax.ShapeDtypeStruct((g.shape[0], g.shape[1] // 2, g.shape[2]), g.dtype) for g in gs],
        scratch_shapes=_dma_sems(na, 2),
    )(*gs)


def _scatter_chips(parts, name):
    na = len(parts)

    def body(*refs):
        srcs, outs, send_sems, recv_sems = refs[:na], refs[na:2 * na], refs[2 * na], refs[2 * na + 1]
        x, y, c = _position()
        me = 2 * x + y
        sends = []
        for i in range(na):
            for k, (px, py) in enumerate(_other_chips(x, y)):
                cp = _remote(srcs[i].at[2 * px + py], outs[i].at[me], send_sems.at[3 * i + k],
                             recv_sems.at[3 * i + k], (px, py, c))
                cp.start()
                sends.append(cp)
        for i in range(na):
            for k, (px, py) in enumerate(_other_chips(x, y)):
                _remote(srcs[i].at[me], outs[i].at[2 * px + py], send_sems.at[3 * i + k], recv_sems.at[3 * i + k],
                        (px, py, c)).wait_recv()
        for cp in sends:
            cp.wait_send()

    return pl.pallas_call(
        body,
        name=name,
        in_specs=[HBM_SPEC] * na,
        out_specs=[HBM_SPEC] * na,
        out_shape=[jax.ShapeDtypeStruct(p.shape, p.dtype) for p in parts],
        scratch_shapes=_dma_sems(3 * na, 2),
    )(*parts)


def _to_sibling(halves, name):
    na = len(halves)

    def body(*refs):
        srcs, outs, send_sems, recv_sems = refs[:na], refs[na:2 * na], refs[2 * na], refs[2 * na + 1]
        x, y, c = _position()
        cps = [_remote(srcs[i], outs[i], send_sems.at[i], recv_sems.at[i], (x, y, 1 - c)) for i in range(na)]
        for cp in cps:
            cp.start()
        for cp in cps:
            cp.wait()

    return pl.pallas_call(
        body,
        name=name,
        in_specs=[HBM_SPEC] * na,
        out_specs=[HBM_SPEC] * na,
        out_shape=[jax.ShapeDtypeStruct(h.shape, h.dtype) for h in halves],
        scratch_shapes=_dma_sems(na, 2),
    )(*halves)


def _allsum_small(v, name):
    shape = v.shape

    def body(v_ref, o_ref, buf, send_sems, recv_sems):
        x, y, c = _position()
        me = 4 * x + 2 * y + c
        buf[me] = v_ref[...]
        peers = []
        for k in range(1, 8):
            fx, fy, fc = (k >> 2) & 1, (k >> 1) & 1, k & 1
            peers.append((jnp.where(fx == 1, 1 - x, x), jnp.where(fy == 1, 1 - y, y), jnp.where(fc == 1, 1 - c, c)))
        sends = []
        for k, peer in enumerate(peers):
            cp = pltpu.make_async_remote_copy(src_ref=v_ref, dst_ref=buf.at[me], send_sem=send_sems.at[k],
                                              recv_sem=recv_sems.at[k], device_id=peer, device_id_type=MESH)
            cp.start()
            sends.append(cp)
        for k, (px, py, pc) in enumerate(peers):
            pltpu.make_async_remote_copy(src_ref=v_ref, dst_ref=buf.at[4 * px + 2 * py + pc],
                                         send_sem=send_sems.at[k], recv_sem=recv_sems.at[k],
                                         device_id=(px, py, pc), device_id_type=MESH).wait_recv()
        for cp in sends:
            cp.wait_send()
        acc = buf[0]
        for d in range(1, 8):
            acc = acc + buf[d]
        o_ref[...] = acc

    return pl.pallas_call(
        body,
        name=name,
        in_specs=[pl.BlockSpec(memory_space=pltpu.VMEM)],
        out_specs=pl.BlockSpec(memory_space=pltpu.VMEM),
        out_shape=jax.ShapeDtypeStruct(shape, F32),
        scratch_shapes=[pltpu.VMEM((8,) + shape, F32), pltpu.SemaphoreType.DMA((7,)), pltpu.SemaphoreType.DMA((7,))],
    )(v)


def _row_tile(h):
    for t in range(256, 0, -16):
        if h % t == 0:
            return t
    return h


def _add_cores(g, a, name):
    n4, k, n = g.shape
    kh = k // 2
    th = _row_tile(kh)

    def body(g_ref, a_ref, f_ref, b_ref):
        c = lax.axis_index("c")
        total = jnp.where(c == 0, g_ref[0, 0], g_ref[0, 1]) + a_ref[0]
        f_ref[0] = total
        b_ref[0] = total.astype(BF16)

    spec = pl.BlockSpec((1, th, n), lambda j, r: (j, r, 0))
    return pl.pallas_call(
        body,
        name=name,
        grid=(n4, kh // th),
        in_specs=[pl.BlockSpec((1, 2, th, n), lambda j, r: (j, 0, r, 0)), spec],
        out_specs=[spec, spec],
        out_shape=[jax.ShapeDtypeStruct((n4, kh, n), F32), jax.ShapeDtypeStruct((n4, kh, n), BF16)],
        compiler_params=_params(),
    )(g.reshape(n4, 2, kh, n), a)


def _add_chips(pf, b, name):
    n4, h, n = pf.shape
    th = _row_tile(h)

    def body(pf_ref, b_ref, o_ref):
        me = 2 * lax.axis_index("x") + lax.axis_index("y")
        terms = [jnp.where(me == j, pf_ref[j], b_ref[j].astype(F32)) for j in range(n4)]
        o_ref[...] = ((terms[0] + terms[1]) + terms[2]) + terms[3]

    spec = pl.BlockSpec((n4, th, n), lambda r: (0, r, 0))
    return pl.pallas_call(
        body,
        name=name,
        grid=(h // th,),
        in_specs=[spec, spec],
        out_specs=pl.BlockSpec((th, n), lambda r: (r, 0)),
        out_shape=jax.ShapeDtypeStruct((h, n), F32),
        compiler_params=_params(),
    )(pf, b)


def _adam_math(w, g, m, v):
    m = ADAM_B1 * m + (1.0 - ADAM_B1) * g
    v = ADAM_B2 * v + (1.0 - ADAM_B2) * (g * g)
    m_hat = m / (1.0 - ADAM_B1 ** ADAM_STEP)
    v_hat = v / (1.0 - ADAM_B2 ** ADAM_STEP)
    delta = -ADAM_LR * (m_hat / (jnp.sqrt(v_hat) + ADAM_EPS) + ADAM_WD * w)
    return delta, m, v


def _adamw_shard(w, m, v, mine, other, name):
    _, k, n = w.shape
    kh = k // 2
    th = _row_tile(kh)
    nb = kh // th

    def body(w_ref, m_ref, v_ref, a_ref, b_ref, g_out, d_out, m_out, v_out):
        g = jnp.where(pl.program_id(0) == lax.axis_index("c"), a_ref[...], b_ref[...])
        g_out[...] = g
        d_out[...], m_out[...], v_out[...] = _adam_math(w_ref[...], g, m_ref[...], v_ref[...])

    full = pl.BlockSpec((None, th, n), lambda hh, r: (0, hh * nb + r, 0))
    half = pl.BlockSpec((th, n), lambda hh, r: (r, 0))
    return pl.pallas_call(
        body,
        name=name,
        grid=(2, nb),
        in_specs=[full, full, full, half, half],
        out_specs=[full] * 4,
        out_shape=[jax.ShapeDtypeStruct((1, k, n), F32)] * 4,
        compiler_params=_params(),
    )(w, m, v, mine, other)


def _adamw_small(w, g, m, v, name):
    cols = w.shape[1]
    return _rowwise(_adam_math, [w, g, m, v], [(cols, F32)] * 3, [], w.shape[0], name)


def _full_matrices(own, gathered):
    me = 2 * lax.axis_index("x") + lax.axis_index("y")
    out = {}
    for (n, _, _, axis), w, g in zip(BIG, own, gathered):
        out[n] = jnp.concatenate([jnp.where(me == j, w, g[j]) for j in range(N_CHIPS)], axis=axis)
    return out


def _split_for_chips(full):
    out = []
    for n, r, c, axis in BIG:
        g = full[n]
        out.append(jnp.stack([g[:, j * c:(j + 1) * c] for j in range(N_CHIPS)]) if axis == 1
                   else g.reshape(N_CHIPS, r, c))
    return out


def _pack_small(vecs, extra=None):
    flat = jnp.concatenate([vecs[n].reshape(-1) for n, _ in SMALL] + ([extra.reshape(-1)] if extra is not None else []))
    flat = jnp.pad(flat, (0, SMALL_ROWS * PACK_COLS - flat.shape[0]))
    return flat.reshape(SMALL_ROWS, PACK_COLS)


def _unpack_small(pack):
    flat = pack.reshape(-1)
    out, off = {}, 0
    for n, k in SMALL:
        out[n] = flat[off:off + k].reshape(1, k)
        off += k
    return out


def _forward_backward(x, pos, target, sm, w):
    s = x.shape[0]
    w_in = w["w_in"]
    w_a = w_in[:, 0:1536]
    zeros = functools.partial(jnp.zeros, dtype=BF16)
    w_b = jnp.concatenate([w_in[:, 1536:2176], zeros((D_MODEL, 64)), w_in[:, 2176:2208], zeros((D_MODEL, 32))], axis=1)
    w_g = w_in[:, 2208:4256]
    w_uq = jnp.pad(w["w_uq"].reshape(MLA_Q_RANK, MLA_HEADS, MLA_QK), ((0, 0), (0, 0), (0, LANES - MLA_QK)))
    w_uq = w_uq.reshape(MLA_Q_RANK, MLA_HEADS * LANES)
    w_ukv = w["w_ukv"]
    w_pm = jnp.pad(w["w_proj_mla"].reshape(MLA_HEADS, HALF, D_MODEL), ((0, 0), (HALF, 0), (0, 0)))
    w_pm = w_pm.reshape(MLA_HEADS * LANES, D_MODEL)
    w_ps, w_out, w_gu, w_dn = w["w_proj_sb"], w["w_out"], w["w_gate_up"], w["w_down"]
    inv_freq = ROPE_THETA ** (-jnp.arange(0, MLA_ROPE, 2, dtype=F32) / MLA_ROPE)
    inv_lane = jnp.concatenate([jnp.zeros((HALF,), F32), inv_freq, inv_freq, jnp.zeros((32,), F32)]).reshape(1, LANES)
    g_pre, g_post, b_gate = sm["norm_mix_pre"], sm["norm_mix_post"], sm["b_gate"]
    g_q, g_kv, g_fpre, g_fpost = sm["q_norm"], sm["kv_norm"], sm["norm_ffn_pre"], sm["norm_ffn_post"]

    def f_h1(x, g):
        return (_rms(x)[1] * g,)

    (h1,) = _rowwise(f_h1, [x, g_pre], [(D_MODEL, BF16)], [], 256, "h1")
    qkv = _mm(h1, w_a, "nn", BF16, "proj_sb")
    lat = _mm(h1, w_b, "nn", F32, "proj_lat")
    gl = _mm(h1, w_g, "nn", F32, "proj_gate")

    def f_lat(lat, gq, gkv, pos, inv):
        cqn = _rms(lat[:, 0:MLA_Q_RANK])[1] * gq
        ckvn = _rms(lat[:, MLA_Q_RANK:MLA_Q_RANK + MLA_KV_RANK])[1] * gkv
        kr = lat[:, 640:768]
        ang = pos.astype(F32) * inv
        lane = _lane(ang.shape)
        cosv, sinv = jnp.cos(ang), jnp.sin(ang)
        rope_c = jnp.where(lane < 64, 1.0, jnp.where(lane < 96, cosv, 0.0))
        rope_s = jnp.where((lane >= 64) & (lane < 80), -sinv, jnp.where((lane >= 80) & (lane < 96), sinv, 0.0))
        krr = jnp.where(lane >= 64, kr * rope_c, 0.0) + _rope_swap(kr) * rope_s
        return cqn, ckvn, krr, rope_c, rope_s

    cqn, ckvn, krr, rope_c, rope_s = _rowwise(
        f_lat, [lat, g_q, g_kv, pos, inv_lane],
        [(MLA_Q_RANK, BF16), (MLA_KV_RANK, BF16), (LANES, F32), (LANES, F32), (LANES, F32)], [], 256, "lat_norm")
    q_raw = _mm(cqn, w_uq, "nn", F32, "q_up")
    kv = _mm(ckvn, w_ukv, "nn", F32, "kv_up")

    def f_heads(q_raw, kv, krr, rc, rs):
        lane = _lane(krr.shape)
        qs, ks, vs = [], [], []
        for h in range(MLA_HEADS):
            qh = q_raw[:, h * LANES:(h + 1) * LANES]
            kvh = kv[:, h * LANES:(h + 1) * LANES]
            qs.append(qh * rc + _rope_swap(qh) * rs)
            ks.append(jnp.where(lane < HALF, kvh, krr))
            vs.append(jnp.where(lane >= HALF, kvh, 0.0))
        return jnp.concatenate(qs, axis=1), jnp.concatenate(ks, axis=1), jnp.concatenate(vs, axis=1)

    wide = MLA_HEADS * LANES
    qc, kc, vp = _rowwise(f_heads, [q_raw, kv, krr, rope_c, rope_s], [(wide, BF16)] * 3, [], 256, "mla_heads")
    o_sb, sb_tot, sb_first = _sb_fwd(qkv, "sb_fwd")
    o_mla, lse = _mla_fwd(qc, kc, vp, "mla_fwd")
    m_sb = _mm(o_sb, w_ps, "nn", F32, "proj_o_sb")
    m_mla = _mm(o_mla, w_pm, "nn", F32, "proj_o_mla")

    def f_merge(m_sb, m_mla, gl, b):
        g = _sigmoid(gl + b)
        return (g[:, :D_MODEL] * m_sb + g[:, D_MODEL:] * m_mla,)

    (merged,) = _rowwise(f_merge, [m_sb, m_mla, gl, b_gate], [(D_MODEL, BF16)], [], 256, "merge")
    y = _mm(merged, w_out, "nn", F32, "mix_out")

    def f_res1(x, y, g2, g3):
        x2 = x + _rms(y)[1] * g2
        return x2, _rms(x2)[1] * g3

    x2, h2 = _rowwise(f_res1, [x, y, g_post, g_fpre], [(D_MODEL, F32), (D_MODEL, BF16)], [], 256, "res1")

    g_ff, u_ff, act = _ffn_up(h2, w_gu, "ffn_up")
    f = _mm(act, w_dn, "nn", F32, "ffn_down")

    def f_loss(x2, f, g4, tgt):
        r, fhat = _rms(f)
        err = x2 + fhat * g4 - tgt
        loss = 0.5 * jnp.sum(jnp.sum(err * err, axis=-1, keepdims=True), axis=0, keepdims=True) / D_MODEL
        dout = err / D_MODEL
        return dout, _rms_bwd(dout, r, fhat, g4), loss, _colsum(dout * fhat)

    dout, df, loss, d_fpost = _rowwise(f_loss, [x2, f, g_fpost, target], [(D_MODEL, F32), (D_MODEL, BF16)],
                                       [1, D_MODEL], 256, "loss")

    dg_ff, du_ff = _ffn_dact(df, w_dn, g_ff, u_ff, "d_act")
    dw_dn = _mm(act, df, "tn", F32, "dw_down")
    dh2 = _mm_nt_sum([(dg_ff, w_gu[:, :D_FF]), (du_ff, w_gu[:, D_FF:])], "d_h2")
    dw_gu = jnp.concatenate([_mm(h2, dg_ff, "tn", F32, "dw_gate"), _mm(h2, du_ff, "tn", F32, "dw_up")], axis=1)

    def f_dres1(dout, dh2, x2, y, g3, g2):
        r3, x2hat = _rms(x2)
        dx2 = dout + _rms_bwd(dh2, r3, x2hat, g3)
        r2, yhat = _rms(y)
        return dx2, _rms_bwd(dx2, r2, yhat, g2), _colsum(dh2 * x2hat), _colsum(dx2 * yhat)

    dx2, dy, d_fpre, d_post = _rowwise(f_dres1, [dout, dh2, x2, y, g_fpre, g_post],
                                       [(D_MODEL, F32), (D_MODEL, BF16)], [D_MODEL, D_MODEL], 256, "d_res1")

    dmerged = _mm(dy, w_out, "nt", F32, "d_merged")
    dw_out = _mm(merged, dy, "tn", F32, "dw_out")

    def f_dmerge(dm, m_sb, m_mla, gl, b):
        g = _sigmoid(gl + b)
        g0, g1 = g[:, :D_MODEL], g[:, D_MODEL:]
        dgl = jnp.concatenate([dm * m_sb * g0 * (1.0 - g0), dm * m_mla * g1 * (1.0 - g1)], axis=1)
        return dm * g0, dm * g1, dgl, _colsum(dgl)

    dm_sb, dm_mla, dgl, d_bgate = _rowwise(f_dmerge, [dmerged, m_sb, m_mla, gl, b_gate],
                                           [(D_MODEL, BF16), (D_MODEL, BF16), (2 * D_MODEL, BF16)], [2 * D_MODEL],
                                           256, "d_merge")
    do_sb = _mm(dm_sb, w_ps, "nt", BF16, "d_o_sb")
    dw_ps = _mm(o_sb, dm_sb, "tn", F32, "dw_proj_sb")
    do_mla = _mm(dm_mla, w_pm, "nt", F32, "d_o_mla")
    dw_pm = _mm(o_mla, dm_mla, "tn", F32, "dw_proj_mla")

    dq_sb, dk_sb, dv_sb = _sb_bwd(qkv, do_sb, sb_tot, sb_first, "sb_bwd")
    dqc, dkc, dvp = _mla_bwd(qc, kc, vp, do_mla, o_mla, lse, "mla_bwd")

    def f_dheads(dqc, dkc, dvp, rc, rs):
        lane = _lane(rc.shape)
        dqs, dkvs = [], []
        dkr = jnp.zeros(rc.shape, F32)
        for h in range(MLA_HEADS):
            dq = dqc[:, h * LANES:(h + 1) * LANES]
            dk = dkc[:, h * LANES:(h + 1) * LANES]
            dqs.append(dq * rc + _rope_swap(dq * rs))
            dkvs.append(jnp.where(lane < HALF, dk, dvp[:, h * LANES:(h + 1) * LANES]))
            dkr = dkr + jnp.where(lane >= HALF, dk, 0.0)
        dkr = dkr * rc + _rope_swap(dkr * rs)
        return jnp.concatenate(dqs, axis=1), jnp.concatenate(dkvs, axis=1), dkr

    dq_raw, dkv, dkr = _rowwise(f_dheads, [dqc, dkc, dvp, rope_c, rope_s],
                                [(wide, BF16), (wide, BF16), (LANES, F32)], [], 256, "d_mla_heads")
    dcqn = _mm(dq_raw, w_uq, "nt", F32, "d_cqn")
    dw_uq = _mm(cqn, dq_raw, "tn", F32, "dw_uq")
    dckvn = _mm(dkv, w_ukv, "nt", F32, "d_ckvn")
    dw_ukv = _mm(ckvn, dkv, "tn", F32, "dw_ukv")

    def f_dlat(lat, dcqn, dckvn, dkr, gq, gkv):
        rq, cqhat = _rms(lat[:, 0:MLA_Q_RANK])
        rkv, ckvhat = _rms(lat[:, MLA_Q_RANK:MLA_Q_RANK + MLA_KV_RANK])
        dlat = jnp.concatenate([_rms_bwd(dcqn, rq, cqhat, gq), _rms_bwd(dckvn, rkv, ckvhat, gkv), dkr], axis=1)
        return dlat, _colsum(dcqn * cqhat), _colsum(dckvn * ckvhat)

    dlat, d_gq, d_gkv = _rowwise(f_dlat, [lat, dcqn, dckvn, dkr, g_q, g_kv], [(768, BF16)],
                                 [MLA_Q_RANK, MLA_KV_RANK], 256, "d_lat")

    sb_parts = [(dq_sb, w_a[:, 0:SB_WIDTH]), (dk_sb, w_a[:, SB_WIDTH:2 * SB_WIDTH]), (dv_sb, w_a[:, 2 * SB_WIDTH:])]
    dh1 = _mm_nt_sum(sb_parts + [(dlat, w_b), (dgl, w_g)], "d_h1")
    dw_sb = [_mm(h1, d, "tn", F32, "dw_in_sb_" + tag) for tag, (d, _) in zip("qkv", sb_parts)]
    dw_b = _mm(h1, dlat, "tn", F32, "dw_in_lat")
    dw_g = _mm(h1, dgl, "tn", F32, "dw_in_gate")

    def f_dx(x, dx2, dh1, g1):
        r, xhat = _rms(x)
        return dx2 + _rms_bwd(dh1, r, xhat, g1), _colsum(dh1 * xhat)

    grad_x, d_pre = _rowwise(f_dx, [x, dx2, dh1, g_pre], [(D_MODEL, F32)], [D_MODEL], 256, "d_x")

    small = {"norm_mix_pre": d_pre, "norm_mix_post": d_post, "b_gate": d_bgate, "q_norm": d_gq, "kv_norm": d_gkv,
             "norm_ffn_pre": d_fpre, "norm_ffn_post": d_fpost}
    big = {
        "w_in": jnp.concatenate(dw_sb + [dw_b[:, 0:640], dw_b[:, 704:736], dw_g], axis=1),
        "w_uq": dw_uq.reshape(MLA_Q_RANK, MLA_HEADS, LANES)[:, :, :MLA_QK].reshape(MLA_Q_RANK, MLA_HEADS * MLA_QK),
        "w_ukv": dw_ukv,
        "w_proj_sb": dw_ps,
        "w_proj_mla": dw_pm.reshape(MLA_HEADS, LANES, D_MODEL)[:, HALF:, :].reshape(MLA_HEADS * HALF, D_MODEL),
        "w_out": dw_out,
        "w_gate_up": dw_gu,
        "w_down": dw_dn,
    }
    return loss, grad_x, small, big


def kernel(x, positions, norm_mix_pre, norm_mix_post, w_in, b_gate, q_norm, w_uq, kv_norm, w_ukv, w_proj_sb, w_proj_mla, w_out, norm_ffn_pre, norm_ffn_post, w_gate_up, w_down, loss_target, m_norm_mix_pre, m_norm_mix_post, m_w_in, m_b_gate, m_q_norm, m_w_uq, m_kv_norm, m_w_ukv, m_w_proj_sb, m_w_proj_mla, m_w_out, m_norm_ffn_pre, m_norm_ffn_post, m_w_gate_up, m_w_down, v_norm_mix_pre, v_norm_mix_post, v_w_in, v_b_gate, v_q_norm, v_w_uq, v_kv_norm, v_w_ukv, v_w_proj_sb, v_w_proj_mla, v_w_out, v_norm_ffn_pre, v_norm_ffn_post, v_w_gate_up, v_w_down):
    given = dict(locals())
    s = x.shape[1]
    names = [n for n, _, _, _ in BIG]
    small_w = {n: given[n] for n, _ in SMALL}

    own = [given[n][0].astype(BF16) for n in names]
    gathered = _gather_weights(own, "gather_weights")
    loss, grad_x, d_small, d_big = _forward_backward(
        x.reshape(s, D_MODEL), positions.reshape(s, 1), loss_target.reshape(s, D_MODEL), small_w,
        _full_matrices(own, gathered))

    pieces = _split_for_chips(d_big)
    from_sibling = _swap_halves(pieces, "grad_swap_halves")
    partial = [_add_cores(g, a, "grad_add_cores_" + n) for n, g, a in zip(names, pieces, from_sibling)]
    received = _scatter_chips([b for _, b in partial], "grad_scatter_chips")
    mine = [_add_chips(f, r, "grad_add_chips_" + n) for n, (f, _), r in zip(names, partial, received)]
    other = _to_sibling(mine, "grad_to_sibling")
    small_sum = _allsum_small(_pack_small(d_small, loss), "small_allsum")

    grads, deltas, new_m, new_v = {}, {}, {}, {}
    for n, a, b in zip(names, mine, other):
        grads[n], deltas[n], new_m[n], new_v[n] = _adamw_shard(given[n], given["m_" + n], given["v_" + n], a, b,
                                                               "adamw_" + n)
    delta_s, m_s, v_s = _adamw_small(_pack_small(small_w), small_sum,
                                     _pack_small({n: given["m_" + n] for n in small_w}),
                                     _pack_small({n: given["v_" + n] for n in small_w}), "adamw_small")

    loss_out = small_sum.reshape(-1)[SMALL_ELEMS]
    grads.update(_unpack_small(small_sum))
    deltas.update(_unpack_small(delta_s))
    new_m.update(_unpack_small(m_s))
    new_v.update(_unpack_small(v_s))
    return (loss_out, grad_x.reshape(1, s, D_MODEL), *[grads[n] for n in WEIGHT_ORDER],
            *[deltas[n] for n in WEIGHT_ORDER], *[new_m[n] for n in WEIGHT_ORDER], *[new_v[n] for n in WEIGHT_ORDER])
```

```python
import functools
import math

import jax
import jax.numpy as jnp
from jax import lax
from jax.experimental import pallas as pl
from jax.experimental.pallas import tpu as pltpu

F32 = jnp.float32
BF16 = jnp.bfloat16
MESH = pl.DeviceIdType.MESH

D_MODEL = 1024
SB_WIDTH = 512
MLA_HEADS = 8
MLA_Q_RANK = 384
MLA_KV_RANK = 256
MLA_ROPE = 32
MLA_QK = 96
D_FF = 2816
ROPE_THETA = 10000.0
EPS = 1e-6
LANES = 128
HALF = 64
ATT_TILE = 256
ROW_TILE = 512
N_CHIPS = 4

ADAM_LR = 0.001
ADAM_B1 = 0.9
ADAM_B2 = 0.999
ADAM_EPS = 1e-08
ADAM_WD = 0.01
ADAM_STEP = 10

VMEM_LIMIT = 56 * 1024 * 1024

BIG = (
    ("w_in", 1024, 1064, 1),
    ("w_uq", 384, 192, 1),
    ("w_ukv", 256, 256, 1),
    ("w_proj_sb", 512, 256, 1),
    ("w_proj_mla", 512, 256, 1),
    ("w_out", 256, 1024, 0),
    ("w_gate_up", 1024, 1408, 1),
    ("w_down", 704, 1024, 0),
)
PACK_COLS = 1024
SMALL = (
    ("norm_mix_pre", 1024),
    ("norm_mix_post", 1024),
    ("b_gate", 2048),
    ("q_norm", 384),
    ("kv_norm", 256),
    ("norm_ffn_pre", 1024),
    ("norm_ffn_post", 1024),
)
SMALL_ELEMS = sum(n for _, n in SMALL)
SMALL_ROWS = 8
WEIGHT_ORDER = ("norm_mix_pre", "norm_mix_post", "w_in", "b_gate", "q_norm", "w_uq", "kv_norm", "w_ukv",
                "w_proj_sb", "w_proj_mla", "w_out", "norm_ffn_pre", "norm_ffn_post", "w_gate_up", "w_down")


def _params():
    return pltpu.CompilerParams(vmem_limit_bytes=VMEM_LIMIT)


def _tile(dim, pref):
    t = (min(pref, dim) // LANES) * LANES
    while t >= LANES:
        if dim % t == 0:
            return t
        t -= LANES
    return dim


def _mm(a, b, mode, out_dtype, name, tm=1408, tn=1408, tk=1408):
    if mode == "nn":
        (m, k), n = a.shape, b.shape[1]
    elif mode == "nt":
        (m, k), n = a.shape, b.shape[0]
    else:
        (k, m), n = a.shape, b.shape[1]
    tm, tn, tk = _tile(m, tm), _tile(n, tn), _tile(k, tk)
    nk = k // tk
    if mode == "tn":
        a_spec = pl.BlockSpec((tk, tm), lambda i, j, l: (l, i))
        lhs_dim = 0
    else:
        a_spec = pl.BlockSpec((tm, tk), lambda i, j, l: (i, l))
        lhs_dim = 1
    if mode == "nt":
        b_spec = pl.BlockSpec((tn, tk), lambda i, j, l: (j, l))
        rhs_dim = 1
    else:
        b_spec = pl.BlockSpec((tk, tn), lambda i, j, l: (l, j))
        rhs_dim = 0
    dims = (((lhs_dim,), (rhs_dim,)), ((), ()))

    def body(a_ref, b_ref, o_ref, acc_ref):
        l = pl.program_id(2)

        @pl.when(l == 0)
        def _():
            acc_ref[...] = jnp.zeros_like(acc_ref)

        acc_ref[...] += lax.dot_general(a_ref[...].astype(BF16), b_ref[...].astype(BF16), dims,
                                        preferred_element_type=F32)

        @pl.when(l == nk - 1)
        def _():
            o_ref[...] = acc_ref[...].astype(o_ref.dtype)

    return pl.pallas_call(
        body,
        name=name,
        grid=(m // tm, n // tn, nk),
        in_specs=[a_spec, b_spec],
        out_specs=pl.BlockSpec((tm, tn), lambda i, j, l: (i, j)),
        out_shape=jax.ShapeDtypeStruct((m, n), out_dtype),
        scratch_shapes=[pltpu.VMEM((tm, tn), F32)],
        compiler_params=_params(),
    )(a, b)


def _mm_nt_sum(pairs, name, tm=512, tn=1024):
    m, n = pairs[0][0].shape[0], pairs[0][1].shape[0]
    tm, tn = _tile(m, tm), _tile(n, tn)

    def body(*refs):
        total = None
        for p in range(len(pairs)):
            part = _dot_nt(refs[2 * p][...].astype(BF16), refs[2 * p + 1][...].astype(BF16))
            total = part if total is None else total + part
        refs[-1][...] = total

    in_specs = []
    for a, b in pairs:
        in_specs += [pl.BlockSpec((tm, a.shape[1]), lambda i, j: (i, 0)),
                     pl.BlockSpec((tn, b.shape[1]), lambda i, j: (j, 0))]
    return pl.pallas_call(
        body,
        name=name,
        grid=(m // tm, n // tn),
        in_specs=in_specs,
        out_specs=pl.BlockSpec((tm, tn), lambda i, j: (i, j)),
        out_shape=jax.ShapeDtypeStruct((m, n), F32),
        compiler_params=_params(),
    )(*[r for pair in pairs for r in pair])


def _ffn_up(h, w_gu, name):
    s, k = h.shape
    tm, tn = _tile(s, 512), _tile(D_FF, 1408)
    nb = D_FF // tn

    def body(h_ref, wg_ref, wu_ref, g_ref, u_ref, act_ref):
        hv = h_ref[...]
        g = jnp.dot(hv, wg_ref[...], preferred_element_type=F32)
        u = jnp.dot(hv, wu_ref[...], preferred_element_type=F32)
        g_ref[...] = g
        u_ref[...] = u
        act_ref[...] = (g * _sigmoid(g) * u).astype(BF16)

    tile = pl.BlockSpec((tm, tn), lambda j, i: (i, j))
    out = jax.ShapeDtypeStruct((s, D_FF), F32)
    return pl.pallas_call(
        body,
        name=name,
        grid=(nb, s // tm),
        in_specs=[pl.BlockSpec((tm, k), lambda j, i: (i, 0)), pl.BlockSpec((k, tn), lambda j, i: (0, j)),
                  pl.BlockSpec((k, tn), lambda j, i: (0, nb + j))],
        out_specs=[tile, tile, tile],
        out_shape=[out, out, jax.ShapeDtypeStruct((s, D_FF), BF16)],
        compiler_params=_params(),
    )(h, w_gu, w_gu)


def _ffn_dact(df, w_dn, g, u, name):
    s, k = df.shape
    tm, tn = _tile(s, 512), _tile(D_FF, 1408)

    def body(df_ref, w_ref, g_ref, u_ref, dg_ref, du_ref):
        dact = _dot_nt(df_ref[...], w_ref[...])
        gv, uv = g_ref[...], u_ref[...]
        sg = _sigmoid(gv)
        dg_ref[...] = (dact * uv * (sg * (1.0 + gv * (1.0 - sg)))).astype(BF16)
        du_ref[...] = (dact * (gv * sg)).astype(BF16)

    tile = pl.BlockSpec((tm, tn), lambda j, i: (i, j))
    out = jax.ShapeDtypeStruct((s, D_FF), BF16)
    return pl.pallas_call(
        body,
        name=name,
        grid=(D_FF // tn, s // tm),
        in_specs=[pl.BlockSpec((tm, k), lambda j, i: (i, 0)), pl.BlockSpec((tn, k), lambda j, i: (j, 0)), tile, tile],
        out_specs=[tile, tile],
        out_shape=[out, out],
        compiler_params=_params(),
    )(df, w_dn, g, u)


def _rowwise(fn, ins, outs, reds, tm, name):
    rows = ins[0].shape[0]
    tm = min(tm, rows)
    n_in, n_out = len(ins), len(outs)

    def spec(shape):
        if shape[0] == rows:
            return pl.BlockSpec((tm, shape[1]), lambda i: (i, 0))
        return pl.BlockSpec(shape, lambda i: (0, 0))

    def body(*refs):
        i = pl.program_id(0)
        vals = fn(*[r[...] for r in refs[:n_in]])
        for r, v in zip(refs[n_in:n_in + n_out], vals[:n_out]):
            r[...] = v.astype(r.dtype)
        for r, v in zip(refs[n_in + n_out:], vals[n_out:]):
            @pl.when(i == 0)
            def _():
                r[...] = jnp.zeros_like(r)

            r[...] += v

    out_shape = [jax.ShapeDtypeStruct((rows, d), dt) for d, dt in outs]
    out_shape += [jax.ShapeDtypeStruct((1, d), F32) for d in reds]
    out_specs = [spec(s.shape) for s in out_shape]
    return pl.pallas_call(
        body,
        name=name,
        grid=(rows // tm,),
        in_specs=[spec(a.shape) for a in ins],
        out_specs=out_specs,
        out_shape=out_shape,
        compiler_params=_params(),
    )(*ins)


def _colsum(v):
    return jnp.sum(v, axis=0, keepdims=True)


def _rms(v):
    r = lax.rsqrt(jnp.mean(v * v, axis=-1, keepdims=True) + EPS)
    return r, v * r


def _rms_bwd(dy, r, vhat, g):
    u = dy * g
    return r * (u - vhat * jnp.mean(u * vhat, axis=-1, keepdims=True))


def _sigmoid(t):
    return 1.0 / (1.0 + jnp.exp(-t))


def _lane(shape):
    return lax.broadcasted_iota(jnp.int32, shape, len(shape) - 1)


def _rope_swap(v):
    lane = _lane(v.shape)
    up = pltpu.roll(v, LANES - 16, 1)
    down = pltpu.roll(v, 16, 1)
    return jnp.where((lane >= 64) & (lane < 80), up, jnp.where((lane >= 80) & (lane < 96), down, 0.0))


SB_DEAD = -120.0


def _softplus(z):
    return jnp.maximum(z, 0.0) + jnp.log(1.0 + jnp.exp(-jnp.abs(z)))


def _split_dot(v, m2):
    hi = v.astype(BF16)
    lo = (v - hi.astype(F32)).astype(BF16)
    return jnp.dot(jnp.concatenate([hi, lo], axis=1), m2, preferred_element_type=F32)


def _dot_nt(a, b):
    return lax.dot_general(a, b, (((1,), (1,)), ((), ())), preferred_element_type=F32)


def _dot_tn(a, b):
    return lax.dot_general(a, b, (((0,), (0,)), ((), ())), preferred_element_type=F32)


def _sb_fwd(qkv, name):
    s = qkv.shape[0]
    t = min(ATT_TILE, s // 2)
    npair = SB_WIDTH // LANES

    def body(q_ref, k_ref, v_ref, o_ref, t_ref, first_ref, c_s, acc_s):
        i = pl.program_id(1)
        lane = _lane((1, LANES))
        row = lax.broadcasted_iota(jnp.int32, (t, t), 0)
        col = lax.broadcasted_iota(jnp.int32, (t, t), 1)
        tri = col < row
        m_from = jnp.where(row >= col, 1.0, 0.0).astype(BF16)
        m_from = jnp.concatenate([m_from, m_from], axis=0)
        masks = (lane < HALF, lane >= HALF)
        qhs = [jnp.where(hm, q_ref[n * t:(n + 1) * t, :], jnp.zeros((t, LANES), BF16)) * 0.125
               for n in range(2) for hm in masks]

        c_s[...] = jnp.zeros_like(c_s)
        acc_s[...] = jnp.zeros_like(acc_s)

        def step(k, diag, chains):
            blocks = {}
            for n in sorted({ch // 2 for ch in chains}):
                rows = pl.ds(pl.multiple_of((2 * i + n - k) * t, t), t)
                blocks[n] = (k_ref[rows, :], v_ref[rows, :])
            zs = {ch: _dot_nt(qhs[ch], blocks[ch // 2][0]) for ch in chains}
            lbs = {}
            for ch in chains:
                lb = -_softplus(zs[ch])
                lbs[ch] = jnp.where(tri, lb, 0.0) if diag else lb
            sums = {ch: _split_dot(lbs[ch], m_from) for ch in chains}
            for ch in chains:
                c = c_s[ch]
                a = jnp.exp(zs[ch] + sums[ch] + jnp.concatenate([c] * (t // LANES), axis=1))
                if diag:
                    a = jnp.where(tri, a, 0.0)
                vs = blocks[ch // 2][1]
                vh = jnp.where(masks[ch % 2], vs, jnp.zeros_like(vs))
                acc_s[ch] += jnp.dot(a.astype(BF16), vh, preferred_element_type=F32)
                c_s[ch] = c + sums[ch][:, 0:1]

        def live():
            top = jnp.max(c_s[0])
            for ch in range(1, 4):
                top = jnp.maximum(top, jnp.max(c_s[ch]))
            return top

        step(0, True, (0, 1, 2, 3))
        last = 2 * i

        def more(state):
            k, top = state
            return (k <= last) & (top > SB_DEAD)

        def walk(state):
            k, _ = state
            step(k, False, (0, 1, 2, 3))
            return k + 1, live()

        k_end, top = lax.while_loop(more, walk, (jnp.int32(1), live()))
        tail = (k_end == last + 1) & (top > SB_DEAD)

        @pl.when(tail)
        def _():
            step(last + 1, False, (2, 3))

        first_ref[0, 0] = jnp.full((8, LANES), (k_end - 1).astype(F32) + jnp.where(tail, 0.5, 0.0))
        for n in range(2):
            o_ref[n * t:(n + 1) * t, :] = acc_s[2 * n] + acc_s[2 * n + 1]
            t_ref[n * t:(n + 1) * t, :] = jnp.where(masks[0], c_s[2 * n], c_s[2 * n + 1])

    blk = pl.BlockSpec((2 * t, LANES), lambda p, i: (i, p))
    steps = s // (2 * t)
    return pl.pallas_call(
        body,
        name=name,
        grid=(npair, steps),
        in_specs=[
            blk,
            pl.BlockSpec((s, LANES), lambda p, i: (0, npair + p)),
            pl.BlockSpec((s, LANES), lambda p, i: (0, 2 * npair + p)),
        ],
        out_specs=[blk, blk, pl.BlockSpec((1, 1, 8, LANES), lambda p, i: (p, i, 0, 0))],
        out_shape=[jax.ShapeDtypeStruct((s, SB_WIDTH), F32), jax.ShapeDtypeStruct((s, SB_WIDTH), F32),
                   jax.ShapeDtypeStruct((npair, steps, 8, LANES), F32)],
        scratch_shapes=[pltpu.VMEM((4, t, LANES), F32)] * 2,
        compiler_params=_params(),
    )(qkv, qkv, qkv)


def _sb_bwd(qkv, do, tot, first, name):
    s = qkv.shape[0]
    t = min(ATT_TILE, s // 2)
    npair = SB_WIDTH // LANES

    def body(q_ref, k_ref, v_ref, do_ref, t_ref, first_ref, dq_ref, dk_ref, dv_ref, bef_s, ebef_s):
        i = pl.program_id(1)

        @pl.when(i == 0)
        def _():
            dk_ref[...] = jnp.zeros_like(dk_ref)
            dv_ref[...] = jnp.zeros_like(dv_ref)

        lane = _lane((1, LANES))
        row = lax.broadcasted_iota(jnp.int32, (t, t), 0)
        col = lax.broadcasted_iota(jnp.int32, (t, t), 1)
        tri = col < row
        m_from = jnp.where(row >= col, 1.0, 0.0).astype(BF16)
        m_from = jnp.concatenate([m_from, m_from], axis=0)
        m_upto = jnp.where(row <= col, 1.0, 0.0).astype(BF16)
        m_upto = jnp.concatenate([m_upto, m_upto], axis=0)
        masks = (lane < HALF, lane >= HALF)
        zero_b = jnp.zeros((t, LANES), BF16)
        qhs, dohs, tots = [], [], []
        for n in range(2):
            rs = slice(n * t, (n + 1) * t)
            do_b = do_ref[rs, :].astype(BF16)
            for h, hm in enumerate(masks):
                qhs.append(jnp.where(hm, q_ref[rs, :], zero_b))
                dohs.append(jnp.where(hm, do_b, zero_b))
                tots.append(t_ref[rs, h * HALF:h * HALF + 1])

        bef_s[...] = jnp.zeros_like(bef_s)
        ebef_s[...] = jnp.zeros_like(ebef_s)
        dq_ref[...] = jnp.zeros_like(dq_ref)
        reps = t // LANES

        def step(k, diag, chains):
            qblocks = sorted({ch // 2 for ch in chains})
            rows, blocks = {}, {}
            for n in qblocks:
                rows[n] = pl.ds(pl.multiple_of((2 * i + n - k) * t, t), t)
                blocks[n] = (k_ref[rows[n], :], v_ref[rows[n], :])
            zs = {ch: _dot_nt(qhs[ch] * 0.125, blocks[ch // 2][0]) for ch in chains}
            das = {ch: _dot_nt(dohs[ch], blocks[ch // 2][1]) for ch in chains}
            lbs, betas = {}, {}
            for ch in chains:
                sp = _softplus(zs[ch])
                betas[ch] = jnp.exp(zs[ch] - sp)
                lbs[ch] = jnp.where(tri, -sp, 0.0) if diag else -sp
            sums = {ch: _split_dot(lbs[ch], m_from) for ch in chains}
            avs, es = {}, {}
            for ch in chains:
                before = bef_s[ch]
                total = sums[ch][:, 0:1]
                a = jnp.exp(zs[ch] + sums[ch] + jnp.concatenate([tots[ch] - before - total] * reps, axis=1))
                if diag:
                    a = jnp.where(tri, a, 0.0)
                avs[ch] = a.astype(BF16)
                es[ch] = a * das[ch]
                bef_s[ch] = before + total
            e_sums = {ch: _split_dot(es[ch], m_upto) for ch in chains}
            dzs = {}
            for ch in chains:
                e_upto = e_sums[ch] + jnp.concatenate([ebef_s[ch]] * reps, axis=1)
                dz = es[ch] - betas[ch] * e_upto
                if diag:
                    dz = jnp.where(tri, dz, 0.0)
                dzs[ch] = (dz * 0.125).astype(BF16)
                ebef_s[ch] = jnp.broadcast_to(e_upto[:, t - 1:t], (t, LANES))
            for n in qblocks:
                ks = blocks[n][0]
                mine = [ch for ch in chains if ch // 2 == n]
                dq = sum(jnp.dot(dzs[ch], jnp.where(masks[ch % 2], ks, jnp.zeros_like(ks)),
                                 preferred_element_type=F32) for ch in mine)
                dq_ref[n * t:(n + 1) * t, :] += dq
                dk_ref[rows[n], :] += sum(_dot_tn(dzs[ch], qhs[ch]) for ch in mine)
                dv_ref[rows[n], :] += sum(_dot_tn(avs[ch], dohs[ch]) for ch in mine)

        walked = jnp.max(first_ref[0, 0])
        full_steps = jnp.clip(walked.astype(jnp.int32), 0, 2 * i)

        @pl.when(walked - full_steps.astype(F32) > 0.25)
        def _():
            step(2 * i + 1, False, (2, 3))

        def walk(n, carry):
            step(full_steps - n, False, (0, 1, 2, 3))
            return carry

        lax.fori_loop(0, full_steps, walk, 0)
        step(0, True, (0, 1, 2, 3))

    blk = pl.BlockSpec((2 * t, LANES), lambda p, i: (i, p))
    full = pl.BlockSpec((s, LANES), lambda p, i: (0, p))
    out = jax.ShapeDtypeStruct((s, SB_WIDTH), F32)
    return pl.pallas_call(
        body,
        name=name,
        grid=(npair, s // (2 * t)),
        in_specs=[
            blk,
            pl.BlockSpec((s, LANES), lambda p, i: (0, npair + p)),
            pl.BlockSpec((s, LANES), lambda p, i: (0, 2 * npair + p)),
            blk,
            blk,
            pl.BlockSpec((1, 1, 8, LANES), lambda p, i: (p, i, 0, 0)),
        ],
        out_specs=[blk, full, full],
        out_shape=[out, out, out],
        scratch_shapes=[pltpu.VMEM((4, t, LANES), F32)] * 2,
        compiler_params=_params(),
    )(qkv, qkv, qkv, do, tot, first)


MLA_SCALE = 1.0 / math.sqrt(MLA_QK)
MLA_SCALE_LOG2E = MLA_SCALE * math.log2(math.e)
NEG = -1e30
MLA_QBLOCKS = 8


def _diag_plan(nq):
    return [tuple(None if n < j else n == j for n in range(nq)) for j in range(nq)]


def _mla_fwd(qc, kc, vp, name):
    s = qc.shape[0]
    nq = MLA_QBLOCKS
    t = min(ATT_TILE, s // nq)

    def body(q_ref, k_ref, v_ref, o_ref, lse_ref, m_s, l_s, acc_s):
        i = pl.program_id(1)
        row = lax.broadcasted_iota(jnp.int32, (t, t), 0)
        col = lax.broadcasted_iota(jnp.int32, (t, t), 1)
        keep = col <= row
        m_s[...] = jnp.full(m_s.shape, NEG, F32)
        l_s[...] = jnp.zeros_like(l_s)
        acc_s[...] = jnp.zeros_like(acc_s)

        def step(kb, diags):
            rows = pl.ds(pl.multiple_of(kb * t, t), t)
            ks = k_ref[rows, :]
            vs = v_ref[rows, :]
            act = [n for n in range(nq) if diags[n] is not None]
            scs = {n: _dot_nt(q_ref[n * t:(n + 1) * t, :], ks) for n in act}
            mid = {}
            for n in act:
                sc = jnp.where(keep, scs[n], NEG) if diags[n] else scs[n]
                m = m_s[n]
                m_new = jnp.maximum(m, jnp.max(sc, axis=-1, keepdims=True))
                alpha = jnp.exp2((m - m_new) * MLA_SCALE_LOG2E)
                p = jnp.exp2((sc - jnp.concatenate([m_new] * (t // LANES), axis=1)) * MLA_SCALE_LOG2E)
                m_s[n] = m_new
                l_s[n] = alpha * l_s[n] + jnp.sum(p, axis=-1, keepdims=True)
                mid[n] = (alpha, p.astype(BF16))
            for n in act:
                alpha, p = mid[n]
                acc_s[n] = alpha * acc_s[n] + jnp.dot(p, vs, preferred_element_type=F32)

        def walk(kb, carry):
            step(kb, (False,) * nq)
            return carry

        lax.fori_loop(0, nq * i, walk, 0)
        for j, diags in enumerate(_diag_plan(nq)):
            step(nq * i + j, diags)
        for n in range(nq):
            l = l_s[n]
            o_ref[n * t:(n + 1) * t, :] = acc_s[n] / l
            lse_ref[n * t:(n + 1) * t, :] = m_s[n] * MLA_SCALE + jnp.log(l)

    blk = pl.BlockSpec((nq * t, LANES), lambda h, i: (i, h))
    full = pl.BlockSpec((s, LANES), lambda h, i: (0, h))
    out = jax.ShapeDtypeStruct((s, MLA_HEADS * LANES), F32)
    return pl.pallas_call(
        body,
        name=name,
        grid=(MLA_HEADS, s // (nq * t)),
        in_specs=[blk, full, full],
        out_specs=[blk, blk],
        out_shape=[out, out],
        scratch_shapes=[pltpu.VMEM((nq, t, LANES), F32)] * 3,
        compiler_params=_params(),
    )(qc, kc, vp)


def _mla_bwd(qc, kc, vp, do, o, lse, name):
    s = qc.shape[0]
    nq = MLA_QBLOCKS
    t = min(ATT_TILE, s // nq)

    def body(q_ref, k_ref, v_ref, do_ref, o_ref, lse_ref, dq_ref, dk_ref, dv_ref):
        i = pl.program_id(1)

        @pl.when(i == 0)
        def _():
            dk_ref[...] = jnp.zeros_like(dk_ref)
            dv_ref[...] = jnp.zeros_like(dv_ref)

        row = lax.broadcasted_iota(jnp.int32, (t, t), 0)
        col = lax.broadcasted_iota(jnp.int32, (t, t), 1)
        keep = col <= row

        def block(n):
            rs = slice(n * t, (n + 1) * t)
            do_f = do_ref[rs, :]
            delta = jnp.sum(do_f * o_ref[rs, :], axis=-1, keepdims=True)
            return q_ref[rs, :], do_f.astype(BF16), delta * MLA_SCALE, lse_ref[rs, 0:1] * math.log2(math.e)

        blocks = [block(n) for n in range(nq)]
        dq_ref[...] = jnp.zeros_like(dq_ref)

        def step(kb, diags):
            rows = pl.ds(pl.multiple_of(kb * t, t), t)
            ks = k_ref[rows, :]
            vs = v_ref[rows, :]
            act = [n for n in range(nq) if diags[n] is not None]
            scs = {n: _dot_nt(blocks[n][0], ks) for n in act}
            dps = {n: _dot_nt(blocks[n][1], vs) for n in act}
            ps, dss = {}, {}
            for n in act:
                _, _, delta_s, lse2 = blocks[n]
                p = jnp.exp2(scs[n] * MLA_SCALE_LOG2E - lse2)
                if diags[n]:
                    p = jnp.where(keep, p, 0.0)
                dss[n] = (p * (dps[n] * MLA_SCALE - delta_s)).astype(BF16)
                ps[n] = p.astype(BF16)
            dk = jnp.zeros((t, LANES), F32)
            dv = jnp.zeros((t, LANES), F32)
            for n in act:
                q, do_b, _, _ = blocks[n]
                dq_ref[n * t:(n + 1) * t, :] += jnp.dot(dss[n], ks, preferred_element_type=F32)
                dk = dk + _dot_tn(dss[n], q)
                dv = dv + _dot_tn(ps[n], do_b)
            dk_ref[rows, :] += dk
            dv_ref[rows, :] += dv

        def walk(kb, carry):
            step(kb, (False,) * nq)
            return carry

        lax.fori_loop(0, nq * i, walk, 0)
        for j, diags in enumerate(_diag_plan(nq)):
            step(nq * i + j, diags)

    blk = pl.BlockSpec((nq * t, LANES), lambda h, i: (i, h))
    full = pl.BlockSpec((s, LANES), lambda h, i: (0, h))
    out = jax.ShapeDtypeStruct((s, MLA_HEADS * LANES), F32)
    return pl.pallas_call(
        body,
        name=name,
        grid=(MLA_HEADS, s // (nq * t)),
        in_specs=[blk, full, full, blk, blk, blk],
        out_specs=[blk, full, full],
        out_shape=[out, out, out],
        compiler_params=_params(),
    )(qc, kc, vp, do, o, lse)


def _position():
    return lax.axis_index("x"), lax.axis_index("y"), lax.axis_index("c")


def _other_chips(x, y):
    return [(1 - x, y), (x, 1 - y), (1 - x, 1 - y)]


HBM_SPEC = pl.BlockSpec(memory_space=pltpu.HBM)


def _remote(src, dst, send_sem, recv_sem, device):
    return pltpu.make_async_remote_copy(src_ref=src, dst_ref=dst, send_sem=send_sem, recv_sem=recv_sem,
                                        device_id=device, device_id_type=MESH)


def _dma_sems(n, count):
    return [pltpu.SemaphoreType.DMA((n,)) for _ in range(count)]


def _gather_weights(ws, name):
    na = len(ws)

    def body(*refs):
        srcs, outs = refs[:na], refs[na:2 * na]
        s1, r1, s2, r2, s3, r3, s4, r4 = refs[2 * na:]
        x, y, c = _position()
        me = 2 * x + y
        px, py = jnp.where(c == 1, 1 - x, x), jnp.where(c == 1, y, 1 - y)
        qx, qy = jnp.where(c == 1, x, 1 - x), jnp.where(c == 1, 1 - y, y)
        p_idx, q_idx, d_idx = 2 * px + py, 2 * qx + qy, 2 * (1 - x) + (1 - y)
        partner, across, sibling = (px, py, c), (qx, qy, c), (x, y, 1 - c)
        my_half, other_half = 1 - c, c

        def half(i, slot, h):
            kh = ws[i].shape[0] // 2
            return outs[i].at[slot, pl.ds(h * kh, kh), :]

        sends = [_remote(srcs[i], outs[i].at[me], s1.at[i], r1.at[i], partner) for i in range(na)]
        for cp in sends:
            cp.start()
        for i in range(na):
            _remote(srcs[i], outs[i].at[p_idx], s1.at[i], r1.at[i], partner).wait_recv()
            onward = [_remote(half(i, p_idx, my_half), half(i, p_idx, my_half), s2.at[i], r2.at[i], across),
                      _remote(outs[i].at[p_idx], outs[i].at[p_idx], s3.at[i], r3.at[i], sibling)]
            for cp in onward:
                cp.start()
            sends += onward
        for i in range(na):
            _remote(half(i, d_idx, my_half), half(i, d_idx, my_half), s2.at[i], r2.at[i], across).wait_recv()
            cp = _remote(half(i, d_idx, my_half), half(i, d_idx, my_half), s4.at[i], r4.at[i], sibling)
            cp.start()
            sends.append(cp)
        for i in range(na):
            _remote(outs[i].at[q_idx], outs[i].at[q_idx], s3.at[i], r3.at[i], sibling).wait_recv()
            _remote(half(i, d_idx, other_half), half(i, d_idx, other_half), s4.at[i], r4.at[i], sibling).wait_recv()
        for cp in sends:
            cp.wait_send()

    return pl.pallas_call(
        body,
        name=name,
        in_specs=[HBM_SPEC] * na,
        out_specs=[HBM_SPEC] * na,
        out_shape=[jax.ShapeDtypeStruct((N_CHIPS,) + w.shape, w.dtype) for w in ws],
        scratch_shapes=_dma_sems(na, 8),
    )(*ws)


def _swap_halves(gs, name):
    na = len(gs)

    def body(*refs):
        srcs, outs, send_sems, recv_sems = refs[:na], refs[na:2 * na], refs[2 * na], refs[2 * na + 1]
        x, y, c = _position()
        cps = []
        for i in range(na):
            kh = gs[i].shape[1] // 2
            cps.append(_remote(srcs[i].at[:, pl.ds((1 - c) * kh, kh), :], outs[i], send_sems.at[i], recv_sems.at[i],
                               (x, y, 1 - c)))
            cps[-1].start()
        for cp in cps:
            cp.wait()

    return pl.pallas_call(
        body,
        name=name,
        in_specs=[HBM_SPEC] * na,
        out_specs=[HBM_SPEC] * na,
        out_shape=[jax.ShapeDtypeStruct((g.shape[0], g.shape[1] // 2, g.shape[2]), g.dtype) for g in gs],
        scratch_shapes=_dma_sems(na, 2),
    )(*gs)


def _scatter_chips(parts, name):
    na = len(parts)

    def body(*refs):
        srcs, outs, send_sems, recv_sems = refs[:na], refs[na:2 * na], refs[2 * na], refs[2 * na + 1]
        x, y, c = _position()
        me = 2 * x + y
        sends = []
        for i in range(na):
            for k, (px, py) in enumerate(_other_chips(x, y)):
                cp = _remote(srcs[i].at[2 * px + py], outs[i].at[me], send_sems.at[3 * i + k],
                             recv_sems.at[3 * i + k], (px, py, c))
                cp.start()
                sends.append(cp)
        for i in range(na):
            for k, (px, py) in enumerate(_other_chips(x, y)):
                _remote(srcs[i].at[me], outs[i].at[2 * px + py], send_sems.at[3 * i + k], recv_sems.at[3 * i + k],
                        (px, py, c)).wait_recv()
        for cp in sends:
            cp.wait_send()

    return pl.pallas_call(
        body,
        name=name,
        in_specs=[HBM_SPEC] * na,
        out_specs=[HBM_SPEC] * na,
        out_shape=[jax.ShapeDtypeStruct(p.shape, p.dtype) for p in parts],
        scratch_shapes=_dma_sems(3 * na, 2),
    )(*parts)


def _to_sibling(halves, name):
    na = len(halves)

    def body(*refs):
        srcs, outs, send_sems, recv_sems = refs[:na], refs[na:2 * na], refs[2 * na], refs[2 * na + 1]
        x, y, c = _position()
        cps = [_remote(srcs[i], outs[i], send_sems.at[i], recv_sems.at[i], (x, y, 1 - c)) for i in range(na)]
        for cp in cps:
            cp.start()
        for cp in cps:
            cp.wait()

    return pl.pallas_call(
        body,
        name=name,
        in_specs=[HBM_SPEC] * na,
        out_specs=[HBM_SPEC] * na,
        out_shape=[jax.ShapeDtypeStruct(h.shape, h.dtype) for h in halves],
        scratch_shapes=_dma_sems(na, 2),
    )(*halves)


def _allsum_small(v, name):
    shape = v.shape

    def body(v_ref, o_ref, buf, send_sems, recv_sems):
        x, y, c = _position()
        me = 4 * x + 2 * y + c
        buf[me] = v_ref[...]
        peers = []
        for k in range(1, 8):
            fx, fy, fc = (k >> 2) & 1, (k >> 1) & 1, k & 1
            peers.append((jnp.where(fx == 1, 1 - x, x), jnp.where(fy == 1, 1 - y, y), jnp.where(fc == 1, 1 - c, c)))
        sends = []
        for k, peer in enumerate(peers):
            cp = pltpu.make_async_remote_copy(src_ref=v_ref, dst_ref=buf.at[me], send_sem=send_sems.at[k],
                                              recv_sem=recv_sems.at[k], device_id=peer, device_id_type=MESH)
            cp.start()
            sends.append(cp)
        for k, (px, py, pc) in enumerate(peers):
            pltpu.make_async_remote_copy(src_ref=v_ref, dst_ref=buf.at[4 * px + 2 * py + pc],
                                         send_sem=send_sems.at[k], recv_sem=recv_sems.at[k],
                                         device_id=(px, py, pc), device_id_type=MESH).wait_recv()
        for cp in sends:
            cp.wait_send()
        acc = buf[0]
        for d in range(1, 8):
            acc = acc + buf[d]
        o_ref[...] = acc

    return pl.pallas_call(
        body,
        name=name,
        in_specs=[pl.BlockSpec(memory_space=pltpu.VMEM)],
        out_specs=pl.BlockSpec(memory_space=pltpu.VMEM),
        out_shape=jax.ShapeDtypeStruct(shape, F32),
        scratch_shapes=[pltpu.VMEM((8,) + shape, F32), pltpu.SemaphoreType.DMA((7,)), pltpu.SemaphoreType.DMA((7,))],
    )(v)


def _row_tile(h):
    for t in range(256, 0, -16):
        if h % t == 0:
            return t
    return h


def _add_cores(g, a, name):
    n4, k, n = g.shape
    kh = k // 2
    th = _row_tile(kh)

    def body(g_ref, a_ref, f_ref, b_ref):
        c = lax.axis_index("c")
        total = jnp.where(c == 0, g_ref[0, 0], g_ref[0, 1]) + a_ref[0]
        f_ref[0] = total
        b_ref[0] = total.astype(BF16)

    spec = pl.BlockSpec((1, th, n), lambda j, r: (j, r, 0))
    return pl.pallas_call(
        body,
        name=name,
        grid=(n4, kh // th),
        in_specs=[pl.BlockSpec((1, 2, th, n), lambda j, r: (j, 0, r, 0)), spec],
        out_specs=[spec, spec],
        out_shape=[jax.ShapeDtypeStruct((n4, kh, n), F32), jax.ShapeDtypeStruct((n4, kh, n), BF16)],
        compiler_params=_params(),
    )(g.reshape(n4, 2, kh, n), a)


def _add_chips(pf, b, name):
    n4, h, n = pf.shape
    th = _row_tile(h)

    def body(pf_ref, b_ref, o_ref):
        me = 2 * lax.axis_index("x") + lax.axis_index("y")
        terms = [jnp.where(me == j, pf_ref[j], b_ref[j].astype(F32)) for j in range(n4)]
        o_ref[...] = ((terms[0] + terms[1]) + terms[2]) + terms[3]

    spec = pl.BlockSpec((n4, th, n), lambda r: (0, r, 0))
    return pl.pallas_call(
        body,
        name=name,
        grid=(h // th,),
        in_specs=[spec, spec],
        out_specs=pl.BlockSpec((th, n), lambda r: (r, 0)),
        out_shape=jax.ShapeDtypeStruct((h, n), F32),
        compiler_params=_params(),
    )(pf, b)


def _adam_math(w, g, m, v):
    m = ADAM_B1 * m + (1.0 - ADAM_B1) * g
    v = ADAM_B2 * v + (1.0 - ADAM_B2) * (g * g)
    m_hat = m / (1.0 - ADAM_B1 ** ADAM_STEP)
    v_hat = v / (1.0 - ADAM_B2 ** ADAM_STEP)
    delta = -ADAM_LR * (m_hat / (jnp.sqrt(v_hat) + ADAM_EPS) + ADAM_WD * w)
    return delta, m, v


def _adamw_shard(w, m, v, mine, other, name):
    _, k, n = w.shape
    kh = k // 2
    th = _row_tile(kh)
    nb = kh // th

    def body(w_ref, m_ref, v_ref, a_ref, b_ref, g_out, d_out, m_out, v_out):
        g = jnp.where(pl.program_id(0) == lax.axis_index("c"), a_ref[...], b_ref[...])
        g_out[...] = g
        d_out[...], m_out[...], v_out[...] = _adam_math(w_ref[...], g, m_ref[...], v_ref[...])

    full = pl.BlockSpec((None, th, n), lambda hh, r: (0, hh * nb + r, 0))
    half = pl.BlockSpec((th, n), lambda hh, r: (r, 0))
    return pl.pallas_call(
        body,
        name=name,
        grid=(2, nb),
        in_specs=[full, full, full, half, half],
        out_specs=[full] * 4,
        out_shape=[jax.ShapeDtypeStruct((1, k, n), F32)] * 4,
        compiler_params=_params(),
    )(w, m, v, mine, other)


def _adamw_small(w, g, m, v, name):
    cols = w.shape[1]
    return _rowwise(_adam_math, [w, g, m, v], [(cols, F32)] * 3, [], w.shape[0], name)


def _full_matrices(own, gathered):
    me = 2 * lax.axis_index("x") + lax.axis_index("y")
    out = {}
    for (n, _, _, axis), w, g in zip(BIG, own, gathered):
        out[n] = jnp.concatenate([jnp.where(me == j, w, g[j]) for j in range(N_CHIPS)], axis=axis)
    return out


def _split_for_chips(full):
    out = []
    for n, r, c, axis in BIG:
        g = full[n]
        out.append(jnp.stack([g[:, j * c:(j + 1) * c] for j in range(N_CHIPS)]) if axis == 1
                   else g.reshape(N_CHIPS, r, c))
    return out


def _pack_small(vecs, extra=None):
    flat = jnp.concatenate([vecs[n].reshape(-1) for n, _ in SMALL] + ([extra.reshape(-1)] if extra is not None else []))
    flat = jnp.pad(flat, (0, SMALL_ROWS * PACK_COLS - flat.shape[0]))
    return flat.reshape(SMALL_ROWS, PACK_COLS)


def _unpack_small(pack):
    flat = pack.reshape(-1)
    out, off = {}, 0
    for n, k in SMALL:
        out[n] = flat[off:off + k].reshape(1, k)
        off += k
    return out


def _forward_backward(x, pos, target, sm, w):
    s = x.shape[0]
    w_in = w["w_in"]
    w_a = w_in[:, 0:1536]
    zeros = functools.partial(jnp.zeros, dtype=BF16)
    w_b = jnp.concatenate([w_in[:, 1536:2176], zeros((D_MODEL, 64)), w_in[:, 2176:2208], zeros((D_MODEL, 32))], axis=1)
    w_g = w_in[:, 2208:4256]
    w_uq = jnp.pad(w["w_uq"].reshape(MLA_Q_RANK, MLA_HEADS, MLA_QK), ((0, 0), (0, 0), (0, LANES - MLA_QK)))
    w_uq = w_uq.reshape(MLA_Q_RANK, MLA_HEADS * LANES)
    w_ukv = w["w_ukv"]
    w_pm = jnp.pad(w["w_proj_mla"].reshape(MLA_HEADS, HALF, D_MODEL), ((0, 0), (HALF, 0), (0, 0)))
    w_pm = w_pm.reshape(MLA_HEADS * LANES, D_MODEL)
    w_ps, w_out, w_gu, w_dn = w["w_proj_sb"], w["w_out"], w["w_gate_up"], w["w_down"]
    inv_freq = ROPE_THETA ** (-jnp.arange(0, MLA_ROPE, 2, dtype=F32) / MLA_ROPE)
    inv_lane = jnp.concatenate([jnp.zeros((HALF,), F32), inv_freq, inv_freq, jnp.zeros((32,), F32)]).reshape(1, LANES)
    g_pre, g_post, b_gate = sm["norm_mix_pre"], sm["norm_mix_post"], sm["b_gate"]
    g_q, g_kv, g_fpre, g_fpost = sm["q_norm"], sm["kv_norm"], sm["norm_ffn_pre"], sm["norm_ffn_post"]

    def f_h1(x, g):
        return (_rms(x)[1] * g,)

    (h1,) = _rowwise(f_h1, [x, g_pre], [(D_MODEL, BF16)], [], ROW_TILE, "h1")
    qkv = _mm(h1, w_a, "nn", BF16, "proj_sb")
    lat = _mm(h1, w_b, "nn", F32, "proj_lat")
    gl = _mm(h1, w_g, "nn", F32, "proj_gate")

    def f_lat(lat, gq, gkv, pos, inv):
        cqn = _rms(lat[:, 0:MLA_Q_RANK])[1] * gq
        ckvn = _rms(lat[:, MLA_Q_RANK:MLA_Q_RANK + MLA_KV_RANK])[1] * gkv
        kr = lat[:, 640:768]
        ang = pos.astype(F32) * inv
        lane = _lane(ang.shape)
        cosv, sinv = jnp.cos(ang), jnp.sin(ang)
        rope_c = jnp.where(lane < 64, 1.0, jnp.where(lane < 96, cosv, 0.0))
        rope_s = jnp.where((lane >= 64) & (lane < 80), -sinv, jnp.where((lane >= 80) & (lane < 96), sinv, 0.0))
        krr = jnp.where(lane >= 64, kr * rope_c, 0.0) + _rope_swap(kr) * rope_s
        return cqn, ckvn, krr, rope_c, rope_s

    cqn, ckvn, krr, rope_c, rope_s = _rowwise(
        f_lat, [lat, g_q, g_kv, pos, inv_lane],
        [(MLA_Q_RANK, BF16), (MLA_KV_RANK, BF16), (LANES, F32), (LANES, F32), (LANES, F32)], [], ROW_TILE, "lat_norm")
    q_raw = _mm(cqn, w_uq, "nn", F32, "q_up")
    kv = _mm(ckvn, w_ukv, "nn", F32, "kv_up")

    def f_heads(q_raw, kv, krr, rc, rs):
        lane = _lane(krr.shape)
        qs, ks, vs = [], [], []
        for h in range(MLA_HEADS):
            qh = q_raw[:, h * LANES:(h + 1) * LANES]
            kvh = kv[:, h * LANES:(h + 1) * LANES]
            qs.append(qh * rc + _rope_swap(qh) * rs)
            ks.append(jnp.where(lane < HALF, kvh, krr))
            vs.append(jnp.where(lane >= HALF, kvh, 0.0))
        return jnp.concatenate(qs, axis=1), jnp.concatenate(ks, axis=1), jnp.concatenate(vs, axis=1)

    wide = MLA_HEADS * LANES
    qc, kc, vp = _rowwise(f_heads, [q_raw, kv, krr, rope_c, rope_s], [(wide, BF16)] * 3, [], ROW_TILE, "mla_heads")
    o_sb, sb_tot, sb_first = _sb_fwd(qkv, "sb_fwd")
    o_mla, lse = _mla_fwd(qc, kc, vp, "mla_fwd")
    m_sb = _mm(o_sb, w_ps, "nn", F32, "proj_o_sb")
    m_mla = _mm(o_mla, w_pm, "nn", F32, "proj_o_mla")

    def f_merge(m_sb, m_mla, gl, b):
        g = _sigmoid(gl + b)
        return (g[:, :D_MODEL] * m_sb + g[:, D_MODEL:] * m_mla,)

    (merged,) = _rowwise(f_merge, [m_sb, m_mla, gl, b_gate], [(D_MODEL, BF16)], [], ROW_TILE, "merge")
    y = _mm(merged, w_out, "nn", F32, "mix_out")

    def f_res1(x, y, g2, g3):
        x2 = x + _rms(y)[1] * g2
        return x2, _rms(x2)[1] * g3

    x2, h2 = _rowwise(f_res1, [x, y, g_post, g_fpre], [(D_MODEL, F32), (D_MODEL, BF16)], [], ROW_TILE, "res1")

    g_ff, u_ff, act = _ffn_up(h2, w_gu, "ffn_up")
    f = _mm(act, w_dn, "nn", F32, "ffn_down")

    def f_loss(x2, f, g4, tgt):
        r, fhat = _rms(f)
        err = x2 + fhat * g4 - tgt
        loss = 0.5 * jnp.sum(jnp.sum(err * err, axis=-1, keepdims=True), axis=0, keepdims=True) / D_MODEL
        dout = err / D_MODEL
        return dout, _rms_bwd(dout, r, fhat, g4), loss, _colsum(dout * fhat)

    dout, df, loss, d_fpost = _rowwise(f_loss, [x2, f, g_fpost, target], [(D_MODEL, F32), (D_MODEL, BF16)],
                                       [1, D_MODEL], ROW_TILE, "loss")

    dg_ff, du_ff = _ffn_dact(df, w_dn, g_ff, u_ff, "d_act")
    dw_dn = _mm(act, df, "tn", F32, "dw_down")
    dh2 = _mm_nt_sum([(dg_ff, w_gu[:, :D_FF]), (du_ff, w_gu[:, D_FF:])], "d_h2")
    dw_gu = jnp.concatenate([_mm(h2, dg_ff, "tn", F32, "dw_gate"), _mm(h2, du_ff, "tn", F32, "dw_up")], axis=1)

    def f_dres1(dout, dh2, x2, y, g3, g2):
        r3, x2hat = _rms(x2)
        dx2 = dout + _rms_bwd(dh2, r3, x2hat, g3)
        r2, yhat = _rms(y)
        return dx2, _rms_bwd(dx2, r2, yhat, g2), _colsum(dh2 * x2hat), _colsum(dx2 * yhat)

    dx2, dy, d_fpre, d_post = _rowwise(f_dres1, [dout, dh2, x2, y, g_fpre, g_post],
                                       [(D_MODEL, F32), (D_MODEL, BF16)], [D_MODEL, D_MODEL], ROW_TILE, "d_res1")

    dmerged = _mm(dy, w_out, "nt", F32, "d_merged")
    dw_out = _mm(merged, dy, "tn", F32, "dw_out")

    def f_dmerge(dm, m_sb, m_mla, gl, b):
        g = _sigmoid(gl + b)
        g0, g1 = g[:, :D_MODEL], g[:, D_MODEL:]
        dgl = jnp.concatenate([dm * m_sb * g0 * (1.0 - g0), dm * m_mla * g1 * (1.0 - g1)], axis=1)
        return dm * g0, dm * g1, dgl, _colsum(dgl)

    dm_sb, dm_mla, dgl, d_bgate = _rowwise(f_dmerge, [dmerged, m_sb, m_mla, gl, b_gate],
                                           [(D_MODEL, BF16), (D_MODEL, BF16), (2 * D_MODEL, BF16)], [2 * D_MODEL],
                                           ROW_TILE, "d_merge")
    do_sb = _mm(dm_sb, w_ps, "nt", BF16, "d_o_sb")
    dw_ps = _mm(o_sb, dm_sb, "tn", F32, "dw_proj_sb")
    do_mla = _mm(dm_mla, w_pm, "nt", F32, "d_o_mla")
    dw_pm = _mm(o_mla, dm_mla, "tn", F32, "dw_proj_mla")

    dq_sb, dk_sb, dv_sb = _sb_bwd(qkv, do_sb, sb_tot, sb_first, "sb_bwd")
    dqc, dkc, dvp = _mla_bwd(qc, kc, vp, do_mla, o_mla, lse, "mla_bwd")

    def f_dheads(dqc, dkc, dvp, rc, rs):
        lane = _lane(rc.shape)
        dqs, dkvs = [], []
        dkr = jnp.zeros(rc.shape, F32)
        for h in range(MLA_HEADS):
            dq = dqc[:, h * LANES:(h + 1) * LANES]
            dk = dkc[:, h * LANES:(h + 1) * LANES]
            dqs.append(dq * rc + _rope_swap(dq * rs))
            dkvs.append(jnp.where(lane < HALF, dk, dvp[:, h * LANES:(h + 1) * LANES]))
            dkr = dkr + jnp.where(lane >= HALF, dk, 0.0)
        dkr = dkr * rc + _rope_swap(dkr * rs)
        return jnp.concatenate(dqs, axis=1), jnp.concatenate(dkvs, axis=1), dkr

    dq_raw, dkv, dkr = _rowwise(f_dheads, [dqc, dkc, dvp, rope_c, rope_s],
                                [(wide, BF16), (wide, BF16), (LANES, F32)], [], ROW_TILE, "d_mla_heads")
    dcqn = _mm(dq_raw, w_uq, "nt", F32, "d_cqn")
    dw_uq = _mm(cqn, dq_raw, "tn", F32, "dw_uq")
    dckvn = _mm(dkv, w_ukv, "nt", F32, "d_ckvn")
    dw_ukv = _mm(ckvn, dkv, "tn", F32, "dw_ukv")

    def f_dlat(lat, dcqn, dckvn, dkr, gq, gkv):
        rq, cqhat = _rms(lat[:, 0:MLA_Q_RANK])
        rkv, ckvhat = _rms(lat[:, MLA_Q_RANK:MLA_Q_RANK + MLA_KV_RANK])
        dlat = jnp.concatenate([_rms_bwd(dcqn, rq, cqhat, gq), _rms_bwd(dckvn, rkv, ckvhat, gkv), dkr], axis=1)
        return dlat, _colsum(dcqn * cqhat), _colsum(dckvn * ckvhat)

    dlat, d_gq, d_gkv = _rowwise(f_dlat, [lat, dcqn, dckvn, dkr, g_q, g_kv], [(768, BF16)],
                                 [MLA_Q_RANK, MLA_KV_RANK], ROW_TILE, "d_lat")

    sb_parts = [(dq_sb, w_a[:, 0:SB_WIDTH]), (dk_sb, w_a[:, SB_WIDTH:2 * SB_WIDTH]), (dv_sb, w_a[:, 2 * SB_WIDTH:])]
    dh1 = _mm_nt_sum(sb_parts + [(dlat, w_b), (dgl, w_g)], "d_h1")
    dw_sb = [_mm(h1, d, "tn", F32, "dw_in_sb_" + tag) for tag, (d, _) in zip("qkv", sb_parts)]
    dw_b = _mm(h1, dlat, "tn", F32, "dw_in_lat")
    dw_g = _mm(h1, dgl, "tn", F32, "dw_in_gate")

    def f_dx(x, dx2, dh1, g1):
        r, xhat = _rms(x)
        return dx2 + _rms_bwd(dh1, r, xhat, g1), _colsum(dh1 * xhat)

    grad_x, d_pre = _rowwise(f_dx, [x, dx2, dh1, g_pre], [(D_MODEL, F32)], [D_MODEL], ROW_TILE, "d_x")

    small = {"norm_mix_pre": d_pre, "norm_mix_post": d_post, "b_gate": d_bgate, "q_norm": d_gq, "kv_norm": d_gkv,
             "norm_ffn_pre": d_fpre, "norm_ffn_post": d_fpost}
    big = {
        "w_in": jnp.concatenate(dw_sb + [dw_b[:, 0:640], dw_b[:, 704:736], dw_g], axis=1),
        "w_uq": dw_uq.reshape(MLA_Q_RANK, MLA_HEADS, LANES)[:, :, :MLA_QK].reshape(MLA_Q_RANK, MLA_HEADS * MLA_QK),
        "w_ukv": dw_ukv,
        "w_proj_sb": dw_ps,
        "w_proj_mla": dw_pm.reshape(MLA_HEADS, LANES, D_MODEL)[:, HALF:, :].reshape(MLA_HEADS * HALF, D_MODEL),
        "w_out": dw_out,
        "w_gate_up": dw_gu,
        "w_down": dw_dn,
    }
    return loss, grad_x, small, big


def kernel(x, positions, norm_mix_pre, norm_mix_post, w_in, b_gate, q_norm, w_uq, kv_norm, w_ukv, w_proj_sb, w_proj_mla, w_out, norm_ffn_pre, norm_ffn_post, w_gate_up, w_down, loss_target, m_norm_mix_pre, m_norm_mix_post, m_w_in, m_b_gate, m_q_norm, m_w_uq, m_kv_norm, m_w_ukv, m_w_proj_sb, m_w_proj_mla, m_w_out, m_norm_ffn_pre, m_norm_ffn_post, m_w_gate_up, m_w_down, v_norm_mix_pre, v_norm_mix_post, v_w_in, v_b_gate, v_q_norm, v_w_uq, v_kv_norm, v_w_ukv, v_w_proj_sb, v_w_proj_mla, v_w_out, v_norm_ffn_pre, v_norm_ffn_post, v_w_gate_up, v_w_down):
    given = dict(locals())
    s = x.shape[1]
    names = [n for n, _, _, _ in BIG]
    small_w = {n: given[n] for n, _ in SMALL}

    own = [given[n][0].astype(BF16) for n in names]
    gathered = _gather_weights(own, "gather_weights")
    loss, grad_x, d_small, d_big = _forward_backward(
        x.reshape(s, D_MODEL), positions.reshape(s, 1), loss_target.reshape(s, D_MODEL), small_w,
        _full_matrices(own, gathered))

    pieces = _split_for_chips(d_big)
    from_sibling = _swap_halves(pieces, "grad_swap_halves")
    partial = [_add_cores(g, a, "grad_add_cores_" + n) for n, g, a in zip(names, pieces, from_sibling)]
    received = _scatter_chips([b for _, b in partial], "grad_scatter_chips")
    mine = [_add_chips(f, r, "grad_add_chips_" + n) for n, (f, _), r in zip(names, partial, received)]
    other = _to_sibling(mine, "grad_to_sibling")
    small_sum = _allsum_small(_pack_small(d_small, loss), "small_allsum")

    grads, deltas, new_m, new_v = {}, {}, {}, {}
    for n, a, b in zip(names, mine, other):
        grads[n], deltas[n], new_m[n], new_v[n] = _adamw_shard(given[n], given["m_" + n], given["v_" + n], a, b,
                                                               "adamw_" + n)
    delta_s, m_s, v_s = _adamw_small(_pack_small(small_w), small_sum,
                                     _pack_small({n: given["m_" + n] for n in small_w}),
                                     _pack_small({n: given["v_" + n] for n in small_w}), "adamw_small")

    loss_out = small_sum.reshape(-1)[SMALL_ELEMS]
    grads.update(_unpack_small(small_sum))
    deltas.update(_unpack_small(delta_s))
    new_m.update(_unpack_small(m_s))
    new_v.update(_unpack_small(v_s))
    return (loss_out, grad_x.reshape(1, s, D_MODEL), *[grads[n] for n in WEIGHT_ORDER],
            *[deltas[n] for n in WEIGHT_ORDER], *[new_m[n] for n in WEIGHT_ORDER], *[new_v[n] for n in WEIGHT_ORDER])
```

```python
import functools
import math

import jax
import jax.numpy as jnp
from jax import lax
from jax.experimental import pallas as pl
from jax.experimental.pallas import tpu as pltpu

F32 = jnp.float32
BF16 = jnp.bfloat16
MESH = pl.DeviceIdType.MESH

D_MODEL = 1024
SB_WIDTH = 512
MLA_HEADS = 8
MLA_Q_RANK = 384
MLA_KV_RANK = 256
MLA_ROPE = 32
MLA_QK = 96
D_FF = 2816
ROPE_THETA = 10000.0
EPS = 1e-6
LANES = 128
HALF = 64
ATT_TILE = 256
ROW_TILE = 512
N_CHIPS = 4

ADAM_LR = 0.001
ADAM_B1 = 0.9
ADAM_B2 = 0.999
ADAM_EPS = 1e-08
ADAM_WD = 0.01
ADAM_STEP = 10

VMEM_LIMIT = 56 * 1024 * 1024

BIG = (
    ("w_in", 1024, 1064, 1),
    ("w_uq", 384, 192, 1),
    ("w_ukv", 256, 256, 1),
    ("w_proj_sb", 512, 256, 1),
    ("w_proj_mla", 512, 256, 1),
    ("w_out", 256, 1024, 0),
    ("w_gate_up", 1024, 1408, 1),
    ("w_down", 704, 1024, 0),
)
PACK_COLS = 1024
SMALL = (
    ("norm_mix_pre", 1024),
    ("norm_mix_post", 1024),
    ("b_gate", 2048),
    ("q_norm", 384),
    ("kv_norm", 256),
    ("norm_ffn_pre", 1024),
    ("norm_ffn_post", 1024),
)
SMALL_ELEMS = sum(n for _, n in SMALL)
SMALL_ROWS = 8
WEIGHT_ORDER = ("norm_mix_pre", "norm_mix_post", "w_in", "b_gate", "q_norm", "w_uq", "kv_norm", "w_ukv",
                "w_proj_sb", "w_proj_mla", "w_out", "norm_ffn_pre", "norm_ffn_post", "w_gate_up", "w_down")


def _params():
    return pltpu.CompilerParams(vmem_limit_bytes=VMEM_LIMIT)


def _tile(dim, pref):
    t = (min(pref, dim) // LANES) * LANES
    while t >= LANES:
        if dim % t == 0:
            return t
        t -= LANES
    return dim


def _mm(a, b, mode, out_dtype, name, tm=1408, tn=1408, tk=1408):
    if mode == "nn":
        (m, k), n = a.shape, b.shape[1]
    elif mode == "nt":
        (m, k), n = a.shape, b.shape[0]
    else:
        (k, m), n = a.shape, b.shape[1]
    tm, tn, tk = _tile(m, tm), _tile(n, tn), _tile(k, tk)
    nk = k // tk
    if mode == "tn":
        a_spec = pl.BlockSpec((tk, tm), lambda i, j, l: (l, i))
        lhs_dim = 0
    else:
        a_spec = pl.BlockSpec((tm, tk), lambda i, j, l: (i, l))
        lhs_dim = 1
    if mode == "nt":
        b_spec = pl.BlockSpec((tn, tk), lambda i, j, l: (j, l))
        rhs_dim = 1
    else:
        b_spec = pl.BlockSpec((tk, tn), lambda i, j, l: (l, j))
        rhs_dim = 0
    dims = (((lhs_dim,), (rhs_dim,)), ((), ()))

    def body(a_ref, b_ref, o_ref, acc_ref):
        l = pl.program_id(2)

        @pl.when(l == 0)
        def _():
            acc_ref[...] = jnp.zeros_like(acc_ref)

        acc_ref[...] += lax.dot_general(a_ref[...].astype(BF16), b_ref[...].astype(BF16), dims,
                                        preferred_element_type=F32)

        @pl.when(l == nk - 1)
        def _():
            o_ref[...] = acc_ref[...].astype(o_ref.dtype)

    return pl.pallas_call(
        body,
        name=name,
        grid=(m // tm, n // tn, nk),
        in_specs=[a_spec, b_spec],
        out_specs=pl.BlockSpec((tm, tn), lambda i, j, l: (i, j)),
        out_shape=jax.ShapeDtypeStruct((m, n), out_dtype),
        scratch_shapes=[pltpu.VMEM((tm, tn), F32)],
        compiler_params=_params(),
    )(a, b)


def _mm_nt_sum(pairs, name, tm=512, tn=1024):
    m, n = pairs[0][0].shape[0], pairs[0][1].shape[0]
    tm, tn = _tile(m, tm), _tile(n, tn)

    def body(*refs):
        total = None
        for p in range(len(pairs)):
            part = _dot_nt(refs[2 * p][...].astype(BF16), refs[2 * p + 1][...].astype(BF16))
            total = part if total is None else total + part
        refs[-1][...] = total

    in_specs = []
    for a, b in pairs:
        in_specs += [pl.BlockSpec((tm, a.shape[1]), lambda i, j: (i, 0)),
                     pl.BlockSpec((tn, b.shape[1]), lambda i, j: (j, 0))]
    return pl.pallas_call(
        body,
        name=name,
        grid=(m // tm, n // tn),
        in_specs=in_specs,
        out_specs=pl.BlockSpec((tm, tn), lambda i, j: (i, j)),
        out_shape=jax.ShapeDtypeStruct((m, n), F32),
        compiler_params=_params(),
    )(*[r for pair in pairs for r in pair])


def _ffn_up(h, w_gu, name):
    s, k = h.shape
    tm, tn = _tile(s, 512), _tile(D_FF, 1408)
    nb = D_FF // tn

    def body(h_ref, wg_ref, wu_ref, g_ref, u_ref, act_ref):
        hv = h_ref[...]
        g = jnp.dot(hv, wg_ref[...], preferred_element_type=F32)
        u = jnp.dot(hv, wu_ref[...], preferred_element_type=F32)
        g_ref[...] = g
        u_ref[...] = u
        act_ref[...] = (g * _sigmoid(g) * u).astype(BF16)

    tile = pl.BlockSpec((tm, tn), lambda j, i: (i, j))
    out = jax.ShapeDtypeStruct((s, D_FF), F32)
    return pl.pallas_call(
        body,
        name=name,
        grid=(nb, s // tm),
        in_specs=[pl.BlockSpec((tm, k), lambda j, i: (i, 0)), pl.BlockSpec((k, tn), lambda j, i: (0, j)),
                  pl.BlockSpec((k, tn), lambda j, i: (0, nb + j))],
        out_specs=[tile, tile, tile],
        out_shape=[out, out, jax.ShapeDtypeStruct((s, D_FF), BF16)],
        compiler_params=_params(),
    )(h, w_gu, w_gu)


def _ffn_dact(df, w_dn, g, u, name):
    s, k = df.shape
    tm, tn = _tile(s, 512), _tile(D_FF, 1408)

    def body(df_ref, w_ref, g_ref, u_ref, dg_ref, du_ref):
        dact = _dot_nt(df_ref[...], w_ref[...])
        gv, uv = g_ref[...], u_ref[...]
        sg = _sigmoid(gv)
        dg_ref[...] = (dact * uv * (sg * (1.0 + gv * (1.0 - sg)))).astype(BF16)
        du_ref[...] = (dact * (gv * sg)).astype(BF16)

    tile = pl.BlockSpec((tm, tn), lambda j, i: (i, j))
    out = jax.ShapeDtypeStruct((s, D_FF), BF16)
    return pl.pallas_call(
        body,
        name=name,
        grid=(D_FF // tn, s // tm),
        in_specs=[pl.BlockSpec((tm, k), lambda j, i: (i, 0)), pl.BlockSpec((tn, k), lambda j, i: (j, 0)), tile, tile],
        out_specs=[tile, tile],
        out_shape=[out, out],
        compiler_params=_params(),
    )(df, w_dn, g, u)


def _rowwise(fn, ins, outs, reds, tm, name):
    rows = ins[0].shape[0]
    tm = min(tm, rows)
    n_in, n_out = len(ins), len(outs)

    def spec(shape):
        if shape[0] == rows:
            return pl.BlockSpec((tm, shape[1]), lambda i: (i, 0))
        return pl.BlockSpec(shape, lambda i: (0, 0))

    def body(*refs):
        i = pl.program_id(0)
        vals = fn(*[r[...] for r in refs[:n_in]])
        for r, v in zip(refs[n_in:n_in + n_out], vals[:n_out]):
            r[...] = v.astype(r.dtype)
        for r, v in zip(refs[n_in + n_out:], vals[n_out:]):
            @pl.when(i == 0)
            def _():
                r[...] = jnp.zeros_like(r)

            r[...] += v

    out_shape = [jax.ShapeDtypeStruct((rows, d), dt) for d, dt in outs]
    out_shape += [jax.ShapeDtypeStruct((1, d), F32) for d in reds]
    out_specs = [spec(s.shape) for s in out_shape]
    return pl.pallas_call(
        body,
        name=name,
        grid=(rows // tm,),
        in_specs=[spec(a.shape) for a in ins],
        out_specs=out_specs,
        out_shape=out_shape,
        compiler_params=_params(),
    )(*ins)


def _colsum(v):
    return jnp.sum(v, axis=0, keepdims=True)


def _rms(v):
    r = lax.rsqrt(jnp.mean(v * v, axis=-1, keepdims=True) + EPS)
    return r, v * r


def _rms_bwd(dy, r, vhat, g):
    u = dy * g
    return r * (u - vhat * jnp.mean(u * vhat, axis=-1, keepdims=True))


def _sigmoid(t):
    return 1.0 / (1.0 + jnp.exp(-t))


def _lane(shape):
    return lax.broadcasted_iota(jnp.int32, shape, len(shape) - 1)


def _rope_swap(v):
    lane = _lane(v.shape)
    up = pltpu.roll(v, LANES - 16, 1)
    down = pltpu.roll(v, 16, 1)
    return jnp.where((lane >= 64) & (lane < 80), up, jnp.where((lane >= 80) & (lane < 96), down, 0.0))


SB_DEAD = -120.0


def _softplus(z):
    return jnp.maximum(z, 0.0) + jnp.log(1.0 + jnp.exp(-jnp.abs(z)))


def _split_dot(v, m2):
    hi = v.astype(BF16)
    lo = (v - hi.astype(F32)).astype(BF16)
    return jnp.dot(jnp.concatenate([hi, lo], axis=1), m2, preferred_element_type=F32)


def _dot_nt(a, b):
    return lax.dot_general(a, b, (((1,), (1,)), ((), ())), preferred_element_type=F32)


def _dot_tn(a, b):
    return lax.dot_general(a, b, (((0,), (0,)), ((), ())), preferred_element_type=F32)


def _sb_fwd(qkv, name):
    s = qkv.shape[0]
    t = min(ATT_TILE, s // 2)
    npair = SB_WIDTH // LANES

    def body(q_ref, k_ref, v_ref, o_ref, t_ref, first_ref, c_s, acc_s):
        i = pl.program_id(1)
        lane = _lane((1, LANES))
        row = lax.broadcasted_iota(jnp.int32, (t, t), 0)
        col = lax.broadcasted_iota(jnp.int32, (t, t), 1)
        tri = col < row
        m_from = jnp.where(row >= col, 1.0, 0.0).astype(BF16)
        m_from = jnp.concatenate([m_from, m_from], axis=0)
        masks = (lane < HALF, lane >= HALF)
        qhs = [jnp.where(hm, q_ref[n * t:(n + 1) * t, :], jnp.zeros((t, LANES), BF16)) * 0.125
               for n in range(2) for hm in masks]

        c_s[...] = jnp.zeros_like(c_s)
        acc_s[...] = jnp.zeros_like(acc_s)

        def step(k, diag, chains):
            blocks = {}
            for n in sorted({ch // 2 for ch in chains}):
                rows = pl.ds(pl.multiple_of((2 * i + n - k) * t, t), t)
                blocks[n] = (k_ref[rows, :], v_ref[rows, :])
            zs = {ch: _dot_nt(qhs[ch], blocks[ch // 2][0]) for ch in chains}
            lbs = {}
            for ch in chains:
                lb = -_softplus(zs[ch])
                lbs[ch] = jnp.where(tri, lb, 0.0) if diag else lb
            sums = {ch: _split_dot(lbs[ch], m_from) for ch in chains}
            for ch in chains:
                c = c_s[ch]
                a = jnp.exp(zs[ch] + sums[ch] + jnp.concatenate([c] * (t // LANES), axis=1))
                if diag:
                    a = jnp.where(tri, a, 0.0)
                vs = blocks[ch // 2][1]
                vh = jnp.where(masks[ch % 2], vs, jnp.zeros_like(vs))
                acc_s[ch] += jnp.dot(a.astype(BF16), vh, preferred_element_type=F32)
                c_s[ch] = c + sums[ch][:, 0:1]

        def live():
            top = jnp.max(c_s[0])
            for ch in range(1, 4):
                top = jnp.maximum(top, jnp.max(c_s[ch]))
            return top

        step(0, True, (0, 1, 2, 3))
        last = 2 * i

        def more(state):
            k, top = state
            return (k <= last) & (top > SB_DEAD)

        def walk(state):
            k, _ = state
            step(k, False, (0, 1, 2, 3))
            return k + 1, live()

        k_end, top = lax.while_loop(more, walk, (jnp.int32(1), live()))
        tail = (k_end == last + 1) & (top > SB_DEAD)

        @pl.when(tail)
        def _():
            step(last + 1, False, (2, 3))

        first_ref[0, 0] = jnp.full((8, LANES), (k_end - 1).astype(F32) + jnp.where(tail, 0.5, 0.0))
        for n in range(2):
            o_ref[n * t:(n + 1) * t, :] = acc_s[2 * n] + acc_s[2 * n + 1]
            t_ref[n * t:(n + 1) * t, :] = jnp.where(masks[0], c_s[2 * n], c_s[2 * n + 1])

    blk = pl.BlockSpec((2 * t, LANES), lambda p, i: (i, p))
    steps = s // (2 * t)
    return pl.pallas_call(
        body,
        name=name,
        grid=(npair, steps),
        in_specs=[
            blk,
            pl.BlockSpec((s, LANES), lambda p, i: (0, npair + p)),
            pl.BlockSpec((s, LANES), lambda p, i: (0, 2 * npair + p)),
        ],
        out_specs=[blk, blk, pl.BlockSpec((1, 1, 8, LANES), lambda p, i: (p, i, 0, 0))],
        out_shape=[jax.ShapeDtypeStruct((s, SB_WIDTH), F32), jax.ShapeDtypeStruct((s, SB_WIDTH), F32),
                   jax.ShapeDtypeStruct((npair, steps, 8, LANES), F32)],
        scratch_shapes=[pltpu.VMEM((4, t, LANES), F32)] * 2,
        compiler_params=_params(),
    )(qkv, qkv, qkv)


def _sb_bwd(qkv, do, tot, first, name):
    s = qkv.shape[0]
    t = min(ATT_TILE, s // 2)
    npair = SB_WIDTH // LANES

    def body(q_ref, k_ref, v_ref, do_ref, t_ref, first_ref, dq_ref, dk_ref, dv_ref, bef_s, ebef_s):
        i = pl.program_id(1)

        @pl.when(i == 0)
        def _():
            dk_ref[...] = jnp.zeros_like(dk_ref)
            dv_ref[...] = jnp.zeros_like(dv_ref)

        lane = _lane((1, LANES))
        row = lax.broadcasted_iota(jnp.int32, (t, t), 0)
        col = lax.broadcasted_iota(jnp.int32, (t, t), 1)
        tri = col < row
        m_from = jnp.where(row >= col, 1.0, 0.0).astype(BF16)
        m_from = jnp.concatenate([m_from, m_from], axis=0)
        m_upto = jnp.where(row <= col, 1.0, 0.0).astype(BF16)
        m_upto = jnp.concatenate([m_upto, m_upto], axis=0)
        masks = (lane < HALF, lane >= HALF)
        zero_b = jnp.zeros((t, LANES), BF16)
        qhs, dohs, tots = [], [], []
        for n in range(2):
            rs = slice(n * t, (n + 1) * t)
            do_b = do_ref[rs, :].astype(BF16)
            for h, hm in enumerate(masks):
                qhs.append(jnp.where(hm, q_ref[rs, :], zero_b))
                dohs.append(jnp.where(hm, do_b, zero_b))
                tots.append(t_ref[rs, h * HALF:h * HALF + 1])

        bef_s[...] = jnp.zeros_like(bef_s)
        ebef_s[...] = jnp.zeros_like(ebef_s)
        dq_ref[...] = jnp.zeros_like(dq_ref)
        reps = t // LANES

        def step(k, diag, chains):
            qblocks = sorted({ch // 2 for ch in chains})
            rows, blocks = {}, {}
            for n in qblocks:
                rows[n] = pl.ds(pl.multiple_of((2 * i + n - k) * t, t), t)
                blocks[n] = (k_ref[rows[n], :], v_ref[rows[n], :])
            zs = {ch: _dot_nt(qhs[ch] * 0.125, blocks[ch // 2][0]) for ch in chains}
            das = {ch: _dot_nt(dohs[ch], blocks[ch // 2][1]) for ch in chains}
            lbs, betas = {}, {}
            for ch in chains:
                sp = _softplus(zs[ch])
                betas[ch] = jnp.exp(zs[ch] - sp)
                lbs[ch] = jnp.where(tri, -sp, 0.0) if diag else -sp
            sums = {ch: _split_dot(lbs[ch], m_from) for ch in chains}
            avs, es = {}, {}
            for ch in chains:
                before = bef_s[ch]
                total = sums[ch][:, 0:1]
                a = jnp.exp(zs[ch] + sums[ch] + jnp.concatenate([tots[ch] - before - total] * reps, axis=1))
                if diag:
                    a = jnp.where(tri, a, 0.0)
                avs[ch] = a.astype(BF16)
                es[ch] = a * das[ch]
                bef_s[ch] = before + total
            e_sums = {ch: _split_dot(es[ch], m_upto) for ch in chains}
            dzs = {}
            for ch in chains:
                e_upto = e_sums[ch] + jnp.concatenate([ebef_s[ch]] * reps, axis=1)
                dz = es[ch] - betas[ch] * e_upto
                if diag:
                    dz = jnp.where(tri, dz, 0.0)
                dzs[ch] = (dz * 0.125).astype(BF16)
                ebef_s[ch] = jnp.broadcast_to(e_upto[:, t - 1:t], (t, LANES))
            for n in qblocks:
                ks = blocks[n][0]
                mine = [ch for ch in chains if ch // 2 == n]
                dq = sum(jnp.dot(dzs[ch], jnp.where(masks[ch % 2], ks, jnp.zeros_like(ks)),
                                 preferred_element_type=F32) for ch in mine)
                dq_ref[n * t:(n + 1) * t, :] += dq
                dk_ref[rows[n], :] += sum(_dot_tn(dzs[ch], qhs[ch]) for ch in mine)
                dv_ref[rows[n], :] += sum(_dot_tn(avs[ch], dohs[ch]) for ch in mine)

        walked = jnp.max(first_ref[0, 0])
        full_steps = jnp.clip(walked.astype(jnp.int32), 0, 2 * i)

        @pl.when(walked - full_steps.astype(F32) > 0.25)
        def _():
            step(2 * i + 1, False, (2, 3))

        def walk(n, carry):
            step(full_steps - n, False, (0, 1, 2, 3))
            return carry

        lax.fori_loop(0, full_steps, walk, 0)
        step(0, True, (0, 1, 2, 3))

    blk = pl.BlockSpec((2 * t, LANES), lambda p, i: (i, p))
    full = pl.BlockSpec((s, LANES), lambda p, i: (0, p))
    out = jax.ShapeDtypeStruct((s, SB_WIDTH), F32)
    return pl.pallas_call(
        body,
        name=name,
        grid=(npair, s // (2 * t)),
        in_specs=[
            blk,
            pl.BlockSpec((s, LANES), lambda p, i: (0, npair + p)),
            pl.BlockSpec((s, LANES), lambda p, i: (0, 2 * npair + p)),
            blk,
            blk,
            pl.BlockSpec((1, 1, 8, LANES), lambda p, i: (p, i, 0, 0)),
        ],
        out_specs=[blk, full, full],
        out_shape=[out, out, out],
        scratch_shapes=[pltpu.VMEM((4, t, LANES), F32)] * 2,
        compiler_params=_params(),
    )(qkv, qkv, qkv, do, tot, first)


MLA_SCALE = 1.0 / math.sqrt(MLA_QK)
MLA_SCALE_LOG2E = MLA_SCALE * math.log2(math.e)
NEG = -1e30
MLA_QBLOCKS = 8


def _diag_plan(nq):
    return [tuple(None if n < j else n == j for n in range(nq)) for j in range(nq)]


def _mla_up(cqn, w_uq, ckvn, w_ukv, krr, rope_c, rope_s, name):
    s = cqn.shape[0]
    tm = _tile(s, 512)
    wide = MLA_HEADS * LANES

    def body(cq_ref, wq_ref, ckv_ref, wkv_ref, krr_ref, rc_ref, rs_ref, q_out, k_out, v_out):
        q = jnp.dot(cq_ref[...], wq_ref[...], preferred_element_type=F32)
        kv = jnp.dot(ckv_ref[...], wkv_ref[...], preferred_element_type=F32)
        krr, rc, rs = krr_ref[...], rc_ref[...], rs_ref[...]
        lane = _lane(krr.shape)
        for h in range(MLA_HEADS):
            cols = slice(h * LANES, (h + 1) * LANES)
            qh, kvh = q[:, cols], kv[:, cols]
            q_out[:, cols] = (qh * rc + _rope_swap(qh) * rs).astype(BF16)
            k_out[:, cols] = jnp.where(lane < HALF, kvh, krr).astype(BF16)
            v_out[:, cols] = jnp.where(lane >= HALF, kvh, 0.0).astype(BF16)

    def rows(a):
        return pl.BlockSpec((tm, a.shape[1]), lambda i: (i, 0))

    def whole(a):
        return pl.BlockSpec(a.shape, lambda i: (0, 0))

    out = jax.ShapeDtypeStruct((s, wide), BF16)
    return pl.pallas_call(
        body,
        name=name,
        grid=(s // tm,),
        in_specs=[rows(cqn), whole(w_uq), rows(ckvn), whole(w_ukv), rows(krr), rows(rope_c), rows(rope_s)],
        out_specs=[pl.BlockSpec((tm, wide), lambda i: (i, 0))] * 3,
        out_shape=[out, out, out],
        compiler_params=_params(),
    )(cqn, w_uq, ckvn, w_ukv, krr, rope_c, rope_s)


def _mla_fwd(qc, kc, vp, name):
    s = qc.shape[0]
    nq = MLA_QBLOCKS
    t = min(ATT_TILE, s // nq)

    def body(q_ref, k_ref, v_ref, o_ref, lse_ref, m_s, l_s, acc_s):
        i = pl.program_id(1)
        row = lax.broadcasted_iota(jnp.int32, (t, t), 0)
        col = lax.broadcasted_iota(jnp.int32, (t, t), 1)
        keep = col <= row
        m_s[...] = jnp.full(m_s.shape, NEG, F32)
        l_s[...] = jnp.zeros_like(l_s)
        acc_s[...] = jnp.zeros_like(acc_s)

        def step(kb, diags):
            rows = pl.ds(pl.multiple_of(kb * t, t), t)
            ks = k_ref[rows, :]
            vs = v_ref[rows, :]
            act = [n for n in range(nq) if diags[n] is not None]
            scs = {n: _dot_nt(q_ref[n * t:(n + 1) * t, :], ks) for n in act}
            mid = {}
            for n in act:
                sc = jnp.where(keep, scs[n], NEG) if diags[n] else scs[n]
                m = m_s[n]
                m_new = jnp.maximum(m, jnp.max(sc, axis=-1, keepdims=True))
                alpha = jnp.exp2((m - m_new) * MLA_SCALE_LOG2E)
                p = jnp.exp2((sc - jnp.concatenate([m_new] * (t // LANES), axis=1)) * MLA_SCALE_LOG2E)
                m_s[n] = m_new
                l_s[n] = alpha * l_s[n] + jnp.sum(p, axis=-1, keepdims=True)
                mid[n] = (alpha, p.astype(BF16))
            for n in act:
                alpha, p = mid[n]
                acc_s[n] = alpha * acc_s[n] + jnp.dot(p, vs, preferred_element_type=F32)

        def walk(kb, carry):
            step(kb, (False,) * nq)
            return carry

        lax.fori_loop(0, nq * i, walk, 0)
        for j, diags in enumerate(_diag_plan(nq)):
            step(nq * i + j, diags)
        for n in range(nq):
            l = l_s[n]
            o_ref[n * t:(n + 1) * t, :] = acc_s[n] / l
            lse_ref[n * t:(n + 1) * t, :] = m_s[n] * MLA_SCALE + jnp.log(l)

    blk = pl.BlockSpec((nq * t, LANES), lambda h, i: (i, h))
    full = pl.BlockSpec((s, LANES), lambda h, i: (0, h))
    out = jax.ShapeDtypeStruct((s, MLA_HEADS * LANES), F32)
    return pl.pallas_call(
        body,
        name=name,
        grid=(MLA_HEADS, s // (nq * t)),
        in_specs=[blk, full, full],
        out_specs=[blk, blk],
        out_shape=[out, out],
        scratch_shapes=[pltpu.VMEM((nq, t, LANES), F32)] * 3,
        compiler_params=_params(),
    )(qc, kc, vp)


def _mla_bwd(qc, kc, vp, do, o, lse, name):
    s = qc.shape[0]
    nq = MLA_QBLOCKS
    t = min(ATT_TILE, s // nq)

    def body(q_ref, k_ref, v_ref, do_ref, o_ref, lse_ref, dq_ref, dk_ref, dv_ref):
        i = pl.program_id(1)

        @pl.when(i == 0)
        def _():
            dk_ref[...] = jnp.zeros_like(dk_ref)
            dv_ref[...] = jnp.zeros_like(dv_ref)

        row = lax.broadcasted_iota(jnp.int32, (t, t), 0)
        col = lax.broadcasted_iota(jnp.int32, (t, t), 1)
        keep = col <= row

        def block(n):
            rs = slice(n * t, (n + 1) * t)
            do_f = do_ref[rs, :]
            delta = jnp.sum(do_f * o_ref[rs, :], axis=-1, keepdims=True)
            return q_ref[rs, :], do_f.astype(BF16), delta * MLA_SCALE, lse_ref[rs, 0:1] * math.log2(math.e)

        blocks = [block(n) for n in range(nq)]
        dq_ref[...] = jnp.zeros_like(dq_ref)

        def step(kb, diags):
            rows = pl.ds(pl.multiple_of(kb * t, t), t)
            ks = k_ref[rows, :]
            vs = v_ref[rows, :]
            act = [n for n in range(nq) if diags[n] is not None]
            scs = {n: _dot_nt(blocks[n][0], ks) for n in act}
            dps = {n: _dot_nt(blocks[n][1], vs) for n in act}
            ps, dss = {}, {}
            for n in act:
                _, _, delta_s, lse2 = blocks[n]
                p = jnp.exp2(scs[n] * MLA_SCALE_LOG2E - lse2)
                if diags[n]:
                    p = jnp.where(keep, p, 0.0)
                dss[n] = (p * (dps[n] * MLA_SCALE - delta_s)).astype(BF16)
                ps[n] = p.astype(BF16)
            dk = jnp.zeros((t, LANES), F32)
            dv = jnp.zeros((t, LANES), F32)
            for n in act:
                q, do_b, _, _ = blocks[n]
                dq_ref[n * t:(n + 1) * t, :] += jnp.dot(dss[n], ks, preferred_element_type=F32)
                dk = dk + _dot_tn(dss[n], q)
                dv = dv + _dot_tn(ps[n], do_b)
            dk_ref[rows, :] += dk
            dv_ref[rows, :] += dv

        def walk(kb, carry):
            step(kb, (False,) * nq)
            return carry

        lax.fori_loop(0, nq * i, walk, 0)
        for j, diags in enumerate(_diag_plan(nq)):
            step(nq * i + j, diags)

    blk = pl.BlockSpec((nq * t, LANES), lambda h, i: (i, h))
    full = pl.BlockSpec((s, LANES), lambda h, i: (0, h))
    out = jax.ShapeDtypeStruct((s, MLA_HEADS * LANES), F32)
    return pl.pallas_call(
        body,
        name=name,
        grid=(MLA_HEADS, s // (nq * t)),
        in_specs=[blk, full, full, blk, blk, blk],
        out_specs=[blk, full, full],
        out_shape=[out, out, out],
        compiler_params=_params(),
    )(qc, kc, vp, do, o, lse)


def _position():
    return lax.axis_index("x"), lax.axis_index("y"), lax.axis_index("c")


def _other_chips(x, y):
    return [(1 - x, y), (x, 1 - y), (1 - x, 1 - y)]


HBM_SPEC = pl.BlockSpec(memory_space=pltpu.HBM)


def _remote(src, dst, send_sem, recv_sem, device):
    return pltpu.make_async_remote_copy(src_ref=src, dst_ref=dst, send_sem=send_sem, recv_sem=recv_sem,
                                        device_id=device, device_id_type=MESH)


def _dma_sems(n, count):
    return [pltpu.SemaphoreType.DMA((n,)) for _ in range(count)]


def _gather_weights(ws, name):
    na = len(ws)

    def body(*refs):
        srcs, outs = refs[:na], refs[na:2 * na]
        s1, r1, s2, r2, s3, r3, s4, r4 = refs[2 * na:]
        x, y, c = _position()
        me = 2 * x + y
        px, py = jnp.where(c == 1, 1 - x, x), jnp.where(c == 1, y, 1 - y)
        qx, qy = jnp.where(c == 1, x, 1 - x), jnp.where(c == 1, 1 - y, y)
        p_idx, q_idx, d_idx = 2 * px + py, 2 * qx + qy, 2 * (1 - x) + (1 - y)
        partner, across, sibling = (px, py, c), (qx, qy, c), (x, y, 1 - c)
        my_half, other_half = 1 - c, c

        def half(i, slot, h):
            kh = ws[i].shape[0] // 2
            return outs[i].at[slot, pl.ds(h * kh, kh), :]

        sends = [_remote(srcs[i], outs[i].at[me], s1.at[i], r1.at[i], partner) for i in range(na)]
        for cp in sends:
            cp.start()
        for i in range(na):
            _remote(srcs[i], outs[i].at[p_idx], s1.at[i], r1.at[i], partner).wait_recv()
            onward = [_remote(half(i, p_idx, my_half), half(i, p_idx, my_half), s2.at[i], r2.at[i], across),
                      _remote(outs[i].at[p_idx], outs[i].at[p_idx], s3.at[i], r3.at[i], sibling)]
            for cp in onward:
                cp.start()
            sends += onward
        for i in range(na):
            _remote(half(i, d_idx, my_half), half(i, d_idx, my_half), s2.at[i], r2.at[i], across).wait_recv()
            cp = _remote(half(i, d_idx, my_half), half(i, d_idx, my_half), s4.at[i], r4.at[i], sibling)
            cp.start()
            sends.append(cp)
        for i in range(na):
            _remote(outs[i].at[q_idx], outs[i].at[q_idx], s3.at[i], r3.at[i], sibling).wait_recv()
            _remote(half(i, d_idx, other_half), half(i, d_idx, other_half), s4.at[i], r4.at[i], sibling).wait_recv()
        for cp in sends:
            cp.wait_send()

    return pl.pallas_call(
        body,
        name=name,
        in_specs=[HBM_SPEC] * na,
        out_specs=[HBM_SPEC] * na,
        out_shape=[jax.ShapeDtypeStruct((N_CHIPS,) + w.shape, w.dtype) for w in ws],
        scratch_shapes=_dma_sems(na, 8),
    )(*ws)


def _swap_halves(gs, name):
    na = len(gs)

    def body(*refs):
        srcs, outs, send_sems, recv_sems = refs[:na], refs[na:2 * na], refs[2 * na], refs[2 * na + 1]
        x, y, c = _position()
        cps = []
        for i in range(na):
            kh = gs[i].shape[1] // 2
            cps.append(_remote(srcs[i].at[:, pl.ds((1 - c) * kh, kh), :], outs[i], send_sems.at[i], recv_sems.at[i],
                               (x, y, 1 - c)))
            cps[-1].start()
        for cp in cps:
            cp.wait()

    return pl.pallas_call(
        body,
        name=name,
        in_specs=[HBM_SPEC] * na,
        out_specs=[HBM_SPEC] * na,
        out_shape=[jax.ShapeDtypeStruct((g.shape[0], g.shape[1] // 2, g.shape[2]), g.dtype) for g in gs],
        scratch_shapes=_dma_sems(na, 2),
    )(*gs)


def _scatter_chips(parts, name):
    na = len(parts)

    def body(*refs):
        srcs, outs, send_sems, recv_sems = refs[:na], refs[na:2 * na], refs[2 * na], refs[2 * na + 1]
        x, y, c = _position()
        me = 2 * x + y
        sends = []
        for i in range(na):
            for k, (px, py) in enumerate(_other_chips(x, y)):
                cp = _remote(srcs[i].at[2 * px + py], outs[i].at[me], send_sems.at[3 * i + k],
                             recv_sems.at[3 * i + k], (px, py, c))
                cp.start()
                sends.append(cp)
        for i in range(na):
            for k, (px, py) in enumerate(_other_chips(x, y)):
                _remote(srcs[i].at[me], outs[i].at[2 * px + py], send_sems.at[3 * i + k], recv_sems.at[3 * i + k],
                        (px, py, c)).wait_recv()
        for cp in sends:
            cp.wait_send()

    return pl.pallas_call(
        body,
        name=name,
        in_specs=[HBM_SPEC] * na,
        out_specs=[HBM_SPEC] * na,
        out_shape=[jax.ShapeDtypeStruct(p.shape, p.dtype) for p in parts],
        scratch_shapes=_dma_sems(3 * na, 2),
    )(*parts)


def _to_sibling(halves, name):
    na = len(halves)

    def body(*refs):
        srcs, outs, send_sems, recv_sems = refs[:na], refs[na:2 * na], refs[2 * na], refs[2 * na + 1]
        x, y, c = _position()
        cps = [_remote(srcs[i], outs[i], send_sems.at[i], recv_sems.at[i], (x, y, 1 - c)) for i in range(na)]
        for cp in cps:
            cp.start()
        for cp in cps:
            cp.wait()

    return pl.pallas_call(
        body,
        name=name,
        in_specs=[HBM_SPEC] * na,
        out_specs=[HBM_SPEC] * na,
        out_shape=[jax.ShapeDtypeStruct(h.shape, h.dtype) for h in halves],
        scratch_shapes=_dma_sems(na, 2),
    )(*halves)


def _allsum_small(v, name):
    shape = v.shape

    def body(v_ref, o_ref, buf, send_sems, recv_sems):
        x, y, c = _position()
        me = 4 * x + 2 * y + c
        buf[me] = v_ref[...]
        peers = []
        for k in range(1, 8):
            fx, fy, fc = (k >> 2) & 1, (k >> 1) & 1, k & 1
            peers.append((jnp.where(fx == 1, 1 - x, x), jnp.where(fy == 1, 1 - y, y), jnp.where(fc == 1, 1 - c, c)))
        sends = []
        for k, peer in enumerate(peers):
            cp = pltpu.make_async_remote_copy(src_ref=v_ref, dst_ref=buf.at[me], send_sem=send_sems.at[k],
                                              recv_sem=recv_sems.at[k], device_id=peer, device_id_type=MESH)
            cp.start()
            sends.append(cp)
        for k, (px, py, pc) in enumerate(peers):
            pltpu.make_async_remote_copy(src_ref=v_ref, dst_ref=buf.at[4 * px + 2 * py + pc],
                                         send_sem=send_sems.at[k], recv_sem=recv_sems.at[k],
                                         device_id=(px, py, pc), device_id_type=MESH).wait_recv()
        for cp in sends:
            cp.wait_send()
        acc = buf[0]
        for d in range(1, 8):
            acc = acc + buf[d]
        o_ref[...] = acc

    return pl.pallas_call(
        body,
        name=name,
        in_specs=[pl.BlockSpec(memory_space=pltpu.VMEM)],
        out_specs=pl.BlockSpec(memory_space=pltpu.VMEM),
        out_shape=jax.ShapeDtypeStruct(shape, F32),
        scratch_shapes=[pltpu.VMEM((8,) + shape, F32), pltpu.SemaphoreType.DMA((7,)), pltpu.SemaphoreType.DMA((7,))],
    )(v)


def _row_tile(h):
    for t in range(256, 0, -16):
        if h % t == 0:
            return t
    return h


def _add_cores(g, a, name):
    n4, k, n = g.shape
    kh = k // 2
    th = _row_tile(kh)

    def body(g_ref, a_ref, f_ref, b_ref):
        c = lax.axis_index("c")
        total = jnp.where(c == 0, g_ref[0, 0], g_ref[0, 1]) + a_ref[0]
        f_ref[0] = total
        b_ref[0] = total.astype(BF16)

    spec = pl.BlockSpec((1, th, n), lambda j, r: (j, r, 0))
    return pl.pallas_call(
        body,
        name=name,
        grid=(n4, kh // th),
        in_specs=[pl.BlockSpec((1, 2, th, n), lambda j, r: (j, 0, r, 0)), spec],
        out_specs=[spec, spec],
        out_shape=[jax.ShapeDtypeStruct((n4, kh, n), F32), jax.ShapeDtypeStruct((n4, kh, n), BF16)],
        compiler_params=_params(),
    )(g.reshape(n4, 2, kh, n), a)


def _add_chips(pf, b, name):
    n4, h, n = pf.shape
    th = _row_tile(h)

    def body(pf_ref, b_ref, o_ref):
        me = 2 * lax.axis_index("x") + lax.axis_index("y")
        terms = [jnp.where(me == j, pf_ref[j], b_ref[j].astype(F32)) for j in range(n4)]
        o_ref[...] = ((terms[0] + terms[1]) + terms[2]) + terms[3]

    spec = pl.BlockSpec((n4, th, n), lambda r: (0, r, 0))
    return pl.pallas_call(
        body,
        name=name,
        grid=(h // th,),
        in_specs=[spec, spec],
        out_specs=pl.BlockSpec((th, n), lambda r: (r, 0)),
        out_shape=jax.ShapeDtypeStruct((h, n), F32),
        compiler_params=_params(),
    )(pf, b)


def _adam_math(w, g, m, v):
    m = ADAM_B1 * m + (1.0 - ADAM_B1) * g
    v = ADAM_B2 * v + (1.0 - ADAM_B2) * (g * g)
    m_hat = m / (1.0 - ADAM_B1 ** ADAM_STEP)
    v_hat = v / (1.0 - ADAM_B2 ** ADAM_STEP)
    delta = -ADAM_LR * (m_hat / (jnp.sqrt(v_hat) + ADAM_EPS) + ADAM_WD * w)
    return delta, m, v


def _adamw_shard(w, m, v, mine, other, name):
    _, k, n = w.shape
    kh = k // 2
    th = _row_tile(kh)
    nb = kh // th

    def body(w_ref, m_ref, v_ref, a_ref, b_ref, g_out, d_out, m_out, v_out):
        g = jnp.where(pl.program_id(0) == lax.axis_index("c"), a_ref[...], b_ref[...])
        g_out[...] = g
        d_out[...], m_out[...], v_out[...] = _adam_math(w_ref[...], g, m_ref[...], v_ref[...])

    full = pl.BlockSpec((None, th, n), lambda hh, r: (0, hh * nb + r, 0))
    half = pl.BlockSpec((th, n), lambda hh, r: (r, 0))
    return pl.pallas_call(
        body,
        name=name,
        grid=(2, nb),
        in_specs=[full, full, full, half, half],
        out_specs=[full] * 4,
        out_shape=[jax.ShapeDtypeStruct((1, k, n), F32)] * 4,
        compiler_params=_params(),
    )(w, m, v, mine, other)


def _adamw_small(w, g, m, v, name):
    cols = w.shape[1]
    return _rowwise(_adam_math, [w, g, m, v], [(cols, F32)] * 3, [], w.shape[0], name)


def _full_matrices(own, gathered):
    me = 2 * lax.axis_index("x") + lax.axis_index("y")
    out = {}
    for (n, _, _, axis), w, g in zip(BIG, own, gathered):
        out[n] = jnp.concatenate([jnp.where(me == j, w, g[j]) for j in range(N_CHIPS)], axis=axis)
    return out


def _split_for_chips(full):
    out = []
    for n, r, c, axis in BIG:
        g = full[n]
        out.append(jnp.stack([g[:, j * c:(j + 1) * c] for j in range(N_CHIPS)]) if axis == 1
                   else g.reshape(N_CHIPS, r, c))
    return out


def _pack_small(vecs, extra=None):
    flat = jnp.concatenate([vecs[n].reshape(-1) for n, _ in SMALL] + ([extra.reshape(-1)] if extra is not None else []))
    flat = jnp.pad(flat, (0, SMALL_ROWS * PACK_COLS - flat.shape[0]))
    return flat.reshape(SMALL_ROWS, PACK_COLS)


def _unpack_small(pack):
    flat = pack.reshape(-1)
    out, off = {}, 0
    for n, k in SMALL:
        out[n] = flat[off:off + k].reshape(1, k)
        off += k
    return out


def _forward_backward(x, pos, target, sm, w):
    s = x.shape[0]
    w_in = w["w_in"]
    w_a = w_in[:, 0:1536]
    zeros = functools.partial(jnp.zeros, dtype=BF16)
    w_b = jnp.concatenate([w_in[:, 1536:2176], zeros((D_MODEL, 64)), w_in[:, 2176:2208], zeros((D_MODEL, 32))], axis=1)
    w_g = w_in[:, 2208:4256]
    w_uq = jnp.pad(w["w_uq"].reshape(MLA_Q_RANK, MLA_HEADS, MLA_QK), ((0, 0), (0, 0), (0, LANES - MLA_QK)))
    w_uq = w_uq.reshape(MLA_Q_RANK, MLA_HEADS * LANES)
    w_ukv = w["w_ukv"]
    w_pm = jnp.pad(w["w_proj_mla"].reshape(MLA_HEADS, HALF, D_MODEL), ((0, 0), (HALF, 0), (0, 0)))
    w_pm = w_pm.reshape(MLA_HEADS * LANES, D_MODEL)
    w_ps, w_out, w_gu, w_dn = w["w_proj_sb"], w["w_out"], w["w_gate_up"], w["w_down"]
    inv_freq = ROPE_THETA ** (-jnp.arange(0, MLA_ROPE, 2, dtype=F32) / MLA_ROPE)
    inv_lane = jnp.concatenate([jnp.zeros((HALF,), F32), inv_freq, inv_freq, jnp.zeros((32,), F32)]).reshape(1, LANES)
    g_pre, g_post, b_gate = sm["norm_mix_pre"], sm["norm_mix_post"], sm["b_gate"]
    g_q, g_kv, g_fpre, g_fpost = sm["q_norm"], sm["kv_norm"], sm["norm_ffn_pre"], sm["norm_ffn_post"]

    def f_h1(x, g):
        return (_rms(x)[1] * g,)

    (h1,) = _rowwise(f_h1, [x, g_pre], [(D_MODEL, BF16)], [], ROW_TILE, "h1")
    qkv = _mm(h1, w_a, "nn", BF16, "proj_sb")
    lat = _mm(h1, w_b, "nn", F32, "proj_lat")
    gl = _mm(h1, w_g, "nn", F32, "proj_gate")

    def f_lat(lat, gq, gkv, pos, inv):
        cqn = _rms(lat[:, 0:MLA_Q_RANK])[1] * gq
        ckvn = _rms(lat[:, MLA_Q_RANK:MLA_Q_RANK + MLA_KV_RANK])[1] * gkv
        kr = lat[:, 640:768]
        ang = pos.astype(F32) * inv
        lane = _lane(ang.shape)
        cosv, sinv = jnp.cos(ang), jnp.sin(ang)
        rope_c = jnp.where(lane < 64, 1.0, jnp.where(lane < 96, cosv, 0.0))
        rope_s = jnp.where((lane >= 64) & (lane < 80), -sinv, jnp.where((lane >= 80) & (lane < 96), sinv, 0.0))
        krr = jnp.where(lane >= 64, kr * rope_c, 0.0) + _rope_swap(kr) * rope_s
        return cqn, ckvn, krr, rope_c, rope_s

    cqn, ckvn, krr, rope_c, rope_s = _rowwise(
        f_lat, [lat, g_q, g_kv, pos, inv_lane],
        [(MLA_Q_RANK, BF16), (MLA_KV_RANK, BF16), (LANES, F32), (LANES, F32), (LANES, F32)], [], ROW_TILE, "lat_norm")
    wide = MLA_HEADS * LANES
    qc, kc, vp = _mla_up(cqn, w_uq, ckvn, w_ukv, krr, rope_c, rope_s, "mla_up")
    o_sb, sb_tot, sb_first = _sb_fwd(qkv, "sb_fwd")
    o_mla, lse = _mla_fwd(qc, kc, vp, "mla_fwd")
    m_sb = _mm(o_sb, w_ps, "nn", F32, "proj_o_sb")
    m_mla = _mm(o_mla, w_pm, "nn", F32, "proj_o_mla")

    def f_merge(m_sb, m_mla, gl, b):
        g = _sigmoid(gl + b)
        return (g[:, :D_MODEL] * m_sb + g[:, D_MODEL:] * m_mla,)

    (merged,) = _rowwise(f_merge, [m_sb, m_mla, gl, b_gate], [(D_MODEL, BF16)], [], ROW_TILE, "merge")
    y = _mm(merged, w_out, "nn", F32, "mix_out")

    def f_res1(x, y, g2, g3):
        x2 = x + _rms(y)[1] * g2
        return x2, _rms(x2)[1] * g3

    x2, h2 = _rowwise(f_res1, [x, y, g_post, g_fpre], [(D_MODEL, F32), (D_MODEL, BF16)], [], ROW_TILE, "res1")

    g_ff, u_ff, act = _ffn_up(h2, w_gu, "ffn_up")
    f = _mm(act, w_dn, "nn", F32, "ffn_down")

    def f_loss(x2, f, g4, tgt):
        r, fhat = _rms(f)
        err = x2 + fhat * g4 - tgt
        loss = 0.5 * jnp.sum(jnp.sum(err * err, axis=-1, keepdims=True), axis=0, keepdims=True) / D_MODEL
        dout = err / D_MODEL
        return dout, _rms_bwd(dout, r, fhat, g4), loss, _colsum(dout * fhat)

    dout, df, loss, d_fpost = _rowwise(f_loss, [x2, f, g_fpost, target], [(D_MODEL, F32), (D_MODEL, BF16)],
                                       [1, D_MODEL], ROW_TILE, "loss")

    dg_ff, du_ff = _ffn_dact(df, w_dn, g_ff, u_ff, "d_act")
    dw_dn = _mm(act, df, "tn", F32, "dw_down")
    dh2 = _mm_nt_sum([(dg_ff, w_gu[:, :D_FF]), (du_ff, w_gu[:, D_FF:])], "d_h2")
    dw_gu = jnp.concatenate([_mm(h2, dg_ff, "tn", F32, "dw_gate"), _mm(h2, du_ff, "tn", F32, "dw_up")], axis=1)

    def f_dres1(dout, dh2, x2, y, g3, g2):
        r3, x2hat = _rms(x2)
        dx2 = dout + _rms_bwd(dh2, r3, x2hat, g3)
        r2, yhat = _rms(y)
        return dx2, _rms_bwd(dx2, r2, yhat, g2), _colsum(dh2 * x2hat), _colsum(dx2 * yhat)

    dx2, dy, d_fpre, d_post = _rowwise(f_dres1, [dout, dh2, x2, y, g_fpre, g_post],
                                       [(D_MODEL, F32), (D_MODEL, BF16)], [D_MODEL, D_MODEL], ROW_TILE, "d_res1")

    dmerged = _mm(dy, w_out, "nt", F32, "d_merged")
    dw_out = _mm(merged, dy, "tn", F32, "dw_out")

    def f_dmerge(dm, m_sb, m_mla, gl, b):
        g = _sigmoid(gl + b)
        g0, g1 = g[:, :D_MODEL], g[:, D_MODEL:]
        dgl = jnp.concatenate([dm * m_sb * g0 * (1.0 - g0), dm * m_mla * g1 * (1.0 - g1)], axis=1)
        return dm * g0, dm * g1, dgl, _colsum(dgl)

    dm_sb, dm_mla, dgl, d_bgate = _rowwise(f_dmerge, [dmerged, m_sb, m_mla, gl, b_gate],
                                           [(D_MODEL, BF16), (D_MODEL, BF16), (2 * D_MODEL, BF16)], [2 * D_MODEL],
                                           ROW_TILE, "d_merge")
    do_sb = _mm(dm_sb, w_ps, "nt", BF16, "d_o_sb")
    dw_ps = _mm(o_sb, dm_sb, "tn", F32, "dw_proj_sb")
    do_mla = _mm(dm_mla, w_pm, "nt", F32, "d_o_mla")
    dw_pm = _mm(o_mla, dm_mla, "tn", F32, "dw_proj_mla")

    dq_sb, dk_sb, dv_sb = _sb_bwd(qkv, do_sb, sb_tot, sb_first, "sb_bwd")
    dqc, dkc, dvp = _mla_bwd(qc, kc, vp, do_mla, o_mla, lse, "mla_bwd")

    def f_dheads(dqc, dkc, dvp, rc, rs):
        lane = _lane(rc.shape)
        dqs, dkvs = [], []
        dkr = jnp.zeros(rc.shape, F32)
        for h in range(MLA_HEADS):
            dq = dqc[:, h * LANES:(h + 1) * LANES]
            dk = dkc[:, h * LANES:(h + 1) * LANES]
            dqs.append(dq * rc + _rope_swap(dq * rs))
            dkvs.append(jnp.where(lane < HALF, dk, dvp[:, h * LANES:(h + 1) * LANES]))
            dkr = dkr + jnp.where(lane >= HALF, dk, 0.0)
        dkr = dkr * rc + _rope_swap(dkr * rs)
        return jnp.concatenate(dqs, axis=1), jnp.concatenate(dkvs, axis=1), dkr

    dq_raw, dkv, dkr = _rowwise(f_dheads, [dqc, dkc, dvp, rope_c, rope_s],
                                [(wide, BF16), (wide, BF16), (LANES, F32)], [], ROW_TILE, "d_mla_heads")
    dcqn = _mm(dq_raw, w_uq, "nt", F32, "d_cqn")
    dw_uq = _mm(cqn, dq_raw, "tn", F32, "dw_uq")
    dckvn = _mm(dkv, w_ukv, "nt", F32, "d_ckvn")
    dw_ukv = _mm(ckvn, dkv, "tn", F32, "dw_ukv")

    def f_dlat(lat, dcqn, dckvn, dkr, gq, gkv):
        rq, cqhat = _rms(lat[:, 0:MLA_Q_RANK])
        rkv, ckvhat = _rms(lat[:, MLA_Q_RANK:MLA_Q_RANK + MLA_KV_RANK])
        dlat = jnp.concatenate([_rms_bwd(dcqn, rq, cqhat, gq), _rms_bwd(dckvn, rkv, ckvhat, gkv), dkr], axis=1)
        return dlat, _colsum(dcqn * cqhat), _colsum(dckvn * ckvhat)

    dlat, d_gq, d_gkv = _rowwise(f_dlat, [lat, dcqn, dckvn, dkr, g_q, g_kv], [(768, BF16)],
                                 [MLA_Q_RANK, MLA_KV_RANK], ROW_TILE, "d_lat")

    sb_parts = [(dq_sb, w_a[:, 0:SB_WIDTH]), (dk_sb, w_a[:, SB_WIDTH:2 * SB_WIDTH]), (dv_sb, w_a[:, 2 * SB_WIDTH:])]
    dh1 = _mm_nt_sum(sb_parts + [(dlat, w_b), (dgl, w_g)], "d_h1")
    dw_sb = [_mm(h1, d, "tn", F32, "dw_in_sb_" + tag) for tag, (d, _) in zip("qkv", sb_parts)]
    dw_b = _mm(h1, dlat, "tn", F32, "dw_in_lat")
    dw_g = _mm(h1, dgl, "tn", F32, "dw_in_gate")

    def f_dx(x, dx2, dh1, g1):
        r, xhat = _rms(x)
        return dx2 + _rms_bwd(dh1, r, xhat, g1), _colsum(dh1 * xhat)

    grad_x, d_pre = _rowwise(f_dx, [x, dx2, dh1, g_pre], [(D_MODEL, F32)], [D_MODEL], ROW_TILE, "d_x")

    small = {"norm_mix_pre": d_pre, "norm_mix_post": d_post, "b_gate": d_bgate, "q_norm": d_gq, "kv_norm": d_gkv,
             "norm_ffn_pre": d_fpre, "norm_ffn_post": d_fpost}
    big = {
        "w_in": jnp.concatenate(dw_sb + [dw_b[:, 0:640], dw_b[:, 704:736], dw_g], axis=1),
        "w_uq": dw_uq.reshape(MLA_Q_RANK, MLA_HEADS, LANES)[:, :, :MLA_QK].reshape(MLA_Q_RANK, MLA_HEADS * MLA_QK),
        "w_ukv": dw_ukv,
        "w_proj_sb": dw_ps,
        "w_proj_mla": dw_pm.reshape(MLA_HEADS, LANES, D_MODEL)[:, HALF:, :].reshape(MLA_HEADS * HALF, D_MODEL),
        "w_out": dw_out,
        "w_gate_up": dw_gu,
        "w_down": dw_dn,
    }
    return loss, grad_x, small, big


def kernel(x, positions, norm_mix_pre, norm_mix_post, w_in, b_gate, q_norm, w_uq, kv_norm, w_ukv, w_proj_sb, w_proj_mla, w_out, norm_ffn_pre, norm_ffn_post, w_gate_up, w_down, loss_target, m_norm_mix_pre, m_norm_mix_post, m_w_in, m_b_gate, m_q_norm, m_w_uq, m_kv_norm, m_w_ukv, m_w_proj_sb, m_w_proj_mla, m_w_out, m_norm_ffn_pre, m_norm_ffn_post, m_w_gate_up, m_w_down, v_norm_mix_pre, v_norm_mix_post, v_w_in, v_b_gate, v_q_norm, v_w_uq, v_kv_norm, v_w_ukv, v_w_proj_sb, v_w_proj_mla, v_w_out, v_norm_ffn_pre, v_norm_ffn_post, v_w_gate_up, v_w_down):
    given = dict(locals())
    s = x.shape[1]
    names = [n for n, _, _, _ in BIG]
    small_w = {n: given[n] for n, _ in SMALL}

    own = [given[n][0].astype(BF16) for n in names]
    gathered = _gather_weights(own, "gather_weights")
    loss, grad_x, d_small, d_big = _forward_backward(
        x.reshape(s, D_MODEL), positions.reshape(s, 1), loss_target.reshape(s, D_MODEL), small_w,
        _full_matrices(own, gathered))

    pieces = _split_for_chips(d_big)
    from_sibling = _swap_halves(pieces, "grad_swap_halves")
    partial = [_add_cores(g, a, "grad_add_cores_" + n) for n, g, a in zip(names, pieces, from_sibling)]
    received = _scatter_chips([b for _, b in partial], "grad_scatter_chips")
    mine = [_add_chips(f, r, "grad_add_chips_" + n) for n, (f, _), r in zip(names, partial, received)]
    other = _to_sibling(mine, "grad_to_sibling")
    small_sum = _allsum_small(_pack_small(d_small, loss), "small_allsum")

    grads, deltas, new_m, new_v = {}, {}, {}, {}
    for n, a, b in zip(names, mine, other):
        grads[n], deltas[n], new_m[n], new_v[n] = _adamw_shard(given[n], given["m_" + n], given["v_" + n], a, b,
                                                               "adamw_" + n)
    delta_s, m_s, v_s = _adamw_small(_pack_small(small_w), small_sum,
                                     _pack_small({n: given["m_" + n] for n in small_w}),
                                     _pack_small({n: given["v_" + n] for n in small_w}), "adamw_small")

    loss_out = small_sum.reshape(-1)[SMALL_ELEMS]
    grads.update(_unpack_small(small_sum))
    deltas.update(_unpack_small(delta_s))
    new_m.update(_unpack_small(m_s))
    new_v.update(_unpack_small(v_s))
    return (loss_out, grad_x.reshape(1, s, D_MODEL), *[grads[n] for n in WEIGHT_ORDER],
            *[deltas[n] for n in WEIGHT_ORDER], *[new_m[n] for n in WEIGHT_ORDER], *[new_v[n] for n in WEIGHT_ORDER])
```

```python
import functools
import math

import jax
import jax.numpy as jnp
from jax import lax
from jax.experimental import pallas as pl
from jax.experimental.pallas import tpu as pltpu

F32 = jnp.float32
BF16 = jnp.bfloat16
MESH = pl.DeviceIdType.MESH

D_MODEL = 1024
SB_WIDTH = 512
MLA_HEADS = 8
MLA_Q_RANK = 384
MLA_KV_RANK = 256
MLA_ROPE = 32
MLA_QK = 96
D_FF = 2816
ROPE_THETA = 10000.0
EPS = 1e-6
LANES = 128
HALF = 64
ATT_TILE = 256
ROW_TILE = 512
N_CHIPS = 4

ADAM_LR = 0.001
ADAM_B1 = 0.9
ADAM_B2 = 0.999
ADAM_EPS = 1e-08
ADAM_WD = 0.01
ADAM_STEP = 10

VMEM_LIMIT = 56 * 1024 * 1024

BIG = (
    ("w_in", 1024, 1064, 1),
    ("w_uq", 384, 192, 1),
    ("w_ukv", 256, 256, 1),
    ("w_proj_sb", 512, 256, 1),
    ("w_proj_mla", 512, 256, 1),
    ("w_out", 256, 1024, 0),
    ("w_gate_up", 1024, 1408, 1),
    ("w_down", 704, 1024, 0),
)
PACK_COLS = 1024
SMALL = (
    ("norm_mix_pre", 1024),
    ("norm_mix_post", 1024),
    ("b_gate", 2048),
    ("q_norm", 384),
    ("kv_norm", 256),
    ("norm_ffn_pre", 1024),
    ("norm_ffn_post", 1024),
)
SMALL_ELEMS = sum(n for _, n in SMALL)
SMALL_ROWS = 8
WEIGHT_ORDER = ("norm_mix_pre", "norm_mix_post", "w_in", "b_gate", "q_norm", "w_uq", "kv_norm", "w_ukv",
                "w_proj_sb", "w_proj_mla", "w_out", "norm_ffn_pre", "norm_ffn_post", "w_gate_up", "w_down")


def _params():
    return pltpu.CompilerParams(vmem_limit_bytes=VMEM_LIMIT)


def _tile(dim, pref):
    t = (min(pref, dim) // LANES) * LANES
    while t >= LANES:
        if dim % t == 0:
            return t
        t -= LANES
    return dim


def _mm(a, b, mode, out_dtype, name, tm=1408, tn=1408, tk=1408):
    if mode == "nn":
        (m, k), n = a.shape, b.shape[1]
    elif mode == "nt":
        (m, k), n = a.shape, b.shape[0]
    else:
        (k, m), n = a.shape, b.shape[1]
    tm, tn, tk = _tile(m, tm), _tile(n, tn), _tile(k, tk)
    nk = k // tk
    if mode == "tn":
        a_spec = pl.BlockSpec((tk, tm), lambda i, j, l: (l, i))
        lhs_dim = 0
    else:
        a_spec = pl.BlockSpec((tm, tk), lambda i, j, l: (i, l))
        lhs_dim = 1
    if mode == "nt":
        b_spec = pl.BlockSpec((tn, tk), lambda i, j, l: (j, l))
        rhs_dim = 1
    else:
        b_spec = pl.BlockSpec((tk, tn), lambda i, j, l: (l, j))
        rhs_dim = 0
    dims = (((lhs_dim,), (rhs_dim,)), ((), ()))

    def body(a_ref, b_ref, o_ref, acc_ref):
        l = pl.program_id(2)

        @pl.when(l == 0)
        def _():
            acc_ref[...] = jnp.zeros_like(acc_ref)

        acc_ref[...] += lax.dot_general(a_ref[...].astype(BF16), b_ref[...].astype(BF16), dims,
                                        preferred_element_type=F32)

        @pl.when(l == nk - 1)
        def _():
            o_ref[...] = acc_ref[...].astype(o_ref.dtype)

    return pl.pallas_call(
        body,
        name=name,
        grid=(m // tm, n // tn, nk),
        in_specs=[a_spec, b_spec],
        out_specs=pl.BlockSpec((tm, tn), lambda i, j, l: (i, j)),
        out_shape=jax.ShapeDtypeStruct((m, n), out_dtype),
        scratch_shapes=[pltpu.VMEM((tm, tn), F32)],
        compiler_params=_params(),
    )(a, b)


def _mm_nt_sum(pairs, name, tm=512, tn=1024):
    m, n = pairs[0][0].shape[0], pairs[0][1].shape[0]
    tm, tn = _tile(m, tm), _tile(n, tn)

    def body(*refs):
        total = None
        for p in range(len(pairs)):
            part = _dot_nt(refs[2 * p][...].astype(BF16), refs[2 * p + 1][...].astype(BF16))
            total = part if total is None else total + part
        refs[-1][...] = total

    in_specs = []
    for a, b in pairs:
        in_specs += [pl.BlockSpec((tm, a.shape[1]), lambda i, j: (i, 0)),
                     pl.BlockSpec((tn, b.shape[1]), lambda i, j: (j, 0))]
    return pl.pallas_call(
        body,
        name=name,
        grid=(m // tm, n // tn),
        in_specs=in_specs,
        out_specs=pl.BlockSpec((tm, tn), lambda i, j: (i, j)),
        out_shape=jax.ShapeDtypeStruct((m, n), F32),
        compiler_params=_params(),
    )(*[r for pair in pairs for r in pair])


def _ffn_up(h, w_gu, name):
    s, k = h.shape
    tm, tn = _tile(s, 512), _tile(D_FF, 1408)
    nb = D_FF // tn

    def body(h_ref, wg_ref, wu_ref, g_ref, u_ref, act_ref):
        hv = h_ref[...]
        g = jnp.dot(hv, wg_ref[...], preferred_element_type=F32)
        u = jnp.dot(hv, wu_ref[...], preferred_element_type=F32)
        g_ref[...] = g
        u_ref[...] = u
        act_ref[...] = (g * _sigmoid(g) * u).astype(BF16)

    tile = pl.BlockSpec((tm, tn), lambda j, i: (i, j))
    out = jax.ShapeDtypeStruct((s, D_FF), F32)
    return pl.pallas_call(
        body,
        name=name,
        grid=(nb, s // tm),
        in_specs=[pl.BlockSpec((tm, k), lambda j, i: (i, 0)), pl.BlockSpec((k, tn), lambda j, i: (0, j)),
                  pl.BlockSpec((k, tn), lambda j, i: (0, nb + j))],
        out_specs=[tile, tile, tile],
        out_shape=[out, out, jax.ShapeDtypeStruct((s, D_FF), BF16)],
        compiler_params=_params(),
    )(h, w_gu, w_gu)


def _ffn_dact(df, w_dn, g, u, name):
    s, k = df.shape
    tm, tn = _tile(s, 512), _tile(D_FF, 1408)

    def body(df_ref, w_ref, g_ref, u_ref, dg_ref, du_ref):
        dact = _dot_nt(df_ref[...], w_ref[...])
        gv, uv = g_ref[...], u_ref[...]
        sg = _sigmoid(gv)
        dg_ref[...] = (dact * uv * (sg * (1.0 + gv * (1.0 - sg)))).astype(BF16)
        du_ref[...] = (dact * (gv * sg)).astype(BF16)

    tile = pl.BlockSpec((tm, tn), lambda j, i: (i, j))
    out = jax.ShapeDtypeStruct((s, D_FF), BF16)
    return pl.pallas_call(
        body,
        name=name,
        grid=(D_FF // tn, s // tm),
        in_specs=[pl.BlockSpec((tm, k), lambda j, i: (i, 0)), pl.BlockSpec((tn, k), lambda j, i: (j, 0)), tile, tile],
        out_specs=[tile, tile],
        out_shape=[out, out],
        compiler_params=_params(),
    )(df, w_dn, g, u)


def _rowwise(fn, ins, outs, reds, tm, name):
    rows = ins[0].shape[0]
    tm = min(tm, rows)
    n_in, n_out = len(ins), len(outs)

    def spec(shape):
        if shape[0] == rows:
            return pl.BlockSpec((tm, shape[1]), lambda i: (i, 0))
        return pl.BlockSpec(shape, lambda i: (0, 0))

    def body(*refs):
        i = pl.program_id(0)
        vals = fn(*[r[...] for r in refs[:n_in]])
        for r, v in zip(refs[n_in:n_in + n_out], vals[:n_out]):
            r[...] = v.astype(r.dtype)
        for r, v in zip(refs[n_in + n_out:], vals[n_out:]):
            @pl.when(i == 0)
            def _():
                r[...] = jnp.zeros_like(r)

            r[...] += v

    out_shape = [jax.ShapeDtypeStruct((rows, d), dt) for d, dt in outs]
    out_shape += [jax.ShapeDtypeStruct((1, d), F32) for d in reds]
    out_specs = [spec(s.shape) for s in out_shape]
    return pl.pallas_call(
        body,
        name=name,
        grid=(rows // tm,),
        in_specs=[spec(a.shape) for a in ins],
        out_specs=out_specs,
        out_shape=out_shape,
        compiler_params=_params(),
    )(*ins)


def _colsum(v):
    return jnp.sum(v, axis=0, keepdims=True)


def _rms(v):
    r = lax.rsqrt(jnp.mean(v * v, axis=-1, keepdims=True) + EPS)
    return r, v * r


def _rms_bwd(dy, r, vhat, g):
    u = dy * g
    return r * (u - vhat * jnp.mean(u * vhat, axis=-1, keepdims=True))


def _sigmoid(t):
    return 1.0 / (1.0 + jnp.exp(-t))


def _lane(shape):
    return lax.broadcasted_iota(jnp.int32, shape, len(shape) - 1)


def _rope_swap(v):
    lane = _lane(v.shape)
    up = pltpu.roll(v, LANES - 16, 1)
    down = pltpu.roll(v, 16, 1)
    return jnp.where((lane >= 64) & (lane < 80), up, jnp.where((lane >= 80) & (lane < 96), down, 0.0))


SB_DEAD = -120.0


def _softplus(z):
    return jnp.maximum(z, 0.0) + jnp.log(1.0 + jnp.exp(-jnp.abs(z)))


def _split_dot(v, m2):
    hi = v.astype(BF16)
    lo = (v - hi.astype(F32)).astype(BF16)
    return jnp.dot(jnp.concatenate([hi, lo], axis=1), m2, preferred_element_type=F32)


def _dot_nt(a, b):
    return lax.dot_general(a, b, (((1,), (1,)), ((), ())), preferred_element_type=F32)


def _dot_tn(a, b):
    return lax.dot_general(a, b, (((0,), (0,)), ((), ())), preferred_element_type=F32)


def _sb_fwd(qkv, name):
    s = qkv.shape[0]
    t = min(ATT_TILE, s // 2)
    npair = SB_WIDTH // LANES

    def body(q_ref, k_ref, v_ref, o_ref, t_ref, first_ref, c_s, acc_s):
        i = pl.program_id(1)
        lane = _lane((1, LANES))
        row = lax.broadcasted_iota(jnp.int32, (t, t), 0)
        col = lax.broadcasted_iota(jnp.int32, (t, t), 1)
        tri = col < row
        m_from = jnp.where(row >= col, 1.0, 0.0).astype(BF16)
        m_from = jnp.concatenate([m_from, m_from], axis=0)
        masks = (lane < HALF, lane >= HALF)
        qhs = [jnp.where(hm, q_ref[n * t:(n + 1) * t, :], jnp.zeros((t, LANES), BF16)) * 0.125
               for n in range(2) for hm in masks]

        c_s[...] = jnp.zeros_like(c_s)
        acc_s[...] = jnp.zeros_like(acc_s)

        def step(k, diag, chains):
            blocks = {}
            for n in sorted({ch // 2 for ch in chains}):
                rows = pl.ds(pl.multiple_of((2 * i + n - k) * t, t), t)
                blocks[n] = (k_ref[rows, :], v_ref[rows, :])
            zs = {ch: _dot_nt(qhs[ch], blocks[ch // 2][0]) for ch in chains}
            lbs = {}
            for ch in chains:
                lb = -_softplus(zs[ch])
                lbs[ch] = jnp.where(tri, lb, 0.0) if diag else lb
            sums = {ch: _split_dot(lbs[ch], m_from) for ch in chains}
            for ch in chains:
                c = c_s[ch]
                a = jnp.exp(zs[ch] + sums[ch] + jnp.concatenate([c] * (t // LANES), axis=1))
                if diag:
                    a = jnp.where(tri, a, 0.0)
                vs = blocks[ch // 2][1]
                vh = jnp.where(masks[ch % 2], vs, jnp.zeros_like(vs))
                acc_s[ch] += jnp.dot(a.astype(BF16), vh, preferred_element_type=F32)
                c_s[ch] = c + sums[ch][:, 0:1]

        def live():
            top = jnp.max(c_s[0])
            for ch in range(1, 4):
                top = jnp.maximum(top, jnp.max(c_s[ch]))
            return top

        step(0, True, (0, 1, 2, 3))
        last = 2 * i

        def more(state):
            k, top = state
            return (k <= last) & (top > SB_DEAD)

        def walk(state):
            k, _ = state
            step(k, False, (0, 1, 2, 3))
            return k + 1, live()

        k_end, top = lax.while_loop(more, walk, (jnp.int32(1), live()))
        tail = (k_end == last + 1) & (top > SB_DEAD)

        @pl.when(tail)
        def _():
            step(last + 1, False, (2, 3))

        first_ref[0, 0] = jnp.full((8, LANES), (k_end - 1).astype(F32) + jnp.where(tail, 0.5, 0.0))
        for n in range(2):
            o_ref[n * t:(n + 1) * t, :] = acc_s[2 * n] + acc_s[2 * n + 1]
            t_ref[n * t:(n + 1) * t, :] = jnp.where(masks[0], c_s[2 * n], c_s[2 * n + 1])

    blk = pl.BlockSpec((2 * t, LANES), lambda p, i: (i, p))
    steps = s // (2 * t)
    return pl.pallas_call(
        body,
        name=name,
        grid=(npair, steps),
        in_specs=[
            blk,
            pl.BlockSpec((s, LANES), lambda p, i: (0, npair + p)),
            pl.BlockSpec((s, LANES), lambda p, i: (0, 2 * npair + p)),
        ],
        out_specs=[blk, blk, pl.BlockSpec((1, 1, 8, LANES), lambda p, i: (p, i, 0, 0))],
        out_shape=[jax.ShapeDtypeStruct((s, SB_WIDTH), F32), jax.ShapeDtypeStruct((s, SB_WIDTH), F32),
                   jax.ShapeDtypeStruct((npair, steps, 8, LANES), F32)],
        scratch_shapes=[pltpu.VMEM((4, t, LANES), F32)] * 2,
        compiler_params=_params(),
    )(qkv, qkv, qkv)


def _sb_bwd(qkv, do, tot, first, name):
    s = qkv.shape[0]
    t = min(ATT_TILE, s // 2)
    npair = SB_WIDTH // LANES

    def body(q_ref, k_ref, v_ref, do_ref, t_ref, first_ref, dq_ref, dk_ref, dv_ref, bef_s, ebef_s):
        i = pl.program_id(1)

        @pl.when(i == 0)
        def _():
            dk_ref[...] = jnp.zeros_like(dk_ref)
            dv_ref[...] = jnp.zeros_like(dv_ref)

        lane = _lane((1, LANES))
        row = lax.broadcasted_iota(jnp.int32, (t, t), 0)
        col = lax.broadcasted_iota(jnp.int32, (t, t), 1)
        tri = col < row
        m_from = jnp.where(row >= col, 1.0, 0.0).astype(BF16)
        m_from = jnp.concatenate([m_from, m_from], axis=0)
        m_upto = jnp.where(row <= col, 1.0, 0.0).astype(BF16)
        m_upto = jnp.concatenate([m_upto, m_upto], axis=0)
        masks = (lane < HALF, lane >= HALF)
        zero_b = jnp.zeros((t, LANES), BF16)
        qhs, dohs, tots = [], [], []
        for n in range(2):
            rs = slice(n * t, (n + 1) * t)
            do_b = do_ref[rs, :].astype(BF16)
            for h, hm in enumerate(masks):
                qhs.append(jnp.where(hm, q_ref[rs, :], zero_b))
                dohs.append(jnp.where(hm, do_b, zero_b))
                tots.append(t_ref[rs, h * HALF:h * HALF + 1])

        bef_s[...] = jnp.zeros_like(bef_s)
        ebef_s[...] = jnp.zeros_like(ebef_s)
        dq_ref[...] = jnp.zeros_like(dq_ref)
        reps = t // LANES

        def step(k, diag, chains):
            qblocks = sorted({ch // 2 for ch in chains})
            rows, blocks = {}, {}
            for n in qblocks:
                rows[n] = pl.ds(pl.multiple_of((2 * i + n - k) * t, t), t)
                blocks[n] = (k_ref[rows[n], :], v_ref[rows[n], :])
            zs = {ch: _dot_nt(qhs[ch] * 0.125, blocks[ch // 2][0]) for ch in chains}
            das = {ch: _dot_nt(dohs[ch], blocks[ch // 2][1]) for ch in chains}
            lbs, betas = {}, {}
            for ch in chains:
                sp = _softplus(zs[ch])
                betas[ch] = jnp.exp(zs[ch] - sp)
                lbs[ch] = jnp.where(tri, -sp, 0.0) if diag else -sp
            sums = {ch: _split_dot(lbs[ch], m_from) for ch in chains}
            avs, es = {}, {}
            for ch in chains:
                before = bef_s[ch]
                total = sums[ch][:, 0:1]
                a = jnp.exp(zs[ch] + sums[ch] + jnp.concatenate([tots[ch] - before - total] * reps, axis=1))
                if diag:
                    a = jnp.where(tri, a, 0.0)
                avs[ch] = a.astype(BF16)
                es[ch] = a * das[ch]
                bef_s[ch] = before + total
            e_sums = {ch: _split_dot(es[ch], m_upto) for ch in chains}
            dzs = {}
            for ch in chains:
                e_upto = e_sums[ch] + jnp.concatenate([ebef_s[ch]] * reps, axis=1)
                dz = es[ch] - betas[ch] * e_upto
                if diag:
                    dz = jnp.where(tri, dz, 0.0)
                dzs[ch] = (dz * 0.125).astype(BF16)
                ebef_s[ch] = jnp.broadcast_to(e_upto[:, t - 1:t], (t, LANES))
            for n in qblocks:
                ks = blocks[n][0]
                mine = [ch for ch in chains if ch // 2 == n]
                dq = sum(jnp.dot(dzs[ch], jnp.where(masks[ch % 2], ks, jnp.zeros_like(ks)),
                                 preferred_element_type=F32) for ch in mine)
                dq_ref[n * t:(n + 1) * t, :] += dq
                dk_ref[rows[n], :] += sum(_dot_tn(dzs[ch], qhs[ch]) for ch in mine)
                dv_ref[rows[n], :] += sum(_dot_tn(avs[ch], dohs[ch]) for ch in mine)

        walked = jnp.max(first_ref[0, 0])
        full_steps = jnp.clip(walked.astype(jnp.int32), 0, 2 * i)

        @pl.when(walked - full_steps.astype(F32) > 0.25)
        def _():
            step(2 * i + 1, False, (2, 3))

        def walk(n, carry):
            step(full_steps - n, False, (0, 1, 2, 3))
            return carry

        lax.fori_loop(0, full_steps, walk, 0)
        step(0, True, (0, 1, 2, 3))

    blk = pl.BlockSpec((2 * t, LANES), lambda p, i: (i, p))
    full = pl.BlockSpec((s, LANES), lambda p, i: (0, p))
    out = jax.ShapeDtypeStruct((s, SB_WIDTH), F32)
    return pl.pallas_call(
        body,
        name=name,
        grid=(npair, s // (2 * t)),
        in_specs=[
            blk,
            pl.BlockSpec((s, LANES), lambda p, i: (0, npair + p)),
            pl.BlockSpec((s, LANES), lambda p, i: (0, 2 * npair + p)),
            blk,
            blk,
            pl.BlockSpec((1, 1, 8, LANES), lambda p, i: (p, i, 0, 0)),
        ],
        out_specs=[blk, full, full],
        out_shape=[out, out, out],
        scratch_shapes=[pltpu.VMEM((4, t, LANES), F32)] * 2,
        compiler_params=_params(),
    )(qkv, qkv, qkv, do, tot, first)


MLA_SCALE = 1.0 / math.sqrt(MLA_QK)
MLA_SCALE_LOG2E = MLA_SCALE * math.log2(math.e)
NEG = -1e30
MLA_QBLOCKS = 8


def _diag_plan(nq):
    return [tuple(None if n < j else n == j for n in range(nq)) for j in range(nq)]


def _mla_up(cqn, w_uq, ckvn, w_ukv, krr, rope_c, rope_s, name):
    s = cqn.shape[0]
    tm = _tile(s, 512)
    wide = MLA_HEADS * LANES

    def body(cq_ref, wq_ref, ckv_ref, wkv_ref, krr_ref, rc_ref, rs_ref, q_out, k_out, v_out):
        q = jnp.dot(cq_ref[...], wq_ref[...], preferred_element_type=F32)
        kv = jnp.dot(ckv_ref[...], wkv_ref[...], preferred_element_type=F32)
        krr, rc, rs = krr_ref[...], rc_ref[...], rs_ref[...]
        lane = _lane(krr.shape)
        for h in range(MLA_HEADS):
            cols = slice(h * LANES, (h + 1) * LANES)
            qh, kvh = q[:, cols], kv[:, cols]
            q_out[:, cols] = (qh * rc + _rope_swap(qh) * rs).astype(BF16)
            k_out[:, cols] = jnp.where(lane < HALF, kvh, krr).astype(BF16)
            v_out[:, cols] = jnp.where(lane >= HALF, kvh, 0.0).astype(BF16)

    def rows(a):
        return pl.BlockSpec((tm, a.shape[1]), lambda i: (i, 0))

    def whole(a):
        return pl.BlockSpec(a.shape, lambda i: (0, 0))

    out = jax.ShapeDtypeStruct((s, wide), BF16)
    return pl.pallas_call(
        body,
        name=name,
        grid=(s // tm,),
        in_specs=[rows(cqn), whole(w_uq), rows(ckvn), whole(w_ukv), rows(krr), rows(rope_c), rows(rope_s)],
        out_specs=[pl.BlockSpec((tm, wide), lambda i: (i, 0))] * 3,
        out_shape=[out, out, out],
        compiler_params=_params(),
    )(cqn, w_uq, ckvn, w_ukv, krr, rope_c, rope_s)


def _mla_fwd(qc, kc, vp, name):
    s = qc.shape[0]
    nq = MLA_QBLOCKS
    t = min(ATT_TILE, s // nq)

    def body(q_ref, k_ref, v_ref, o_ref, lse_ref, m_s, l_s, acc_s):
        i = pl.program_id(1)
        row = lax.broadcasted_iota(jnp.int32, (t, t), 0)
        col = lax.broadcasted_iota(jnp.int32, (t, t), 1)
        keep = col <= row
        m_s[...] = jnp.full(m_s.shape, NEG, F32)
        l_s[...] = jnp.zeros_like(l_s)
        acc_s[...] = jnp.zeros_like(acc_s)

        def step(kb, diags):
            rows = pl.ds(pl.multiple_of(kb * t, t), t)
            ks = k_ref[rows, :]
            vs = v_ref[rows, :]
            act = [n for n in range(nq) if diags[n] is not None]
            scs = {n: _dot_nt(q_ref[n * t:(n + 1) * t, :], ks) for n in act}
            mid = {}
            for n in act:
                sc = jnp.where(keep, scs[n], NEG) if diags[n] else scs[n]
                m = m_s[n]
                m_new = jnp.maximum(m, jnp.max(sc, axis=-1, keepdims=True))
                alpha = jnp.exp2((m - m_new) * MLA_SCALE_LOG2E)
                p = jnp.exp2((sc - jnp.concatenate([m_new] * (t // LANES), axis=1)) * MLA_SCALE_LOG2E)
                m_s[n] = m_new
                l_s[n] = alpha * l_s[n] + jnp.sum(p, axis=-1, keepdims=True)
                mid[n] = (alpha, p.astype(BF16))
            for n in act:
                alpha, p = mid[n]
                acc_s[n] = alpha * acc_s[n] + jnp.dot(p, vs, preferred_element_type=F32)

        def walk(kb, carry):
            step(kb, (False,) * nq)
            return carry

        lax.fori_loop(0, nq * i, walk, 0)
        for j, diags in enumerate(_diag_plan(nq)):
            step(nq * i + j, diags)
        for n in range(nq):
            l = l_s[n]
            o_ref[n * t:(n + 1) * t, :] = acc_s[n] / l
            lse_ref[n * t:(n + 1) * t, :] = m_s[n] * MLA_SCALE + jnp.log(l)

    blk = pl.BlockSpec((nq * t, LANES), lambda h, i: (i, h))
    full = pl.BlockSpec((s, LANES), lambda h, i: (0, h))
    out = jax.ShapeDtypeStruct((s, MLA_HEADS * LANES), F32)
    return pl.pallas_call(
        body,
        name=name,
        grid=(MLA_HEADS, s // (nq * t)),
        in_specs=[blk, full, full],
        out_specs=[blk, blk],
        out_shape=[out, out],
        scratch_shapes=[pltpu.VMEM((nq, t, LANES), F32)] * 3,
        compiler_params=_params(),
    )(qc, kc, vp)


def _mla_bwd(qc, kc, vp, do, o, lse, name):
    s = qc.shape[0]
    nq = MLA_QBLOCKS
    t = min(ATT_TILE, s // nq)

    def body(q_ref, k_ref, v_ref, do_ref, o_ref, lse_ref, dq_ref, dk_ref, dv_ref):
        i = pl.program_id(1)

        @pl.when(i == 0)
        def _():
            dk_ref[...] = jnp.zeros_like(dk_ref)
            dv_ref[...] = jnp.zeros_like(dv_ref)

        row = lax.broadcasted_iota(jnp.int32, (t, t), 0)
        col = lax.broadcasted_iota(jnp.int32, (t, t), 1)
        keep = col <= row

        def block(n):
            rs = slice(n * t, (n + 1) * t)
            do_f = do_ref[rs, :]
            delta = jnp.sum(do_f * o_ref[rs, :], axis=-1, keepdims=True)
            return q_ref[rs, :], do_f.astype(BF16), delta * MLA_SCALE, lse_ref[rs, 0:1] * math.log2(math.e)

        blocks = [block(n) for n in range(nq)]
        dq_ref[...] = jnp.zeros_like(dq_ref)

        def step(kb, diags):
            rows = pl.ds(pl.multiple_of(kb * t, t), t)
            ks = k_ref[rows, :]
            vs = v_ref[rows, :]
            act = [n for n in range(nq) if diags[n] is not None]
            scs = {n: _dot_nt(blocks[n][0], ks) for n in act}
            dps = {n: _dot_nt(blocks[n][1], vs) for n in act}
            ps, dss = {}, {}
            for n in act:
                _, _, delta_s, lse2 = blocks[n]
                p = jnp.exp2(scs[n] * MLA_SCALE_LOG2E - lse2)
                if diags[n]:
                    p = jnp.where(keep, p, 0.0)
                dss[n] = (p * (dps[n] * MLA_SCALE - delta_s)).astype(BF16)
                ps[n] = p.astype(BF16)
            for n in act:
                dq_ref[n * t:(n + 1) * t, :] += jnp.dot(dss[n], ks, preferred_element_type=F32)
            first = act[0] * t
            q_all = q_ref[first:nq * t, :]
            do_all = do_ref[first:nq * t, :].astype(BF16)
            dk_ref[rows, :] += _dot_tn(jnp.concatenate([dss[n] for n in act], axis=0), q_all)
            dv_ref[rows, :] += _dot_tn(jnp.concatenate([ps[n] for n in act], axis=0), do_all)

        def walk(kb, carry):
            step(kb, (False,) * nq)
            return carry

        lax.fori_loop(0, nq * i, walk, 0)
        for j, diags in enumerate(_diag_plan(nq)):
            step(nq * i + j, diags)

    blk = pl.BlockSpec((nq * t, LANES), lambda h, i: (i, h))
    full = pl.BlockSpec((s, LANES), lambda h, i: (0, h))
    out = jax.ShapeDtypeStruct((s, MLA_HEADS * LANES), F32)
    return pl.pallas_call(
        body,
        name=name,
        grid=(MLA_HEADS, s // (nq * t)),
        in_specs=[blk, full, full, blk, blk, blk],
        out_specs=[blk, full, full],
        out_shape=[out, out, out],
        compiler_params=_params(),
    )(qc, kc, vp, do, o, lse)


def _position():
    return lax.axis_index("x"), lax.axis_index("y"), lax.axis_index("c")


def _other_chips(x, y):
    return [(1 - x, y), (x, 1 - y), (1 - x, 1 - y)]


HBM_SPEC = pl.BlockSpec(memory_space=pltpu.HBM)


def _remote(src, dst, send_sem, recv_sem, device):
    return pltpu.make_async_remote_copy(src_ref=src, dst_ref=dst, send_sem=send_sem, recv_sem=recv_sem,
                                        device_id=device, device_id_type=MESH)


def _dma_sems(n, count):
    return [pltpu.SemaphoreType.DMA((n,)) for _ in range(count)]


def _gather_weights(ws, name):
    na = len(ws)

    def body(*refs):
        srcs, outs = refs[:na], refs[na:2 * na]
        s1, r1, s2, r2, s3, r3, s4, r4 = refs[2 * na:]
        x, y, c = _position()
        me = 2 * x + y
        px, py = jnp.where(c == 1, 1 - x, x), jnp.where(c == 1, y, 1 - y)
        qx, qy = jnp.where(c == 1, x, 1 - x), jnp.where(c == 1, 1 - y, y)
        p_idx, q_idx, d_idx = 2 * px + py, 2 * qx + qy, 2 * (1 - x) + (1 - y)
        partner, across, sibling = (px, py, c), (qx, qy, c), (x, y, 1 - c)
        my_half, other_half = 1 - c, c

        def half(i, slot, h):
            kh = ws[i].shape[0] // 2
            return outs[i].at[slot, pl.ds(h * kh, kh), :]

        sends = [_remote(srcs[i], outs[i].at[me], s1.at[i], r1.at[i], partner) for i in range(na)]
        for cp in sends:
            cp.start()
        for i in range(na):
            _remote(srcs[i], outs[i].at[p_idx], s1.at[i], r1.at[i], partner).wait_recv()
            onward = [_remote(half(i, p_idx, my_half), half(i, p_idx, my_half), s2.at[i], r2.at[i], across),
                      _remote(outs[i].at[p_idx], outs[i].at[p_idx], s3.at[i], r3.at[i], sibling)]
            for cp in onward:
                cp.start()
            sends += onward
        for i in range(na):
            _remote(half(i, d_idx, my_half), half(i, d_idx, my_half), s2.at[i], r2.at[i], across).wait_recv()
            cp = _remote(half(i, d_idx, my_half), half(i, d_idx, my_half), s4.at[i], r4.at[i], sibling)
            cp.start()
            sends.append(cp)
        for i in range(na):
            _remote(outs[i].at[q_idx], outs[i].at[q_idx], s3.at[i], r3.at[i], sibling).wait_recv()
            _remote(half(i, d_idx, other_half), half(i, d_idx, other_half), s4.at[i], r4.at[i], sibling).wait_recv()
        for cp in sends:
            cp.wait_send()

    return pl.pallas_call(
        body,
        name=name,
        in_specs=[HBM_SPEC] * na,
        out_specs=[HBM_SPEC] * na,
        out_shape=[jax.ShapeDtypeStruct((N_CHIPS,) + w.shape, w.dtype) for w in ws],
        scratch_shapes=_dma_sems(na, 8),
    )(*ws)


def _swap_halves(gs, name):
    na = len(gs)

    def body(*refs):
        srcs, outs, send_sems, recv_sems = refs[:na], refs[na:2 * na], refs[2 * na], refs[2 * na + 1]
        x, y, c = _position()
        cps = []
        for i in range(na):
            kh = gs[i].shape[1] // 2
            cps.append(_remote(srcs[i].at[:, pl.ds((1 - c) * kh, kh), :], outs[i], send_sems.at[i], recv_sems.at[i],
                               (x, y, 1 - c)))
            cps[-1].start()
        for cp in cps:
            cp.wait()

    return pl.pallas_call(
        body,
        name=name,
        in_specs=[HBM_SPEC] * na,
        out_specs=[HBM_SPEC] * na,
        out_shape=[jax.ShapeDtypeStruct((g.shape[0], g.shape[1] // 2, g.shape[2]), g.dtype) for g in gs],
        scratch_shapes=_dma_sems(na, 2),
    )(*gs)


def _scatter_chips(parts, name):
    na = len(parts)

    def body(*refs):
        srcs, outs, send_sems, recv_sems = refs[:na], refs[na:2 * na], refs[2 * na], refs[2 * na + 1]
        x, y, c = _position()
        me = 2 * x + y
        sends = []
        for i in range(na):
            for k, (px, py) in enumerate(_other_chips(x, y)):
                cp = _remote(srcs[i].at[2 * px + py], outs[i].at[me], send_sems.at[3 * i + k],
                             recv_sems.at[3 * i + k], (px, py, c))
                cp.start()
                sends.append(cp)
        for i in range(na):
            for k, (px, py) in enumerate(_other_chips(x, y)):
                _remote(srcs[i].at[me], outs[i].at[2 * px + py], send_sems.at[3 * i + k], recv_sems.at[3 * i + k],
                        (px, py, c)).wait_recv()
        for cp in sends:
            cp.wait_send()

    return pl.pallas_call(
        body,
        name=name,
        in_specs=[HBM_SPEC] * na,
        out_specs=[HBM_SPEC] * na,
        out_shape=[jax.ShapeDtypeStruct(p.shape, p.dtype) for p in parts],
        scratch_shapes=_dma_sems(3 * na, 2),
    )(*parts)


def _to_sibling(halves, name):
    na = len(halves)

    def body(*refs):
        srcs, outs, send_sems, recv_sems = refs[:na], refs[na:2 * na], refs[2 * na], refs[2 * na + 1]
        x, y, c = _position()
        cps = [_remote(srcs[i], outs[i], send_sems.at[i], recv_sems.at[i], (x, y, 1 - c)) for i in range(na)]
        for cp in cps:
            cp.start()
        for cp in cps:
            cp.wait()

    return pl.pallas_call(
        body,
        name=name,
        in_specs=[HBM_SPEC] * na,
        out_specs=[HBM_SPEC] * na,
        out_shape=[jax.ShapeDtypeStruct(h.shape, h.dtype) for h in halves],
        scratch_shapes=_dma_sems(na, 2),
    )(*halves)


def _allsum_small(v, name):
    shape = v.shape

    def body(v_ref, o_ref, buf, send_sems, recv_sems):
        x, y, c = _position()
        me = 4 * x + 2 * y + c
        buf[me] = v_ref[...]
        peers = []
        for k in range(1, 8):
            fx, fy, fc = (k >> 2) & 1, (k >> 1) & 1, k & 1
            peers.append((jnp.where(fx == 1, 1 - x, x), jnp.where(fy == 1, 1 - y, y), jnp.where(fc == 1, 1 - c, c)))
        sends = []
        for k, peer in enumerate(peers):
            cp = pltpu.make_async_remote_copy(src_ref=v_ref, dst_ref=buf.at[me], send_sem=send_sems.at[k],
                                              recv_sem=recv_sems.at[k], device_id=peer, device_id_type=MESH)
            cp.start()
            sends.append(cp)
        for k, (px, py, pc) in enumerate(peers):
            pltpu.make_async_remote_copy(src_ref=v_ref, dst_ref=buf.at[4 * px + 2 * py + pc],
                                         send_sem=send_sems.at[k], recv_sem=recv_sems.at[k],
                                         device_id=(px, py, pc), device_id_type=MESH).wait_recv()
        for cp in sends:
            cp.wait_send()
        acc = buf[0]
        for d in range(1, 8):
            acc = acc + buf[d]
        o_ref[...] = acc

    return pl.pallas_call(
        body,
        name=name,
        in_specs=[pl.BlockSpec(memory_space=pltpu.VMEM)],
        out_specs=pl.BlockSpec(memory_space=pltpu.VMEM),
        out_shape=jax.ShapeDtypeStruct(shape, F32),
        scratch_shapes=[pltpu.VMEM((8,) + shape, F32), pltpu.SemaphoreType.DMA((7,)), pltpu.SemaphoreType.DMA((7,))],
    )(v)


def _row_tile(h):
    for t in range(256, 0, -16):
        if h % t == 0:
            return t
    return h


def _add_cores(g, a, name):
    n4, k, n = g.shape
    kh = k // 2
    th = _row_tile(kh)

    def body(g_ref, a_ref, f_ref, b_ref):
        c = lax.axis_index("c")
        total = jnp.where(c == 0, g_ref[0, 0], g_ref[0, 1]) + a_ref[0]
        f_ref[0] = total
        b_ref[0] = total.astype(BF16)

    spec = pl.BlockSpec((1, th, n), lambda j, r: (j, r, 0))
    return pl.pallas_call(
        body,
        name=name,
        grid=(n4, kh // th),
        in_specs=[pl.BlockSpec((1, 2, th, n), lambda j, r: (j, 0, r, 0)), spec],
        out_specs=[spec, spec],
        out_shape=[jax.ShapeDtypeStruct((n4, kh, n), F32), jax.ShapeDtypeStruct((n4, kh, n), BF16)],
        compiler_params=_params(),
    )(g.reshape(n4, 2, kh, n), a)


def _add_chips(pf, b, name):
    n4, h, n = pf.shape
    th = _row_tile(h)

    def body(pf_ref, b_ref, o_ref):
        me = 2 * lax.axis_index("x") + lax.axis_index("y")
        terms = [jnp.where(me == j, pf_ref[j], b_ref[j].astype(F32)) for j in range(n4)]
        o_ref[...] = ((terms[0] + terms[1]) + terms[2]) + terms[3]

    spec = pl.BlockSpec((n4, th, n), lambda r: (0, r, 0))
    return pl.pallas_call(
        body,
        name=name,
        grid=(h // th,),
        in_specs=[spec, spec],
        out_specs=pl.BlockSpec((th, n), lambda r: (r, 0)),
        out_shape=jax.ShapeDtypeStruct((h, n), F32),
        compiler_params=_params(),
    )(pf, b)


def _adam_math(w, g, m, v):
    m = ADAM_B1 * m + (1.0 - ADAM_B1) * g
    v = ADAM_B2 * v + (1.0 - ADAM_B2) * (g * g)
    m_hat = m / (1.0 - ADAM_B1 ** ADAM_STEP)
    v_hat = v / (1.0 - ADAM_B2 ** ADAM_STEP)
    delta = -ADAM_LR * (m_hat / (jnp.sqrt(v_hat) + ADAM_EPS) + ADAM_WD * w)
    return delta, m, v


def _adamw_shard(w, m, v, mine, other, name):
    _, k, n = w.shape
    kh = k // 2
    th = _row_tile(kh)
    nb = kh // th

    def body(w_ref, m_ref, v_ref, a_ref, b_ref, g_out, d_out, m_out, v_out):
        g = jnp.where(pl.program_id(0) == lax.axis_index("c"), a_ref[...], b_ref[...])
        g_out[...] = g
        d_out[...], m_out[...], v_out[...] = _adam_math(w_ref[...], g, m_ref[...], v_ref[...])

    full = pl.BlockSpec((None, th, n), lambda hh, r: (0, hh * nb + r, 0))
    half = pl.BlockSpec((th, n), lambda hh, r: (r, 0))
    return pl.pallas_call(
        body,
        name=name,
        grid=(2, nb),
        in_specs=[full, full, full, half, half],
        out_specs=[full] * 4,
        out_shape=[jax.ShapeDtypeStruct((1, k, n), F32)] * 4,
        compiler_params=_params(),
    )(w, m, v, mine, other)


def _adamw_small(w, g, m, v, name):
    cols = w.shape[1]
    return _rowwise(_adam_math, [w, g, m, v], [(cols, F32)] * 3, [], w.shape[0], name)


def _full_matrices(own, gathered):
    me = 2 * lax.axis_index("x") + lax.axis_index("y")
    out = {}
    for (n, _, _, axis), w, g in zip(BIG, own, gathered):
        out[n] = jnp.concatenate([jnp.where(me == j, w, g[j]) for j in range(N_CHIPS)], axis=axis)
    return out


def _split_for_chips(full):
    out = []
    for n, r, c, axis in BIG:
        g = full[n]
        out.append(jnp.stack([g[:, j * c:(j + 1) * c] for j in range(N_CHIPS)]) if axis == 1
                   else g.reshape(N_CHIPS, r, c))
    return out


def _pack_small(vecs, extra=None):
    flat = jnp.concatenate([vecs[n].reshape(-1) for n, _ in SMALL] + ([extra.reshape(-1)] if extra is not None else []))
    flat = jnp.pad(flat, (0, SMALL_ROWS * PACK_COLS - flat.shape[0]))
    return flat.reshape(SMALL_ROWS, PACK_COLS)


def _unpack_small(pack):
    flat = pack.reshape(-1)
    out, off = {}, 0
    for n, k in SMALL:
        out[n] = flat[off:off + k].reshape(1, k)
        off += k
    return out


def _forward_backward(x, pos, target, sm, w):
    s = x.shape[0]
    w_in = w["w_in"]
    w_a = w_in[:, 0:1536]
    zeros = functools.partial(jnp.zeros, dtype=BF16)
    w_b = jnp.concatenate([w_in[:, 1536:2176], zeros((D_MODEL, 64)), w_in[:, 2176:2208], zeros((D_MODEL, 32))], axis=1)
    w_g = w_in[:, 2208:4256]
    w_uq = jnp.pad(w["w_uq"].reshape(MLA_Q_RANK, MLA_HEADS, MLA_QK), ((0, 0), (0, 0), (0, LANES - MLA_QK)))
    w_uq = w_uq.reshape(MLA_Q_RANK, MLA_HEADS * LANES)
    w_ukv = w["w_ukv"]
    w_pm = jnp.pad(w["w_proj_mla"].reshape(MLA_HEADS, HALF, D_MODEL), ((0, 0), (HALF, 0), (0, 0)))
    w_pm = w_pm.reshape(MLA_HEADS * LANES, D_MODEL)
    w_ps, w_out, w_gu, w_dn = w["w_proj_sb"], w["w_out"], w["w_gate_up"], w["w_down"]
    inv_freq = ROPE_THETA ** (-jnp.arange(0, MLA_ROPE, 2, dtype=F32) / MLA_ROPE)
    inv_lane = jnp.concatenate([jnp.zeros((HALF,), F32), inv_freq, inv_freq, jnp.zeros((32,), F32)]).reshape(1, LANES)
    g_pre, g_post, b_gate = sm["norm_mix_pre"], sm["norm_mix_post"], sm["b_gate"]
    g_q, g_kv, g_fpre, g_fpost = sm["q_norm"], sm["kv_norm"], sm["norm_ffn_pre"], sm["norm_ffn_post"]

    def f_h1(x, g):
        return (_rms(x)[1] * g,)

    (h1,) = _rowwise(f_h1, [x, g_pre], [(D_MODEL, BF16)], [], ROW_TILE, "h1")
    qkv = _mm(h1, w_a, "nn", BF16, "proj_sb")
    lat = _mm(h1, w_b, "nn", F32, "proj_lat")
    gl = _mm(h1, w_g, "nn", F32, "proj_gate")

    def f_lat(lat, gq, gkv, pos, inv):
        cqn = _rms(lat[:, 0:MLA_Q_RANK])[1] * gq
        ckvn = _rms(lat[:, MLA_Q_RANK:MLA_Q_RANK + MLA_KV_RANK])[1] * gkv
        kr = lat[:, 640:768]
        ang = pos.astype(F32) * inv
        lane = _lane(ang.shape)
        cosv, sinv = jnp.cos(ang), jnp.sin(ang)
        rope_c = jnp.where(lane < 64, 1.0, jnp.where(lane < 96, cosv, 0.0))
        rope_s = jnp.where((lane >= 64) & (lane < 80), -sinv, jnp.where((lane >= 80) & (lane < 96), sinv, 0.0))
        krr = jnp.where(lane >= 64, kr * rope_c, 0.0) + _rope_swap(kr) * rope_s
        return cqn, ckvn, krr, rope_c, rope_s

    cqn, ckvn, krr, rope_c, rope_s = _rowwise(
        f_lat, [lat, g_q, g_kv, pos, inv_lane],
        [(MLA_Q_RANK, BF16), (MLA_KV_RANK, BF16), (LANES, F32), (LANES, F32), (LANES, F32)], [], ROW_TILE, "lat_norm")
    wide = MLA_HEADS * LANES
    qc, kc, vp = _mla_up(cqn, w_uq, ckvn, w_ukv, krr, rope_c, rope_s, "mla_up")
    o_sb, sb_tot, sb_first = _sb_fwd(qkv, "sb_fwd")
    o_mla, lse = _mla_fwd(qc, kc, vp, "mla_fwd")
    m_sb = _mm(o_sb, w_ps, "nn", F32, "proj_o_sb")
    m_mla = _mm(o_mla, w_pm, "nn", F32, "proj_o_mla")

    def f_merge(m_sb, m_mla, gl, b):
        g = _sigmoid(gl + b)
        return (g[:, :D_MODEL] * m_sb + g[:, D_MODEL:] * m_mla,)

    (merged,) = _rowwise(f_merge, [m_sb, m_mla, gl, b_gate], [(D_MODEL, BF16)], [], ROW_TILE, "merge")
    y = _mm(merged, w_out, "nn", F32, "mix_out")

    def f_res1(x, y, g2, g3):
        x2 = x + _rms(y)[1] * g2
        return x2, _rms(x2)[1] * g3

    x2, h2 = _rowwise(f_res1, [x, y, g_post, g_fpre], [(D_MODEL, F32), (D_MODEL, BF16)], [], ROW_TILE, "res1")

    g_ff, u_ff, act = _ffn_up(h2, w_gu, "ffn_up")
    f = _mm(act, w_dn, "nn", F32, "ffn_down")

    def f_loss(x2, f, g4, tgt):
        r, fhat = _rms(f)
        err = x2 + fhat * g4 - tgt
        loss = 0.5 * jnp.sum(jnp.sum(err * err, axis=-1, keepdims=True), axis=0, keepdims=True) / D_MODEL
        dout = err / D_MODEL
        return dout, _rms_bwd(dout, r, fhat, g4), loss, _colsum(dout * fhat)

    dout, df, loss, d_fpost = _rowwise(f_loss, [x2, f, g_fpost, target], [(D_MODEL, F32), (D_MODEL, BF16)],
                                       [1, D_MODEL], ROW_TILE, "loss")

    dg_ff, du_ff = _ffn_dact(df, w_dn, g_ff, u_ff, "d_act")
    dw_dn = _mm(act, df, "tn", F32, "dw_down")
    dh2 = _mm_nt_sum([(dg_ff, w_gu[:, :D_FF]), (du_ff, w_gu[:, D_FF:])], "d_h2")
    dw_gu = jnp.concatenate([_mm(h2, dg_ff, "tn", F32, "dw_gate"), _mm(h2, du_ff, "tn", F32, "dw_up")], axis=1)

    def f_dres1(dout, dh2, x2, y, g3, g2):
        r3, x2hat = _rms(x2)
        dx2 = dout + _rms_bwd(dh2, r3, x2hat, g3)
        r2, yhat = _rms(y)
        return dx2, _rms_bwd(dx2, r2, yhat, g2), _colsum(dh2 * x2hat), _colsum(dx2 * yhat)

    dx2, dy, d_fpre, d_post = _rowwise(f_dres1, [dout, dh2, x2, y, g_fpre, g_post],
                                       [(D_MODEL, F32), (D_MODEL, BF16)], [D_MODEL, D_MODEL], ROW_TILE, "d_res1")

    dmerged = _mm(dy, w_out, "nt", F32, "d_merged")
    dw_out = _mm(merged, dy, "tn", F32, "dw_out")

    def f_dmerge(dm, m_sb, m_mla, gl, b):
        g = _sigmoid(gl + b)
        g0, g1 = g[:, :D_MODEL], g[:, D_MODEL:]
        dgl = jnp.concatenate([dm * m_sb * g0 * (1.0 - g0), dm * m_mla * g1 * (1.0 - g1)], axis=1)
        return dm * g0, dm * g1, dgl, _colsum(dgl)

    dm_sb, dm_mla, dgl, d_bgate = _rowwise(f_dmerge, [dmerged, m_sb, m_mla, gl, b_gate],
                                           [(D_MODEL, BF16), (D_MODEL, BF16), (2 * D_MODEL, BF16)], [2 * D_MODEL],
                                           ROW_TILE, "d_merge")
    do_sb = _mm(dm_sb, w_ps, "nt", BF16, "d_o_sb")
    dw_ps = _mm(o_sb, dm_sb, "tn", F32, "dw_proj_sb")
    do_mla = _mm(dm_mla, w_pm, "nt", F32, "d_o_mla")
    dw_pm = _mm(o_mla, dm_mla, "tn", F32, "dw_proj_mla")

    dq_sb, dk_sb, dv_sb = _sb_bwd(qkv, do_sb, sb_tot, sb_first, "sb_bwd")
    dqc, dkc, dvp = _mla_bwd(qc, kc, vp, do_mla, o_mla, lse, "mla_bwd")

    def f_dheads(dqc, dkc, dvp, rc, rs):
        lane = _lane(rc.shape)
        dqs, dkvs = [], []
        dkr = jnp.zeros(rc.shape, F32)
        for h in range(MLA_HEADS):
            dq = dqc[:, h * LANES:(h + 1) * LANES]
            dk = dkc[:, h * LANES:(h + 1) * LANES]
            dqs.append(dq * rc + _rope_swap(dq * rs))
            dkvs.append(jnp.where(lane < HALF, dk, dvp[:, h * LANES:(h + 1) * LANES]))
            dkr = dkr + jnp.where(lane >= HALF, dk, 0.0)
        dkr = dkr * rc + _rope_swap(dkr * rs)
        return jnp.concatenate(dqs, axis=1), jnp.concatenate(dkvs, axis=1), dkr

    dq_raw, dkv, dkr = _rowwise(f_dheads, [dqc, dkc, dvp, rope_c, rope_s],
                                [(wide, BF16), (wide, BF16), (LANES, F32)], [], ROW_TILE, "d_mla_heads")
    dcqn = _mm(dq_raw, w_uq, "nt", F32, "d_cqn")
    dw_uq = _mm(cqn, dq_raw, "tn", F32, "dw_uq")
    dckvn = _mm(dkv, w_ukv, "nt", F32, "d_ckvn")
    dw_ukv = _mm(ckvn, dkv, "tn", F32, "dw_ukv")

    def f_dlat(lat, dcqn, dckvn, dkr, gq, gkv):
        rq, cqhat = _rms(lat[:, 0:MLA_Q_RANK])
        rkv, ckvhat = _rms(lat[:, MLA_Q_RANK:MLA_Q_RANK + MLA_KV_RANK])
        dlat = jnp.concatenate([_rms_bwd(dcqn, rq, cqhat, gq), _rms_bwd(dckvn, rkv, ckvhat, gkv), dkr], axis=1)
        return dlat, _colsum(dcqn * cqhat), _colsum(dckvn * ckvhat)

    dlat, d_gq, d_gkv = _rowwise(f_dlat, [lat, dcqn, dckvn, dkr, g_q, g_kv], [(768, BF16)],
                                 [MLA_Q_RANK, MLA_KV_RANK], ROW_TILE, "d_lat")

    sb_parts = [(dq_sb, w_a[:, 0:SB_WIDTH]), (dk_sb, w_a[:, SB_WIDTH:2 * SB_WIDTH]), (dv_sb, w_a[:, 2 * SB_WIDTH:])]
    dh1 = _mm_nt_sum(sb_parts + [(dlat, w_b), (dgl, w_g)], "d_h1")
    dw_sb = [_mm(h1, d, "tn", F32, "dw_in_sb_" + tag) for tag, (d, _) in zip("qkv", sb_parts)]
    dw_b = _mm(h1, dlat, "tn", F32, "dw_in_lat")
    dw_g = _mm(h1, dgl, "tn", F32, "dw_in_gate")

    def f_dx(x, dx2, dh1, g1):
        r, xhat = _rms(x)
        return dx2 + _rms_bwd(dh1, r, xhat, g1), _colsum(dh1 * xhat)

    grad_x, d_pre = _rowwise(f_dx, [x, dx2, dh1, g_pre], [(D_MODEL, F32)], [D_MODEL], ROW_TILE, "d_x")

    small = {"norm_mix_pre": d_pre, "norm_mix_post": d_post, "b_gate": d_bgate, "q_norm": d_gq, "kv_norm": d_gkv,
             "norm_ffn_pre": d_fpre, "norm_ffn_post": d_fpost}
    big = {
        "w_in": jnp.concatenate(dw_sb + [dw_b[:, 0:640], dw_b[:, 704:736], dw_g], axis=1),
        "w_uq": dw_uq.reshape(MLA_Q_RANK, MLA_HEADS, LANES)[:, :, :MLA_QK].reshape(MLA_Q_RANK, MLA_HEADS * MLA_QK),
        "w_ukv": dw_ukv,
        "w_proj_sb": dw_ps,
        "w_proj_mla": dw_pm.reshape(MLA_HEADS, LANES, D_MODEL)[:, HALF:, :].reshape(MLA_HEADS * HALF, D_MODEL),
        "w_out": dw_out,
        "w_gate_up": dw_gu,
        "w_down": dw_dn,
    }
    return loss, grad_x, small, big


def kernel(x, positions, norm_mix_pre, norm_mix_post, w_in, b_gate, q_norm, w_uq, kv_norm, w_ukv, w_proj_sb, w_proj_mla, w_out, norm_ffn_pre, norm_ffn_post, w_gate_up, w_down, loss_target, m_norm_mix_pre, m_norm_mix_post, m_w_in, m_b_gate, m_q_norm, m_w_uq, m_kv_norm, m_w_ukv, m_w_proj_sb, m_w_proj_mla, m_w_out, m_norm_ffn_pre, m_norm_ffn_post, m_w_gate_up, m_w_down, v_norm_mix_pre, v_norm_mix_post, v_w_in, v_b_gate, v_q_norm, v_w_uq, v_kv_norm, v_w_ukv, v_w_proj_sb, v_w_proj_mla, v_w_out, v_norm_ffn_pre, v_norm_ffn_post, v_w_gate_up, v_w_down):
    given = dict(locals())
    s = x.shape[1]
    names = [n for n, _, _, _ in BIG]
    small_w = {n: given[n] for n, _ in SMALL}

    own = [given[n][0].astype(BF16) for n in names]
    gathered = _gather_weights(own, "gather_weights")
    loss, grad_x, d_small, d_big = _forward_backward(
        x.reshape(s, D_MODEL), positions.reshape(s, 1), loss_target.reshape(s, D_MODEL), small_w,
        _full_matrices(own, gathered))

    pieces = _split_for_chips(d_big)
    from_sibling = _swap_halves(pieces, "grad_swap_halves")
    partial = [_add_cores(g, a, "grad_add_cores_" + n) for n, g, a in zip(names, pieces, from_sibling)]
    received = _scatter_chips([b for _, b in partial], "grad_scatter_chips")
    mine = [_add_chips(f, r, "grad_add_chips_" + n) for n, (f, _), r in zip(names, partial, received)]
    other = _to_sibling(mine, "grad_to_sibling")
    small_sum = _allsum_small(_pack_small(d_small, loss), "small_allsum")

    grads, deltas, new_m, new_v = {}, {}, {}, {}
    for n, a, b in zip(names, mine, other):
        grads[n], deltas[n], new_m[n], new_v[n] = _adamw_shard(given[n], given["m_" + n], given["v_" + n], a, b,
                                                               "adamw_" + n)
    delta_s, m_s, v_s = _adamw_small(_pack_small(small_w), small_sum,
                                     _pack_small({n: given["m_" + n] for n in small_w}),
                                     _pack_small({n: given["v_" + n] for n in small_w}), "adamw_small")

    loss_out = small_sum.reshape(-1)[SMALL_ELEMS]
    grads.update(_unpack_small(small_sum))
    deltas.update(_unpack_small(delta_s))
    new_m.update(_unpack_small(m_s))
    new_v.update(_unpack_small(v_s))
    return (loss_out, grad_x.reshape(1, s, D_MODEL), *[grads[n] for n in WEIGHT_ORDER],
            *[deltas[n] for n in WEIGHT_ORDER], *[new_m[n] for n in WEIGHT_ORDER], *[new_v[n] for n in WEIGHT_ORDER])
```
